```python
import math
import jax, jax.numpy as jnp
from jax import lax
import numpy as np

D_MODEL = 1024
BATCH = 8
SEQ = 2048
DEPTH = 4
DEC_BATCH = 32
DEC_SEQ = 1
PAST_LEN = 8192
PAGE_SIZE = 128

SSM_EXPAND = 2
D_INNER = SSM_EXPAND * D_MODEL
SSM_HEAD_DIM = 64
SSM_HEADS = D_INNER // SSM_HEAD_DIM
SSM_GROUPS = 4
SSM_HPG = SSM_HEADS // SSM_GROUPS
D_STATE = 128
SSM_CONV = 4
CONV_DIM = D_INNER + 2 * SSM_GROUPS * D_STATE
SSM_IN_DIM = D_INNER + CONV_DIM + SSM_HEADS
SSM_CHUNK = 128

ATT_GROUPS = ((128, 1), (512, 4), (2048, 16))
ATT_HPG = 4
ATT_HEAD_DIM = 64
ATT_HEADS = ATT_HPG * len(ATT_GROUPS)
ATT_WIDTH = ATT_HEADS * ATT_HEAD_DIM
ROT_DIM = ATT_HEAD_DIM // 4
ROPE_THETA = 500000.0

N_MEM = 256
MEM_HEADS = 4
MEM_HEAD_DIM = D_MODEL // MEM_HEADS

D_FF = 2816
FFN_CONV = 3

EPS = 1e-6

kernel_name = "hybrid_ssd_dilated_swa_decoder_step"


def rms_norm(x, g):
    xf = x.astype(jnp.float32)
    y = xf * lax.rsqrt(jnp.mean(xf * xf, axis=-1, keepdims=True) + EPS)
    return (y * g.astype(jnp.float32)).astype(x.dtype)


def rope_partial(x, pos):
    half = ROT_DIM // 2
    inv = ROPE_THETA ** (-jnp.arange(half, dtype=jnp.float32) / half)
    ang = pos.astype(jnp.float32)[:, None] * inv[None, :]
    cos = jnp.cos(ang)[None, :, None, :]
    sin = jnp.sin(ang)[None, :, None, :]
    xr = x[..., :ROT_DIM].astype(jnp.float32)
    x1, x2 = xr[..., :half], xr[..., half:]
    rot = jnp.concatenate([x1 * cos - x2 * sin, x2 * cos + x1 * sin], axis=-1)
    return jnp.concatenate([rot.astype(x.dtype), x[..., ROT_DIM:]], axis=-1)


def causal_dwconv(xp, w, b):
    k_w = w.shape[0]
    t_len = xp.shape[1] - k_w + 1
    out = xp[:, 0:t_len] * w[0] + b
    for k in range(1, k_w):
        out = out + xp[:, k:k + t_len] * w[k]
    return out


def ssd_scan(xdt, a, bm, cm, h0, chunk):
    n, t_len, g, j, p = xdt.shape
    nc = t_len // chunk
    xdt = xdt.reshape(n, nc, chunk, g, j, p)
    a = a.reshape(n, nc, chunk, g, j)
    bm = bm.reshape(n, nc, chunk, g, -1)
    cm = cm.reshape(n, nc, chunk, g, -1)
    acum = jnp.cumsum(a, axis=2)
    acum_gj = jnp.moveaxis(acum, 2, -1)
    seg = acum_gj[..., :, None] - acum_gj[..., None, :]
    causal = jnp.tril(jnp.ones((chunk, chunk), dtype=bool))
    decay = jnp.exp(jnp.where(causal, seg, -jnp.inf))
    cb = jnp.einsum('nctgk,ncsgk->ncgts', cm, bm)
    w = cb[:, :, :, None] * decay
    y_intra = jnp.einsum('ncgjts,ncsgjp->nctgjp', w, xdt)
    decay_end = jnp.exp(acum[:, :, -1:] - acum)
    s_chunk = jnp.einsum('nclgk,nclgjp->ncgjpk', bm, xdt * decay_end[..., None])
    chunk_decay = jnp.exp(acum[:, :, -1])

    def step(h, inp):
        dec, s = inp
        return h * dec[..., None, None] + s, h

    h_last, h_in = lax.scan(step, h0, (jnp.moveaxis(chunk_decay, 1, 0), jnp.moveaxis(s_chunk, 1, 0)))
    h_in = jnp.moveaxis(h_in, 0, 1)
    y_inter = jnp.einsum('nctgk,ncgjpk->nctgjp', cm, h_in) * jnp.exp(acum)[..., None]
    return (y_intra + y_inter).reshape(n, t_len, g, j, p), h_last


def ssd_mixer(u, hist, h0, w_in, conv_w, conv_b, dt_bias, a_log, d_skip, norm_w, w_out):
    n, t_len, _ = u.shape
    proj = u @ w_in
    z = proj[..., :D_INNER]
    xbc = proj[..., D_INNER:D_INNER + CONV_DIM]
    dt_raw = proj[..., D_INNER + CONV_DIM:]
    xbc_full = jnp.concatenate([hist.astype(xbc.dtype), xbc], axis=1)
    conv_last = xbc_full[:, -(SSM_CONV - 1):]
    xbc = jax.nn.silu(causal_dwconv(xbc_full, conv_w, conv_b).astype(jnp.float32))
    xs = xbc[..., :D_INNER].reshape(n, t_len, SSM_GROUPS, SSM_HPG, SSM_HEAD_DIM)
    bm = xbc[..., D_INNER:D_INNER + SSM_GROUPS * D_STATE].reshape(n, t_len, SSM_GROUPS, D_STATE)
    cm = xbc[..., D_INNER + SSM_GROUPS * D_STATE:].reshape(n, t_len, SSM_GROUPS, D_STATE)
    dt = jax.nn.softplus(dt_raw.astype(jnp.float32) + dt_bias.astype(jnp.float32))
    dt = dt.reshape(n, t_len, SSM_GROUPS, SSM_HPG)
    a = dt * (-jnp.exp(a_log.astype(jnp.float32))).reshape(SSM_GROUPS, SSM_HPG)
    chunk = SSM_CHUNK if t_len % SSM_CHUNK == 0 else t_len
    h0g = h0.astype(jnp.float32).reshape(n, SSM_GROUPS, SSM_HPG, SSM_HEAD_DIM, D_STATE)
    y, h_last = ssd_scan(xs * dt[..., None], a, bm, cm, h0g, chunk)
    y = y + xs * d_skip.astype(jnp.float32).reshape(SSM_GROUPS, SSM_HPG, 1)
    gated = y.reshape(n, t_len, D_INNER) * jax.nn.silu(z.astype(jnp.float32))
    gated = gated.reshape(n, t_len, SSM_GROUPS, D_INNER // SSM_GROUPS)
    gated = gated * lax.rsqrt(jnp.mean(gated * gated, axis=-1, keepdims=True) + EPS)
    out = (gated.reshape(n, t_len, D_INNER) * norm_w.astype(jnp.float32)).astype(u.dtype) @ w_out
    return out, h_last.reshape(n, SSM_HEADS, SSM_HEAD_DIM, D_STATE).astype(h0.dtype), conv_last


def dilated_attn_prompt(q, k, v, window, dil):
    n, t_len, h, e = q.shape
    span = window // dil
    sub = t_len // dil
    nb = -(-sub // span)
    pad = nb * span - sub

    def to_blocks(x):
        x = x.reshape(n, sub, dil, h, e).transpose(0, 2, 1, 3, 4).reshape(n * dil, sub, h, e)
        x = jnp.pad(x, ((0, 0), (0, pad), (0, 0), (0, 0)))
        return x.reshape(n * dil, nb, span, h, e)

    def with_prev(x):
        prev = jnp.concatenate([jnp.zeros_like(x[:, :1]), x[:, :-1]], axis=1)
        return jnp.concatenate([prev, x], axis=2)

    qb = to_blocks(q)
    kk = with_prev(to_blocks(k))
    vv = with_prev(to_blocks(v))
    s = jnp.einsum('rbqhe,rbkhe->rbhqk', qb, kk).astype(jnp.float32) * (e ** -0.5)
    qi = jnp.arange(span)[:, None] + span
    ki = jnp.arange(2 * span)[None, :]
    band = (qi - ki >= 0) & (qi - ki <= span)
    not_first = jnp.arange(nb)[:, None, None] > 0
    valid = band[None] & (not_first | (ki >= span)[None])
    s = jnp.where(valid[None, :, None], s, -jnp.inf)
    m = jnp.max(s, axis=-1, keepdims=True)
    p = jnp.exp(s - m)
    den = jnp.sum(p, axis=-1, keepdims=True)
    o = jnp.einsum('rbhqk,rbkhe->rbqhe', p / den, vv.astype(jnp.float32))
    lse = jnp.moveaxis((m + jnp.log(den))[..., 0], 2, 3)

    def from_blocks(x):
        x = x.reshape(n * dil, nb * span, *x.shape[3:])[:, :sub]
        x = x.reshape(n, dil, sub, *x.shape[2:])
        return jnp.swapaxes(x, 1, 2).reshape(n, t_len, *x.shape[3:])

    return from_blocks(o), from_blocks(lse)


def dilated_attn_sample(q, kv_buf, k_new, v_new, window, dil):
    n, s_len, h, e = q.shape
    wb = kv_buf.shape[1]
    span = window // dil
    k_all = jnp.concatenate([kv_buf[:, :, 0].astype(k_new.dtype), k_new], axis=1)
    v_all = jnp.concatenate([kv_buf[:, :, 1].astype(v_new.dtype), v_new], axis=1)
    idx = wb + jnp.arange(s_len)[:, None] - dil * jnp.arange(span + 1)[None, :]
    valid = idx >= 0
    idx = jnp.maximum(idx, 0)
    kg = k_all[:, idx]
    vg = v_all[:, idx]
    s = jnp.einsum('nshe,nskhe->nhsk', q, kg).astype(jnp.float32) * (e ** -0.5)
    s = jnp.where(valid[None, None], s, -jnp.inf)
    m = jnp.max(s, axis=-1, keepdims=True)
    p = jnp.exp(s - m)
    den = jnp.sum(p, axis=-1, keepdims=True)
    o = jnp.einsum('nhsk,nskhe->nshe', p / den, vg.astype(jnp.float32))
    lse = jnp.moveaxis((m + jnp.log(den))[..., 0], 1, 2)
    return o, lse


def dilated_mixer(u, pos, w_qkv, w_o, bufs):
    n, t_len, _ = u.shape
    qkv = (u @ w_qkv).reshape(n, t_len, 3, ATT_HEADS, ATT_HEAD_DIM)
    q = rope_partial(qkv[:, :, 0], pos)
    k = rope_partial(qkv[:, :, 1], pos)
    v = qkv[:, :, 2]
    outs, lses, new_bufs = [], [], []
    for g, (win, dil) in enumerate(ATT_GROUPS):
        hs = slice(g * ATT_HPG, (g + 1) * ATT_HPG)
        qg, kg, vg = q[:, :, hs], k[:, :, hs], v[:, :, hs]
        if bufs is None:
            o, l = dilated_attn_prompt(qg, kg, vg, win, dil)
            keep = min(win, t_len)
            new_bufs.append(jnp.stack([kg[:, -keep:], vg[:, -keep:]], axis=2))
        else:
            buf = bufs[g]
            o, l = dilated_attn_sample(qg, buf, kg, vg, win, dil)
            wb = buf.shape[1]
            full = jnp.concatenate([buf.astype(kg.dtype), jnp.stack([kg, vg], axis=2)], axis=1)
            new_bufs.append(full[:, -wb:])
        outs.append(o)
        lses.append(l)
    alpha = jax.nn.softmax(jnp.stack(lses, axis=0), axis=0)
    o = jnp.concatenate([alpha[g][..., None] * outs[g] for g in range(len(ATT_GROUPS))], axis=2)
    y = o.reshape(n, t_len, ATT_WIDTH).astype(u.dtype) @ w_o
    return y, new_bufs


def memory_kv(mem, g, w_kv):
    n = mem.shape[0]
    return (rms_norm(mem, g) @ w_kv).reshape(n, N_MEM, 2, MEM_HEADS, MEM_HEAD_DIM)


def cross_attn(u, kv, w_q, w_o):
    n, t_len, _ = u.shape
    q = (u @ w_q).reshape(n, t_len, MEM_HEADS, MEM_HEAD_DIM)
    s = jnp.einsum('nthe,nmhe->nhtm', q, kv[:, :, 0].astype(q.dtype)).astype(jnp.float32) * (MEM_HEAD_DIM ** -0.5)
    p = jax.nn.softmax(s, axis=-1)
    o = jnp.einsum('nhtm,nmhe->nthe', p, kv[:, :, 1].astype(jnp.float32))
    return o.reshape(n, t_len, D_MODEL).astype(u.dtype) @ w_o


def conv_ffn(u, hist, w_gu, conv_w, conv_b, w_down):
    gu = u @ w_gu
    gate, up = gu[..., :D_FF], gu[..., D_FF:]
    g_full = jnp.concatenate([hist.astype(gate.dtype), gate], axis=1)
    new_hist = g_full[:, -(FFN_CONV - 1):]
    gc = causal_dwconv(g_full, conv_w, conv_b)
    hmid = jax.nn.silu(gc.astype(jnp.float32)) * up.astype(jnp.float32)
    return hmid.astype(u.dtype) @ w_down, new_hist


def trunk(x, start, prm, mem, st):
    n, t_len, _ = x.shape
    pos = start + jnp.arange(t_len, dtype=jnp.int32)
    new = {'ssm': [], 'ssm_conv': [], 'swa0': [], 'swa1': [], 'swa2': [], 'mem_kv': [], 'ffn_conv': []}
    for i in range(DEPTH):
        j = i // 2
        g = prm['norms'][i]
        u = rms_norm(x, g[0])
        if i % 2 == 0:
            if st is None:
                hist = jnp.zeros((n, SSM_CONV - 1, CONV_DIM), x.dtype)
                h0 = jnp.zeros((n, SSM_HEADS, SSM_HEAD_DIM, D_STATE), x.dtype)
            else:
                hist, h0 = st['ssm_conv'][j], st['ssm'][j]
            mix, h_last, conv_last = ssd_mixer(u, hist, h0, prm['ssm_w_in'][j], prm['ssm_conv_w'][j],
                                               prm['ssm_conv_b'][j], prm['ssm_dt_bias'][j], prm['ssm_a_log'][j],
                                               prm['ssm_d'][j], prm['ssm_norm_w'][j], prm['ssm_w_out'][j])
            new['ssm'].append(h_last)
            new['ssm_conv'].append(conv_last)
        else:
            bufs = None if st is None else [st['swa%d' % q][j] for q in range(len(ATT_GROUPS))]
            mix, new_bufs = dilated_mixer(u, pos, prm['att_w_qkv'][j], prm['att_w_o'][j], bufs)
            for q in range(len(ATT_GROUPS)):
                new['swa%d' % q].append(new_bufs[q])
        x = x + rms_norm(mix, g[1])
        if st is None:
            kv = memory_kv(mem, prm['mem_norm'][i], prm['xa_w_kv'][i])
            new['mem_kv'].append(kv)
        else:
            kv = st['mem_kv'][i]
        u = rms_norm(x, g[2])
        x = x + rms_norm(cross_attn(u, kv, prm['xa_w_q'][i], prm['xa_w_o'][i]), g[3])
        hist = jnp.zeros((n, FFN_CONV - 1, D_FF), x.dtype) if st is None else st['ffn_conv'][i]
        u = rms_norm(x, g[4])
        f, f_hist = conv_ffn(u, hist, prm['ffn_w_gu'][i], prm['ffn_conv_w'][i], prm['ffn_conv_b'][i],
                             prm['ffn_w_down'][i])
        new['ffn_conv'].append(f_hist)
        x = x + rms_norm(f, g[5])
    return x, {k: jnp.stack(v, axis=0) for k, v in new.items() if v}


def setup_inputs(seed: int = 0) -> dict:
    key = jax.random.key(seed)
    ks = iter(jax.random.split(key, 40))
    f32 = jnp.float32
    n_ssm = (DEPTH + 1) // 2
    n_att = DEPTH // 2

    def nrm(shape, scale=1.0):
        return scale * jax.random.normal(next(ks), shape, f32)

    dt0 = jnp.exp(jax.random.uniform(next(ks), (n_ssm, SSM_HEADS), f32, math.log(1e-3), math.log(1e-1)))
    inp = {}
    inp['x_prompt'] = nrm((BATCH, SEQ, D_MODEL))
    inp['x_sample'] = nrm((DEC_BATCH, DEC_SEQ, D_MODEL))
    inp['mem_prompt'] = nrm((BATCH, N_MEM, D_MODEL))
    inp['state_ssm'] = nrm((n_ssm, DEC_BATCH, SSM_HEADS, SSM_HEAD_DIM, D_STATE), 0.1)
    inp['state_ssm_conv'] = nrm((n_ssm, DEC_BATCH, SSM_CONV - 1, CONV_DIM))
    inp['cache_swa_kv_w128'] = nrm((n_att, DEC_BATCH, min(ATT_GROUPS[0][0], PAST_LEN), 2, ATT_HPG, ATT_HEAD_DIM))
    inp['cache_swa_kv_w512'] = nrm((n_att, DEC_BATCH, min(ATT_GROUPS[1][0], PAST_LEN), 2, ATT_HPG, ATT_HEAD_DIM))
    inp['cache_swa_kv_w2048'] = nrm((n_att, DEC_BATCH, min(ATT_GROUPS[2][0], PAST_LEN), 2, ATT_HPG, ATT_HEAD_DIM))
    inp['cache_mem_kv'] = nrm((DEPTH, DEC_BATCH, N_MEM, 2, MEM_HEADS, MEM_HEAD_DIM))
    inp['state_ffn_conv'] = nrm((DEPTH, DEC_BATCH, FFN_CONV - 1, D_FF))
    inp['norms'] = 1.0 + nrm((DEPTH, 6, D_MODEL), 0.02)
    inp['ssm_w_in'] = nrm((n_ssm, D_MODEL, SSM_IN_DIM), D_MODEL ** -0.5)
    inp['ssm_conv_w'] = nrm((n_ssm, SSM_CONV, CONV_DIM), SSM_CONV ** -0.5)
    inp['ssm_conv_b'] = nrm((n_ssm, CONV_DIM), 0.02)
    inp['ssm_dt_bias'] = dt0 + jnp.log(-jnp.expm1(-dt0))
    inp['ssm_a_log'] = jnp.log(jax.random.uniform(next(ks), (n_ssm, SSM_HEADS), f32, 1.0, 16.0))
    inp['ssm_d'] = 1.0 + nrm((n_ssm, SSM_HEADS), 0.1)
    inp['ssm_norm_w'] = 1.0 + nrm((n_ssm, D_INNER), 0.02)
    inp['ssm_w_out'] = nrm((n_ssm, D_INNER, D_MODEL), D_INNER ** -0.5)
    inp['att_w_qkv'] = nrm((n_att, D_MODEL, 3 * ATT_WIDTH), D_MODEL ** -0.5)
    inp['att_w_o'] = nrm((n_att, ATT_WIDTH, D_MODEL), ATT_WIDTH ** -0.5)
    inp['mem_norm'] = 1.0 + nrm((DEPTH, D_MODEL), 0.02)
    inp['xa_w_q'] = nrm((DEPTH, D_MODEL, D_MODEL), D_MODEL ** -0.5)
    inp['xa_w_kv'] = nrm((DEPTH, D_MODEL, 2 * D_MODEL), D_MODEL ** -0.5)
    inp['xa_w_o'] = nrm((DEPTH, D_MODEL, D_MODEL), D_MODEL ** -0.5)
    inp['ffn_w_gu'] = nrm((DEPTH, D_MODEL, 2 * D_FF), D_MODEL ** -0.5)
    inp['ffn_conv_w'] = nrm((DEPTH, FFN_CONV, D_FF), FFN_CONV ** -0.5)
    inp['ffn_conv_b'] = nrm((DEPTH, D_FF), 0.02)
    inp['ffn_w_down'] = nrm((DEPTH, D_FF, D_MODEL), D_FF ** -0.5)
    return inp


def reference(x_prompt, x_sample, mem_prompt, state_ssm, state_ssm_conv, cache_swa_kv_w128, cache_swa_kv_w512,
              cache_swa_kv_w2048, cache_mem_kv, state_ffn_conv, norms, ssm_w_in, ssm_conv_w, ssm_conv_b,
              ssm_dt_bias, ssm_a_log, ssm_d, ssm_norm_w, ssm_w_out, att_w_qkv, att_w_o, mem_norm, xa_w_q,
              xa_w_kv, xa_w_o, ffn_w_gu, ffn_conv_w, ffn_conv_b, ffn_w_down):
    prm = {'norms': norms, 'ssm_w_in': ssm_w_in, 'ssm_conv_w': ssm_conv_w, 'ssm_conv_b': ssm_conv_b,
           'ssm_dt_bias': ssm_dt_bias, 'ssm_a_log': ssm_a_log, 'ssm_d': ssm_d, 'ssm_norm_w': ssm_norm_w,
           'ssm_w_out': ssm_w_out, 'att_w_qkv': att_w_qkv, 'att_w_o': att_w_o, 'mem_norm': mem_norm,
           'xa_w_q': xa_w_q, 'xa_w_kv': xa_w_kv, 'xa_w_o': xa_w_o, 'ffn_w_gu': ffn_w_gu,
           'ffn_conv_w': ffn_conv_w, 'ffn_conv_b': ffn_conv_b, 'ffn_w_down': ffn_w_down}
    st = {'ssm': state_ssm, 'ssm_conv': state_ssm_conv, 'swa0': cache_swa_kv_w128, 'swa1': cache_swa_kv_w512,
          'swa2': cache_swa_kv_w2048, 'mem_kv': cache_mem_kv, 'ffn_conv': state_ffn_conv}
    y_prompt, sp = trunk(x_prompt, 0, prm, mem_prompt, None)
    y_sample, ss = trunk(x_sample, PAST_LEN, prm, None, st)
    return (y_prompt, y_sample,
            sp['ssm'], sp['ssm_conv'], sp['swa0'], sp['swa1'], sp['swa2'], sp['mem_kv'], sp['ffn_conv'],
            ss['ssm'], ss['ssm_conv'], ss['swa0'], ss['swa1'], ss['swa2'], ss['ffn_conv'])
```

```python
import functools
import math

import numpy as np
import jax
import jax.numpy as jnp
from jax import lax
from jax.experimental import pallas as pl
from jax.experimental.pallas import tpu as pltpu

F32 = jnp.float32
BF16 = jnp.bfloat16

EPS = 1e-6
PAST_LEN = 8192
SSM_HEAD_DIM = 64
SSM_GROUPS = 4
D_STATE = 128
SSM_CONV = 4
SSM_CHUNK = 128
ATT_GROUPS = ((128, 1), (512, 4), (2048, 16))
ATT_HPG = 4
ATT_HEAD_DIM = 64
ATT_GW = ATT_HPG * ATT_HEAD_DIM
ROT_DIM = ATT_HEAD_DIM // 4
ROPE_THETA = 500000.0
MEM_HEADS = 4
FFN_CONV = 3

LANE = 128
SUBLANE = 8
VMEM_LIMIT = 56 * 1024 * 1024


def _cparams(sem):
    return pltpu.CompilerParams(dimension_semantics=sem, vmem_limit_bytes=VMEM_LIMIT)


def _rms(x, g):
    return x * lax.rsqrt(jnp.mean(x * x, axis=-1, keepdims=True) + EPS) * g


def _silu(x):
    return x / (1.0 + jnp.exp(-x))


def _softplus(x):
    return jnp.maximum(x, 0.0) + jnp.log(1.0 + jnp.exp(-jnp.abs(x)))


def _dot(a, b):
    return jnp.dot(a, b, preferred_element_type=F32)


def _dot_nt(a, b):
    return lax.dot_general(a, b, (((1,), (1,)), ((), ())), preferred_element_type=F32)


def _split3(v):
    hi = v.astype(BF16)
    r = v - hi.astype(F32)
    mid = r.astype(BF16)
    lo = (r - mid.astype(F32)).astype(BF16)
    return hi, mid, lo


def _expand(v, e):
    hi, mid, lo = _split3(v)
    return _dot(hi, e) + _dot(mid, e) + _dot(lo, e)


def _full(shape):
    return pl.BlockSpec(shape, lambda *_: (0,) * len(shape))


def _norm_proj_body(x_ref, g_ref, w_ref, o_ref):
    u = _rms(x_ref[...], g_ref[...]).astype(BF16)
    o_ref[...] = _dot(u, w_ref[...])


def _norm_proj(x, g, w, tm):
    m, d = x.shape
    n = w.shape[1]
    return pl.pallas_call(
        _norm_proj_body,
        grid=(m // tm,),
        in_specs=[pl.BlockSpec((tm, d), lambda i: (i, 0)), _full((1, d)), _full((d, n))],
        out_specs=pl.BlockSpec((tm, n), lambda i: (i, 0)),
        out_shape=jax.ShapeDtypeStruct((m, n), F32),
        compiler_params=_cparams(("parallel",)),
        name="norm_proj",
    )(x, g, w)


def _norm_qkv_rope_body(x_ref, g_ref, w_ref, cos_ref, sa_ref, sb_ref, o_ref, *, n_rot):
    u = _rms(x_ref[...], g_ref[...]).astype(BF16)
    y = _dot(u, w_ref[...])
    cos, sa, sb = cos_ref[...], sa_ref[...], sb_ref[...]
    for c in range(n_rot // LANE):
        t = y[:, c * LANE:(c + 1) * LANE]
        o_ref[:, c * LANE:(c + 1) * LANE] = (t * cos + pltpu.roll(t, LANE - ROT_DIM // 2, 1) * sa
                                             + pltpu.roll(t, ROT_DIM // 2, 1) * sb)
    o_ref[:, n_rot:] = y[:, n_rot:]


def _norm_qkv_rope(x, g, w, tabs, tm, seq_blocks):
    m, d = x.shape
    n = w.shape[1]
    tab_spec = pl.BlockSpec((tm, LANE), lambda i: (i % seq_blocks, 0))
    return pl.pallas_call(
        functools.partial(_norm_qkv_rope_body, n_rot=2 * n // 3),
        grid=(m // tm,),
        in_specs=[pl.BlockSpec((tm, d), lambda i: (i, 0)), _full((1, d)), _full((d, n)),
                  tab_spec, tab_spec, tab_spec],
        out_specs=pl.BlockSpec((tm, n), lambda i: (i, 0)),
        out_shape=jax.ShapeDtypeStruct((m, n), F32),
        compiler_params=_cparams(("parallel",)),
        name="norm_qkv_rope",
    )(x, g, w, *tabs)


def _rope_tables(pos):
    half = ROT_DIM // 2
    inv = ROPE_THETA ** (-jnp.arange(half, dtype=F32) / half)
    ang = pos.astype(F32)[:, None] * inv[None, :]
    cos, sin = jnp.cos(ang), jnp.sin(ang)
    p = pos.shape[0]
    rest = ATT_HEAD_DIM - ROT_DIM
    c = jnp.concatenate([cos, cos, jnp.ones((p, rest), F32)], axis=1)
    sa = jnp.concatenate([-sin, jnp.zeros((p, half + rest), F32)], axis=1)
    sb = jnp.concatenate([jnp.zeros((p, half), F32), sin, jnp.zeros((p, rest), F32)], axis=1)
    rep = LANE // ATT_HEAD_DIM
    return tuple(jnp.tile(t, (1, rep)) for t in (c, sa, sb))


def _proj_res_body(y_ref, w_ref, x_ref, g_ref, o_ref):
    f = _dot(y_ref[...].astype(BF16), w_ref[...])
    o_ref[...] = x_ref[...] + _rms(f, g_ref[...])


def _proj_res(y, w, x, g, tm):
    m, k = y.shape
    d = w.shape[1]
    return pl.pallas_call(
        _proj_res_body,
        grid=(m // tm,),
        in_specs=[pl.BlockSpec((tm, k), lambda i: (i, 0)), _full((k, d)),
                  pl.BlockSpec((tm, d), lambda i: (i, 0)), _full((1, d))],
        out_specs=pl.BlockSpec((tm, d), lambda i: (i, 0)),
        out_shape=jax.ShapeDtypeStruct((m, d), F32),
        compiler_params=_cparams(("parallel",)),
        name="proj_res",
    )(y, w, x, g)


def _attn_out_body(o0, o1, o2, l0, l1, l2, w_ref, x_ref, g_ref, out_ref):
    ls = [l0[...], l1[...], l2[...]]
    mx = jnp.maximum(jnp.maximum(ls[0], ls[1]), ls[2])
    es = [jnp.exp(l - mx) for l in ls]
    den = es[0] + es[1] + es[2]
    f = None
    for gi, o_ref in enumerate((o0, o1, o2)):
        og = (es[gi] / den * o_ref[...]).astype(BF16)
        part = _dot(og, w_ref[gi * ATT_GW:(gi + 1) * ATT_GW, :])
        f = part if f is None else f + part
    out_ref[...] = x_ref[...] + _rms(f, g_ref[...])


def _attn_out(os_, ls_, w, x, g, tm):
    m, d = x.shape
    blk = pl.BlockSpec((tm, ATT_GW), lambda i: (i, 0))
    return pl.pallas_call(
        _attn_out_body,
        grid=(m // tm,),
        in_specs=[blk] * 6 + [_full(w.shape), pl.BlockSpec((tm, d), lambda i: (i, 0)), _full((1, d))],
        out_specs=pl.BlockSpec((tm, d), lambda i: (i, 0)),
        out_shape=jax.ShapeDtypeStruct((m, d), F32),
        compiler_params=_cparams(("parallel",)),
        name="attn_out",
    )(*os_, *ls_, w, x, g)


def _mem_kv_body(x_ref, g_ref, w_ref, o_ref):
    u = _rms(x_ref[...], g_ref[...]).astype(BF16)
    o_ref[...] = _dot(u, w_ref[...])


def _mem_kv(mem, g, w, tm):
    m, d = mem.shape
    depth, _, n = w.shape
    return pl.pallas_call(
        _mem_kv_body,
        grid=(depth, m // tm),
        in_specs=[pl.BlockSpec((tm, d), lambda l, i: (i, 0)),
                  pl.BlockSpec((None, 1, d), lambda l, i: (l, 0, 0)),
                  pl.BlockSpec((None, d, n), lambda l, i: (l, 0, 0))],
        out_specs=pl.BlockSpec((None, tm, n), lambda l, i: (l, i, 0)),
        out_shape=jax.ShapeDtypeStruct((depth, m, n), F32),
        compiler_params=_cparams(("parallel", "parallel")),
        name="mem_kv",
    )(mem, g, w)


def _xattn_prompt_body(x_ref, gpre_ref, gpost_ref, wq_ref, kv_ref, wo_ref, o_ref, obuf):
    x = x_ref[...]
    d = x.shape[1]
    hd = d // MEM_HEADS
    u = _rms(x, gpre_ref[...]).astype(BF16)
    q = _dot(u, wq_ref[...]).astype(BF16)
    scale = hd ** -0.5
    for h in range(MEM_HEADS):
        kh = kv_ref[:, h * hd:(h + 1) * hd].astype(BF16)
        vh = kv_ref[:, d + h * hd:d + (h + 1) * hd].astype(BF16)
        s = _dot_nt(q[:, h * hd:(h + 1) * hd], kh) * scale
        mx = jnp.max(s, axis=-1, keepdims=True)
        p = jnp.exp(s - mx)
        den = jnp.sum(p, axis=-1, keepdims=True)
        obuf[:, h * hd:(h + 1) * hd] = (_dot(p.astype(BF16), vh) / den).astype(BF16)
    f = _dot(obuf[...], wo_ref[...])
    o_ref[...] = x + _rms(f, gpost_ref[...])


def _xattn_prompt(x, gpre, gpost, wq, kv, wo, n_seq, tm):
    m, d = x.shape
    tb = m // n_seq // tm
    n_mem = kv.shape[0] // n_seq
    return pl.pallas_call(
        _xattn_prompt_body,
        grid=(n_seq, tb),
        in_specs=[pl.BlockSpec((tm, d), lambda b, t: (b * tb + t, 0)), _full((1, d)), _full((1, d)),
                  _full((d, d)), pl.BlockSpec((n_mem, 2 * d), lambda b, t: (b, 0)), _full((d, d))],
        out_specs=pl.BlockSpec((tm, d), lambda b, t: (b * tb + t, 0)),
        out_shape=jax.ShapeDtypeStruct((m, d), F32),
        scratch_shapes=[pltpu.VMEM((tm, d), BF16)],
        compiler_params=_cparams(("parallel", "parallel")),
        name="xattn_prompt",
    )(x, gpre, gpost, wq, kv, wo)


def _xattn_sample_body(x_ref, gpre_ref, gpost_ref, wq_ref, kv_ref, wo_ref, o_ref, q_scr, o_scr):
    b = pl.program_id(0)
    d = x_ref.shape[1]
    hd = d // MEM_HEADS

    @pl.when(b == 0)
    def _():
        u = _rms(x_ref[...], gpre_ref[...]).astype(BF16)
        q_scr[...] = _dot(u, wq_ref[...]) * (hd ** -0.5)

    qb = q_scr[pl.ds(b, 1), :]
    for h in range(MEM_HEADS):
        kh = kv_ref[:, h * hd:(h + 1) * hd]
        s = jnp.sum(kh * qb[:, h * hd:(h + 1) * hd], axis=1, keepdims=True)
        mx = jnp.max(s, axis=0, keepdims=True)
        p = jnp.exp(s - mx)
        den = jnp.sum(p, axis=0, keepdims=True)
        vh = kv_ref[:, d + h * hd:d + (h + 1) * hd]
        o_scr[pl.ds(b, 1), h * hd:(h + 1) * hd] = jnp.sum(p * vh, axis=0, keepdims=True) / den

    @pl.when(b == pl.num_programs(0) - 1)
    def _():
        f = _dot(o_scr[...].astype(BF16), wo_ref[...])
        o_ref[...] = x_ref[...] + _rms(f, gpost_ref[...])


def _xattn_sample(x, gpre, gpost, wq, kv_all, layer, wo):
    m, d = x.shape
    n_mem = kv_all.shape[1]
    return pl.pallas_call(
        _xattn_sample_body,
        grid=(m,),
        in_specs=[_full((m, d)), _full((1, d)), _full((1, d)), _full((d, d)),
                  pl.BlockSpec((None, n_mem, 2 * d), lambda b: (layer * m + b, 0, 0)), _full((d, d))],
        out_specs=_full((m, d)),
        out_shape=jax.ShapeDtypeStruct((m, d), F32),
        scratch_shapes=[pltpu.VMEM((m, d), F32), pltpu.VMEM((m, d), F32)],
        compiler_params=_cparams(("arbitrary",)),
        name="xattn_sample",
    )(x, gpre, gpost, wq, kv_all, wo)


def _ffn_prompt_body(x_ref, gpre_ref, gpost_ref, wg_ref, wu_ref, cw_ref, cb_ref, wd_ref, hist_ref,
                     o_ref, nh_ref, gbuf):
    t = pl.program_id(1)
    tm = x_ref.shape[0]
    k = FFN_CONV - 1
    base = SUBLANE - k

    @pl.when(t == 0)
    def _():
        gbuf[base:SUBLANE, :] = hist_ref[...]

    x = x_ref[...]
    u = _rms(x, gpre_ref[...]).astype(BF16)
    gbuf[SUBLANE:SUBLANE + tm, :] = _dot(u, wg_ref[...])
    up = _dot(u, wu_ref[...])
    gc = gbuf[base:base + tm, :] * cw_ref[0:1, :] + cb_ref[...]
    for j in range(1, FFN_CONV):
        gc = gc + gbuf[base + j:base + j + tm, :] * cw_ref[j:j + 1, :]
    hmid = (_silu(gc) * up).astype(BF16)
    f = _dot(hmid, wd_ref[...])
    o_ref[...] = x + _rms(f, gpost_ref[...])
    last = gbuf[tm + base:tm + SUBLANE, :]
    nh_ref[...] = last
    gbuf[base:SUBLANE, :] = last


def _ffn_prompt(x, gpre, gpost, wg, wu, cw, cb, wd, hist, n_seq, tm):
    m, d = x.shape
    f = wg.shape[1]
    tb = m // n_seq // tm
    k = FFN_CONV - 1
    return pl.pallas_call(
        _ffn_prompt_body,
        grid=(n_seq, tb),
        in_specs=[pl.BlockSpec((tm, d), lambda b, t: (b * tb + t, 0)), _full((1, d)), _full((1, d)),
                  _full((d, f)), _full((d, f)), _full((FFN_CONV, f)), _full((1, f)), _full((f, d)),
                  pl.BlockSpec((None, k, f), lambda b, t: (b, 0, 0))],
        out_specs=[pl.BlockSpec((tm, d), lambda b, t: (b * tb + t, 0)),
                   pl.BlockSpec((None, k, f), lambda b, t: (b, 0, 0))],
        out_shape=[jax.ShapeDtypeStruct((m, d), F32), jax.ShapeDtypeStruct((n_seq, k, f), F32)],
        scratch_shapes=[pltpu.VMEM((tm + SUBLANE, f), F32)],
        compiler_params=_cparams(("parallel", "arbitrary")),
        name="ffn_prompt",
    )(x, gpre, gpost, wg, wu, cw, cb, wd, hist)


def _ffn_sample_body(x_ref, gpre_ref, gpost_ref, wg_ref, wu_ref, cw_ref, cb_ref, wd_ref, hist_ref,
                     o_ref, nh_ref, u_scr, acc):
    j = pl.program_id(0)

    @pl.when(j == 0)
    def _():
        u_scr[...] = _rms(x_ref[...], gpre_ref[...]).astype(BF16)
        acc[...] = jnp.zeros_like(acc)

    u = u_scr[...]
    gate = _dot(u, wg_ref[...])
    up = _dot(u, wu_ref[...])
    gc = hist_ref[0] * cw_ref[0:1, :] + cb_ref[...]
    for k in range(1, FFN_CONV - 1):
        gc = gc + hist_ref[k] * cw_ref[k:k + 1, :]
        nh_ref[k - 1] = hist_ref[k]
    gc = gc + gate * cw_ref[FFN_CONV - 1:FFN_CONV, :]
    nh_ref[FFN_CONV - 2] = gate
    hmid = (_silu(gc) * up).astype(BF16)
    acc[...] += _dot(hmid, wd_ref[...])

    @pl.when(j == pl.num_programs(0) - 1)
    def _():
        o_ref[...] = x_ref[...] + _rms(acc[...], gpost_ref[...])


def _ffn_sample(x, gpre, gpost, wg, wu, cw, cb, wd, hist_t, tn):
    m, d = x.shape
    f = wg.shape[1]
    k = FFN_CONV - 1
    return pl.pallas_call(
        _ffn_sample_body,
        grid=(f // tn,),
        in_specs=[_full((m, d)), _full((1, d)), _full((1, d)),
                  pl.BlockSpec((d, tn), lambda j: (0, j)), pl.BlockSpec((d, tn), lambda j: (0, j)),
                  pl.BlockSpec((FFN_CONV, tn), lambda j: (0, j)), pl.BlockSpec((1, tn), lambda j: (0, j)),
                  pl.BlockSpec((tn, d), lambda j: (j, 0)),
                  pl.BlockSpec((k, m, tn), lambda j: (0, 0, j))],
        out_specs=[_full((m, d)), pl.BlockSpec((k, m, tn), lambda j: (0, 0, j))],
        out_shape=[jax.ShapeDtypeStruct((m, d), F32), jax.ShapeDtypeStruct((k, m, f), F32)],
        scratch_shapes=[pltpu.VMEM((m, d), BF16), pltpu.VMEM((m, d), F32)],
        compiler_params=_cparams(("arbitrary",)),
        name="ffn_sample",
    )(x, gpre, gpost, wg, wu, cw, cb, wd, hist_t)


def _ssd_prompt_body(proj_ref, cw_ref, cb_ref, dtb_ref, alog_ref, dx_ref, nw_ref, e_ref, tril_ref,
                     hist_ref, h0_ref, y_ref, hlast_ref, clast_ref, cbuf, state, ybuf,
                     *, d_inner, n_heads):
    c = pl.program_id(1)
    L = SSM_CHUNK
    k = SSM_CONV - 1
    base = SUBLANE - k
    gn = SSM_GROUPS * D_STATE
    conv_dim = d_inner + 2 * gn
    hpg = n_heads // SSM_GROUPS
    gw = d_inner // SSM_GROUPS

    @pl.when(c == 0)
    def _():
        cbuf[base:SUBLANE, :] = hist_ref[...]
        state[...] = h0_ref[...]

    cbuf[SUBLANE:SUBLANE + L, :] = proj_ref[:, d_inner:d_inner + conv_dim]
    xc = cbuf[base:base + L, :] * cw_ref[0:1, :] + cb_ref[...]
    for j in range(1, SSM_CONV):
        xc = xc + cbuf[base + j:base + j + L, :] * cw_ref[j:j + 1, :]
    xc = _silu(xc)
    last = cbuf[L + base:L + SUBLANE, :]
    clast_ref[...] = last
    cbuf[base:SUBLANE, :] = last

    xs = xc[:, :d_inner]
    e = e_ref[...]
    dt = _softplus(proj_ref[:, d_inner + conv_dim:] + dtb_ref[...])
    a = dt * (-jnp.exp(alog_ref[...]))
    tril = tril_ref[...]
    a_hi, a_mid, a_lo = _split3(a)
    acum = _dot(tril, a_hi) + _dot(tril, a_mid) + _dot(tril, a_lo)
    acum_t = acum.T
    a_last = acum[L - 1:L, :]
    xdt = xs * _expand(dt, e)
    d_acc = _expand(jnp.exp(acum), e)
    xde = xdt * _expand(jnp.exp(a_last - acum), e)
    row = lax.broadcasted_iota(jnp.int32, (L, L), 0)
    col = lax.broadcasted_iota(jnp.int32, (L, L), 1)
    causal = row >= col

    for g in range(SSM_GROUPS):
        bg = xc[:, d_inner + g * D_STATE:d_inner + (g + 1) * D_STATE].astype(BF16)
        cg = xc[:, d_inner + gn + g * D_STATE:d_inner + gn + (g + 1) * D_STATE].astype(BF16)
        cb = _dot_nt(cg, bg)
        hg = state[g * gw:(g + 1) * gw, :]
        y_inter = _dot_nt(cg, hg.astype(BF16)) * d_acc[:, g * gw:(g + 1) * gw]
        for j in range(hpg):
            hd = g * hpg + j
            seg = acum[:, hd:hd + 1] - acum_t[hd:hd + 1, :]
            w = cb * jnp.exp(jnp.where(causal, seg, -jnp.inf))
            sl = slice(hd * SSM_HEAD_DIM, (hd + 1) * SSM_HEAD_DIM)
            ybuf[:, sl] = _dot(w.astype(BF16), xdt[:, sl].astype(BF16))
        ybuf[:, g * gw:(g + 1) * gw] += y_inter
        s_g = _dot(xde[:, g * gw:(g + 1) * gw].T.astype(BF16), bg)
        for j in range(hpg):
            hd = g * hpg + j
            rs = slice(hd * SSM_HEAD_DIM, (hd + 1) * SSM_HEAD_DIM)
            cd = jnp.exp(acum_t[hd:hd + 1, L - 1:L])
            state[rs, :] = state[rs, :] * cd + s_g[j * SSM_HEAD_DIM:(j + 1) * SSM_HEAD_DIM, :]

    y = ybuf[...] + xs * dx_ref[...]
    gated = y * _silu(proj_ref[:, :d_inner])
    for g in range(SSM_GROUPS):
        seg = gated[:, g * gw:(g + 1) * gw]
        y_ref[:, g * gw:(g + 1) * gw] = _rms(seg, nw_ref[:, g * gw:(g + 1) * gw]).astype(BF16)

    @pl.when(c == pl.num_programs(1) - 1)
    def _():
        hlast_ref[...] = state[...]


def _ssd_prompt(proj, cw, cb, dtb, alog, dx, nw, e, tril, hist, h0, n_seq, d_inner, n_heads):
    m, pw = proj.shape
    L = SSM_CHUNK
    nc = m // n_seq // L
    conv_dim = cw.shape[1]
    k = SSM_CONV - 1
    rows = n_heads * SSM_HEAD_DIM
    body = functools.partial(_ssd_prompt_body, d_inner=d_inner, n_heads=n_heads)
    return pl.pallas_call(
        body,
        grid=(n_seq, nc),
        in_specs=[pl.BlockSpec((L, pw), lambda b, c: (b * nc + c, 0)),
                  _full(cw.shape), _full(cb.shape), _full(dtb.shape), _full(alog.shape), _full(dx.shape),
                  _full(nw.shape), _full(e.shape), _full(tril.shape),
                  pl.BlockSpec((None, k, conv_dim), lambda b, c: (b, 0, 0)),
                  pl.BlockSpec((None, rows, D_STATE), lambda b, c: (b, 0, 0))],
        out_specs=[pl.BlockSpec((L, d_inner), lambda b, c: (b * nc + c, 0)),
                   pl.BlockSpec((None, rows, D_STATE), lambda b, c: (b, 0, 0)),
                   pl.BlockSpec((None, k, conv_dim), lambda b, c: (b, 0, 0))],
        out_shape=[jax.ShapeDtypeStruct((m, d_inner), BF16),
                   jax.ShapeDtypeStruct((n_seq, rows, D_STATE), F32),
                   jax.ShapeDtypeStruct((n_seq, k, conv_dim), F32)],
        scratch_shapes=[pltpu.VMEM((L + SUBLANE, conv_dim), F32), pltpu.VMEM((rows, D_STATE), F32),
                        pltpu.VMEM((L, d_inner), F32)],
        compiler_params=_cparams(("parallel", "arbitrary")),
        name="ssd_prompt",
    )(proj, cw, cb, dtb, alog, dx, nw, e, tril, hist, h0)


def _ssd_step_pre_body(proj_ref, hist_ref, cw_ref, cb_ref, dtb_ref, alog_ref, e_ref,
                       xs_ref, b_ref, c_ref, xdt_t_ref, dec_t_ref, clast_ref, *, d_inner):
    gn = SSM_GROUPS * D_STATE
    conv_dim = d_inner + 2 * gn
    m = proj_ref.shape[0]
    xbc = proj_ref[:, d_inner:d_inner + conv_dim]
    xc = hist_ref[0] * cw_ref[0:1, :] + cb_ref[...]
    for j in range(1, SSM_CONV - 1):
        xc = xc + hist_ref[j] * cw_ref[j:j + 1, :]
        clast_ref[j - 1] = hist_ref[j]
    xc = _silu(xc + xbc * cw_ref[SSM_CONV - 1:SSM_CONV, :])
    clast_ref[SSM_CONV - 2] = xbc
    xs = xc[:, :d_inner]
    xs_ref[...] = xs
    b_ref[...] = xc[:, d_inner:d_inner + gn]
    c_ref[...] = xc[:, d_inner + gn:]
    e = e_ref[...]
    dt = _softplus(proj_ref[:, d_inner + conv_dim:] + dtb_ref[...])
    dec = jnp.exp(dt * (-jnp.exp(alog_ref[...])))
    pad = jnp.zeros((LANE - m, d_inner), F32)
    xdt_t_ref[...] = jnp.concatenate([xs * _expand(dt, e), pad], axis=0).T
    dec_t_ref[...] = jnp.concatenate([_expand(dec, e), pad], axis=0).T


def _ssd_step_pre(proj, hist_t, cw, cb, dtb, alog, e, d_inner):
    m = proj.shape[0]
    gn = SSM_GROUPS * D_STATE
    conv_dim = cw.shape[1]
    k = SSM_CONV - 1
    args = (proj, hist_t, cw, cb, dtb, alog, e)
    return pl.pallas_call(
        functools.partial(_ssd_step_pre_body, d_inner=d_inner),
        grid=(1,),
        in_specs=[_full(a.shape) for a in args],
        out_specs=[_full((m, d_inner)), _full((m, gn)), _full((m, gn)), _full((d_inner, LANE)),
                   _full((d_inner, LANE)), _full((k, m, conv_dim))],
        out_shape=[jax.ShapeDtypeStruct((m, d_inner), F32), jax.ShapeDtypeStruct((m, gn), F32),
                   jax.ShapeDtypeStruct((m, gn), F32), jax.ShapeDtypeStruct((d_inner, LANE), F32),
                   jax.ShapeDtypeStruct((d_inner, LANE), F32), jax.ShapeDtypeStruct((k, m, conv_dim), F32)],
        compiler_params=_cparams(("arbitrary",)),
        name="ssd_step_pre",
    )(*args)


def _ssd_step_state_body(h0_ref, xdt_t_ref, dec_t_ref, b_ref, c_ref, hn_ref, y_t_ref):
    m = h0_ref.shape[0]
    lane = lax.broadcasted_iota(jnp.int32, (SSM_HEAD_DIM, LANE), 1)
    acc = jnp.zeros((SSM_HEAD_DIM, LANE), F32)
    for b in range(m):
        h = h0_ref[b] * dec_t_ref[:, b:b + 1] + xdt_t_ref[:, b:b + 1] * b_ref[b:b + 1, :]
        hn_ref[b] = h
        ycol = jnp.sum(h * c_ref[b:b + 1, :], axis=1, keepdims=True)
        acc = jnp.where(lane == b, ycol, acc)
    y_t_ref[...] = acc


def _ssd_step_state(h_all, layer, xdt_t, dec_t, bm, cm, n_heads):
    m = bm.shape[0]
    hpg = n_heads // SSM_GROUPS
    return pl.pallas_call(
        _ssd_step_state_body,
        grid=(n_heads,),
        in_specs=[pl.BlockSpec((m, None, SSM_HEAD_DIM, D_STATE), lambda j: (layer, j, 0, 0)),
                  pl.BlockSpec((SSM_HEAD_DIM, LANE), lambda j: (j, 0)),
                  pl.BlockSpec((SSM_HEAD_DIM, LANE), lambda j: (j, 0)),
                  pl.BlockSpec((m, D_STATE), lambda j: (0, j // hpg)),
                  pl.BlockSpec((m, D_STATE), lambda j: (0, j // hpg))],
        out_specs=[pl.BlockSpec((m, None, SSM_HEAD_DIM, D_STATE), lambda j: (0, j, 0, 0)),
                   pl.BlockSpec((SSM_HEAD_DIM, LANE), lambda j: (j, 0))],
        out_shape=[jax.ShapeDtypeStruct((m, n_heads, SSM_HEAD_DIM, D_STATE), F32),
                   jax.ShapeDtypeStruct((n_heads * SSM_HEAD_DIM, LANE), F32)],
        compiler_params=_cparams(("parallel",)),
        name="ssd_step_state",
    )(h_all, xdt_t, dec_t, bm, cm)


def _ssd_step_post_body(y_t_ref, xs_ref, z_ref, dx_ref, nw_ref, o_ref):
    m, d_inner = xs_ref.shape
    gw = d_inner // SSM_GROUPS
    y = y_t_ref[...].T[:m, :] + xs_ref[...] * dx_ref[...]
    gated = y * _silu(z_ref[...])
    for g in range(SSM_GROUPS):
        seg = gated[:, g * gw:(g + 1) * gw]
        o_ref[:, g * gw:(g + 1) * gw] = _rms(seg, nw_ref[:, g * gw:(g + 1) * gw]).astype(BF16)


def _ssd_step_post(y_t, xs, proj, dx, nw):
    m, d_inner = xs.shape
    return pl.pallas_call(
        _ssd_step_post_body,
        grid=(1,),
        in_specs=[_full(y_t.shape), _full(xs.shape), pl.BlockSpec((m, d_inner), lambda i: (0, 0)),
                  _full(dx.shape), _full(nw.shape)],
        out_specs=_full((m, d_inner)),
        out_shape=jax.ShapeDtypeStruct((m, d_inner), BF16),
        compiler_params=_cparams(("arbitrary",)),
        name="ssd_step_post",
    )(y_t, xs, proj, dx, nw)


def _band_attn_body(q_ref, kc_ref, kp_ref, vc_ref, vp_ref, o_ref, l_ref):
    blk = pl.program_id(2)
    span = q_ref.shape[0]
    q = (q_ref[...] * (ATT_HEAD_DIM ** -0.5)).astype(BF16)
    kk = jnp.concatenate([kp_ref[...], kc_ref[...]], axis=0).astype(BF16)
    vv = jnp.concatenate([vp_ref[...], vc_ref[...]], axis=0).astype(BF16)
    qi = lax.broadcasted_iota(jnp.int32, (span, 2 * span), 0)
    ki = lax.broadcasted_iota(jnp.int32, (span, 2 * span), 1)
    valid = (ki >= qi) & (ki <= qi + span) & ((blk > 0) | (ki >= span))
    for h in range(ATT_HPG):
        sl = slice(h * ATT_HEAD_DIM, (h + 1) * ATT_HEAD_DIM)
        s = jnp.where(valid, _dot_nt(q[:, sl], kk[:, sl]), -jnp.inf)
        mx = jnp.max(s, axis=-1, keepdims=True)
        p = jnp.exp(s - mx)
        den = jnp.sum(p, axis=-1, keepdims=True)
        o_ref[:, sl] = _dot(p.astype(BF16), vv[:, sl]) / den
        l_ref[:, sl] = jnp.broadcast_to(mx + jnp.log(den), (span, ATT_HEAD_DIM))


def _band_attn(qkv, n_seq, gi, win, dil):
    m, w3 = qkv.shape
    t_len = m // n_seq
    span = win // dil
    sub = t_len // dil
    nb = sub // span
    ncol = w3 // ATT_GW
    ng = len(ATT_GROUPS)
    view = qkv.reshape(n_seq, sub, dil * w3)

    def spec(sec, prev):
        def imap(b, r, k):
            kk = jnp.maximum(k - 1, 0) if prev else k
            return (b, kk, r * ncol + sec * ng + gi)
        return pl.BlockSpec((None, span, ATT_GW), imap)

    out_spec = pl.BlockSpec((None, span, ATT_GW), lambda b, r, k: (b, k, r))
    out_sd = jax.ShapeDtypeStruct((n_seq, sub, dil * ATT_GW), F32)
    o, l = pl.pallas_call(
        _band_attn_body,
        grid=(n_seq, dil, nb),
        in_specs=[spec(0, False), spec(1, False), spec(1, True), spec(2, False), spec(2, True)],
        out_specs=[out_spec, out_spec],
        out_shape=[out_sd, out_sd],
        compiler_params=_cparams(("parallel", "parallel", "arbitrary")),
        name="band_attn_w%d" % win,
    )(view, view, view, view, view)
    return o.reshape(m, ATT_GW), l.reshape(m, ATT_GW)


SHIFT_ROWS = 128
ROW_TILES = 2 * ATT_GW // LANE


def _attn_sample_body(*refs, dils, n_prev):
    ng = len(dils)
    qkv_ref = refs[0]
    bufs = refs[1:1 + ng]
    outs = refs[1 + ng + n_prev:]
    o_refs, l_refs, c_refs = outs[0:ng], outs[ng:2 * ng], outs[2 * ng:3 * ng]
    scale = ATT_HEAD_DIM ** -0.5
    for gi in range(ng):
        buf, o_ref, l_ref, c_ref = bufs[gi], o_refs[gi], l_refs[gi], c_refs[gi]
        nrow = buf.shape[0]
        span = nrow // (ROW_TILES * dils[gi])
        q = qkv_ref[:, gi * ATT_GW:(gi + 1) * ATT_GW]
        kn = qkv_ref[:, (ng + gi) * ATT_GW:(ng + gi + 1) * ATT_GW]
        vn = qkv_ref[:, (2 * ng + gi) * ATT_GW:(2 * ng + gi + 1) * ATT_GW]
        new_tiles = [kn[:, :LANE], kn[:, LANE:], vn[:, :LANE], vn[:, LANE:]]
        tiles = [buf[pl.ds(c, span, stride=ROW_TILES * dils[gi]), :] for c in range(ROW_TILES)]
        pn = q * kn
        for h in range(ATT_HPG):
            sl = slice(h * ATT_HEAD_DIM, (h + 1) * ATT_HEAD_DIM)
            tl = slice((h % 2) * ATT_HEAD_DIM, (h % 2 + 1) * ATT_HEAD_DIM)
            kh = tiles[h // 2][:, tl]
            vh = tiles[2 + h // 2][:, tl]
            s = jnp.sum(kh * q[:, sl], axis=1, keepdims=True) * scale
            sn = jnp.sum(pn[:, sl], axis=1, keepdims=True) * scale
            mx = jnp.maximum(jnp.max(s, axis=0, keepdims=True), sn)
            p = jnp.exp(s - mx)
            p_new = jnp.exp(sn - mx)
            den = jnp.sum(p, axis=0, keepdims=True) + p_new
            o_ref[:, sl] = (jnp.sum(p * vh, axis=0, keepdims=True) + p_new * vn[:, sl]) / den
            l_ref[:, sl] = jnp.broadcast_to(mx + jnp.log(den), (1, ATT_HEAD_DIM))
        for r0 in range(0, nrow - SHIFT_ROWS, SHIFT_ROWS):
            c_ref[r0:r0 + SHIFT_ROWS, :] = buf[r0 + ROW_TILES:r0 + ROW_TILES + SHIFT_ROWS, :]
        r0 = nrow - SHIFT_ROWS
        c_ref[r0:nrow - SUBLANE, :] = buf[r0 + ROW_TILES:nrow - SUBLANE + ROW_TILES, :]
        tail = pltpu.roll(buf[nrow - SUBLANE:nrow, :], SUBLANE - ROW_TILES, 0)
        rows = lax.broadcasted_iota(jnp.int32, tail.shape, 0)
        for c in range(ROW_TILES):
            tail = jnp.where(rows == SUBLANE - ROW_TILES + c, new_tiles[c], tail)
        c_ref[nrow - SUBLANE:nrow, :] = tail


def _attn_sample(qkv, cache_views, prev, layer):
    m, w3 = qkv.shape
    ng = len(ATT_GROUPS)
    dils = tuple(dil for _, dil in ATT_GROUPS)
    for v, (win, _) in zip(cache_views, ATT_GROUPS):
        assert v.shape[1] == win * ROW_TILES and v.shape[1] % SHIFT_ROWS == 0
    c_specs = [pl.BlockSpec((None, v.shape[1], LANE), lambda b: (layer * m + b, 0, 0)) for v in cache_views]
    o_sd = jax.ShapeDtypeStruct((m, 1, ATT_GW), F32)
    o_spec = pl.BlockSpec((None, 1, ATT_GW), lambda b: (b, 0, 0))
    n_in = 1 + ng
    res = pl.pallas_call(
        functools.partial(_attn_sample_body, dils=dils, n_prev=len(prev)),
        grid=(m,),
        in_specs=[pl.BlockSpec((None, 1, w3), lambda b: (b, 0, 0))] + c_specs
                 + [pl.BlockSpec(memory_space=pl.ANY)] * len(prev),
        out_specs=[o_spec] * (2 * ng) + c_specs,
        out_shape=[o_sd] * (2 * ng) + [jax.ShapeDtypeStruct(v.shape, v.dtype) for v in cache_views],
        input_output_aliases={n_in + k: 2 * ng + k for k in range(len(prev))},
        compiler_params=_cparams(("parallel",)),
        name="attn_sample",
    )(qkv.reshape(m, 1, w3), *cache_views, *prev)
    os_ = [r.reshape(m, ATT_GW) for r in res[0:ng]]
    ls_ = [r.reshape(m, ATT_GW) for r in res[ng:2 * ng]]
    return os_, ls_, list(res[2 * ng:])


def _prep_weights(norms, ssm_w_in, ssm_conv_b, ssm_dt_bias, ssm_a_log, ssm_d, ssm_norm_w, ssm_w_out, att_w_qkv,
                  att_w_o, mem_norm, xa_w_q, xa_w_kv, xa_w_o, ffn_w_gu, ffn_conv_b, ffn_w_down):
    n_ssm, d_model, in_dim = ssm_w_in.shape
    n_heads = ssm_dt_bias.shape[1]
    d_inner = n_heads * SSM_HEAD_DIM
    d_ff = ffn_w_down.shape[1]
    pad_heads = LANE - n_heads
    w = {}
    w['n_heads'], w['d_inner'] = n_heads, d_inner
    w['norms'] = norms[:, :, None, :]
    w['ssm_w_in'] = jnp.pad(ssm_w_in, ((0, 0), (0, 0), (0, pad_heads))).astype(BF16)
    w['ssm_conv_b'] = ssm_conv_b[:, None, :]
    w['ssm_dt_bias'] = jnp.pad(ssm_dt_bias, ((0, 0), (0, pad_heads)))[:, None, :]
    w['ssm_a_log'] = jnp.pad(ssm_a_log, ((0, 0), (0, pad_heads)))[:, None, :]
    w['ssm_dx'] = jnp.repeat(ssm_d, SSM_HEAD_DIM, axis=1)[:, None, :]
    w['ssm_norm_w'] = ssm_norm_w[:, None, :]
    w['ssm_w_out'] = ssm_w_out.astype(BF16)
    w['att_w_qkv'] = att_w_qkv.astype(BF16)
    w['att_w_o'] = att_w_o.astype(BF16)
    w['mem_norm'] = mem_norm[:, None, :]
    w['xa_w_q'] = xa_w_q.astype(BF16)
    w['xa_w_kv'] = xa_w_kv.astype(BF16)
    w['xa_w_o'] = xa_w_o.astype(BF16)
    w['ffn_w_g'] = ffn_w_gu[:, :, :d_ff].astype(BF16)
    w['ffn_w_u'] = ffn_w_gu[:, :, d_ff:].astype(BF16)
    w['ffn_conv_b'] = ffn_conv_b[:, None, :]
    w['ffn_w_down'] = ffn_w_down.astype(BF16)
    e = np.zeros((LANE, d_inner), np.float32)
    for h in range(n_heads):
        e[h, h * SSM_HEAD_DIM:(h + 1) * SSM_HEAD_DIM] = 1.0
    w['expand'] = jnp.asarray(e, BF16)
    w['tril'] = jnp.asarray(np.tril(np.ones((SSM_CHUNK, SSM_CHUNK), np.float32)), BF16)
    return w


def _prompt_trunk(x3, mem3, w, ssm_conv_w, ffn_conv_w):
    n, t_len, d = x3.shape
    assert t_len % SSM_CHUNK == 0 and all(t_len % win == 0 for win, _ in ATT_GROUPS)
    depth = w['norms'].shape[0]
    n_heads, d_inner = w['n_heads'], w['d_inner']
    conv_dim = ssm_conv_w.shape[2]
    d_ff = ffn_conv_w.shape[2]
    tm = 256
    x = x3.reshape(n * t_len, d)
    mem = mem3.reshape(-1, d)
    kv_all = _mem_kv(mem, w['mem_norm'], w['xa_w_kv'], tm)
    tabs = _rope_tables(jnp.arange(t_len, dtype=jnp.int32))
    zero_hist = jnp.zeros((n, SSM_CONV - 1, conv_dim), F32)
    zero_h = jnp.zeros((n, n_heads * SSM_HEAD_DIM, D_STATE), F32)
    zero_fh = jnp.zeros((n, FFN_CONV - 1, d_ff), F32)
    new = {'ssm': [], 'ssm_conv': [], 'swa': [[], [], []], 'ffn_conv': []}
    for i in range(depth):
        j = i // 2
        g = w['norms'][i]
        if i % 2 == 0:
            proj = _norm_proj(x, g[0], w['ssm_w_in'][j], tm)
            y, h_last, c_last = _ssd_prompt(proj, ssm_conv_w[j], w['ssm_conv_b'][j], w['ssm_dt_bias'][j],
                                            w['ssm_a_log'][j], w['ssm_dx'][j], w['ssm_norm_w'][j], w['expand'],
                                            w['tril'], zero_hist, zero_h, n, d_inner, n_heads)
            x = _proj_res(y, w['ssm_w_out'][j], x, g[1], tm)
            new['ssm'].append(h_last.reshape(n, n_heads, SSM_HEAD_DIM, D_STATE))
            new['ssm_conv'].append(c_last)
        else:
            qkv = _norm_qkv_rope(x, g[0], w['att_w_qkv'][j], tabs, tm, t_len // tm)
            os_, ls_ = [], []
            ng = len(ATT_GROUPS)
            qkv3 = qkv.reshape(n, t_len, 3, ng, ATT_HPG, ATT_HEAD_DIM)
            for gi, (win, dil) in enumerate(ATT_GROUPS):
                o, l = _band_attn(qkv, n, gi, win, dil)
                os_.append(o)
                ls_.append(l)
                keep = min(win, t_len)
                new['swa'][gi].append(jnp.stack([qkv3[:, t_len - keep:, 1, gi], qkv3[:, t_len - keep:, 2, gi]],
                                                axis=2))
            x = _attn_out(os_, ls_, w['att_w_o'][j], x, g[1], tm)
        x = _xattn_prompt(x, g[2], g[3], w['xa_w_q'][i], kv_all[i], w['xa_w_o'][i], n, tm)
        x, f_hist = _ffn_prompt(x, g[4], g[5], w['ffn_w_g'][i], w['ffn_w_u'][i], ffn_conv_w[i],
                                w['ffn_conv_b'][i], w['ffn_w_down'][i], zero_fh, n, tm)
        new['ffn_conv'].append(f_hist)
    n_mem = mem3.shape[1]
    return (x.reshape(n, t_len, d), jnp.stack(new['ssm']), jnp.stack(new['ssm_conv']),
            [jnp.stack(s) for s in new['swa']],
            kv_all.reshape(depth, n, n_mem, 2, MEM_HEADS, d // MEM_HEADS), jnp.stack(new['ffn_conv']))


def _sample_trunk(x3, w, ssm_conv_w, ffn_conv_w, state_ssm, state_ssm_conv, caches, cache_mem_kv, state_ffn_conv):
    m, t_len, d = x3.shape
    assert t_len == 1
    depth = w['norms'].shape[0]
    n_heads, d_inner = w['n_heads'], w['d_inner']
    n_ssm = state_ssm.shape[0]
    x = x3.reshape(m, d)
    tabs = _rope_tables(jnp.full((m,), PAST_LEN, jnp.int32))
    h_all = state_ssm.reshape(n_ssm * m, n_heads, SSM_HEAD_DIM, D_STATE)
    kv_all = cache_mem_kv.reshape(depth * m, cache_mem_kv.shape[2], 2 * d)
    cache_views = [c.reshape(c.shape[0] * m, c.shape[2] * ROW_TILES, LANE) for c in caches]
    new_caches = []
    new = {'ssm': [], 'ssm_conv': [], 'ffn_conv': []}
    for i in range(depth):
        j = i // 2
        g = w['norms'][i]
        if i % 2 == 0:
            proj = _norm_proj(x, g[0], w['ssm_w_in'][j], m)
            hist_t = jnp.swapaxes(state_ssm_conv[j], 0, 1)
            xs, bm, cm, xdt_t, dec_t, c_last_t = _ssd_step_pre(proj, hist_t, ssm_conv_w[j], w['ssm_conv_b'][j],
                                                                w['ssm_dt_bias'][j], w['ssm_a_log'][j],
                                                                w['expand'], d_inner)
            h_new, y_t = _ssd_step_state(h_all, j, xdt_t, dec_t, bm, cm, n_heads)
            y = _ssd_step_post(y_t, xs, proj, w['ssm_dx'][j], w['ssm_norm_w'][j])
            x = _proj_res(y, w['ssm_w_out'][j], x, g[1], m)
            new['ssm'].append(h_new)
            new['ssm_conv'].append(jnp.swapaxes(c_last_t, 0, 1))
        else:
            qkv = _norm_qkv_rope(x, g[0], w['att_w_qkv'][j], tabs, m, 1)
            os_, ls_, new_caches = _attn_sample(qkv, cache_views, new_caches, j)
            x = _attn_out(os_, ls_, w['att_w_o'][j], x, g[1], m)
        x = _xattn_sample(x, g[2], g[3], w['xa_w_q'][i], kv_all, i, w['xa_w_o'][i])
        hist_t = jnp.swapaxes(state_ffn_conv[i], 0, 1)
        x, f_hist_t = _ffn_sample(x, g[4], g[5], w['ffn_w_g'][i], w['ffn_w_u'][i], ffn_conv_w[i],
                                  w['ffn_conv_b'][i], w['ffn_w_down'][i], hist_t, 256)
        new['ffn_conv'].append(jnp.swapaxes(f_hist_t, 0, 1))
    new_caches = [nc.reshape(c.shape) for nc, c in zip(new_caches, caches)]
    return (x.reshape(m, 1, d), jnp.stack(new['ssm']), jnp.stack(new['ssm_conv']), new_caches,
            jnp.stack(new['ffn_conv']))


def kernel(x_prompt, x_sample, mem_prompt, state_ssm, state_ssm_conv, cache_swa_kv_w128, cache_swa_kv_w512,
           cache_swa_kv_w2048, cache_mem_kv, state_ffn_conv, norms, ssm_w_in, ssm_conv_w, ssm_conv_b,
           ssm_dt_bias, ssm_a_log, ssm_d, ssm_norm_w, ssm_w_out, att_w_qkv, att_w_o, mem_norm, xa_w_q,
           xa_w_kv, xa_w_o, ffn_w_gu, ffn_conv_w, ffn_conv_b, ffn_w_down):
    w = _prep_weights(norms, ssm_w_in, ssm_conv_b, ssm_dt_bias, ssm_a_log, ssm_d, ssm_norm_w, ssm_w_out,
                      att_w_qkv, att_w_o, mem_norm, xa_w_q, xa_w_kv, xa_w_o, ffn_w_gu, ffn_conv_b, ffn_w_down)
    caches = [cache_swa_kv_w128, cache_swa_kv_w512, cache_swa_kv_w2048]
    yp, p_ssm, p_conv, p_swa, p_mem, p_ffn = _prompt_trunk(x_prompt, mem_prompt, w, ssm_conv_w, ffn_conv_w)
    ys, s_ssm, s_conv, s_swa, s_ffn = _sample_trunk(x_sample, w, ssm_conv_w, ffn_conv_w, state_ssm,
                                                    state_ssm_conv, caches, cache_mem_kv, state_ffn_conv)
    return (yp, ys, p_ssm, p_conv, p_swa[0], p_swa[1], p_swa[2], p_mem, p_ffn,
            s_ssm, s_conv, s_swa[0], s_swa[1], s_swa[2], s_ffn)
```

```python
import functools
import math

import numpy as np
import jax
import jax.numpy as jnp
from jax import lax
from jax.experimental import pallas as pl
from jax.experimental.pallas import tpu as pltpu

F32 = jnp.float32
BF16 = jnp.bfloat16

EPS = 1e-6
PAST_LEN = 8192
SSM_HEAD_DIM = 64
SSM_GROUPS = 4
D_STATE = 128
SSM_CONV = 4
SSM_CHUNK = 128
ATT_GROUPS = ((128, 1), (512, 4), (2048, 16))
ATT_HPG = 4
ATT_HEAD_DIM = 64
ATT_GW = ATT_HPG * ATT_HEAD_DIM
ROT_DIM = ATT_HEAD_DIM // 4
ROPE_THETA = 500000.0
MEM_HEADS = 4
FFN_CONV = 3

LANE = 128
SUBLANE = 8
VMEM_LIMIT = 56 * 1024 * 1024


def _cparams(sem):
    return pltpu.CompilerParams(dimension_semantics=sem, vmem_limit_bytes=VMEM_LIMIT)


def _rms(x, g):
    return x * lax.rsqrt(jnp.mean(x * x, axis=-1, keepdims=True) + EPS) * g


def _silu(x):
    return x / (1.0 + jnp.exp(-x))


def _softplus(x):
    return jnp.maximum(x, 0.0) + jnp.log(1.0 + jnp.exp(-jnp.abs(x)))


def _dot(a, b):
    return jnp.dot(a, b, preferred_element_type=F32)


def _dot_nt(a, b):
    return lax.dot_general(a, b, (((1,), (1,)), ((), ())), preferred_element_type=F32)


def _split3(v):
    hi = v.astype(BF16)
    r = v - hi.astype(F32)
    mid = r.astype(BF16)
    lo = (r - mid.astype(F32)).astype(BF16)
    return hi, mid, lo


def _expand(v, e):
    hi, mid, lo = _split3(v)
    return _dot(hi, e) + _dot(mid, e) + _dot(lo, e)


def _full(shape):
    return pl.BlockSpec(shape, lambda *_: (0,) * len(shape))


def _norm_proj_body(x_ref, g_ref, w_ref, o_ref):
    u = _rms(x_ref[...], g_ref[...]).astype(BF16)
    o_ref[...] = _dot(u, w_ref[...])


def _norm_proj(x, g, w, tm):
    m, d = x.shape
    n = w.shape[1]
    return pl.pallas_call(
        _norm_proj_body,
        grid=(m // tm,),
        in_specs=[pl.BlockSpec((tm, d), lambda i: (i, 0)), _full((1, d)), _full((d, n))],
        out_specs=pl.BlockSpec((tm, n), lambda i: (i, 0)),
        out_shape=jax.ShapeDtypeStruct((m, n), F32),
        compiler_params=_cparams(("parallel",)),
        name="norm_proj",
    )(x, g, w)


def _norm_qkv_rope_body(x_ref, g_ref, w_ref, cos_ref, sa_ref, sb_ref, o_ref, *, n_rot):
    u = _rms(x_ref[...], g_ref[...]).astype(BF16)
    y = _dot(u, w_ref[...])
    cos, sa, sb = cos_ref[...], sa_ref[...], sb_ref[...]
    for c in range(n_rot // LANE):
        t = y[:, c * LANE:(c + 1) * LANE]
        o_ref[:, c * LANE:(c + 1) * LANE] = (t * cos + pltpu.roll(t, LANE - ROT_DIM // 2, 1) * sa
                                             + pltpu.roll(t, ROT_DIM // 2, 1) * sb)
    o_ref[:, n_rot:] = y[:, n_rot:]


def _norm_qkv_rope(x, g, w, tabs, tm, seq_blocks):
    m, d = x.shape
    n = w.shape[1]
    tab_spec = pl.BlockSpec((tm, LANE), lambda i: (i % seq_blocks, 0))
    return pl.pallas_call(
        functools.partial(_norm_qkv_rope_body, n_rot=2 * n // 3),
        grid=(m // tm,),
        in_specs=[pl.BlockSpec((tm, d), lambda i: (i, 0)), _full((1, d)), _full((d, n)),
                  tab_spec, tab_spec, tab_spec],
        out_specs=pl.BlockSpec((tm, n), lambda i: (i, 0)),
        out_shape=jax.ShapeDtypeStruct((m, n), F32),
        compiler_params=_cparams(("parallel",)),
        name="norm_qkv_rope",
    )(x, g, w, *tabs)


def _rope_chunk(t, cos, sa, sb):
    return t * cos + pltpu.roll(t, LANE - ROT_DIM // 2, 1) * sa + pltpu.roll(t, ROT_DIM // 2, 1) * sb


def _norm_qkv_prompt_body(*refs, n_rot, n_prev, nblks):
    x_ref, g_ref, w_ref, cos_ref, sa_ref, sb_ref = refs[:6]
    o_ref = refs[6 + n_prev]
    c_refs = refs[7 + n_prev:]
    ng = len(c_refs)
    t = pl.program_id(1)
    tb = pl.num_programs(1)
    tm = x_ref.shape[0]
    u = _rms(x_ref[...], g_ref[...]).astype(BF16)
    y = _dot(u, w_ref[...])
    cos, sa, sb = cos_ref[...], sa_ref[...], sb_ref[...]
    vals = []
    for c in range(y.shape[1] // LANE):
        v = y[:, c * LANE:(c + 1) * LANE]
        if c * LANE < n_rot:
            v = _rope_chunk(v, cos, sa, sb)
        o_ref[c] = v
        vals.append(v)
    per_sec = len(vals) // 3
    per_grp = ATT_GW // LANE
    for gi, c_ref in enumerate(c_refs):
        kw = c_ref.shape[2]
        nblk = nblks[gi]

        @pl.when(t >= tb - nblk)
        def _(gi=gi, c_ref=c_ref, kw=kw):
            for kv in range(2):
                for hp in range(per_grp):
                    v = vals[(1 + kv) * per_sec + gi * per_grp + hp]
                    c_ref[kv, hp * LANE:(hp + 1) * LANE, :] = v[tm - kw:, :].T


def _norm_qkv_prompt(x, g, w, tabs, prev, layer, n_layers, n_seq, tm):
    m, d = x.shape
    n = w.shape[1]
    t_len = m // n_seq
    tb = t_len // tm
    tab_spec = pl.BlockSpec((tm, LANE), lambda b, t: (t, 0))
    c_specs, c_shapes, nblks = [], [], []
    for gi, (win, _) in enumerate(ATT_GROUPS):
        keep = min(win, t_len)
        kw = min(keep, tm)
        nblk = keep // kw
        assert keep % kw == 0
        nblks.append(nblk)
        c_specs.append(pl.BlockSpec((None, 2, ATT_GW, kw),
                                    lambda b, t, nblk=nblk: (layer * n_seq + b, 0, 0, jnp.maximum(t - (tb - nblk), 0))))
        c_shapes.append(jax.ShapeDtypeStruct((n_layers * n_seq, 2, ATT_GW, keep), F32))
    res = pl.pallas_call(
        functools.partial(_norm_qkv_prompt_body, n_rot=2 * n // 3, n_prev=len(prev), nblks=tuple(nblks)),
        grid=(n_seq, tb),
        in_specs=[pl.BlockSpec((tm, d), lambda b, t: (b * tb + t, 0)), _full((1, d)), _full((d, n)),
                  tab_spec, tab_spec, tab_spec] + [pl.BlockSpec(memory_space=pl.ANY)] * len(prev),
        out_specs=[pl.BlockSpec((n // LANE, tm, LANE), lambda b, t: (0, b * tb + t, 0))] + c_specs,
        out_shape=[jax.ShapeDtypeStruct((n // LANE, m, LANE), F32)] + c_shapes,
        input_output_aliases={6 + k: 1 + k for k in range(len(prev))},
        compiler_params=_cparams(("parallel", "arbitrary")),
        name="norm_qkv_prompt",
    )(x, g, w, *tabs, *prev)
    return res[0], list(res[1:])


def _rope_tables(pos):
    half = ROT_DIM // 2
    inv = ROPE_THETA ** (-jnp.arange(half, dtype=F32) / half)
    ang = pos.astype(F32)[:, None] * inv[None, :]
    cos, sin = jnp.cos(ang), jnp.sin(ang)
    p = pos.shape[0]
    rest = ATT_HEAD_DIM - ROT_DIM
    c = jnp.concatenate([cos, cos, jnp.ones((p, rest), F32)], axis=1)
    sa = jnp.concatenate([-sin, jnp.zeros((p, half + rest), F32)], axis=1)
    sb = jnp.concatenate([jnp.zeros((p, half), F32), sin, jnp.zeros((p, rest), F32)], axis=1)
    rep = LANE // ATT_HEAD_DIM
    return tuple(jnp.tile(t, (1, rep)) for t in (c, sa, sb))


def _proj_res_body(y_ref, w_ref, x_ref, g_ref, o_ref):
    f = _dot(y_ref[...].astype(BF16), w_ref[...])
    o_ref[...] = x_ref[...] + _rms(f, g_ref[...])


def _proj_res(y, w, x, g, tm):
    m, k = y.shape
    d = w.shape[1]
    return pl.pallas_call(
        _proj_res_body,
        grid=(m // tm,),
        in_specs=[pl.BlockSpec((tm, k), lambda i: (i, 0)), _full((k, d)),
                  pl.BlockSpec((tm, d), lambda i: (i, 0)), _full((1, d))],
        out_specs=pl.BlockSpec((tm, d), lambda i: (i, 0)),
        out_shape=jax.ShapeDtypeStruct((m, d), F32),
        compiler_params=_cparams(("parallel",)),
        name="proj_res",
    )(y, w, x, g)


def _attn_out_body(o0, o1, o2, l0, l1, l2, w_ref, x_ref, g_ref, out_ref):
    def load(ref):
        if len(ref.shape) == 2:
            return ref[...]
        return jnp.concatenate([ref[c] for c in range(ref.shape[0])], axis=1)

    ls = [load(l0), load(l1), load(l2)]
    mx = jnp.maximum(jnp.maximum(ls[0], ls[1]), ls[2])
    es = [jnp.exp(l - mx) for l in ls]
    den = es[0] + es[1] + es[2]
    og = jnp.concatenate([(es[gi] / den * load(o_ref)).astype(BF16) for gi, o_ref in enumerate((o0, o1, o2))], axis=1)
    out_ref[...] = x_ref[...] + _rms(_dot(og, w_ref[...]), g_ref[...])


def _attn_out(os_, ls_, w, x, g, tm):
    m, d = x.shape
    if os_[0].ndim == 2:
        blk = pl.BlockSpec((tm, ATT_GW), lambda i: (i, 0))
    else:
        blk = pl.BlockSpec((ATT_GW // LANE, tm, LANE), lambda i: (0, i, 0))
    return pl.pallas_call(
        _attn_out_body,
        grid=(m // tm,),
        in_specs=[blk] * 6 + [_full(w.shape), pl.BlockSpec((tm, d), lambda i: (i, 0)), _full((1, d))],
        out_specs=pl.BlockSpec((tm, d), lambda i: (i, 0)),
        out_shape=jax.ShapeDtypeStruct((m, d), F32),
        compiler_params=_cparams(("parallel",)),
        name="attn_out",
    )(*os_, *ls_, w, x, g)


def _mem_kv_body(x_ref, g_ref, w_ref, o_ref, t_ref):
    tm = x_ref.shape[0]
    u = _rms(x_ref[...], g_ref[...]).astype(BF16)
    y = _dot(u, w_ref[...])
    o_ref[...] = y
    hd = y.shape[1] // (2 * MEM_HEADS)
    nch = hd // LANE
    tok_rows = 2 * nch * MEM_HEADS
    for kv in range(2):
        for h in range(MEM_HEADS):
            for c in range(nch):
                col = (kv * MEM_HEADS + h) * hd + c * LANE
                t_ref[pl.ds((kv * nch + c) * MEM_HEADS + h, tm, stride=tok_rows), :] = y[:, col:col + LANE]


def _mem_kv(mem, g, w, tm):
    m, d = mem.shape
    depth, _, n = w.shape
    tok_rows = n // LANE
    return pl.pallas_call(
        _mem_kv_body,
        grid=(depth, m // tm),
        in_specs=[pl.BlockSpec((tm, d), lambda l, i: (i, 0)),
                  pl.BlockSpec((None, 1, d), lambda l, i: (l, 0, 0)),
                  pl.BlockSpec((None, d, n), lambda l, i: (l, 0, 0))],
        out_specs=[pl.BlockSpec((None, tm, n), lambda l, i: (l, i, 0)),
                   pl.BlockSpec((None, tm * tok_rows, LANE), lambda l, i: (l, i, 0))],
        out_shape=[jax.ShapeDtypeStruct((depth, m, n), F32),
                   jax.ShapeDtypeStruct((depth, m * tok_rows, LANE), F32)],
        compiler_params=_cparams(("parallel", "parallel")),
        name="mem_kv",
    )(mem, g, w)


def _xattn_prompt_body(x_ref, gpre_ref, gpost_ref, wq_ref, kv_ref, wo_ref, o_ref, obuf):
    x = x_ref[...]
    d = x.shape[1]
    hd = d // MEM_HEADS
    u = _rms(x, gpre_ref[...]).astype(BF16)
    q = _dot(u, wq_ref[...]).astype(BF16)
    scale = hd ** -0.5
    for h in range(MEM_HEADS):
        kh = kv_ref[:, h * hd:(h + 1) * hd].astype(BF16)
        vh = kv_ref[:, d + h * hd:d + (h + 1) * hd].astype(BF16)
        s = _dot_nt(q[:, h * hd:(h + 1) * hd], kh) * scale
        mx = jnp.max(s, axis=-1, keepdims=True)
        p = jnp.exp(s - mx)
        den = jnp.sum(p, axis=-1, keepdims=True)
        obuf[:, h * hd:(h + 1) * hd] = (_dot(p.astype(BF16), vh) / den).astype(BF16)
    f = _dot(obuf[...], wo_ref[...])
    o_ref[...] = x + _rms(f, gpost_ref[...])


def _xattn_prompt(x, gpre, gpost, wq, kv, wo, n_seq, tm):
    m, d = x.shape
    tb = m // n_seq // tm
    n_mem = kv.shape[0] // n_seq
    return pl.pallas_call(
        _xattn_prompt_body,
        grid=(n_seq, tb),
        in_specs=[pl.BlockSpec((tm, d), lambda b, t: (b * tb + t, 0)), _full((1, d)), _full((1, d)),
                  _full((d, d)), pl.BlockSpec((n_mem, 2 * d), lambda b, t: (b, 0)), _full((d, d))],
        out_specs=pl.BlockSpec((tm, d), lambda b, t: (b * tb + t, 0)),
        out_shape=jax.ShapeDtypeStruct((m, d), F32),
        scratch_shapes=[pltpu.VMEM((tm, d), BF16)],
        compiler_params=_cparams(("parallel", "parallel")),
        name="xattn_prompt",
    )(x, gpre, gpost, wq, kv, wo)


def _xattn_sample_body(x_ref, gpre_ref, gpost_ref, wq_ref, kv_ref, wo_ref, o_ref, q_scr, o_scr):
    b = pl.program_id(0)
    d = x_ref.shape[1]
    hd = d // MEM_HEADS

    @pl.when(b == 0)
    def _():
        u = _rms(x_ref[...], gpre_ref[...]).astype(BF16)
        q_scr[...] = _dot(u, wq_ref[...]) * (hd ** -0.5)

    qb = q_scr[pl.ds(b, 1), :]
    nch = hd // LANE
    tok_rows = 2 * nch * MEM_HEADS
    n_mem = kv_ref.shape[0] // tok_rows
    o_parts = []
    for h in range(MEM_HEADS):
        s = None
        for c in range(nch):
            kc = kv_ref[pl.ds(c * MEM_HEADS + h, n_mem, stride=tok_rows), :]
            part = jnp.sum(kc * qb[:, h * hd + c * LANE:h * hd + (c + 1) * LANE], axis=1, keepdims=True)
            s = part if s is None else s + part
        mx = jnp.max(s, axis=0, keepdims=True)
        p = jnp.exp(s - mx)
        den = jnp.sum(p, axis=0, keepdims=True)
        for c in range(nch):
            vc = kv_ref[pl.ds((nch + c) * MEM_HEADS + h, n_mem, stride=tok_rows), :]
            o_parts.append(jnp.sum(p * vc, axis=0, keepdims=True) / den)
    o_scr[pl.ds(b, 1), :] = jnp.concatenate(o_parts, axis=1)

    @pl.when(b == pl.num_programs(0) - 1)
    def _():
        f = _dot(o_scr[...].astype(BF16), wo_ref[...])
        o_ref[...] = x_ref[...] + _rms(f, gpost_ref[...])


def _xattn_sample(x, gpre, gpost, wq, kv_all, layer, wo):
    m, d = x.shape
    return pl.pallas_call(
        _xattn_sample_body,
        grid=(m,),
        in_specs=[_full((m, d)), _full((1, d)), _full((1, d)), _full((d, d)),
                  pl.BlockSpec((None, kv_all.shape[1], LANE), lambda b: (layer * m + b, 0, 0)), _full((d, d))],
        out_specs=_full((m, d)),
        out_shape=jax.ShapeDtypeStruct((m, d), F32),
        scratch_shapes=[pltpu.VMEM((m, d), F32), pltpu.VMEM((m, d), F32)],
        compiler_params=_cparams(("arbitrary",)),
        name="xattn_sample",
    )(x, gpre, gpost, wq, kv_all, wo)


def _ffn_prompt_body(x_ref, gpre_ref, gpost_ref, wg_ref, wu_ref, cw_ref, cb_ref, wd_ref, hist_ref,
                     o_ref, nh_ref, gbuf):
    t = pl.program_id(1)
    tm = x_ref.shape[0]
    k = FFN_CONV - 1
    base = SUBLANE - k

    @pl.when(t == 0)
    def _():
        gbuf[base:SUBLANE, :] = hist_ref[...]

    x = x_ref[...]
    u = _rms(x, gpre_ref[...]).astype(BF16)
    gbuf[SUBLANE:SUBLANE + tm, :] = _dot(u, wg_ref[...])
    up = _dot(u, wu_ref[...])
    gc = gbuf[base:base + tm, :] * cw_ref[0:1, :] + cb_ref[...]
    for j in range(1, FFN_CONV):
        gc = gc + gbuf[base + j:base + j + tm, :] * cw_ref[j:j + 1, :]
    hmid = (_silu(gc) * up).astype(BF16)
    f = _dot(hmid, wd_ref[...])
    o_ref[...] = x + _rms(f, gpost_ref[...])
    last = gbuf[tm + base:tm + SUBLANE, :]
    nh_ref[...] = last
    gbuf[base:SUBLANE, :] = last


def _ffn_prompt(x, gpre, gpost, wg, wu, cw, cb, wd, hist, n_seq, tm):
    m, d = x.shape
    f = wg.shape[1]
    tb = m // n_seq // tm
    k = FFN_CONV - 1
    return pl.pallas_call(
        _ffn_prompt_body,
        grid=(n_seq, tb),
        in_specs=[pl.BlockSpec((tm, d), lambda b, t: (b * tb + t, 0)), _full((1, d)), _full((1, d)),
                  _full((d, f)), _full((d, f)), _full((FFN_CONV, f)), _full((1, f)), _full((f, d)),
                  pl.BlockSpec((None, k, f), lambda b, t: (b, 0, 0))],
        out_specs=[pl.BlockSpec((tm, d), lambda b, t: (b * tb + t, 0)),
                   pl.BlockSpec((None, k, f), lambda b, t: (b, 0, 0))],
        out_shape=[jax.ShapeDtypeStruct((m, d), F32), jax.ShapeDtypeStruct((n_seq, k, f), F32)],
        scratch_shapes=[pltpu.VMEM((tm + SUBLANE, f), F32)],
        compiler_params=_cparams(("parallel", "arbitrary")),
        name="ffn_prompt",
    )(x, gpre, gpost, wg, wu, cw, cb, wd, hist)


def _ffn_sample_body(x_ref, gpre_ref, gpost_ref, wg_ref, wu_ref, cw_ref, cb_ref, wd_ref, hist_ref,
                     o_ref, nh_ref, u_scr, acc):
    j = pl.program_id(0)

    @pl.when(j == 0)
    def _():
        u_scr[...] = _rms(x_ref[...], gpre_ref[...]).astype(BF16)
        acc[...] = jnp.zeros_like(acc)

    u = u_scr[...]
    gate = _dot(u, wg_ref[...])
    up = _dot(u, wu_ref[...])
    gc = hist_ref[0] * cw_ref[0:1, :] + cb_ref[...]
    for k in range(1, FFN_CONV - 1):
        gc = gc + hist_ref[k] * cw_ref[k:k + 1, :]
        nh_ref[k - 1] = hist_ref[k]
    gc = gc + gate * cw_ref[FFN_CONV - 1:FFN_CONV, :]
    nh_ref[FFN_CONV - 2] = gate
    hmid = (_silu(gc) * up).astype(BF16)
    acc[...] += _dot(hmid, wd_ref[...])

    @pl.when(j == pl.num_programs(0) - 1)
    def _():
        o_ref[...] = x_ref[...] + _rms(acc[...], gpost_ref[...])


def _ffn_sample(x, gpre, gpost, wg, wu, cw, cb, wd, hist_t, tn):
    m, d = x.shape
    f = wg.shape[1]
    k = FFN_CONV - 1
    return pl.pallas_call(
        _ffn_sample_body,
        grid=(f // tn,),
        in_specs=[_full((m, d)), _full((1, d)), _full((1, d)),
                  pl.BlockSpec((d, tn), lambda j: (0, j)), pl.BlockSpec((d, tn), lambda j: (0, j)),
                  pl.BlockSpec((FFN_CONV, tn), lambda j: (0, j)), pl.BlockSpec((1, tn), lambda j: (0, j)),
                  pl.BlockSpec((tn, d), lambda j: (j, 0)),
                  pl.BlockSpec((k, m, tn), lambda j: (0, 0, j))],
        out_specs=[_full((m, d)), pl.BlockSpec((k, m, tn), lambda j: (0, 0, j))],
        out_shape=[jax.ShapeDtypeStruct((m, d), F32), jax.ShapeDtypeStruct((k, m, f), F32)],
        scratch_shapes=[pltpu.VMEM((m, d), BF16), pltpu.VMEM((m, d), F32)],
        compiler_params=_cparams(("arbitrary",)),
        name="ffn_sample",
    )(x, gpre, gpost, wg, wu, cw, cb, wd, hist_t)


def _ssd_prompt_body(proj_ref, cw_ref, cb_ref, dtb_ref, alog_ref, dx_ref, nw_ref, e_ref, tril_ref,
                     hist_ref, h0_ref, *rest, d_inner, n_heads):
    y_ref, hlast_ref, clast_ref, cbuf, state, ybuf = rest[-6:]
    c = pl.program_id(1)
    L = SSM_CHUNK
    k = SSM_CONV - 1
    base = SUBLANE - k
    gn = SSM_GROUPS * D_STATE
    conv_dim = d_inner + 2 * gn
    hpg = n_heads // SSM_GROUPS
    gw = d_inner // SSM_GROUPS

    @pl.when(c == 0)
    def _():
        cbuf[base:SUBLANE, :] = hist_ref[...]
        state[...] = h0_ref[...]

    cbuf[SUBLANE:SUBLANE + L, :] = proj_ref[:, d_inner:d_inner + conv_dim]
    xc = cbuf[base:base + L, :] * cw_ref[0:1, :] + cb_ref[...]
    for j in range(1, SSM_CONV):
        xc = xc + cbuf[base + j:base + j + L, :] * cw_ref[j:j + 1, :]
    xc = _silu(xc)
    last = cbuf[L + base:L + SUBLANE, :]
    clast_ref[...] = last
    cbuf[base:SUBLANE, :] = last

    xs = xc[:, :d_inner]
    e = e_ref[...]
    dt = _softplus(proj_ref[:, d_inner + conv_dim:] + dtb_ref[...])
    a = dt * (-jnp.exp(alog_ref[...]))
    tril = tril_ref[...]
    a_hi, a_mid, a_lo = _split3(a)
    acum = _dot(tril, a_hi) + _dot(tril, a_mid) + _dot(tril, a_lo)
    acum_t = acum.T
    a_last = acum[L - 1:L, :]
    xdt = xs * _expand(dt, e)
    d_acc = _expand(jnp.exp(acum), e)
    xde = xdt * _expand(jnp.exp(a_last - acum), e)
    row = lax.broadcasted_iota(jnp.int32, (L, L), 0)
    col = lax.broadcasted_iota(jnp.int32, (L, L), 1)
    causal = row >= col

    for g in range(SSM_GROUPS):
        bg = xc[:, d_inner + g * D_STATE:d_inner + (g + 1) * D_STATE].astype(BF16)
        cg = xc[:, d_inner + gn + g * D_STATE:d_inner + gn + (g + 1) * D_STATE].astype(BF16)
        cb = _dot_nt(cg, bg)
        hg = state[g * gw:(g + 1) * gw, :]
        y_inter = _dot_nt(cg, hg.astype(BF16)) * d_acc[:, g * gw:(g + 1) * gw]
        for j in range(hpg):
            hd = g * hpg + j
            seg = acum[:, hd:hd + 1] - acum_t[hd:hd + 1, :]
            w = cb * jnp.exp(jnp.where(causal, seg, -jnp.inf))
            sl = slice(hd * SSM_HEAD_DIM, (hd + 1) * SSM_HEAD_DIM)
            ybuf[:, sl] = _dot(w.astype(BF16), xdt[:, sl].astype(BF16))
        ybuf[:, g * gw:(g + 1) * gw] += y_inter
        s_g = _dot(xde[:, g * gw:(g + 1) * gw].T.astype(BF16), bg)
        for j in range(hpg):
            hd = g * hpg + j
            rs = slice(hd * SSM_HEAD_DIM, (hd + 1) * SSM_HEAD_DIM)
            cd = jnp.exp(acum_t[hd:hd + 1, L - 1:L])
            state[rs, :] = state[rs, :] * cd + s_g[j * SSM_HEAD_DIM:(j + 1) * SSM_HEAD_DIM, :]

    y = ybuf[...] + xs * dx_ref[...]
    gated = y * _silu(proj_ref[:, :d_inner])
    for g in range(SSM_GROUPS):
        seg = gated[:, g * gw:(g + 1) * gw]
        y_ref[:, g * gw:(g + 1) * gw] = _rms(seg, nw_ref[:, g * gw:(g + 1) * gw]).astype(BF16)

    @pl.when(c == pl.num_programs(1) - 1)
    def _():
        hlast_ref[...] = state[...]


def _ssd_prompt(proj, cw, cb, dtb, alog, dx, nw, e, tril, hist, h0, prev, layer, n_layers, n_seq, d_inner, n_heads):
    m, pw = proj.shape
    L = SSM_CHUNK
    nc = m // n_seq // L
    conv_dim = cw.shape[1]
    k = SSM_CONV - 1
    rows = n_heads * SSM_HEAD_DIM
    body = functools.partial(_ssd_prompt_body, d_inner=d_inner, n_heads=n_heads)
    return pl.pallas_call(
        body,
        grid=(n_seq, nc),
        in_specs=[pl.BlockSpec((L, pw), lambda b, c: (b * nc + c, 0)),
                  _full(cw.shape), _full(cb.shape), _full(dtb.shape), _full(alog.shape), _full(dx.shape),
                  _full(nw.shape), _full(e.shape), _full(tril.shape),
                  pl.BlockSpec((None, k, conv_dim), lambda b, c: (b, 0, 0)),
                  pl.BlockSpec((None, rows, D_STATE), lambda b, c: (b, 0, 0))]
                 + [pl.BlockSpec(memory_space=pl.ANY)] * len(prev),
        out_specs=[pl.BlockSpec((L, d_inner), lambda b, c: (b * nc + c, 0)),
                   pl.BlockSpec((None, rows, D_STATE), lambda b, c: (layer * n_seq + b, 0, 0)),
                   pl.BlockSpec((None, k, conv_dim), lambda b, c: (b, 0, 0))],
        out_shape=[jax.ShapeDtypeStruct((m, d_inner), BF16),
                   jax.ShapeDtypeStruct((n_layers * n_seq, rows, D_STATE), F32),
                   jax.ShapeDtypeStruct((n_seq, k, conv_dim), F32)],
        input_output_aliases={11: 1} if prev else {},
        scratch_shapes=[pltpu.VMEM((L + SUBLANE, conv_dim), F32), pltpu.VMEM((rows, D_STATE), F32),
                        pltpu.VMEM((L, d_inner), F32)],
        compiler_params=_cparams(("parallel", "arbitrary")),
        name="ssd_prompt",
    )(proj, cw, cb, dtb, alog, dx, nw, e, tril, hist, h0, *prev)


def _ssd_step_pre_body(proj_ref, hist_ref, cw_ref, cb_ref, dtb_ref, alog_ref, e_ref,
                       xs_ref, b_ref, c_ref, xdt_t_ref, dec_t_ref, clast_ref, *, d_inner):
    gn = SSM_GROUPS * D_STATE
    conv_dim = d_inner + 2 * gn
    m = proj_ref.shape[0]
    xbc = proj_ref[:, d_inner:d_inner + conv_dim]
    xc = hist_ref[0] * cw_ref[0:1, :] + cb_ref[...]
    for j in range(1, SSM_CONV - 1):
        xc = xc + hist_ref[j] * cw_ref[j:j + 1, :]
        clast_ref[j - 1] = hist_ref[j]
    xc = _silu(xc + xbc * cw_ref[SSM_CONV - 1:SSM_CONV, :])
    clast_ref[SSM_CONV - 2] = xbc
    xs = xc[:, :d_inner]
    xs_ref[...] = xs
    b_ref[...] = xc[:, d_inner:d_inner + gn]
    c_ref[...] = xc[:, d_inner + gn:]
    e = e_ref[...]
    dt = _softplus(proj_ref[:, d_inner + conv_dim:] + dtb_ref[...])
    dec = jnp.exp(dt * (-jnp.exp(alog_ref[...])))
    pad = jnp.zeros((LANE - m, d_inner), F32)
    xdt_t_ref[...] = jnp.concatenate([xs * _expand(dt, e), pad], axis=0).T
    dec_t_ref[...] = jnp.concatenate([_expand(dec, e), pad], axis=0).T


def _ssd_step_pre(proj, hist_t, cw, cb, dtb, alog, e, d_inner):
    m = proj.shape[0]
    gn = SSM_GROUPS * D_STATE
    conv_dim = cw.shape[1]
    k = SSM_CONV - 1
    args = (proj, hist_t, cw, cb, dtb, alog, e)
    return pl.pallas_call(
        functools.partial(_ssd_step_pre_body, d_inner=d_inner),
        grid=(1,),
        in_specs=[_full(a.shape) for a in args],
        out_specs=[_full((m, d_inner)), _full((m, gn)), _full((m, gn)), _full((d_inner, LANE)),
                   _full((d_inner, LANE)), _full((k, m, conv_dim))],
        out_shape=[jax.ShapeDtypeStruct((m, d_inner), F32), jax.ShapeDtypeStruct((m, gn), F32),
                   jax.ShapeDtypeStruct((m, gn), F32), jax.ShapeDtypeStruct((d_inner, LANE), F32),
                   jax.ShapeDtypeStruct((d_inner, LANE), F32), jax.ShapeDtypeStruct((k, m, conv_dim), F32)],
        compiler_params=_cparams(("arbitrary",)),
        name="ssd_step_pre",
    )(*args)


def _ssd_step_state_body(h0_ref, xdt_t_ref, dec_t_ref, b_ref, c_ref, *rest):
    hn_ref, y_t_ref = rest[-2:]
    m = h0_ref.shape[0]
    lane = lax.broadcasted_iota(jnp.int32, (SSM_HEAD_DIM, LANE), 1)
    acc = jnp.zeros((SSM_HEAD_DIM, LANE), F32)
    for b in range(m):
        h = h0_ref[b] * dec_t_ref[:, b:b + 1] + xdt_t_ref[:, b:b + 1] * b_ref[b:b + 1, :]
        hn_ref[b] = h
        ycol = jnp.sum(h * c_ref[b:b + 1, :], axis=1, keepdims=True)
        acc = jnp.where(lane == b, ycol, acc)
    y_t_ref[...] = acc


def _ssd_step_state(h_all, prev, layer, xdt_t, dec_t, bm, cm, n_heads):
    m = bm.shape[0]
    hpg = n_heads // SSM_GROUPS
    h_spec = pl.BlockSpec((m, None, SSM_HEAD_DIM, D_STATE), lambda j: (layer, j, 0, 0))
    return pl.pallas_call(
        _ssd_step_state_body,
        grid=(n_heads,),
        in_specs=[h_spec,
                  pl.BlockSpec((SSM_HEAD_DIM, LANE), lambda j: (j, 0)),
                  pl.BlockSpec((SSM_HEAD_DIM, LANE), lambda j: (j, 0)),
                  pl.BlockSpec((m, D_STATE), lambda j: (0, j // hpg)),
                  pl.BlockSpec((m, D_STATE), lambda j: (0, j // hpg))]
                 + [pl.BlockSpec(memory_space=pl.ANY)] * len(prev),
        out_specs=[h_spec, pl.BlockSpec((SSM_HEAD_DIM, LANE), lambda j: (j, 0))],
        out_shape=[jax.ShapeDtypeStruct(h_all.shape, F32),
                   jax.ShapeDtypeStruct((n_heads * SSM_HEAD_DIM, LANE), F32)],
        input_output_aliases={5: 0} if prev else {},
        compiler_params=_cparams(("parallel",)),
        name="ssd_step_state",
    )(h_all, xdt_t, dec_t, bm, cm, *prev)


def _ssd_step_post_body(y_t_ref, xs_ref, z_ref, dx_ref, nw_ref, o_ref):
    m, d_inner = xs_ref.shape
    gw = d_inner // SSM_GROUPS
    y = y_t_ref[...].T[:m, :] + xs_ref[...] * dx_ref[...]
    gated = y * _silu(z_ref[...])
    for g in range(SSM_GROUPS):
        seg = gated[:, g * gw:(g + 1) * gw]
        o_ref[:, g * gw:(g + 1) * gw] = _rms(seg, nw_ref[:, g * gw:(g + 1) * gw]).astype(BF16)


def _ssd_step_post(y_t, xs, proj, dx, nw):
    m, d_inner = xs.shape
    return pl.pallas_call(
        _ssd_step_post_body,
        grid=(1,),
        in_specs=[_full(y_t.shape), _full(xs.shape), pl.BlockSpec((m, d_inner), lambda i: (0, 0)),
                  _full(dx.shape), _full(nw.shape)],
        out_specs=_full((m, d_inner)),
        out_shape=jax.ShapeDtypeStruct((m, d_inner), BF16),
        compiler_params=_cparams(("arbitrary",)),
        name="ssd_step_post",
    )(y_t, xs, proj, dx, nw)


def _band_attn_body(q_ref, kc_ref, kp_ref, vc_ref, vp_ref, o_ref, l_ref, obuf, lbuf, *, dil):
    w = pl.program_id(2)
    win = q_ref.shape[0]
    span = win // dil
    heads = LANE // ATT_HEAD_DIM
    qi = lax.broadcasted_iota(jnp.int32, (span, 2 * span), 0)
    ki = lax.broadcasted_iota(jnp.int32, (span, 2 * span), 1)
    valid = (ki >= qi) & (ki <= qi + span) & ((w > 0) | (ki >= span))

    def rows(ref, r):
        return ref[...] if dil == 1 else ref[pl.ds(r, span, stride=dil), :]

    def one(r):
        q = (rows(q_ref, r) * (ATT_HEAD_DIM ** -0.5)).astype(BF16)
        kk = jnp.concatenate([rows(kp_ref, r), rows(kc_ref, r)], axis=0).astype(BF16)
        vv = jnp.concatenate([rows(vp_ref, r), rows(vc_ref, r)], axis=0).astype(BF16)
        for h in range(heads):
            sl = slice(h * ATT_HEAD_DIM, (h + 1) * ATT_HEAD_DIM)
            s = jnp.where(valid, _dot_nt(q[:, sl], kk[:, sl]), -jnp.inf)
            mx = jnp.max(s, axis=-1, keepdims=True)
            p = jnp.exp(s - mx)
            den = jnp.sum(p, axis=-1, keepdims=True)
            obuf[:, sl] = _dot(p.astype(BF16), vv[:, sl]) / den
            lbuf[:, sl] = jnp.broadcast_to(mx + jnp.log(den), (span, ATT_HEAD_DIM))
        if dil == 1:
            o_ref[...] = obuf[...]
            l_ref[...] = lbuf[...]
        else:
            o_ref[pl.ds(r, span, stride=dil), :] = obuf[...]
            l_ref[pl.ds(r, span, stride=dil), :] = lbuf[...]

    if dil == 1:
        one(0)
    else:
        def step(r, carry):
            one(r)
            return carry
        lax.fori_loop(0, dil, step, 0)


def _band_attn(qkv_c, n_seq, gi, win, dil):
    nch, m, _ = qkv_c.shape
    t_len = m // n_seq
    span = win // dil
    nbw = t_len // win
    per_sec = nch // 3
    per_grp = ATT_GW // LANE

    def spec(sec, prev):
        def imap(b, hp, w):
            ww = jnp.maximum(w - 1, 0) if prev else w
            return (sec * per_sec + gi * per_grp + hp, b * nbw + ww, 0)
        return pl.BlockSpec((None, win, LANE), imap)

    out_spec = pl.BlockSpec((None, win, LANE), lambda b, hp, w: (hp, b * nbw + w, 0))
    out_sd = jax.ShapeDtypeStruct((per_grp, m, LANE), F32)
    return pl.pallas_call(
        functools.partial(_band_attn_body, dil=dil),
        grid=(n_seq, per_grp, nbw),
        in_specs=[spec(0, False), spec(1, False), spec(1, True), spec(2, False), spec(2, True)],
        out_specs=[out_spec, out_spec],
        out_shape=[out_sd, out_sd],
        scratch_shapes=[pltpu.VMEM((span, LANE), F32), pltpu.VMEM((span, LANE), F32)],
        compiler_params=_cparams(("parallel", "parallel", "arbitrary")),
        name="band_attn_w%d" % win,
    )(qkv_c, qkv_c, qkv_c, qkv_c, qkv_c)


def _row_to_cols(v):
    return jnp.concatenate([jnp.broadcast_to(v[:, c * LANE:(c + 1) * LANE], (LANE, LANE)).T
                            for c in range(v.shape[1] // LANE)], axis=0)


def _col_to_row(v):
    return jnp.concatenate([jnp.broadcast_to(v[c * LANE:(c + 1) * LANE, :], (LANE, LANE)).T[0:1, :]
                            for c in range(v.shape[0] // LANE)], axis=1)


def _attn_sample_body(*refs, dils, n_prev):
    ng = len(dils)
    qkv_ref = refs[0]
    bufs = refs[1:1 + ng]
    outs = refs[1 + ng + n_prev:]
    o_refs, l_refs, c_refs = outs[0:ng], outs[ng:2 * ng], outs[2 * ng:3 * ng]
    scale = ATT_HEAD_DIM ** -0.5
    for gi in range(ng):
        buf, o_ref, l_ref, c_ref = bufs[gi], o_refs[gi], l_refs[gi], c_refs[gi]
        wb = buf.shape[2]
        nch = wb // LANE
        q_c = _row_to_cols(qkv_ref[:, gi * ATT_GW:(gi + 1) * ATT_GW] * scale)
        kn_c = _row_to_cols(qkv_ref[:, (ng + gi) * ATT_GW:(ng + gi + 1) * ATT_GW])
        vn_c = _row_to_cols(qkv_ref[:, (2 * ng + gi) * ATT_GW:(2 * ng + gi + 1) * ATT_GW])
        lane = lax.broadcasted_iota(jnp.int32, (1, wb), 1)
        valid = (lane & (dils[gi] - 1)) == 0
        o_cols = []
        for h in range(ATT_HPG):
            rs = slice(h * ATT_HEAD_DIM, (h + 1) * ATT_HEAD_DIM)
            qh = q_c[rs, :]
            s = jnp.concatenate([jnp.sum(buf[0, rs, c * LANE:(c + 1) * LANE] * qh, axis=0, keepdims=True)
                                 for c in range(nch)], axis=1)
            s = jnp.where(valid, s, -jnp.inf)
            sn = jnp.sum(qh[:, 0:1] * kn_c[rs, 0:1], axis=0, keepdims=True)
            mx = jnp.maximum(jnp.max(s, axis=1, keepdims=True), sn)
            p = jnp.exp(s - mx)
            p_new = jnp.exp(sn - mx)
            den = jnp.sum(p, axis=1, keepdims=True) + p_new
            acc = buf[1, rs, 0:LANE] * p[:, 0:LANE]
            for c in range(1, nch):
                acc = acc + buf[1, rs, c * LANE:(c + 1) * LANE] * p[:, c * LANE:(c + 1) * LANE]
            o_cols.append((jnp.sum(acc, axis=1, keepdims=True) + p_new * vn_c[rs, 0:1]) / den)
            l_ref[:, rs] = jnp.broadcast_to(mx + jnp.log(den), (1, ATT_HEAD_DIM))
        o_ref[...] = _col_to_row(jnp.concatenate(o_cols, axis=0))
        last = lax.broadcasted_iota(jnp.int32, (ATT_HEAD_DIM, LANE), 1) == LANE - 1
        for kv, new_c in ((0, kn_c), (1, vn_c)):
            for h in range(ATT_HPG):
                rs = slice(h * ATT_HEAD_DIM, (h + 1) * ATT_HEAD_DIM)
                rolled = pltpu.roll(buf[kv, rs, :], wb - 1, 1)
                if nch > 1:
                    c_ref[kv, rs, 0:wb - LANE] = rolled[:, 0:wb - LANE]
                c_ref[kv, rs, wb - LANE:wb] = jnp.where(last, new_c[rs, :], rolled[:, wb - LANE:wb])


def _attn_sample(qkv, cache_views, prev, layer):
    m, w3 = qkv.shape
    ng = len(ATT_GROUPS)
    dils = tuple(dil for _, dil in ATT_GROUPS)
    for v, (win, dil) in zip(cache_views, ATT_GROUPS):
        assert v.shape[3] == win and win % LANE == 0 and dil & (dil - 1) == 0
    c_specs = [pl.BlockSpec((None, 2, ATT_GW, v.shape[3]), lambda b: (layer * m + b, 0, 0, 0)) for v in cache_views]
    o_sd = jax.ShapeDtypeStruct((m, 1, ATT_GW), F32)
    o_spec = pl.BlockSpec((None, 1, ATT_GW), lambda b: (b, 0, 0))
    n_in = 1 + ng
    res = pl.pallas_call(
        functools.partial(_attn_sample_body, dils=dils, n_prev=len(prev)),
        grid=(m,),
        in_specs=[pl.BlockSpec((None, 1, w3), lambda b: (b, 0, 0))] + c_specs
                 + [pl.BlockSpec(memory_space=pl.ANY)] * len(prev),
        out_specs=[o_spec] * (2 * ng) + c_specs,
        out_shape=[o_sd] * (2 * ng) + [jax.ShapeDtypeStruct(v.shape, v.dtype) for v in cache_views],
        input_output_aliases={n_in + k: 2 * ng + k for k in range(len(prev))},
        compiler_params=_cparams(("parallel",)),
        name="attn_sample",
    )(qkv.reshape(m, 1, w3), *cache_views, *prev)
    os_ = [r.reshape(m, ATT_GW) for r in res[0:ng]]
    ls_ = [r.reshape(m, ATT_GW) for r in res[ng:2 * ng]]
    return os_, ls_, list(res[2 * ng:])


def _prep_weights(norms, ssm_w_in, ssm_conv_b, ssm_dt_bias, ssm_a_log, ssm_d, ssm_norm_w, ssm_w_out, att_w_qkv,
                  att_w_o, mem_norm, xa_w_q, xa_w_kv, xa_w_o, ffn_w_gu, ffn_conv_b, ffn_w_down):
    n_ssm, d_model, in_dim = ssm_w_in.shape
    n_heads = ssm_dt_bias.shape[1]
    d_inner = n_heads * SSM_HEAD_DIM
    d_ff = ffn_w_down.shape[1]
    pad_heads = LANE - n_heads
    w = {}
    w['n_heads'], w['d_inner'] = n_heads, d_inner
    w['norms'] = norms[:, :, None, :]
    w['ssm_w_in'] = jnp.pad(ssm_w_in, ((0, 0), (0, 0), (0, pad_heads))).astype(BF16)
    w['ssm_conv_b'] = ssm_conv_b[:, None, :]
    w['ssm_dt_bias'] = jnp.pad(ssm_dt_bias, ((0, 0), (0, pad_heads)))[:, None, :]
    w['ssm_a_log'] = jnp.pad(ssm_a_log, ((0, 0), (0, pad_heads)))[:, None, :]
    w['ssm_dx'] = jnp.repeat(ssm_d, SSM_HEAD_DIM, axis=1)[:, None, :]
    w['ssm_norm_w'] = ssm_norm_w[:, None, :]
    w['ssm_w_out'] = ssm_w_out.astype(BF16)
    w['att_w_qkv'] = att_w_qkv.astype(BF16)
    w['att_w_o'] = att_w_o.astype(BF16)
    w['mem_norm'] = mem_norm[:, None, :]
    w['xa_w_q'] = xa_w_q.astype(BF16)
    w['xa_w_kv'] = xa_w_kv.astype(BF16)
    w['xa_w_o'] = xa_w_o.astype(BF16)
    w['ffn_w_g'] = ffn_w_gu[:, :, :d_ff].astype(BF16)
    w['ffn_w_u'] = ffn_w_gu[:, :, d_ff:].astype(BF16)
    w['ffn_conv_b'] = ffn_conv_b[:, None, :]
    w['ffn_w_down'] = ffn_w_down.astype(BF16)
    e = np.zeros((LANE, d_inner), np.float32)
    for h in range(n_heads):
        e[h, h * SSM_HEAD_DIM:(h + 1) * SSM_HEAD_DIM] = 1.0
    w['expand'] = jnp.asarray(e, BF16)
    w['tril'] = jnp.asarray(np.tril(np.ones((SSM_CHUNK, SSM_CHUNK), np.float32)), BF16)
    return w


def _prompt_trunk(x3, mem3, w, ssm_conv_w, ffn_conv_w):
    n, t_len, d = x3.shape
    assert t_len % SSM_CHUNK == 0 and all(t_len % win == 0 for win, _ in ATT_GROUPS)
    depth = w['norms'].shape[0]
    n_heads, d_inner = w['n_heads'], w['d_inner']
    conv_dim = ssm_conv_w.shape[2]
    d_ff = ffn_conv_w.shape[2]
    tm = 256
    x = x3.reshape(n * t_len, d)
    mem = mem3.reshape(-1, d)
    kv_all, kv_rows = _mem_kv(mem, w['mem_norm'], w['xa_w_kv'], tm)
    tabs = _rope_tables(jnp.arange(t_len, dtype=jnp.int32))
    zero_hist = jnp.zeros((n, SSM_CONV - 1, conv_dim), F32)
    zero_h = jnp.zeros((n, n_heads * SSM_HEAD_DIM, D_STATE), F32)
    zero_fh = jnp.zeros((n, FFN_CONV - 1, d_ff), F32)
    n_ssm, n_att = (depth + 1) // 2, depth // 2
    new = {'ssm': [], 'ssm_conv': [], 'swa': [], 'ffn_conv': []}
    for i in range(depth):
        j = i // 2
        g = w['norms'][i]
        if i % 2 == 0:
            proj = _norm_proj(x, g[0], w['ssm_w_in'][j], tm)
            y, h_last, c_last = _ssd_prompt(proj, ssm_conv_w[j], w['ssm_conv_b'][j], w['ssm_dt_bias'][j],
                                            w['ssm_a_log'][j], w['ssm_dx'][j], w['ssm_norm_w'][j], w['expand'],
                                            w['tril'], zero_hist, zero_h, new['ssm'], j, n_ssm, n, d_inner, n_heads)
            x = _proj_res(y, w['ssm_w_out'][j], x, g[1], tm)
            new['ssm'] = [h_last]
            new['ssm_conv'].append(c_last)
        else:
            qkv_c, new['swa'] = _norm_qkv_prompt(x, g[0], w['att_w_qkv'][j], tabs, new['swa'], j, n_att, n, tm)
            os_, ls_ = [], []
            for gi, (win, dil) in enumerate(ATT_GROUPS):
                o, l = _band_attn(qkv_c, n, gi, win, dil)
                os_.append(o)
                ls_.append(l)
            x = _attn_out(os_, ls_, w['att_w_o'][j], x, g[1], tm)
        x = _xattn_prompt(x, g[2], g[3], w['xa_w_q'][i], kv_all[i], w['xa_w_o'][i], n, tm)
        x, f_hist = _ffn_prompt(x, g[4], g[5], w['ffn_w_g'][i], w['ffn_w_u'][i], ffn_conv_w[i],
                                w['ffn_conv_b'][i], w['ffn_w_down'][i], zero_fh, n, tm)
        new['ffn_conv'].append(f_hist)
    n_mem = mem3.shape[1]
    hd = d // MEM_HEADS
    swa = [jnp.transpose(c.reshape(n_att, n, 2, ATT_HPG, ATT_HEAD_DIM, c.shape[3]), (0, 1, 5, 2, 3, 4))
           for c in new['swa']]
    p_mem = jnp.transpose(kv_rows.reshape(depth, n, n_mem, 2, hd // LANE, MEM_HEADS, LANE),
                          (0, 1, 2, 3, 5, 4, 6)).reshape(depth, n, n_mem, 2, MEM_HEADS, hd)
    return (x.reshape(n, t_len, d), new['ssm'][0].reshape(n_ssm, n, n_heads, SSM_HEAD_DIM, D_STATE),
            jnp.stack(new['ssm_conv']), swa, p_mem, jnp.stack(new['ffn_conv']))


def _sample_trunk(x3, w, ssm_conv_w, ffn_conv_w, state_ssm, state_ssm_conv, caches, cache_mem_kv, state_ffn_conv):
    m, t_len, d = x3.shape
    assert t_len == 1
    depth = w['norms'].shape[0]
    n_heads, d_inner = w['n_heads'], w['d_inner']
    n_ssm = state_ssm.shape[0]
    x = x3.reshape(m, d)
    tabs = _rope_tables(jnp.full((m,), PAST_LEN, jnp.int32))
    h_all = state_ssm.reshape(n_ssm * m, n_heads, SSM_HEAD_DIM, D_STATE)
    n_mem, hd = cache_mem_kv.shape[2], d // MEM_HEADS
    kv_all = jnp.transpose(cache_mem_kv.reshape(depth, m, n_mem, 2, MEM_HEADS, hd // LANE, LANE),
                           (0, 1, 2, 3, 5, 4, 6)).reshape(depth * m, n_mem * 2 * MEM_HEADS * (hd // LANE), LANE)
    cache_views = [jnp.transpose(c, (0, 1, 3, 4, 5, 2)).reshape(c.shape[0] * m, 2, ATT_GW, c.shape[2])
                   for c in caches]
    new_caches = []
    new = {'ssm': [], 'ssm_conv': [], 'ffn_conv': []}
    for i in range(depth):
        j = i // 2
        g = w['norms'][i]
        if i % 2 == 0:
            proj = _norm_proj(x, g[0], w['ssm_w_in'][j], m)
            hist_t = jnp.swapaxes(state_ssm_conv[j], 0, 1)
            xs, bm, cm, xdt_t, dec_t, c_last_t = _ssd_step_pre(proj, hist_t, ssm_conv_w[j], w['ssm_conv_b'][j],
                                                                w['ssm_dt_bias'][j], w['ssm_a_log'][j],
                                                                w['expand'], d_inner)
            h_new, y_t = _ssd_step_state(h_all, new['ssm'], j, xdt_t, dec_t, bm, cm, n_heads)
            y = _ssd_step_post(y_t, xs, proj, w['ssm_dx'][j], w['ssm_norm_w'][j])
            x = _proj_res(y, w['ssm_w_out'][j], x, g[1], m)
            new['ssm'] = [h_new]
            new['ssm_conv'].append(jnp.swapaxes(c_last_t, 0, 1))
        else:
            qkv = _norm_qkv_rope(x, g[0], w['att_w_qkv'][j], tabs, m, 1)
            os_, ls_, new_caches = _attn_sample(qkv, cache_views, new_caches, j)
            x = _attn_out(os_, ls_, w['att_w_o'][j], x, g[1], m)
        x = _xattn_sample(x, g[2], g[3], w['xa_w_q'][i], kv_all, i, w['xa_w_o'][i])
        hist_t = jnp.swapaxes(state_ffn_conv[i], 0, 1)
        x, f_hist_t = _ffn_sample(x, g[4], g[5], w['ffn_w_g'][i], w['ffn_w_u'][i], ffn_conv_w[i],
                                  w['ffn_conv_b'][i], w['ffn_w_down'][i], hist_t, 256)
        new['ffn_conv'].append(jnp.swapaxes(f_hist_t, 0, 1))
    new_caches = [jnp.transpose(nc.reshape(c.shape[0], m, 2, ATT_HPG, ATT_HEAD_DIM, c.shape[2]), (0, 1, 5, 2, 3, 4))
                  for nc, c in zip(new_caches, caches)]
    return (x.reshape(m, 1, d), new['ssm'][0].reshape(state_ssm.shape), jnp.stack(new['ssm_conv']), new_caches,
            jnp.stack(new['ffn_conv']))


def kernel(x_prompt, x_sample, mem_prompt, state_ssm, state_ssm_conv, cache_swa_kv_w128, cache_swa_kv_w512,
           cache_swa_kv_w2048, cache_mem_kv, state_ffn_conv, norms, ssm_w_in, ssm_conv_w, ssm_conv_b,
           ssm_dt_bias, ssm_a_log, ssm_d, ssm_norm_w, ssm_w_out, att_w_qkv, att_w_o, mem_norm, xa_w_q,
           xa_w_kv, xa_w_o, ffn_w_gu, ffn_conv_w, ffn_conv_b, ffn_w_down):
    w = _prep_weights(norms, ssm_w_in, ssm_conv_b, ssm_dt_bias, ssm_a_log, ssm_d, ssm_norm_w, ssm_w_out,
                      att_w_qkv, att_w_o, mem_norm, xa_w_q, xa_w_kv, xa_w_o, ffn_w_gu, ffn_conv_b, ffn_w_down)
    caches = [cache_swa_kv_w128, cache_swa_kv_w512, cache_swa_kv_w2048]
    yp, p_ssm, p_conv, p_swa, p_mem, p_ffn = _prompt_trunk(x_prompt, mem_prompt, w, ssm_conv_w, ffn_conv_w)
    ys, s_ssm, s_conv, s_swa, s_ffn = _sample_trunk(x_sample, w, ssm_conv_w, ffn_conv_w, state_ssm,
                                                    state_ssm_conv, caches, cache_mem_kv, state_ffn_conv)
    return (yp, ys, p_ssm, p_conv, p_swa[0], p_swa[1], p_swa[2], p_mem, p_ffn,
            s_ssm, s_conv, s_swa[0], s_swa[1], s_swa[2], s_ffn)
```

```python
import functools
import math

import numpy as np
import jax
import jax.numpy as jnp
from jax import lax
from jax.experimental import pallas as pl
from jax.experimental.pallas import tpu as pltpu

F32 = jnp.float32
BF16 = jnp.bfloat16

EPS = 1e-6
PAST_LEN = 8192
SSM_HEAD_DIM = 64
SSM_GROUPS = 4
D_STATE = 128
SSM_CONV = 4
SSM_CHUNK = 128
CONV_COLS = 1024
ATT_GROUPS = ((128, 1), (512, 4), (2048, 16))
ATT_HPG = 4
ATT_HEAD_DIM = 64
ATT_GW = ATT_HPG * ATT_HEAD_DIM
BAND_BLOCK = 512
ROT_DIM = ATT_HEAD_DIM // 4
ROPE_THETA = 500000.0
MEM_HEADS = 4
FFN_CONV = 3

LANE = 128
SUBLANE = 8
VMEM_LIMIT = 56 * 1024 * 1024


def _cparams(sem):
    return pltpu.CompilerParams(dimension_semantics=sem, vmem_limit_bytes=VMEM_LIMIT)


def _rms(x, g):
    return x * lax.rsqrt(jnp.mean(x * x, axis=-1, keepdims=True) + EPS) * g


def _silu(x):
    return x / (1.0 + jnp.exp(-x))


def _softplus(x):
    return jnp.maximum(x, 0.0) + jnp.log(1.0 + jnp.exp(-jnp.abs(x)))


def _dot(a, b):
    return jnp.dot(a, b, preferred_element_type=F32)


def _dot_nt(a, b):
    return lax.dot_general(a, b, (((1,), (1,)), ((), ())), preferred_element_type=F32)


def _split3(v):
    hi = v.astype(BF16)
    r = v - hi.astype(F32)
    mid = r.astype(BF16)
    lo = (r - mid.astype(F32)).astype(BF16)
    return hi, mid, lo


def _expand(v, e):
    hi, mid, lo = _split3(v)
    return _dot(hi, e) + _dot(mid, e) + _dot(lo, e)


def _full(shape):
    return pl.BlockSpec(shape, lambda *_: (0,) * len(shape))


def _norm_proj_body(x_ref, g_ref, w_ref, o_ref):
    u = _rms(x_ref[...], g_ref[...]).astype(BF16)
    o_ref[...] = _dot(u, w_ref[...])


def _norm_proj(x, g, w, tm):
    m, d = x.shape
    n = w.shape[1]
    return pl.pallas_call(
        _norm_proj_body,
        grid=(m // tm,),
        in_specs=[pl.BlockSpec((tm, d), lambda i: (i, 0)), _full((1, d)), _full((d, n))],
        out_specs=pl.BlockSpec((tm, n), lambda i: (i, 0)),
        out_shape=jax.ShapeDtypeStruct((m, n), F32),
        compiler_params=_cparams(("parallel",)),
        name="norm_proj",
    )(x, g, w)


def _norm_qkv_rope_body(x_ref, g_ref, w_ref, cos_ref, sa_ref, sb_ref, o_ref, *, n_rot):
    u = _rms(x_ref[...], g_ref[...]).astype(BF16)
    y = _dot(u, w_ref[...])
    cos, sa, sb = cos_ref[...], sa_ref[...], sb_ref[...]
    for c in range(n_rot // LANE):
        t = y[:, c * LANE:(c + 1) * LANE]
        o_ref[:, c * LANE:(c + 1) * LANE] = (t * cos + pltpu.roll(t, LANE - ROT_DIM // 2, 1) * sa
                                             + pltpu.roll(t, ROT_DIM // 2, 1) * sb)
    o_ref[:, n_rot:] = y[:, n_rot:]


def _norm_qkv_rope(x, g, w, tabs, tm, seq_blocks):
    m, d = x.shape
    n = w.shape[1]
    tab_spec = pl.BlockSpec((tm, LANE), lambda i: (i % seq_blocks, 0))
    return pl.pallas_call(
        functools.partial(_norm_qkv_rope_body, n_rot=2 * n // 3),
        grid=(m // tm,),
        in_specs=[pl.BlockSpec((tm, d), lambda i: (i, 0)), _full((1, d)), _full((d, n)),
                  tab_spec, tab_spec, tab_spec],
        out_specs=pl.BlockSpec((tm, n), lambda i: (i, 0)),
        out_shape=jax.ShapeDtypeStruct((m, n), F32),
        compiler_params=_cparams(("parallel",)),
        name="norm_qkv_rope",
    )(x, g, w, *tabs)


def _rope_chunk(t, cos, sa, sb):
    return t * cos + pltpu.roll(t, LANE - ROT_DIM // 2, 1) * sa + pltpu.roll(t, ROT_DIM // 2, 1) * sb


def _norm_qkv_prompt_body(*refs, n_rot, n_prev, nblks):
    x_ref, g_ref, w_ref, cos_ref, sa_ref, sb_ref = refs[:6]
    o_ref = refs[6 + n_prev]
    c_refs = refs[7 + n_prev:]
    ng = len(c_refs)
    t = pl.program_id(1)
    tb = pl.num_programs(1)
    tm = x_ref.shape[0]
    u = _rms(x_ref[...], g_ref[...]).astype(BF16)
    y = _dot(u, w_ref[...])
    cos, sa, sb = cos_ref[...], sa_ref[...], sb_ref[...]
    vals = []
    for c in range(y.shape[1] // LANE):
        v = y[:, c * LANE:(c + 1) * LANE]
        if c * LANE < n_rot:
            v = _rope_chunk(v, cos, sa, sb)
        o_ref[c] = v
        vals.append(v)
    per_sec = len(vals) // 3
    per_grp = ATT_GW // LANE
    for gi, c_ref in enumerate(c_refs):
        kw = c_ref.shape[2]
        nblk = nblks[gi]

        @pl.when(t >= tb - nblk)
        def _(gi=gi, c_ref=c_ref, kw=kw):
            for kv in range(2):
                for hp in range(per_grp):
                    v = vals[(1 + kv) * per_sec + gi * per_grp + hp]
                    c_ref[kv, hp * LANE:(hp + 1) * LANE, :] = v[tm - kw:, :].T


def _norm_qkv_prompt(x, g, w, tabs, prev, layer, n_layers, n_seq, tm):
    m, d = x.shape
    n = w.shape[1]
    t_len = m // n_seq
    tb = t_len // tm
    tab_spec = pl.BlockSpec((tm, LANE), lambda b, t: (t, 0))
    c_specs, c_shapes, nblks = [], [], []
    for gi, (win, _) in enumerate(ATT_GROUPS):
        keep = min(win, t_len)
        kw = min(keep, tm)
        nblk = keep // kw
        assert keep % kw == 0
        nblks.append(nblk)
        c_specs.append(pl.BlockSpec((None, 2, ATT_GW, kw),
                                    lambda b, t, nblk=nblk: (layer * n_seq + b, 0, 0, jnp.maximum(t - (tb - nblk), 0))))
        c_shapes.append(jax.ShapeDtypeStruct((n_layers * n_seq, 2, ATT_GW, keep), F32))
    res = pl.pallas_call(
        functools.partial(_norm_qkv_prompt_body, n_rot=2 * n // 3, n_prev=len(prev), nblks=tuple(nblks)),
        grid=(n_seq, tb),
        in_specs=[pl.BlockSpec((tm, d), lambda b, t: (b * tb + t, 0)), _full((1, d)), _full((d, n)),
                  tab_spec, tab_spec, tab_spec] + [pl.BlockSpec(memory_space=pl.ANY)] * len(prev),
        out_specs=[pl.BlockSpec((n // LANE, tm, LANE), lambda b, t: (0, b * tb + t, 0))] + c_specs,
        out_shape=[jax.ShapeDtypeStruct((n // LANE, m, LANE), F32)] + c_shapes,
        input_output_aliases={6 + k: 1 + k for k in range(len(prev))},
        compiler_params=_cparams(("parallel", "arbitrary")),
        name="norm_qkv_prompt",
    )(x, g, w, *tabs, *prev)
    return res[0], list(res[1:])


def _rope_tables(pos):
    half = ROT_DIM // 2
    inv = ROPE_THETA ** (-jnp.arange(half, dtype=F32) / half)
    ang = pos.astype(F32)[:, None] * inv[None, :]
    cos, sin = jnp.cos(ang), jnp.sin(ang)
    p = pos.shape[0]
    rest = ATT_HEAD_DIM - ROT_DIM
    c = jnp.concatenate([cos, cos, jnp.ones((p, rest), F32)], axis=1)
    sa = jnp.concatenate([-sin, jnp.zeros((p, half + rest), F32)], axis=1)
    sb = jnp.concatenate([jnp.zeros((p, half), F32), sin, jnp.zeros((p, rest), F32)], axis=1)
    rep = LANE // ATT_HEAD_DIM
    return tuple(jnp.tile(t, (1, rep)) for t in (c, sa, sb))


def _proj_res_body(y_ref, w_ref, x_ref, g_ref, o_ref):
    f = _dot(y_ref[...].astype(BF16), w_ref[...])
    o_ref[...] = x_ref[...] + _rms(f, g_ref[...])


def _proj_res(y, w, x, g, tm):
    m, k = y.shape
    d = w.shape[1]
    return pl.pallas_call(
        _proj_res_body,
        grid=(m // tm,),
        in_specs=[pl.BlockSpec((tm, k), lambda i: (i, 0)), _full((k, d)),
                  pl.BlockSpec((tm, d), lambda i: (i, 0)), _full((1, d))],
        out_specs=pl.BlockSpec((tm, d), lambda i: (i, 0)),
        out_shape=jax.ShapeDtypeStruct((m, d), F32),
        compiler_params=_cparams(("parallel",)),
        name="proj_res",
    )(y, w, x, g)


def _attn_out_body(o0, o1, o2, l0, l1, l2, w_ref, x_ref, g_ref, out_ref):
    def load(ref):
        if len(ref.shape) == 2:
            return ref[...]
        return jnp.concatenate([ref[c] for c in range(ref.shape[0])], axis=1)

    ls = [load(l0), load(l1), load(l2)]
    mx = jnp.maximum(jnp.maximum(ls[0], ls[1]), ls[2])
    es = [jnp.exp(l - mx) for l in ls]
    den = es[0] + es[1] + es[2]
    og = jnp.concatenate([(es[gi] / den * load(o_ref)).astype(BF16) for gi, o_ref in enumerate((o0, o1, o2))], axis=1)
    out_ref[...] = x_ref[...] + _rms(_dot(og, w_ref[...]), g_ref[...])


def _attn_out(os_, ls_, w, x, g, tm):
    m, d = x.shape
    if os_[0].ndim == 2:
        blk = pl.BlockSpec((tm, ATT_GW), lambda i: (i, 0))
    else:
        blk = pl.BlockSpec((ATT_GW // LANE, tm, LANE), lambda i: (0, i, 0))
    return pl.pallas_call(
        _attn_out_body,
        grid=(m // tm,),
        in_specs=[blk] * 6 + [_full(w.shape), pl.BlockSpec((tm, d), lambda i: (i, 0)), _full((1, d))],
        out_specs=pl.BlockSpec((tm, d), lambda i: (i, 0)),
        out_shape=jax.ShapeDtypeStruct((m, d), F32),
        compiler_params=_cparams(("parallel",)),
        name="attn_out",
    )(*os_, *ls_, w, x, g)


def _mem_kv_body(x_ref, g_ref, w_ref, o_ref, t_ref):
    tm = x_ref.shape[0]
    u = _rms(x_ref[...], g_ref[...]).astype(BF16)
    y = _dot(u, w_ref[...])
    o_ref[...] = y
    hd = y.shape[1] // (2 * MEM_HEADS)
    nch = hd // LANE
    tok_rows = 2 * nch * MEM_HEADS
    for kv in range(2):
        for h in range(MEM_HEADS):
            for c in range(nch):
                col = (kv * MEM_HEADS + h) * hd + c * LANE
                t_ref[pl.ds((kv * nch + c) * MEM_HEADS + h, tm, stride=tok_rows), :] = y[:, col:col + LANE]


def _mem_kv(mem, g, w, tm):
    m, d = mem.shape
    depth, _, n = w.shape
    tok_rows = n // LANE
    return pl.pallas_call(
        _mem_kv_body,
        grid=(depth, m // tm),
        in_specs=[pl.BlockSpec((tm, d), lambda l, i: (i, 0)),
                  pl.BlockSpec((None, 1, d), lambda l, i: (l, 0, 0)),
                  pl.BlockSpec((None, d, n), lambda l, i: (l, 0, 0))],
        out_specs=[pl.BlockSpec((None, tm, n), lambda l, i: (l, i, 0)),
                   pl.BlockSpec((None, tm * tok_rows, LANE), lambda l, i: (l, i, 0))],
        out_shape=[jax.ShapeDtypeStruct((depth, m, n), F32),
                   jax.ShapeDtypeStruct((depth, m * tok_rows, LANE), F32)],
        compiler_params=_cparams(("parallel", "parallel")),
        name="mem_kv",
    )(mem, g, w)


def _xattn_prompt_body(x_ref, gpre_ref, gpost_ref, wq_ref, kv_ref, wo_ref, o_ref, obuf):
    x = x_ref[...]
    d = x.shape[1]
    hd = d // MEM_HEADS
    u = _rms(x, gpre_ref[...]).astype(BF16)
    q = _dot(u, wq_ref[...]).astype(BF16)
    scale = hd ** -0.5
    for h in range(MEM_HEADS):
        kh = kv_ref[:, h * hd:(h + 1) * hd].astype(BF16)
        vh = kv_ref[:, d + h * hd:d + (h + 1) * hd].astype(BF16)
        s = _dot_nt(q[:, h * hd:(h + 1) * hd], kh) * scale
        mx = jnp.max(s, axis=-1, keepdims=True)
        p = jnp.exp(s - mx)
        den = jnp.sum(p, axis=-1, keepdims=True)
        obuf[:, h * hd:(h + 1) * hd] = (_dot(p.astype(BF16), vh) / den).astype(BF16)
    f = _dot(obuf[...], wo_ref[...])
    o_ref[...] = x + _rms(f, gpost_ref[...])


def _xattn_prompt(x, gpre, gpost, wq, kv, wo, n_seq, tm):
    m, d = x.shape
    tb = m // n_seq // tm
    n_mem = kv.shape[0] // n_seq
    return pl.pallas_call(
        _xattn_prompt_body,
        grid=(n_seq, tb),
        in_specs=[pl.BlockSpec((tm, d), lambda b, t: (b * tb + t, 0)), _full((1, d)), _full((1, d)),
                  _full((d, d)), pl.BlockSpec((n_mem, 2 * d), lambda b, t: (b, 0)), _full((d, d))],
        out_specs=pl.BlockSpec((tm, d), lambda b, t: (b * tb + t, 0)),
        out_shape=jax.ShapeDtypeStruct((m, d), F32),
        scratch_shapes=[pltpu.VMEM((tm, d), BF16)],
        compiler_params=_cparams(("parallel", "parallel")),
        name="xattn_prompt",
    )(x, gpre, gpost, wq, kv, wo)


def _xattn_sample_body(x_ref, gpre_ref, gpost_ref, wq_ref, kv_ref, wo_ref, o_ref, q_scr, o_scr):
    b = pl.program_id(0)
    d = x_ref.shape[1]
    hd = d // MEM_HEADS

    @pl.when(b == 0)
    def _():
        u = _rms(x_ref[...], gpre_ref[...]).astype(BF16)
        q_scr[...] = _dot(u, wq_ref[...]) * (hd ** -0.5)

    qb = q_scr[pl.ds(b, 1), :]
    nch = hd // LANE
    tok_rows = 2 * nch * MEM_HEADS
    n_mem = kv_ref.shape[0] // tok_rows
    o_parts = []
    for h in range(MEM_HEADS):
        s = None
        for c in range(nch):
            kc = kv_ref[pl.ds(c * MEM_HEADS + h, n_mem, stride=tok_rows), :]
            part = jnp.sum(kc * qb[:, h * hd + c * LANE:h * hd + (c + 1) * LANE], axis=1, keepdims=True)
            s = part if s is None else s + part
        mx = jnp.max(s, axis=0, keepdims=True)
        p = jnp.exp(s - mx)
        den = jnp.sum(p, axis=0, keepdims=True)
        for c in range(nch):
            vc = kv_ref[pl.ds((nch + c) * MEM_HEADS + h, n_mem, stride=tok_rows), :]
            o_parts.append(jnp.sum(p * vc, axis=0, keepdims=True) / den)
    o_scr[pl.ds(b, 1), :] = jnp.concatenate(o_parts, axis=1)

    @pl.when(b == pl.num_programs(0) - 1)
    def _():
        f = _dot(o_scr[...].astype(BF16), wo_ref[...])
        o_ref[...] = x_ref[...] + _rms(f, gpost_ref[...])


def _xattn_sample(x, gpre, gpost, wq, kv_all, layer, wo):
    m, d = x.shape
    return pl.pallas_call(
        _xattn_sample_body,
        grid=(m,),
        in_specs=[_full((m, d)), _full((1, d)), _full((1, d)), _full((d, d)),
                  pl.BlockSpec((None, kv_all.shape[1], LANE), lambda b: (layer * m + b, 0, 0)), _full((d, d))],
        out_specs=_full((m, d)),
        out_shape=jax.ShapeDtypeStruct((m, d), F32),
        scratch_shapes=[pltpu.VMEM((m, d), F32), pltpu.VMEM((m, d), F32)],
        compiler_params=_cparams(("arbitrary",)),
        name="xattn_sample",
    )(x, gpre, gpost, wq, kv_all, wo)


def _ffn_prompt_body(x_ref, gpre_ref, gpost_ref, wg_ref, wu_ref, cw_ref, cb_ref, wd_ref, hist_ref,
                     o_ref, nh_ref, gbuf):
    t = pl.program_id(1)
    tm = x_ref.shape[0]
    k = FFN_CONV - 1
    base = SUBLANE - k

    @pl.when(t == 0)
    def _():
        gbuf[base:SUBLANE, :] = hist_ref[...]

    x = x_ref[...]
    u = _rms(x, gpre_ref[...]).astype(BF16)
    gbuf[SUBLANE:SUBLANE + tm, :] = _dot(u, wg_ref[...])
    up = _dot(u, wu_ref[...])
    gc = gbuf[base:base + tm, :] * cw_ref[0:1, :] + cb_ref[...]
    for j in range(1, FFN_CONV):
        gc = gc + gbuf[base + j:base + j + tm, :] * cw_ref[j:j + 1, :]
    hmid = (_silu(gc) * up).astype(BF16)
    f = _dot(hmid, wd_ref[...])
    o_ref[...] = x + _rms(f, gpost_ref[...])
    last = gbuf[tm + base:tm + SUBLANE, :]
    nh_ref[...] = last
    gbuf[base:SUBLANE, :] = last


def _ffn_prompt(x, gpre, gpost, wg, wu, cw, cb, wd, hist, n_seq, tm):
    m, d = x.shape
    f = wg.shape[1]
    tb = m // n_seq // tm
    k = FFN_CONV - 1
    return pl.pallas_call(
        _ffn_prompt_body,
        grid=(n_seq, tb),
        in_specs=[pl.BlockSpec((tm, d), lambda b, t: (b * tb + t, 0)), _full((1, d)), _full((1, d)),
                  _full((d, f)), _full((d, f)), _full((FFN_CONV, f)), _full((1, f)), _full((f, d)),
                  pl.BlockSpec((None, k, f), lambda b, t: (b, 0, 0))],
        out_specs=[pl.BlockSpec((tm, d), lambda b, t: (b * tb + t, 0)),
                   pl.BlockSpec((None, k, f), lambda b, t: (b, 0, 0))],
        out_shape=[jax.ShapeDtypeStruct((m, d), F32), jax.ShapeDtypeStruct((n_seq, k, f), F32)],
        scratch_shapes=[pltpu.VMEM((tm + SUBLANE, f), F32)],
        compiler_params=_cparams(("parallel", "arbitrary")),
        name="ffn_prompt",
    )(x, gpre, gpost, wg, wu, cw, cb, wd, hist)


def _ffn_sample_body(x_ref, gpre_ref, gpost_ref, wg_ref, wu_ref, cw_ref, cb_ref, wd_ref, hist_ref,
                     o_ref, nh_ref, u_scr, acc):
    j = pl.program_id(0)

    @pl.when(j == 0)
    def _():
        u_scr[...] = _rms(x_ref[...], gpre_ref[...]).astype(BF16)
        acc[...] = jnp.zeros_like(acc)

    u = u_scr[...]
    gate = _dot(u, wg_ref[...])
    up = _dot(u, wu_ref[...])
    gc = hist_ref[0] * cw_ref[0:1, :] + cb_ref[...]
    for k in range(1, FFN_CONV - 1):
        gc = gc + hist_ref[k] * cw_ref[k:k + 1, :]
        nh_ref[k - 1] = hist_ref[k]
    gc = gc + gate * cw_ref[FFN_CONV - 1:FFN_CONV, :]
    nh_ref[FFN_CONV - 2] = gate
    hmid = (_silu(gc) * up).astype(BF16)
    acc[...] += _dot(hmid, wd_ref[...])

    @pl.when(j == pl.num_programs(0) - 1)
    def _():
        o_ref[...] = x_ref[...] + _rms(acc[...], gpost_ref[...])


def _ffn_sample(x, gpre, gpost, wg, wu, cw, cb, wd, hist_t, tn):
    m, d = x.shape
    f = wg.shape[1]
    k = FFN_CONV - 1
    return pl.pallas_call(
        _ffn_sample_body,
        grid=(f // tn,),
        in_specs=[_full((m, d)), _full((1, d)), _full((1, d)),
                  pl.BlockSpec((d, tn), lambda j: (0, j)), pl.BlockSpec((d, tn), lambda j: (0, j)),
                  pl.BlockSpec((FFN_CONV, tn), lambda j: (0, j)), pl.BlockSpec((1, tn), lambda j: (0, j)),
                  pl.BlockSpec((tn, d), lambda j: (j, 0)),
                  pl.BlockSpec((k, m, tn), lambda j: (0, 0, j))],
        out_specs=[_full((m, d)), pl.BlockSpec((k, m, tn), lambda j: (0, 0, j))],
        out_shape=[jax.ShapeDtypeStruct((m, d), F32), jax.ShapeDtypeStruct((k, m, f), F32)],
        scratch_shapes=[pltpu.VMEM((m, d), BF16), pltpu.VMEM((m, d), F32)],
        compiler_params=_cparams(("arbitrary",)),
        name="ffn_sample",
    )(x, gpre, gpost, wg, wu, cw, cb, wd, hist_t)


def _ssd_in_proj_body(x_ref, g_ref, w_ref, cw_ref, cb_ref, dtb_ref, hist_ref, o_ref, clast_ref, cbuf, *, d_inner):
    t = pl.program_id(1)
    tm = x_ref.shape[0]
    k = SSM_CONV - 1
    base = SUBLANE - k
    conv_dim = cw_ref.shape[1]

    @pl.when(t == 0)
    def _():
        cbuf[base:SUBLANE, :] = hist_ref[...]

    u = _rms(x_ref[...], g_ref[...]).astype(BF16)
    o_ref[:, :d_inner] = _silu(_dot(u, w_ref[:, :d_inner]))
    o_ref[:, d_inner + conv_dim:] = _softplus(_dot(u, w_ref[:, d_inner + conv_dim:]) + dtb_ref[...])
    for c0 in range(0, conv_dim, CONV_COLS):
        cs = slice(c0, c0 + CONV_COLS)
        cbuf[SUBLANE:SUBLANE + tm, cs] = _dot(u, w_ref[:, d_inner + c0:d_inner + c0 + CONV_COLS])
        xc = cbuf[base:base + tm, cs] * cw_ref[0:1, cs] + cb_ref[:, cs]
        for j in range(1, SSM_CONV):
            xc = xc + cbuf[base + j:base + j + tm, cs] * cw_ref[j:j + 1, cs]
        o_ref[:, d_inner + c0:d_inner + c0 + CONV_COLS] = _silu(xc)
    last = cbuf[tm + base:tm + SUBLANE, :]
    clast_ref[...] = last
    cbuf[base:SUBLANE, :] = last


def _ssd_in_proj(x, g, w, cw, cb, dtb, hist, n_seq, d_inner, tm):
    m, d = x.shape
    n = w.shape[1]
    tb = m // n_seq // tm
    conv_dim = cw.shape[1]
    k = SSM_CONV - 1
    assert conv_dim % CONV_COLS == 0
    return pl.pallas_call(
        functools.partial(_ssd_in_proj_body, d_inner=d_inner),
        grid=(n_seq, tb),
        in_specs=[pl.BlockSpec((tm, d), lambda b, t: (b * tb + t, 0)), _full((1, d)), _full((d, n)),
                  _full(cw.shape), _full(cb.shape), _full(dtb.shape),
                  pl.BlockSpec((None, k, conv_dim), lambda b, t: (b, 0, 0))],
        out_specs=[pl.BlockSpec((tm, n), lambda b, t: (b * tb + t, 0)),
                   pl.BlockSpec((None, k, conv_dim), lambda b, t: (b, 0, 0))],
        out_shape=[jax.ShapeDtypeStruct((m, n), F32), jax.ShapeDtypeStruct((n_seq, k, conv_dim), F32)],
        scratch_shapes=[pltpu.VMEM((tm + SUBLANE, conv_dim), F32)],
        compiler_params=_cparams(("parallel", "arbitrary")),
        name="ssd_in_proj",
    )(x, g, w, cw, cb, dtb, hist)


def _ssd_prompt_body(act_ref, alog_ref, dx_ref, nw_ref, e_ref, tril_ref, h0_ref, *rest, d_inner, n_heads):
    y_ref, hlast_ref, state, ybuf = rest[-4:]
    c = pl.program_id(1)
    L = SSM_CHUNK
    gn = SSM_GROUPS * D_STATE
    conv_dim = d_inner + 2 * gn
    hpg = n_heads // SSM_GROUPS
    gw = d_inner // SSM_GROUPS

    @pl.when(c == 0)
    def _():
        state[...] = h0_ref[...]

    xs = act_ref[:, d_inner:2 * d_inner]
    xs_b = xs.astype(BF16)
    e = e_ref[...]
    dt = act_ref[:, d_inner + conv_dim:]
    a = dt * (-jnp.exp(alog_ref[...]))
    tril = tril_ref[...]
    a_hi, a_mid, a_lo = _split3(a)
    acum = _dot(tril, a_hi) + _dot(tril, a_mid) + _dot(tril, a_lo)
    acum_t = acum.T
    dt_t = dt.T
    a_last = acum[L - 1:L, :]
    d_acc = _expand(jnp.exp(acum), e)
    xde = xs * _expand(dt * jnp.exp(a_last - acum), e)
    row = lax.broadcasted_iota(jnp.int32, (L, L), 0)
    col = lax.broadcasted_iota(jnp.int32, (L, L), 1)
    causal = row >= col

    for g in range(SSM_GROUPS):
        b0 = 2 * d_inner + g * D_STATE
        bg = act_ref[:, b0:b0 + D_STATE].astype(BF16)
        cg = act_ref[:, b0 + gn:b0 + gn + D_STATE].astype(BF16)
        cb = _dot_nt(cg, bg)
        hg = state[g * gw:(g + 1) * gw, :]
        y_inter = _dot_nt(cg, hg.astype(BF16)) * d_acc[:, g * gw:(g + 1) * gw]
        for j in range(hpg):
            hd = g * hpg + j
            seg = acum[:, hd:hd + 1] - acum_t[hd:hd + 1, :]
            w = cb * jnp.exp(jnp.where(causal, seg, -jnp.inf)) * dt_t[hd:hd + 1, :]
            sl = slice(hd * SSM_HEAD_DIM, (hd + 1) * SSM_HEAD_DIM)
            ybuf[:, sl] = _dot(w.astype(BF16), xs_b[:, sl])
        ybuf[:, g * gw:(g + 1) * gw] += y_inter
        s_g = _dot(xde[:, g * gw:(g + 1) * gw].T.astype(BF16), bg)
        for j in range(hpg):
            hd = g * hpg + j
            rs = slice(hd * SSM_HEAD_DIM, (hd + 1) * SSM_HEAD_DIM)
            cd = jnp.exp(acum_t[hd:hd + 1, L - 1:L])
            state[rs, :] = state[rs, :] * cd + s_g[j * SSM_HEAD_DIM:(j + 1) * SSM_HEAD_DIM, :]

    y = ybuf[...] + xs * dx_ref[...]
    gated = y * act_ref[:, :d_inner]
    for g in range(SSM_GROUPS):
        seg = gated[:, g * gw:(g + 1) * gw]
        y_ref[:, g * gw:(g + 1) * gw] = _rms(seg, nw_ref[:, g * gw:(g + 1) * gw]).astype(BF16)

    @pl.when(c == pl.num_programs(1) - 1)
    def _():
        hlast_ref[...] = state[...]


def _ssd_prompt(act, alog, dx, nw, e, tril, h0, prev, layer, n_layers, n_seq, d_inner, n_heads):
    m, pw = act.shape
    L = SSM_CHUNK
    nc = m // n_seq // L
    rows = n_heads * SSM_HEAD_DIM
    body = functools.partial(_ssd_prompt_body, d_inner=d_inner, n_heads=n_heads)
    return pl.pallas_call(
        body,
        grid=(n_seq, nc),
        in_specs=[pl.BlockSpec((L, pw), lambda b, c: (b * nc + c, 0)),
                  _full(alog.shape), _full(dx.shape), _full(nw.shape), _full(e.shape), _full(tril.shape),
                  pl.BlockSpec((None, rows, D_STATE), lambda b, c: (b, 0, 0))]
                 + [pl.BlockSpec(memory_space=pl.ANY)] * len(prev),
        out_specs=[pl.BlockSpec((L, d_inner), lambda b, c: (b * nc + c, 0)),
                   pl.BlockSpec((None, rows, D_STATE), lambda b, c: (layer * n_seq + b, 0, 0))],
        out_shape=[jax.ShapeDtypeStruct((m, d_inner), BF16),
                   jax.ShapeDtypeStruct((n_layers * n_seq, rows, D_STATE), F32)],
        input_output_aliases={7: 1} if prev else {},
        scratch_shapes=[pltpu.VMEM((rows, D_STATE), F32), pltpu.VMEM((L, d_inner), F32)],
        compiler_params=_cparams(("parallel", "arbitrary")),
        name="ssd_prompt",
    )(act, alog, dx, nw, e, tril, h0, *prev)


def _ssd_step_pre_body(proj_ref, hist_ref, cw_ref, cb_ref, dtb_ref, alog_ref, e_ref,
                       xs_ref, b_ref, c_ref, xdt_t_ref, dec_t_ref, clast_ref, *, d_inner):
    gn = SSM_GROUPS * D_STATE
    conv_dim = d_inner + 2 * gn
    m = proj_ref.shape[0]
    xbc = proj_ref[:, d_inner:d_inner + conv_dim]
    xc = hist_ref[0] * cw_ref[0:1, :] + cb_ref[...]
    for j in range(1, SSM_CONV - 1):
        xc = xc + hist_ref[j] * cw_ref[j:j + 1, :]
        clast_ref[j - 1] = hist_ref[j]
    xc = _silu(xc + xbc * cw_ref[SSM_CONV - 1:SSM_CONV, :])
    clast_ref[SSM_CONV - 2] = xbc
    xs = xc[:, :d_inner]
    xs_ref[...] = xs
    b_ref[...] = xc[:, d_inner:d_inner + gn]
    c_ref[...] = xc[:, d_inner + gn:]
    e = e_ref[...]
    dt = _softplus(proj_ref[:, d_inner + conv_dim:] + dtb_ref[...])
    dec = jnp.exp(dt * (-jnp.exp(alog_ref[...])))
    pad = jnp.zeros((LANE - m, d_inner), F32)
    xdt_t_ref[...] = jnp.concatenate([xs * _expand(dt, e), pad], axis=0).T
    dec_t_ref[...] = jnp.concatenate([_expand(dec, e), pad], axis=0).T


def _ssd_step_pre(proj, hist_t, cw, cb, dtb, alog, e, d_inner):
    m = proj.shape[0]
    gn = SSM_GROUPS * D_STATE
    conv_dim = cw.shape[1]
    k = SSM_CONV - 1
    args = (proj, hist_t, cw, cb, dtb, alog, e)
    return pl.pallas_call(
        functools.partial(_ssd_step_pre_body, d_inner=d_inner),
        grid=(1,),
        in_specs=[_full(a.shape) for a in args],
        out_specs=[_full((m, d_inner)), _full((m, gn)), _full((m, gn)), _full((d_inner, LANE)),
                   _full((d_inner, LANE)), _full((k, m, conv_dim))],
        out_shape=[jax.ShapeDtypeStruct((m, d_inner), F32), jax.ShapeDtypeStruct((m, gn), F32),
                   jax.ShapeDtypeStruct((m, gn), F32), jax.ShapeDtypeStruct((d_inner, LANE), F32),
                   jax.ShapeDtypeStruct((d_inner, LANE), F32), jax.ShapeDtypeStruct((k, m, conv_dim), F32)],
        compiler_params=_cparams(("arbitrary",)),
        name="ssd_step_pre",
    )(*args)


def _ssd_step_state_body(h0_ref, xdt_t_ref, dec_t_ref, b_ref, c_ref, *rest):
    hn_ref, y_t_ref = rest[-2:]
    m = h0_ref.shape[0]
    lane = lax.broadcasted_iota(jnp.int32, (SSM_HEAD_DIM, LANE), 1)
    acc = jnp.zeros((SSM_HEAD_DIM, LANE), F32)
    for b in range(m):
        h = h0_ref[b] * dec_t_ref[:, b:b + 1] + xdt_t_ref[:, b:b + 1] * b_ref[b:b + 1, :]
        hn_ref[b] = h
        ycol = jnp.sum(h * c_ref[b:b + 1, :], axis=1, keepdims=True)
        acc = jnp.where(lane == b, ycol, acc)
    y_t_ref[...] = acc


def _ssd_step_state(h_all, prev, layer, xdt_t, dec_t, bm, cm, n_heads):
    m = bm.shape[0]
    hpg = n_heads // SSM_GROUPS
    h_spec = pl.BlockSpec((m, None, SSM_HEAD_DIM, D_STATE), lambda j: (layer, j, 0, 0))
    return pl.pallas_call(
        _ssd_step_state_body,
        grid=(n_heads,),
        in_specs=[h_spec,
                  pl.BlockSpec((SSM_HEAD_DIM, LANE), lambda j: (j, 0)),
                  pl.BlockSpec((SSM_HEAD_DIM, LANE), lambda j: (j, 0)),
                  pl.BlockSpec((m, D_STATE), lambda j: (0, j // hpg)),
                  pl.BlockSpec((m, D_STATE), lambda j: (0, j // hpg))]
                 + [pl.BlockSpec(memory_space=pl.ANY)] * len(prev),
        out_specs=[h_spec, pl.BlockSpec((SSM_HEAD_DIM, LANE), lambda j: (j, 0))],
        out_shape=[jax.ShapeDtypeStruct(h_all.shape, F32),
                   jax.ShapeDtypeStruct((n_heads * SSM_HEAD_DIM, LANE), F32)],
        input_output_aliases={5: 0} if prev else {},
        compiler_params=_cparams(("parallel",)),
        name="ssd_step_state",
    )(h_all, xdt_t, dec_t, bm, cm, *prev)


def _ssd_step_post_body(y_t_ref, xs_ref, z_ref, dx_ref, nw_ref, o_ref):
    m, d_inner = xs_ref.shape
    gw = d_inner // SSM_GROUPS
    y = y_t_ref[...].T[:m, :] + xs_ref[...] * dx_ref[...]
    gated = y * _silu(z_ref[...])
    for g in range(SSM_GROUPS):
        seg = gated[:, g * gw:(g + 1) * gw]
        o_ref[:, g * gw:(g + 1) * gw] = _rms(seg, nw_ref[:, g * gw:(g + 1) * gw]).astype(BF16)


def _ssd_step_post(y_t, xs, proj, dx, nw):
    m, d_inner = xs.shape
    return pl.pallas_call(
        _ssd_step_post_body,
        grid=(1,),
        in_specs=[_full(y_t.shape), _full(xs.shape), pl.BlockSpec((m, d_inner), lambda i: (0, 0)),
                  _full(dx.shape), _full(nw.shape)],
        out_specs=_full((m, d_inner)),
        out_shape=jax.ShapeDtypeStruct((m, d_inner), BF16),
        compiler_params=_cparams(("arbitrary",)),
        name="ssd_step_post",
    )(y_t, xs, proj, dx, nw)


def _band_attn_body(q_ref, kc_ref, kp_ref, vc_ref, vp_ref, o_ref, l_ref, *, win, dil):
    blk = pl.program_id(1)
    n_hp, bt, _ = q_ref.shape
    span = win // dil
    heads = LANE // ATT_HEAD_DIM
    qi = lax.broadcasted_iota(jnp.int32, (span, 2 * span), 0)
    ki = lax.broadcasted_iota(jnp.int32, (span, 2 * span), 1)
    band = (ki >= qi) & (ki <= qi + span)
    band_first = band & ((blk > 0) | (ki >= span))
    lane_q = lax.broadcasted_iota(jnp.int32, (span, LANE), 1)
    lane_k = lax.broadcasted_iota(jnp.int32, (2 * span, LANE), 1)

    def rows(ref, hp, start, r):
        if dil == 1:
            return ref[hp, start:start + span, :]
        return ref[hp, pl.ds(start + r, span, stride=dil), :]

    for hp in range(n_hp):
        for wi in range(bt // win):
            for r in range(dil):
                q = (rows(q_ref, hp, wi * win, r) * (ATT_HEAD_DIM ** -0.5)).astype(BF16)
                if wi == 0:
                    k_prev = rows(kp_ref, hp, kp_ref.shape[1] - win, r)
                    v_prev = rows(vp_ref, hp, vp_ref.shape[1] - win, r)
                else:
                    k_prev = rows(kc_ref, hp, (wi - 1) * win, r)
                    v_prev = rows(vc_ref, hp, (wi - 1) * win, r)
                kk = jnp.concatenate([k_prev, rows(kc_ref, hp, wi * win, r)], axis=0).astype(BF16)
                vv = jnp.concatenate([v_prev, rows(vc_ref, hp, wi * win, r)], axis=0).astype(BF16)
                valid = band_first if wi == 0 else band
                o_acc, l_acc = None, None
                for h in range(heads):
                    in_q = (lane_q >= h * ATT_HEAD_DIM) & (lane_q < (h + 1) * ATT_HEAD_DIM)
                    in_k = (lane_k >= h * ATT_HEAD_DIM) & (lane_k < (h + 1) * ATT_HEAD_DIM)
                    s = jnp.where(valid, _dot_nt(jnp.where(in_q, q, jnp.zeros_like(q)), kk), -jnp.inf)
                    mx = jnp.max(s, axis=-1, keepdims=True)
                    p = jnp.exp(s - mx)
                    den = jnp.sum(p, axis=-1, keepdims=True)
                    o_h = _dot(p.astype(BF16), jnp.where(in_k, vv, jnp.zeros_like(vv))) / den
                    l_h = jnp.where(in_q, mx + jnp.log(den), 0.0)
                    o_acc = o_h if o_acc is None else o_acc + o_h
                    l_acc = l_h if l_acc is None else l_acc + l_h
                if dil == 1:
                    o_ref[hp, wi * win:wi * win + span, :] = o_acc
                    l_ref[hp, wi * win:wi * win + span, :] = l_acc
                else:
                    o_ref[hp, pl.ds(wi * win + r, span, stride=dil), :] = o_acc
                    l_ref[hp, pl.ds(wi * win + r, span, stride=dil), :] = l_acc


def _band_attn(qkv_c, n_seq, gi, win, dil):
    nch, m, _ = qkv_c.shape
    t_len = m // n_seq
    bt = max(win, min(BAND_BLOCK, t_len))
    assert bt % win == 0 and t_len % bt == 0
    nblk = t_len // bt
    wpb = bt // win
    per_grp = ATT_GW // LANE
    ng = len(ATT_GROUPS)

    def cur(sec):
        return pl.BlockSpec((per_grp, bt, LANE), lambda b, k: (sec * ng + gi, b * nblk + k, 0))

    def prev(sec):
        return pl.BlockSpec((per_grp, win, LANE),
                            lambda b, k: (sec * ng + gi, jnp.maximum((b * nblk + k) * wpb - 1, 0), 0))

    out_spec = pl.BlockSpec((per_grp, bt, LANE), lambda b, k: (0, b * nblk + k, 0))
    out_sd = jax.ShapeDtypeStruct((per_grp, m, LANE), F32)
    return pl.pallas_call(
        functools.partial(_band_attn_body, win=win, dil=dil),
        grid=(n_seq, nblk),
        in_specs=[cur(0), cur(1), prev(1), cur(2), prev(2)],
        out_specs=[out_spec, out_spec],
        out_shape=[out_sd, out_sd],
        compiler_params=_cparams(("parallel", "arbitrary")),
        name="band_attn_w%d" % win,
    )(qkv_c, qkv_c, qkv_c, qkv_c, qkv_c)


def _row_to_cols(v):
    return jnp.concatenate([jnp.broadcast_to(v[:, c * LANE:(c + 1) * LANE], (LANE, LANE)).T
                            for c in range(v.shape[1] // LANE)], axis=0)


def _col_to_row(v):
    return jnp.concatenate([jnp.broadcast_to(v[c * LANE:(c + 1) * LANE, :], (LANE, LANE)).T[0:1, :]
                            for c in range(v.shape[0] // LANE)], axis=1)


def _attn_sample_body(*refs, dils, n_prev):
    ng = len(dils)
    qkv_ref = refs[0]
    bufs = refs[1:1 + ng]
    outs = refs[1 + ng + n_prev:]
    o_refs, l_refs, c_refs = outs[0:ng], outs[ng:2 * ng], outs[2 * ng:3 * ng]
    scale = ATT_HEAD_DIM ** -0.5
    for gi in range(ng):
        buf, o_ref, l_ref, c_ref = bufs[gi], o_refs[gi], l_refs[gi], c_refs[gi]
        wb = buf.shape[2]
        nch = wb // LANE
        q_c = _row_to_cols(qkv_ref[:, gi * ATT_GW:(gi + 1) * ATT_GW] * scale)
        kn_c = _row_to_cols(qkv_ref[:, (ng + gi) * ATT_GW:(ng + gi + 1) * ATT_GW])
        vn_c = _row_to_cols(qkv_ref[:, (2 * ng + gi) * ATT_GW:(2 * ng + gi + 1) * ATT_GW])
        lane = lax.broadcasted_iota(jnp.int32, (1, wb), 1)
        valid = (lane & (dils[gi] - 1)) == 0
        o_cols = []
        for h in range(ATT_HPG):
            rs = slice(h * ATT_HEAD_DIM, (h + 1) * ATT_HEAD_DIM)
            qh = q_c[rs, :]
            s = jnp.concatenate([jnp.sum(buf[0, rs, c * LANE:(c + 1) * LANE] * qh, axis=0, keepdims=True)
                                 for c in range(nch)], axis=1)
            s = jnp.where(valid, s, -jnp.inf)
            sn = jnp.sum(qh[:, 0:1] * kn_c[rs, 0:1], axis=0, keepdims=True)
            mx = jnp.maximum(jnp.max(s, axis=1, keepdims=True), sn)
            p = jnp.exp(s - mx)
            p_new = jnp.exp(sn - mx)
            den = jnp.sum(p, axis=1, keepdims=True) + p_new
            acc = buf[1, rs, 0:LANE] * p[:, 0:LANE]
            for c in range(1, nch):
                acc = acc + buf[1, rs, c * LANE:(c + 1) * LANE] * p[:, c * LANE:(c + 1) * LANE]
            o_cols.append((jnp.sum(acc, axis=1, keepdims=True) + p_new * vn_c[rs, 0:1]) / den)
            l_ref[:, rs] = jnp.broadcast_to(mx + jnp.log(den), (1, ATT_HEAD_DIM))
        o_ref[...] = _col_to_row(jnp.concatenate(o_cols, axis=0))
        last = lax.broadcasted_iota(jnp.int32, (ATT_HEAD_DIM, LANE), 1) == LANE - 1
        for kv, new_c in ((0, kn_c), (1, vn_c)):
            for h in range(ATT_HPG):
                rs = slice(h * ATT_HEAD_DIM, (h + 1) * ATT_HEAD_DIM)
                rolled = pltpu.roll(buf[kv, rs, :], wb - 1, 1)
                if nch > 1:
                    c_ref[kv, rs, 0:wb - LANE] = rolled[:, 0:wb - LANE]
                c_ref[kv, rs, wb - LANE:wb] = jnp.where(last, new_c[rs, :], rolled[:, wb - LANE:wb])


def _attn_sample(qkv, cache_views, prev, layer):
    m, w3 = qkv.shape
    ng = len(ATT_GROUPS)
    dils = tuple(dil for _, dil in ATT_GROUPS)
    for v, (win, dil) in zip(cache_views, ATT_GROUPS):
        assert v.shape[3] == win and win % LANE == 0 and dil & (dil - 1) == 0
    c_specs = [pl.BlockSpec((None, 2, ATT_GW, v.shape[3]), lambda b: (layer * m + b, 0, 0, 0)) for v in cache_views]
    o_sd = jax.ShapeDtypeStruct((m, 1, ATT_GW), F32)
    o_spec = pl.BlockSpec((None, 1, ATT_GW), lambda b: (b, 0, 0))
    n_in = 1 + ng
    res = pl.pallas_call(
        functools.partial(_attn_sample_body, dils=dils, n_prev=len(prev)),
        grid=(m,),
        in_specs=[pl.BlockSpec((None, 1, w3), lambda b: (b, 0, 0))] + c_specs
                 + [pl.BlockSpec(memory_space=pl.ANY)] * len(prev),
        out_specs=[o_spec] * (2 * ng) + c_specs,
        out_shape=[o_sd] * (2 * ng) + [jax.ShapeDtypeStruct(v.shape, v.dtype) for v in cache_views],
        input_output_aliases={n_in + k: 2 * ng + k for k in range(len(prev))},
        compiler_params=_cparams(("parallel",)),
        name="attn_sample",
    )(qkv.reshape(m, 1, w3), *cache_views, *prev)
    os_ = [r.reshape(m, ATT_GW) for r in res[0:ng]]
    ls_ = [r.reshape(m, ATT_GW) for r in res[ng:2 * ng]]
    return os_, ls_, list(res[2 * ng:])


def _prep_weights(norms, ssm_w_in, ssm_conv_b, ssm_dt_bias, ssm_a_log, ssm_d, ssm_norm_w, ssm_w_out, att_w_qkv,
                  att_w_o, mem_norm, xa_w_q, xa_w_kv, xa_w_o, ffn_w_gu, ffn_conv_b, ffn_w_down):
    n_ssm, d_model, in_dim = ssm_w_in.shape
    n_heads = ssm_dt_bias.shape[1]
    d_inner = n_heads * SSM_HEAD_DIM
    d_ff = ffn_w_down.shape[1]
    pad_heads = LANE - n_heads
    w = {}
    w['n_heads'], w['d_inner'] = n_heads, d_inner
    w['norms'] = norms[:, :, None, :]
    w['ssm_w_in'] = jnp.pad(ssm_w_in, ((0, 0), (0, 0), (0, pad_heads))).astype(BF16)
    w['ssm_conv_b'] = ssm_conv_b[:, None, :]
    w['ssm_dt_bias'] = jnp.pad(ssm_dt_bias, ((0, 0), (0, pad_heads)))[:, None, :]
    w['ssm_a_log'] = jnp.pad(ssm_a_log, ((0, 0), (0, pad_heads)))[:, None, :]
    w['ssm_dx'] = jnp.repeat(ssm_d, SSM_HEAD_DIM, axis=1)[:, None, :]
    w['ssm_norm_w'] = ssm_norm_w[:, None, :]
    w['ssm_w_out'] = ssm_w_out.astype(BF16)
    w['att_w_qkv'] = att_w_qkv.astype(BF16)
    w['att_w_o'] = att_w_o.astype(BF16)
    w['mem_norm'] = mem_norm[:, None, :]
    w['xa_w_q'] = xa_w_q.astype(BF16)
    w['xa_w_kv'] = xa_w_kv.astype(BF16)
    w['xa_w_o'] = xa_w_o.astype(BF16)
    w['ffn_w_g'] = ffn_w_gu[:, :, :d_ff].astype(BF16)
    w['ffn_w_u'] = ffn_w_gu[:, :, d_ff:].astype(BF16)
    w['ffn_conv_b'] = ffn_conv_b[:, None, :]
    w['ffn_w_down'] = ffn_w_down.astype(BF16)
    e = np.zeros((LANE, d_inner), np.float32)
    for h in range(n_heads):
        e[h, h * SSM_HEAD_DIM:(h + 1) * SSM_HEAD_DIM] = 1.0
    w['expand'] = jnp.asarray(e, BF16)
    w['tril'] = jnp.asarray(np.tril(np.ones((SSM_CHUNK, SSM_CHUNK), np.float32)), BF16)
    return w


def _prompt_trunk(x3, mem3, w, ssm_conv_w, ffn_conv_w):
    n, t_len, d = x3.shape
    assert t_len % SSM_CHUNK == 0 and all(t_len % win == 0 for win, _ in ATT_GROUPS)
    depth = w['norms'].shape[0]
    n_heads, d_inner = w['n_heads'], w['d_inner']
    conv_dim = ssm_conv_w.shape[2]
    d_ff = ffn_conv_w.shape[2]
    tm = 256
    x = x3.reshape(n * t_len, d)
    mem = mem3.reshape(-1, d)
    kv_all, kv_rows = _mem_kv(mem, w['mem_norm'], w['xa_w_kv'], tm)
    tabs = _rope_tables(jnp.arange(t_len, dtype=jnp.int32))
    zero_hist = jnp.zeros((n, SSM_CONV - 1, conv_dim), F32)
    zero_h = jnp.zeros((n, n_heads * SSM_HEAD_DIM, D_STATE), F32)
    zero_fh = jnp.zeros((n, FFN_CONV - 1, d_ff), F32)
    n_ssm, n_att = (depth + 1) // 2, depth // 2
    new = {'ssm': [], 'ssm_conv': [], 'swa': [], 'ffn_conv': []}
    for i in range(depth):
        j = i // 2
        g = w['norms'][i]
        if i % 2 == 0:
            act, c_last = _ssd_in_proj(x, g[0], w['ssm_w_in'][j], ssm_conv_w[j], w['ssm_conv_b'][j],
                                       w['ssm_dt_bias'][j], zero_hist, n, d_inner, tm)
            y, h_last = _ssd_prompt(act, w['ssm_a_log'][j], w['ssm_dx'][j], w['ssm_norm_w'][j], w['expand'],
                                    w['tril'], zero_h, new['ssm'], j, n_ssm, n, d_inner, n_heads)
            x = _proj_res(y, w['ssm_w_out'][j], x, g[1], tm)
            new['ssm'] = [h_last]
            new['ssm_conv'].append(c_last)
        else:
            qkv_c, new['swa'] = _norm_qkv_prompt(x, g[0], w['att_w_qkv'][j], tabs, new['swa'], j, n_att, n, tm)
            os_, ls_ = [], []
            for gi, (win, dil) in enumerate(ATT_GROUPS):
                o, l = _band_attn(qkv_c, n, gi, win, dil)
                os_.append(o)
                ls_.append(l)
            x = _attn_out(os_, ls_, w['att_w_o'][j], x, g[1], tm)
        x = _xattn_prompt(x, g[2], g[3], w['xa_w_q'][i], kv_all[i], w['xa_w_o'][i], n, tm)
        x, f_hist = _ffn_prompt(x, g[4], g[5], w['ffn_w_g'][i], w['ffn_w_u'][i], ffn_conv_w[i],
                                w['ffn_conv_b'][i], w['ffn_w_down'][i], zero_fh, n, tm)
        new['ffn_conv'].append(f_hist)
    n_mem = mem3.shape[1]
    hd = d // MEM_HEADS
    swa = [jnp.transpose(c.reshape(n_att, n, 2, ATT_HPG, ATT_HEAD_DIM, c.shape[3]), (0, 1, 5, 2, 3, 4))
           for c in new['swa']]
    p_mem = jnp.transpose(kv_rows.reshape(depth, n, n_mem, 2, hd // LANE, MEM_HEADS, LANE),
                          (0, 1, 2, 3, 5, 4, 6)).reshape(depth, n, n_mem, 2, MEM_HEADS, hd)
    return (x.reshape(n, t_len, d), new['ssm'][0].reshape(n_ssm, n, n_heads, SSM_HEAD_DIM, D_STATE),
            jnp.stack(new['ssm_conv']), swa, p_mem, jnp.stack(new['ffn_conv']))


def _sample_trunk(x3, w, ssm_conv_w, ffn_conv_w, state_ssm, state_ssm_conv, caches, cache_mem_kv, state_ffn_conv):
    m, t_len, d = x3.shape
    assert t_len == 1
    depth = w['norms'].shape[0]
    n_heads, d_inner = w['n_heads'], w['d_inner']
    n_ssm = state_ssm.shape[0]
    x = x3.reshape(m, d)
    tabs = _rope_tables(jnp.full((m,), PAST_LEN, jnp.int32))
    h_all = state_ssm.reshape(n_ssm * m, n_heads, SSM_HEAD_DIM, D_STATE)
    n_mem, hd = cache_mem_kv.shape[2], d // MEM_HEADS
    kv_all = jnp.transpose(cache_mem_kv.reshape(depth, m, n_mem, 2, MEM_HEADS, hd // LANE, LANE),
                           (0, 1, 2, 3, 5, 4, 6)).reshape(depth * m, n_mem * 2 * MEM_HEADS * (hd // LANE), LANE)
    cache_views = [jnp.transpose(c, (0, 1, 3, 4, 5, 2)).reshape(c.shape[0] * m, 2, ATT_GW, c.shape[2])
                   for c in caches]
    new_caches = []
    new = {'ssm': [], 'ssm_conv': [], 'ffn_conv': []}
    for i in range(depth):
        j = i // 2
        g = w['norms'][i]
        if i % 2 == 0:
            proj = _norm_proj(x, g[0], w['ssm_w_in'][j], m)
            hist_t = jnp.swapaxes(state_ssm_conv[j], 0, 1)
            xs, bm, cm, xdt_t, dec_t, c_last_t = _ssd_step_pre(proj, hist_t, ssm_conv_w[j], w['ssm_conv_b'][j],
                                                                w['ssm_dt_bias'][j], w['ssm_a_log'][j],
                                                                w['expand'], d_inner)
            h_new, y_t = _ssd_step_state(h_all, new['ssm'], j, xdt_t, dec_t, bm, cm, n_heads)
            y = _ssd_step_post(y_t, xs, proj, w['ssm_dx'][j], w['ssm_norm_w'][j])
            x = _proj_res(y, w['ssm_w_out'][j], x, g[1], m)
            new['ssm'] = [h_new]
            new['ssm_conv'].append(jnp.swapaxes(c_last_t, 0, 1))
        else:
            qkv = _norm_qkv_rope(x, g[0], w['att_w_qkv'][j], tabs, m, 1)
            os_, ls_, new_caches = _attn_sample(qkv, cache_views, new_caches, j)
            x = _attn_out(os_, ls_, w['att_w_o'][j], x, g[1], m)
        x = _xattn_sample(x, g[2], g[3], w['xa_w_q'][i], kv_all, i, w['xa_w_o'][i])
        hist_t = jnp.swapaxes(state_ffn_conv[i], 0, 1)
        x, f_hist_t = _ffn_sample(x, g[4], g[5], w['ffn_w_g'][i], w['ffn_w_u'][i], ffn_conv_w[i],
                                  w['ffn_conv_b'][i], w['ffn_w_down'][i], hist_t, 256)
        new['ffn_conv'].append(jnp.swapaxes(f_hist_t, 0, 1))
    new_caches = [jnp.transpose(nc.reshape(c.shape[0], m, 2, ATT_HPG, ATT_HEAD_DIM, c.shape[2]), (0, 1, 5, 2, 3, 4))
                  for nc, c in zip(new_caches, caches)]
    return (x.reshape(m, 1, d), new['ssm'][0].reshape(state_ssm.shape), jnp.stack(new['ssm_conv']), new_caches,
            jnp.stack(new['ffn_conv']))


def kernel(x_prompt, x_sample, mem_prompt, state_ssm, state_ssm_conv, cache_swa_kv_w128, cache_swa_kv_w512,
           cache_swa_kv_w2048, cache_mem_kv, state_ffn_conv, norms, ssm_w_in, ssm_conv_w, ssm_conv_b,
           ssm_dt_bias, ssm_a_log, ssm_d, ssm_norm_w, ssm_w_out, att_w_qkv, att_w_o, mem_norm, xa_w_q,
           xa_w_kv, xa_w_o, ffn_w_gu, ffn_conv_w, ffn_conv_b, ffn_w_down):
    w = _prep_weights(norms, ssm_w_in, ssm_conv_b, ssm_dt_bias, ssm_a_log, ssm_d, ssm_norm_w, ssm_w_out,
                      att_w_qkv, att_w_o, mem_norm, xa_w_q, xa_w_kv, xa_w_o, ffn_w_gu, ffn_conv_b, ffn_w_down)
    caches = [cache_swa_kv_w128, cache_swa_kv_w512, cache_swa_kv_w2048]
    yp, p_ssm, p_conv, p_swa, p_mem, p_ffn = _prompt_trunk(x_prompt, mem_prompt, w, ssm_conv_w, ffn_conv_w)
    ys, s_ssm, s_conv, s_swa, s_ffn = _sample_trunk(x_sample, w, ssm_conv_w, ffn_conv_w, state_ssm,
                                                    state_ssm_conv, caches, cache_mem_kv, state_ffn_conv)
    return (yp, ys, p_ssm, p_conv, p_swa[0], p_swa[1], p_swa[2], p_mem, p_ffn,
            s_ssm, s_conv, s_swa[0], s_swa[1], s_swa[2], s_ffn)
```

```python
import functools
import math

import numpy as np
import jax
import jax.numpy as jnp
from jax import lax
from jax.experimental import pallas as pl
from jax.experimental.pallas import tpu as pltpu

F32 = jnp.float32
BF16 = jnp.bfloat16

EPS = 1e-6
PAST_LEN = 8192
SSM_HEAD_DIM = 64
SSM_GROUPS = 4
D_STATE = 128
SSM_CONV = 4
SSM_CHUNK = 128
CONV_COLS = 1024
ATT_GROUPS = ((128, 1), (512, 4), (2048, 16))
ATT_HPG = 4
ATT_HEAD_DIM = 64
ATT_GW = ATT_HPG * ATT_HEAD_DIM
BAND_BLOCK = 512
ROT_DIM = ATT_HEAD_DIM // 4
ROPE_THETA = 500000.0
MEM_HEADS = 4
FFN_CONV = 3

LANE = 128
SUBLANE = 8
VMEM_LIMIT = 56 * 1024 * 1024
PROMPT_ROWS = 512
PROMPT_ROWS_SSD = 256


def _cparams(sem):
    return pltpu.CompilerParams(dimension_semantics=sem, vmem_limit_bytes=VMEM_LIMIT)


def _rms(x, g):
    return x * lax.rsqrt(jnp.mean(x * x, axis=-1, keepdims=True) + EPS) * g


def _silu(x):
    return x / (1.0 + jnp.exp(-x))


def _softplus(x):
    return jnp.maximum(x, 0.0) + jnp.log(1.0 + jnp.exp(-jnp.abs(x)))


def _dot(a, b):
    return jnp.dot(a, b, preferred_element_type=F32)


def _dot_nt(a, b):
    return lax.dot_general(a, b, (((1,), (1,)), ((), ())), preferred_element_type=F32)


def _split3(v):
    hi = v.astype(BF16)
    r = v - hi.astype(F32)
    mid = r.astype(BF16)
    lo = (r - mid.astype(F32)).astype(BF16)
    return hi, mid, lo


def _expand(v, e):
    hi, mid, lo = _split3(v)
    return _dot(hi, e) + _dot(mid, e) + _dot(lo, e)


def _full(shape):
    return pl.BlockSpec(shape, lambda *_: (0,) * len(shape))


def _resident(shape):
    return pl.BlockSpec(shape, lambda *_: (0,) * len(shape), pipeline_mode=pl.Buffered(1))


def _norm_proj_body(x_ref, g_ref, w_ref, o_ref):
    u = _rms(x_ref[...], g_ref[...]).astype(BF16)
    o_ref[...] = _dot(u, w_ref[...])


def _norm_proj(x, g, w, tm):
    m, d = x.shape
    n = w.shape[1]
    return pl.pallas_call(
        _norm_proj_body,
        grid=(m // tm,),
        in_specs=[pl.BlockSpec((tm, d), lambda i: (i, 0)), _full((1, d)), _full((d, n))],
        out_specs=pl.BlockSpec((tm, n), lambda i: (i, 0)),
        out_shape=jax.ShapeDtypeStruct((m, n), F32),
        compiler_params=_cparams(("parallel",)),
        name="norm_proj",
    )(x, g, w)


def _norm_qkv_rope_body(x_ref, g_ref, w_ref, cos_ref, sa_ref, sb_ref, o_ref, *, n_rot):
    u = _rms(x_ref[...], g_ref[...]).astype(BF16)
    y = _dot(u, w_ref[...])
    cos, sa, sb = cos_ref[...], sa_ref[...], sb_ref[...]
    for c in range(n_rot // LANE):
        t = y[:, c * LANE:(c + 1) * LANE]
        o_ref[:, c * LANE:(c + 1) * LANE] = (t * cos + pltpu.roll(t, LANE - ROT_DIM // 2, 1) * sa
                                             + pltpu.roll(t, ROT_DIM // 2, 1) * sb)
    o_ref[:, n_rot:] = y[:, n_rot:]


def _norm_qkv_rope(x, g, w, tabs, tm, seq_blocks):
    m, d = x.shape
    n = w.shape[1]
    tab_spec = pl.BlockSpec((tm, LANE), lambda i: (i % seq_blocks, 0))
    return pl.pallas_call(
        functools.partial(_norm_qkv_rope_body, n_rot=2 * n // 3),
        grid=(m // tm,),
        in_specs=[pl.BlockSpec((tm, d), lambda i: (i, 0)), _full((1, d)), _full((d, n)),
                  tab_spec, tab_spec, tab_spec],
        out_specs=pl.BlockSpec((tm, n), lambda i: (i, 0)),
        out_shape=jax.ShapeDtypeStruct((m, n), F32),
        compiler_params=_cparams(("parallel",)),
        name="norm_qkv_rope",
    )(x, g, w, *tabs)


def _rope_chunk(t, cos, sa, sb):
    return t * cos + pltpu.roll(t, LANE - ROT_DIM // 2, 1) * sa + pltpu.roll(t, ROT_DIM // 2, 1) * sb


def _norm_qkv_prompt_body(*refs, n_rot, n_prev, nblks):
    x_ref, g_ref, w_ref, cos_ref, sa_ref, sb_ref = refs[:6]
    o_ref = refs[6 + n_prev]
    c_refs = refs[7 + n_prev:]
    ng = len(c_refs)
    t = pl.program_id(1)
    tb = pl.num_programs(1)
    tm = x_ref.shape[0]
    u = _rms(x_ref[...], g_ref[...]).astype(BF16)
    y = _dot(u, w_ref[...])
    cos, sa, sb = cos_ref[...], sa_ref[...], sb_ref[...]
    vals = []
    for c in range(y.shape[1] // LANE):
        v = y[:, c * LANE:(c + 1) * LANE]
        if c * LANE < n_rot:
            v = _rope_chunk(v, cos, sa, sb)
        o_ref[c] = v
        vals.append(v)
    per_sec = len(vals) // 3
    per_grp = ATT_GW // LANE
    for gi, c_ref in enumerate(c_refs):
        kw = c_ref.shape[2]
        nblk = nblks[gi]

        @pl.when(t >= tb - nblk)
        def _(gi=gi, c_ref=c_ref, kw=kw):
            for kv in range(2):
                for hp in range(per_grp):
                    v = vals[(1 + kv) * per_sec + gi * per_grp + hp]
                    c_ref[kv, hp * LANE:(hp + 1) * LANE, :] = v[tm - kw:, :].T


def _norm_qkv_prompt(x, g, w, tabs, prev, layer, n_layers, n_seq, tm):
    m, d = x.shape
    n = w.shape[1]
    t_len = m // n_seq
    tb = t_len // tm
    tab_spec = pl.BlockSpec((tm, LANE), lambda b, t: (t, 0))
    c_specs, c_shapes, nblks = [], [], []
    for gi, (win, _) in enumerate(ATT_GROUPS):
        keep = min(win, t_len)
        kw = min(keep, tm)
        nblk = keep // kw
        assert keep % kw == 0
        nblks.append(nblk)
        c_specs.append(pl.BlockSpec((None, 2, ATT_GW, kw),
                                    lambda b, t, nblk=nblk: (layer * n_seq + b, 0, 0, jnp.maximum(t - (tb - nblk), 0))))
        c_shapes.append(jax.ShapeDtypeStruct((n_layers * n_seq, 2, ATT_GW, keep), F32))
    res = pl.pallas_call(
        functools.partial(_norm_qkv_prompt_body, n_rot=2 * n // 3, n_prev=len(prev), nblks=tuple(nblks)),
        grid=(n_seq, tb),
        in_specs=[pl.BlockSpec((tm, d), lambda b, t: (b * tb + t, 0)), _full((1, d)), _resident((d, n)),
                  tab_spec, tab_spec, tab_spec] + [pl.BlockSpec(memory_space=pl.ANY)] * len(prev),
        out_specs=[pl.BlockSpec((n // LANE, tm, LANE), lambda b, t: (0, b * tb + t, 0))] + c_specs,
        out_shape=[jax.ShapeDtypeStruct((n // LANE, m, LANE), F32)] + c_shapes,
        input_output_aliases={6 + k: 1 + k for k in range(len(prev))},
        compiler_params=_cparams(("parallel", "arbitrary")),
        name="norm_qkv_prompt",
    )(x, g, w, *tabs, *prev)
    return res[0], list(res[1:])


def _rope_tables(pos):
    half = ROT_DIM // 2
    inv = ROPE_THETA ** (-jnp.arange(half, dtype=F32) / half)
    ang = pos.astype(F32)[:, None] * inv[None, :]
    cos, sin = jnp.cos(ang), jnp.sin(ang)
    p = pos.shape[0]
    rest = ATT_HEAD_DIM - ROT_DIM
    c = jnp.concatenate([cos, cos, jnp.ones((p, rest), F32)], axis=1)
    sa = jnp.concatenate([-sin, jnp.zeros((p, half + rest), F32)], axis=1)
    sb = jnp.concatenate([jnp.zeros((p, half), F32), sin, jnp.zeros((p, rest), F32)], axis=1)
    rep = LANE // ATT_HEAD_DIM
    return tuple(jnp.tile(t, (1, rep)) for t in (c, sa, sb))


def _proj_res_body(y_ref, w_ref, x_ref, g_ref, o_ref):
    f = _dot(y_ref[...].astype(BF16), w_ref[...])
    o_ref[...] = x_ref[...] + _rms(f, g_ref[...])


def _proj_res(y, w, x, g, tm):
    m, k = y.shape
    d = w.shape[1]
    return pl.pallas_call(
        _proj_res_body,
        grid=(m // tm,),
        in_specs=[pl.BlockSpec((tm, k), lambda i: (i, 0)), _full((k, d)),
                  pl.BlockSpec((tm, d), lambda i: (i, 0)), _full((1, d))],
        out_specs=pl.BlockSpec((tm, d), lambda i: (i, 0)),
        out_shape=jax.ShapeDtypeStruct((m, d), F32),
        compiler_params=_cparams(("parallel",)),
        name="proj_res",
    )(y, w, x, g)


def _attn_out_body(o0, o1, o2, l0, l1, l2, w_ref, x_ref, g_ref, out_ref):
    def load(ref):
        if len(ref.shape) == 2:
            return ref[...]
        return jnp.concatenate([ref[c] for c in range(ref.shape[0])], axis=1)

    ls = [load(l0), load(l1), load(l2)]
    mx = jnp.maximum(jnp.maximum(ls[0], ls[1]), ls[2])
    es = [jnp.exp(l - mx) for l in ls]
    den = es[0] + es[1] + es[2]
    og = jnp.concatenate([(es[gi] / den * load(o_ref)).astype(BF16) for gi, o_ref in enumerate((o0, o1, o2))], axis=1)
    out_ref[...] = x_ref[...] + _rms(_dot(og, w_ref[...]), g_ref[...])


def _attn_out(os_, ls_, w, x, g, tm):
    m, d = x.shape
    if os_[0].ndim == 2:
        blk = pl.BlockSpec((tm, ATT_GW), lambda i: (i, 0))
    else:
        blk = pl.BlockSpec((ATT_GW // LANE, tm, LANE), lambda i: (0, i, 0))
    return pl.pallas_call(
        _attn_out_body,
        grid=(m // tm,),
        in_specs=[blk] * 6 + [_full(w.shape), pl.BlockSpec((tm, d), lambda i: (i, 0)), _full((1, d))],
        out_specs=pl.BlockSpec((tm, d), lambda i: (i, 0)),
        out_shape=jax.ShapeDtypeStruct((m, d), F32),
        compiler_params=_cparams(("parallel",)),
        name="attn_out",
    )(*os_, *ls_, w, x, g)


def _mem_kv_body(x_ref, g_ref, w_ref, o_ref, t_ref):
    tm = x_ref.shape[0]
    u = _rms(x_ref[...], g_ref[...]).astype(BF16)
    y = _dot(u, w_ref[...])
    o_ref[...] = y
    hd = y.shape[1] // (2 * MEM_HEADS)
    nch = hd // LANE
    tok_rows = 2 * nch * MEM_HEADS
    for kv in range(2):
        for h in range(MEM_HEADS):
            for c in range(nch):
                col = (kv * MEM_HEADS + h) * hd + c * LANE
                t_ref[pl.ds((kv * nch + c) * MEM_HEADS + h, tm, stride=tok_rows), :] = y[:, col:col + LANE]


def _mem_kv(mem, g, w, tm):
    m, d = mem.shape
    depth, _, n = w.shape
    tok_rows = n // LANE
    return pl.pallas_call(
        _mem_kv_body,
        grid=(depth, m // tm),
        in_specs=[pl.BlockSpec((tm, d), lambda l, i: (i, 0)),
                  pl.BlockSpec((None, 1, d), lambda l, i: (l, 0, 0)),
                  pl.BlockSpec((None, d, n), lambda l, i: (l, 0, 0))],
        out_specs=[pl.BlockSpec((None, tm, n), lambda l, i: (l, i, 0)),
                   pl.BlockSpec((None, tm * tok_rows, LANE), lambda l, i: (l, i, 0))],
        out_shape=[jax.ShapeDtypeStruct((depth, m, n), F32),
                   jax.ShapeDtypeStruct((depth, m * tok_rows, LANE), F32)],
        compiler_params=_cparams(("parallel", "parallel")),
        name="mem_kv",
    )(mem, g, w)


def _xattn_prompt_body(x_ref, gpre_ref, gpost_ref, wq_ref, kv_ref, wo_ref, o_ref, obuf):
    x = x_ref[...]
    d = x.shape[1]
    hd = d // MEM_HEADS
    u = _rms(x, gpre_ref[...]).astype(BF16)
    q = _dot(u, wq_ref[...]).astype(BF16)
    scale = hd ** -0.5
    for h in range(MEM_HEADS):
        kh = kv_ref[:, h * hd:(h + 1) * hd].astype(BF16)
        vh = kv_ref[:, d + h * hd:d + (h + 1) * hd].astype(BF16)
        s = _dot_nt(q[:, h * hd:(h + 1) * hd], kh) * scale
        mx = jnp.max(s, axis=-1, keepdims=True)
        p = jnp.exp(s - mx)
        den = jnp.sum(p, axis=-1, keepdims=True)
        obuf[:, h * hd:(h + 1) * hd] = (_dot(p.astype(BF16), vh) / den).astype(BF16)
    f = _dot(obuf[...], wo_ref[...])
    o_ref[...] = x + _rms(f, gpost_ref[...])


def _xattn_prompt(x, gpre, gpost, wq, kv, wo, n_seq, tm):
    m, d = x.shape
    tb = m // n_seq // tm
    n_mem = kv.shape[0] // n_seq
    return pl.pallas_call(
        _xattn_prompt_body,
        grid=(n_seq, tb),
        in_specs=[pl.BlockSpec((tm, d), lambda b, t: (b * tb + t, 0)), _full((1, d)), _full((1, d)),
                  _resident((d, d)), pl.BlockSpec((n_mem, 2 * d), lambda b, t: (b, 0)), _resident((d, d))],
        out_specs=pl.BlockSpec((tm, d), lambda b, t: (b * tb + t, 0)),
        out_shape=jax.ShapeDtypeStruct((m, d), F32),
        scratch_shapes=[pltpu.VMEM((tm, d), BF16)],
        compiler_params=_cparams(("parallel", "parallel")),
        name="xattn_prompt",
    )(x, gpre, gpost, wq, kv, wo)


def _xattn_sample_body(x_ref, gpre_ref, gpost_ref, wq_ref, kv_ref, wo_ref, o_ref, q_scr, o_scr):
    b = pl.program_id(0)
    d = x_ref.shape[1]
    hd = d // MEM_HEADS

    @pl.when(b == 0)
    def _():
        u = _rms(x_ref[...], gpre_ref[...]).astype(BF16)
        q_scr[...] = _dot(u, wq_ref[...]) * (hd ** -0.5)

    qb = q_scr[pl.ds(b, 1), :]
    nch = hd // LANE
    tok_rows = 2 * nch * MEM_HEADS
    n_mem = kv_ref.shape[0] // tok_rows
    o_parts = []
    for h in range(MEM_HEADS):
        s = None
        for c in range(nch):
            kc = kv_ref[pl.ds(c * MEM_HEADS + h, n_mem, stride=tok_rows), :]
            part = jnp.sum(kc * qb[:, h * hd + c * LANE:h * hd + (c + 1) * LANE], axis=1, keepdims=True)
            s = part if s is None else s + part
        mx = jnp.max(s, axis=0, keepdims=True)
        p = jnp.exp(s - mx)
        den = jnp.sum(p, axis=0, keepdims=True)
        for c in range(nch):
            vc = kv_ref[pl.ds((nch + c) * MEM_HEADS + h, n_mem, stride=tok_rows), :]
            o_parts.append(jnp.sum(p * vc, axis=0, keepdims=True) / den)
    o_scr[pl.ds(b, 1), :] = jnp.concatenate(o_parts, axis=1)

    @pl.when(b == pl.num_programs(0) - 1)
    def _():
        f = _dot(o_scr[...].astype(BF16), wo_ref[...])
        o_ref[...] = x_ref[...] + _rms(f, gpost_ref[...])


def _xattn_sample(x, gpre, gpost, wq, kv_all, layer, wo):
    m, d = x.shape
    return pl.pallas_call(
        _xattn_sample_body,
        grid=(m,),
        in_specs=[_full((m, d)), _full((1, d)), _full((1, d)), _full((d, d)),
                  pl.BlockSpec((None, kv_all.shape[1], LANE), lambda b: (layer * m + b, 0, 0)), _full((d, d))],
        out_specs=_full((m, d)),
        out_shape=jax.ShapeDtypeStruct((m, d), F32),
        scratch_shapes=[pltpu.VMEM((m, d), F32), pltpu.VMEM((m, d), F32)],
        compiler_params=_cparams(("arbitrary",)),
        name="xattn_sample",
    )(x, gpre, gpost, wq, kv_all, wo)


def _ffn_prompt_body(x_ref, gpre_ref, gpost_ref, wg_ref, wu_ref, cw_ref, cb_ref, wd_ref, hist_ref,
                     o_ref, nh_ref, gbuf):
    t = pl.program_id(1)
    tm = x_ref.shape[0]
    k = FFN_CONV - 1
    base = SUBLANE - k

    @pl.when(t == 0)
    def _():
        gbuf[base:SUBLANE, :] = hist_ref[...]

    x = x_ref[...]
    u = _rms(x, gpre_ref[...]).astype(BF16)
    gbuf[SUBLANE:SUBLANE + tm, :] = _dot(u, wg_ref[...])
    up = _dot(u, wu_ref[...])
    gc = gbuf[base:base + tm, :] * cw_ref[0:1, :] + cb_ref[...]
    for j in range(1, FFN_CONV):
        gc = gc + gbuf[base + j:base + j + tm, :] * cw_ref[j:j + 1, :]
    hmid = (_silu(gc) * up).astype(BF16)
    f = _dot(hmid, wd_ref[...])
    o_ref[...] = x + _rms(f, gpost_ref[...])
    last = gbuf[tm + base:tm + SUBLANE, :]
    nh_ref[...] = last
    gbuf[base:SUBLANE, :] = last


def _ffn_prompt(x, gpre, gpost, wg, wu, cw, cb, wd, hist, n_seq, tm):
    m, d = x.shape
    f = wg.shape[1]
    tb = m // n_seq // tm
    k = FFN_CONV - 1
    return pl.pallas_call(
        _ffn_prompt_body,
        grid=(n_seq, tb),
        in_specs=[pl.BlockSpec((tm, d), lambda b, t: (b * tb + t, 0)), _full((1, d)), _full((1, d)),
                  _resident((d, f)), _resident((d, f)), _full((FFN_CONV, f)), _full((1, f)), _resident((f, d)),
                  pl.BlockSpec((None, k, f), lambda b, t: (b, 0, 0))],
        out_specs=[pl.BlockSpec((tm, d), lambda b, t: (b * tb + t, 0)),
                   pl.BlockSpec((None, k, f), lambda b, t: (b, 0, 0))],
        out_shape=[jax.ShapeDtypeStruct((m, d), F32), jax.ShapeDtypeStruct((n_seq, k, f), F32)],
        scratch_shapes=[pltpu.VMEM((tm + SUBLANE, f), F32)],
        compiler_params=_cparams(("parallel", "arbitrary")),
        name="ffn_prompt",
    )(x, gpre, gpost, wg, wu, cw, cb, wd, hist)


def _ffn_sample_body(x_ref, gpre_ref, gpost_ref, wg_ref, wu_ref, cw_ref, cb_ref, wd_ref, hist_ref,
                     o_ref, nh_ref, u_scr, acc):
    j = pl.program_id(0)

    @pl.when(j == 0)
    def _():
        u_scr[...] = _rms(x_ref[...], gpre_ref[...]).astype(BF16)
        acc[...] = jnp.zeros_like(acc)

    u = u_scr[...]
    gate = _dot(u, wg_ref[...])
    up = _dot(u, wu_ref[...])
    gc = hist_ref[0] * cw_ref[0:1, :] + cb_ref[...]
    for k in range(1, FFN_CONV - 1):
        gc = gc + hist_ref[k] * cw_ref[k:k + 1, :]
        nh_ref[k - 1] = hist_ref[k]
    gc = gc + gate * cw_ref[FFN_CONV - 1:FFN_CONV, :]
    nh_ref[FFN_CONV - 2] = gate
    hmid = (_silu(gc) * up).astype(BF16)
    acc[...] += _dot(hmid, wd_ref[...])

    @pl.when(j == pl.num_programs(0) - 1)
    def _():
        o_ref[...] = x_ref[...] + _rms(acc[...], gpost_ref[...])


def _ffn_sample(x, gpre, gpost, wg, wu, cw, cb, wd, hist_t, tn):
    m, d = x.shape
    f = wg.shape[1]
    k = FFN_CONV - 1
    return pl.pallas_call(
        _ffn_sample_body,
        grid=(f // tn,),
        in_specs=[_full((m, d)), _full((1, d)), _full((1, d)),
                  pl.BlockSpec((d, tn), lambda j: (0, j)), pl.BlockSpec((d, tn), lambda j: (0, j)),
                  pl.BlockSpec((FFN_CONV, tn), lambda j: (0, j)), pl.BlockSpec((1, tn), lambda j: (0, j)),
                  pl.BlockSpec((tn, d), lambda j: (j, 0)),
                  pl.BlockSpec((k, m, tn), lambda j: (0, 0, j))],
        out_specs=[_full((m, d)), pl.BlockSpec((k, m, tn), lambda j: (0, 0, j))],
        out_shape=[jax.ShapeDtypeStruct((m, d), F32), jax.ShapeDtypeStruct((k, m, f), F32)],
        scratch_shapes=[pltpu.VMEM((m, d), BF16), pltpu.VMEM((m, d), F32)],
        compiler_params=_cparams(("arbitrary",)),
        name="ffn_sample",
    )(x, gpre, gpost, wg, wu, cw, cb, wd, hist_t)


def _ssd_activations(u, w_ref, cw_ref, cb_ref, dtb_ref, act, cbuf, clast_ref, d_inner):
    tm = u.shape[0]
    k = SSM_CONV - 1
    base = SUBLANE - k
    conv_dim = cw_ref.shape[1]
    act[:, :d_inner] = _silu(_dot(u, w_ref[:, :d_inner]))
    act[:, d_inner + conv_dim:] = _softplus(_dot(u, w_ref[:, d_inner + conv_dim:]) + dtb_ref[...])
    for c0 in range(0, conv_dim, CONV_COLS):
        cs = slice(c0, c0 + CONV_COLS)
        cbuf[SUBLANE:SUBLANE + tm, cs] = _dot(u, w_ref[:, d_inner + c0:d_inner + c0 + CONV_COLS])
        xc = cbuf[base:base + tm, cs] * cw_ref[0:1, cs] + cb_ref[:, cs]
        for j in range(1, SSM_CONV):
            xc = xc + cbuf[base + j:base + j + tm, cs] * cw_ref[j:j + 1, cs]
        act[:, d_inner + c0:d_inner + c0 + CONV_COLS] = _silu(xc)
    last = cbuf[tm + base:tm + SUBLANE, :]
    clast_ref[...] = last
    cbuf[base:SUBLANE, :] = last


def _ssd_chunk(act_ref, alog_ref, dx_ref, nw_ref, e_ref, tril_ref, state, ybuf, y_ref, d_inner, n_heads):
    L = SSM_CHUNK
    gn = SSM_GROUPS * D_STATE
    conv_dim = d_inner + 2 * gn
    hpg = n_heads // SSM_GROUPS
    gw = d_inner // SSM_GROUPS
    xs = act_ref[:, d_inner:2 * d_inner]
    xs_b = xs.astype(BF16)
    e = e_ref[...]
    dt = act_ref[:, d_inner + conv_dim:]
    a = dt * (-jnp.exp(alog_ref[...]))
    tril = tril_ref[...]
    a_hi, a_mid, a_lo = _split3(a)
    acum = _dot(tril, a_hi) + _dot(tril, a_mid) + _dot(tril, a_lo)
    acum_t = acum.T
    dt_t = dt.T
    a_last = acum[L - 1:L, :]
    d_acc = _expand(jnp.exp(acum), e)
    xde = xs * _expand(dt * jnp.exp(a_last - acum), e)
    row = lax.broadcasted_iota(jnp.int32, (L, L), 0)
    col = lax.broadcasted_iota(jnp.int32, (L, L), 1)
    causal = row >= col

    for g in range(SSM_GROUPS):
        b0 = 2 * d_inner + g * D_STATE
        bg = act_ref[:, b0:b0 + D_STATE].astype(BF16)
        cg = act_ref[:, b0 + gn:b0 + gn + D_STATE].astype(BF16)
        cb = _dot_nt(cg, bg)
        hg = state[g * gw:(g + 1) * gw, :]
        y_inter = _dot_nt(cg, hg.astype(BF16)) * d_acc[:, g * gw:(g + 1) * gw]
        for j in range(hpg):
            hd = g * hpg + j
            seg = acum[:, hd:hd + 1] - acum_t[hd:hd + 1, :]
            w = cb * jnp.exp(jnp.where(causal, seg, -jnp.inf)) * dt_t[hd:hd + 1, :]
            sl = slice(hd * SSM_HEAD_DIM, (hd + 1) * SSM_HEAD_DIM)
            ybuf[:, sl] = _dot(w.astype(BF16), xs_b[:, sl])
        ybuf[:, g * gw:(g + 1) * gw] += y_inter
        s_g = _dot(xde[:, g * gw:(g + 1) * gw].T.astype(BF16), bg)
        for j in range(hpg):
            hd = g * hpg + j
            rs = slice(hd * SSM_HEAD_DIM, (hd + 1) * SSM_HEAD_DIM)
            cd = jnp.exp(acum_t[hd:hd + 1, L - 1:L])
            state[rs, :] = state[rs, :] * cd + s_g[j * SSM_HEAD_DIM:(j + 1) * SSM_HEAD_DIM, :]

    y = ybuf[...] + xs * dx_ref[...]
    gated = y * act_ref[:, :d_inner]
    for g in range(SSM_GROUPS):
        seg = gated[:, g * gw:(g + 1) * gw]
        y_ref[:, g * gw:(g + 1) * gw] = _rms(seg, nw_ref[:, g * gw:(g + 1) * gw]).astype(BF16)


N_SSD_IN = 15


def _ssd_layer_body(x_ref, gpre_ref, gpost_ref, w_ref, cw_ref, cb_ref, dtb_ref, alog_ref, dx_ref, nw_ref,
                    e_ref, tril_ref, wo_ref, hist_ref, h0_ref, *rest, d_inner, n_heads):
    o_ref, hlast_ref, clast_ref, act, cbuf, state, ybuf, ybf = rest[-8:]
    t = pl.program_id(1)
    tm = x_ref.shape[0]
    L = SSM_CHUNK
    base = SUBLANE - (SSM_CONV - 1)

    @pl.when(t == 0)
    def _():
        cbuf[base:SUBLANE, :] = hist_ref[...]
        state[...] = h0_ref[...]

    x = x_ref[...]
    u = _rms(x, gpre_ref[...]).astype(BF16)
    _ssd_activations(u, w_ref, cw_ref, cb_ref, dtb_ref, act, cbuf, clast_ref, d_inner)
    for ci in range(tm // L):
        _ssd_chunk(act.at[ci * L:(ci + 1) * L, :], alog_ref, dx_ref, nw_ref, e_ref, tril_ref, state, ybuf,
                   ybf.at[ci * L:(ci + 1) * L, :], d_inner, n_heads)
    o_ref[...] = x + _rms(_dot(ybf[...], wo_ref[...]), gpost_ref[...])

    @pl.when(t == pl.num_programs(1) - 1)
    def _():
        hlast_ref[...] = state[...]


def _ssd_layer(x, gpre, gpost, w, cw, cb, dtb, alog, dx, nw, e, tril, wo, hist, h0, prev, layer, n_layers,
               n_seq, d_inner, n_heads, tm):
    m, d = x.shape
    n = w.shape[1]
    tb = m // n_seq // tm
    conv_dim = cw.shape[1]
    k = SSM_CONV - 1
    rows = n_heads * SSM_HEAD_DIM
    assert conv_dim % CONV_COLS == 0 and tm % SSM_CHUNK == 0
    ins = (x, gpre, gpost, w, cw, cb, dtb, alog, dx, nw, e, tril, wo, hist, h0)
    assert len(ins) == N_SSD_IN
    x_spec = pl.BlockSpec((tm, d), lambda b, t: (b * tb + t, 0))
    return pl.pallas_call(
        functools.partial(_ssd_layer_body, d_inner=d_inner, n_heads=n_heads),
        grid=(n_seq, tb),
        in_specs=[x_spec, _full((1, d)), _full((1, d)), _resident(w.shape), _full(cw.shape), _full(cb.shape),
                  _full(dtb.shape), _full(alog.shape), _full(dx.shape), _full(nw.shape), _full(e.shape),
                  _full(tril.shape), _resident(wo.shape),
                  pl.BlockSpec((None, k, conv_dim), lambda b, t: (b, 0, 0)),
                  pl.BlockSpec((None, rows, D_STATE), lambda b, t: (b, 0, 0))]
                 + [pl.BlockSpec(memory_space=pl.ANY)] * len(prev),
        out_specs=[x_spec,
                   pl.BlockSpec((None, rows, D_STATE), lambda b, t: (layer * n_seq + b, 0, 0)),
                   pl.BlockSpec((None, k, conv_dim), lambda b, t: (b, 0, 0))],
        out_shape=[jax.ShapeDtypeStruct((m, d), F32),
                   jax.ShapeDtypeStruct((n_layers * n_seq, rows, D_STATE), F32),
                   jax.ShapeDtypeStruct((n_seq, k, conv_dim), F32)],
        input_output_aliases={N_SSD_IN: 1} if prev else {},
        scratch_shapes=[pltpu.VMEM((tm, n), F32), pltpu.VMEM((tm + SUBLANE, conv_dim), F32),
                        pltpu.VMEM((rows, D_STATE), F32), pltpu.VMEM((SSM_CHUNK, d_inner), F32),
                        pltpu.VMEM((tm, d_inner), BF16)],
        compiler_params=_cparams(("parallel", "arbitrary")),
        name="ssd_layer",
    )(*ins, *prev)


def _ssd_step_pre_body(proj_ref, hist_ref, cw_ref, cb_ref, dtb_ref, alog_ref, e_ref,
                       xs_ref, b_ref, c_ref, xdt_t_ref, dec_t_ref, clast_ref, *, d_inner):
    gn = SSM_GROUPS * D_STATE
    conv_dim = d_inner + 2 * gn
    m = proj_ref.shape[0]
    xbc = proj_ref[:, d_inner:d_inner + conv_dim]
    xc = hist_ref[0] * cw_ref[0:1, :] + cb_ref[...]
    for j in range(1, SSM_CONV - 1):
        xc = xc + hist_ref[j] * cw_ref[j:j + 1, :]
        clast_ref[j - 1] = hist_ref[j]
    xc = _silu(xc + xbc * cw_ref[SSM_CONV - 1:SSM_CONV, :])
    clast_ref[SSM_CONV - 2] = xbc
    xs = xc[:, :d_inner]
    xs_ref[...] = xs
    b_ref[...] = xc[:, d_inner:d_inner + gn]
    c_ref[...] = xc[:, d_inner + gn:]
    e = e_ref[...]
    dt = _softplus(proj_ref[:, d_inner + conv_dim:] + dtb_ref[...])
    dec = jnp.exp(dt * (-jnp.exp(alog_ref[...])))
    pad = jnp.zeros((LANE - m, d_inner), F32)
    xdt_t_ref[...] = jnp.concatenate([xs * _expand(dt, e), pad], axis=0).T
    dec_t_ref[...] = jnp.concatenate([_expand(dec, e), pad], axis=0).T


def _ssd_step_pre(proj, hist_t, cw, cb, dtb, alog, e, d_inner):
    m = proj.shape[0]
    gn = SSM_GROUPS * D_STATE
    conv_dim = cw.shape[1]
    k = SSM_CONV - 1
    args = (proj, hist_t, cw, cb, dtb, alog, e)
    return pl.pallas_call(
        functools.partial(_ssd_step_pre_body, d_inner=d_inner),
        grid=(1,),
        in_specs=[_full(a.shape) for a in args],
        out_specs=[_full((m, d_inner)), _full((m, gn)), _full((m, gn)), _full((d_inner, LANE)),
                   _full((d_inner, LANE)), _full((k, m, conv_dim))],
        out_shape=[jax.ShapeDtypeStruct((m, d_inner), F32), jax.ShapeDtypeStruct((m, gn), F32),
                   jax.ShapeDtypeStruct((m, gn), F32), jax.ShapeDtypeStruct((d_inner, LANE), F32),
                   jax.ShapeDtypeStruct((d_inner, LANE), F32), jax.ShapeDtypeStruct((k, m, conv_dim), F32)],
        compiler_params=_cparams(("arbitrary",)),
        name="ssd_step_pre",
    )(*args)


def _ssd_step_state_body(h0_ref, xdt_t_ref, dec_t_ref, b_ref, c_ref, *rest):
    hn_ref, y_t_ref = rest[-2:]
    m = h0_ref.shape[0]
    lane = lax.broadcasted_iota(jnp.int32, (SSM_HEAD_DIM, LANE), 1)
    acc = jnp.zeros((SSM_HEAD_DIM, LANE), F32)
    for b in range(m):
        h = h0_ref[b] * dec_t_ref[:, b:b + 1] + xdt_t_ref[:, b:b + 1] * b_ref[b:b + 1, :]
        hn_ref[b] = h
        ycol = jnp.sum(h * c_ref[b:b + 1, :], axis=1, keepdims=True)
        acc = jnp.where(lane == b, ycol, acc)
    y_t_ref[...] = acc


def _ssd_step_state(h_all, prev, layer, xdt_t, dec_t, bm, cm, n_heads):
    m = bm.shape[0]
    hpg = n_heads // SSM_GROUPS
    h_spec = pl.BlockSpec((m, None, SSM_HEAD_DIM, D_STATE), lambda j: (layer, j, 0, 0))
    return pl.pallas_call(
        _ssd_step_state_body,
        grid=(n_heads,),
        in_specs=[h_spec,
                  pl.BlockSpec((SSM_HEAD_DIM, LANE), lambda j: (j, 0)),
                  pl.BlockSpec((SSM_HEAD_DIM, LANE), lambda j: (j, 0)),
                  pl.BlockSpec((m, D_STATE), lambda j: (0, j // hpg)),
                  pl.BlockSpec((m, D_STATE), lambda j: (0, j // hpg))]
                 + [pl.BlockSpec(memory_space=pl.ANY)] * len(prev),
        out_specs=[h_spec, pl.BlockSpec((SSM_HEAD_DIM, LANE), lambda j: (j, 0))],
        out_shape=[jax.ShapeDtypeStruct(h_all.shape, F32),
                   jax.ShapeDtypeStruct((n_heads * SSM_HEAD_DIM, LANE), F32)],
        input_output_aliases={5: 0} if prev else {},
        compiler_params=_cparams(("parallel",)),
        name="ssd_step_state",
    )(h_all, xdt_t, dec_t, bm, cm, *prev)


def _ssd_step_post_body(y_t_ref, xs_ref, z_ref, dx_ref, nw_ref, o_ref):
    m, d_inner = xs_ref.shape
    gw = d_inner // SSM_GROUPS
    y = y_t_ref[...].T[:m, :] + xs_ref[...] * dx_ref[...]
    gated = y * _silu(z_ref[...])
    for g in range(SSM_GROUPS):
        seg = gated[:, g * gw:(g + 1) * gw]
        o_ref[:, g * gw:(g + 1) * gw] = _rms(seg, nw_ref[:, g * gw:(g + 1) * gw]).astype(BF16)


def _ssd_step_post(y_t, xs, proj, dx, nw):
    m, d_inner = xs.shape
    return pl.pallas_call(
        _ssd_step_post_body,
        grid=(1,),
        in_specs=[_full(y_t.shape), _full(xs.shape), pl.BlockSpec((m, d_inner), lambda i: (0, 0)),
                  _full(dx.shape), _full(nw.shape)],
        out_specs=_full((m, d_inner)),
        out_shape=jax.ShapeDtypeStruct((m, d_inner), BF16),
        compiler_params=_cparams(("arbitrary",)),
        name="ssd_step_post",
    )(y_t, xs, proj, dx, nw)


def _band_attn_body(q_ref, kc_ref, kp_ref, vc_ref, vp_ref, o_ref, l_ref, *, win, dil):
    blk = pl.program_id(1)
    n_hp, bt, _ = q_ref.shape
    span = win // dil
    heads = LANE // ATT_HEAD_DIM
    qi = lax.broadcasted_iota(jnp.int32, (span, 2 * span), 0)
    ki = lax.broadcasted_iota(jnp.int32, (span, 2 * span), 1)
    band = (ki >= qi) & (ki <= qi + span)
    band_first = band & ((blk > 0) | (ki >= span))
    lane_q = lax.broadcasted_iota(jnp.int32, (span, LANE), 1)
    lane_k = lax.broadcasted_iota(jnp.int32, (2 * span, LANE), 1)

    def rows(ref, hp, start, r):
        if dil == 1:
            return ref[hp, start:start + span, :]
        return ref[hp, pl.ds(start + r, span, stride=dil), :]

    for hp in range(n_hp):
        for wi in range(bt // win):
            for r in range(dil):
                q = (rows(q_ref, hp, wi * win, r) * (ATT_HEAD_DIM ** -0.5)).astype(BF16)
                if wi == 0:
                    k_prev = rows(kp_ref, hp, kp_ref.shape[1] - win, r)
                    v_prev = rows(vp_ref, hp, vp_ref.shape[1] - win, r)
                else:
                    k_prev = rows(kc_ref, hp, (wi - 1) * win, r)
                    v_prev = rows(vc_ref, hp, (wi - 1) * win, r)
                kk = jnp.concatenate([k_prev, rows(kc_ref, hp, wi * win, r)], axis=0).astype(BF16)
                vv = jnp.concatenate([v_prev, rows(vc_ref, hp, wi * win, r)], axis=0).astype(BF16)
                valid = band_first if wi == 0 else band
                o_acc, l_acc = None, None
                for h in range(heads):
                    in_q = (lane_q >= h * ATT_HEAD_DIM) & (lane_q < (h + 1) * ATT_HEAD_DIM)
                    in_k = (lane_k >= h * ATT_HEAD_DIM) & (lane_k < (h + 1) * ATT_HEAD_DIM)
                    s = jnp.where(valid, _dot_nt(jnp.where(in_q, q, jnp.zeros_like(q)), kk), -jnp.inf)
                    mx = jnp.max(s, axis=-1, keepdims=True)
                    p = jnp.exp(s - mx)
                    den = jnp.sum(p, axis=-1, keepdims=True)
                    o_h = _dot(p.astype(BF16), jnp.where(in_k, vv, jnp.zeros_like(vv))) / den
                    l_h = jnp.where(in_q, mx + jnp.log(den), 0.0)
                    o_acc = o_h if o_acc is None else o_acc + o_h
                    l_acc = l_h if l_acc is None else l_acc + l_h
                if dil == 1:
                    o_ref[hp, wi * win:wi * win + span, :] = o_acc
                    l_ref[hp, wi * win:wi * win + span, :] = l_acc
                else:
                    o_ref[hp, pl.ds(wi * win + r, span, stride=dil), :] = o_acc
                    l_ref[hp, pl.ds(wi * win + r, span, stride=dil), :] = l_acc


def _band_attn(qkv_c, n_seq, gi, win, dil):
    nch, m, _ = qkv_c.shape
    t_len = m // n_seq
    bt = max(win, min(BAND_BLOCK, t_len))
    assert bt % win == 0 and t_len % bt == 0
    nblk = t_len // bt
    wpb = bt // win
    per_grp = ATT_GW // LANE
    ng = len(ATT_GROUPS)

    def cur(sec):
        return pl.BlockSpec((per_grp, bt, LANE), lambda b, k: (sec * ng + gi, b * nblk + k, 0))

    def prev(sec):
        return pl.BlockSpec((per_grp, win, LANE),
                            lambda b, k: (sec * ng + gi, jnp.maximum((b * nblk + k) * wpb - 1, 0), 0))

    out_spec = pl.BlockSpec((per_grp, bt, LANE), lambda b, k: (0, b * nblk + k, 0))
    out_sd = jax.ShapeDtypeStruct((per_grp, m, LANE), F32)
    return pl.pallas_call(
        functools.partial(_band_attn_body, win=win, dil=dil),
        grid=(n_seq, nblk),
        in_specs=[cur(0), cur(1), prev(1), cur(2), prev(2)],
        out_specs=[out_spec, out_spec],
        out_shape=[out_sd, out_sd],
        compiler_params=_cparams(("parallel", "arbitrary")),
        name="band_attn_w%d" % win,
    )(qkv_c, qkv_c, qkv_c, qkv_c, qkv_c)


def _row_to_cols(v):
    return jnp.concatenate([jnp.broadcast_to(v[:, c * LANE:(c + 1) * LANE], (LANE, LANE)).T
                            for c in range(v.shape[1] // LANE)], axis=0)


def _col_to_row(v):
    return jnp.concatenate([jnp.broadcast_to(v[c * LANE:(c + 1) * LANE, :], (LANE, LANE)).T[0:1, :]
                            for c in range(v.shape[0] // LANE)], axis=1)


def _attn_sample_body(*refs, dils, n_prev):
    ng = len(dils)
    qkv_ref = refs[0]
    bufs = refs[1:1 + ng]
    outs = refs[1 + ng + n_prev:]
    o_refs, l_refs, c_refs = outs[0:ng], outs[ng:2 * ng], outs[2 * ng:3 * ng]
    scale = ATT_HEAD_DIM ** -0.5
    for gi in range(ng):
        buf, o_ref, l_ref, c_ref = bufs[gi], o_refs[gi], l_refs[gi], c_refs[gi]
        wb = buf.shape[2]
        nch = wb // LANE
        q_c = _row_to_cols(qkv_ref[:, gi * ATT_GW:(gi + 1) * ATT_GW] * scale)
        kn_c = _row_to_cols(qkv_ref[:, (ng + gi) * ATT_GW:(ng + gi + 1) * ATT_GW])
        vn_c = _row_to_cols(qkv_ref[:, (2 * ng + gi) * ATT_GW:(2 * ng + gi + 1) * ATT_GW])
        lane = lax.broadcasted_iota(jnp.int32, (1, wb), 1)
        valid = (lane & (dils[gi] - 1)) == 0
        o_cols = []
        for h in range(ATT_HPG):
            rs = slice(h * ATT_HEAD_DIM, (h + 1) * ATT_HEAD_DIM)
            qh = q_c[rs, :]
            s = jnp.concatenate([jnp.sum(buf[0, rs, c * LANE:(c + 1) * LANE] * qh, axis=0, keepdims=True)
                                 for c in range(nch)], axis=1)
            s = jnp.where(valid, s, -jnp.inf)
            sn = jnp.sum(qh[:, 0:1] * kn_c[rs, 0:1], axis=0, keepdims=True)
            mx = jnp.maximum(jnp.max(s, axis=1, keepdims=True), sn)
            p = jnp.exp(s - mx)
            p_new = jnp.exp(sn - mx)
            den = jnp.sum(p, axis=1, keepdims=True) + p_new
            acc = buf[1, rs, 0:LANE] * p[:, 0:LANE]
            for c in range(1, nch):
                acc = acc + buf[1, rs, c * LANE:(c + 1) * LANE] * p[:, c * LANE:(c + 1) * LANE]
            o_cols.append((jnp.sum(acc, axis=1, keepdims=True) + p_new * vn_c[rs, 0:1]) / den)
            l_ref[:, rs] = jnp.broadcast_to(mx + jnp.log(den), (1, ATT_HEAD_DIM))
        o_ref[...] = _col_to_row(jnp.concatenate(o_cols, axis=0))
        last = lax.broadcasted_iota(jnp.int32, (ATT_HEAD_DIM, LANE), 1) == LANE - 1
        for kv, new_c in ((0, kn_c), (1, vn_c)):
            for h in range(ATT_HPG):
                rs = slice(h * ATT_HEAD_DIM, (h + 1) * ATT_HEAD_DIM)
                rolled = pltpu.roll(buf[kv, rs, :], wb - 1, 1)
                if nch > 1:
                    c_ref[kv, rs, 0:wb - LANE] = rolled[:, 0:wb - LANE]
                c_ref[kv, rs, wb - LANE:wb] = jnp.where(last, new_c[rs, :], rolled[:, wb - LANE:wb])


def _attn_sample(qkv, cache_views, prev, layer):
    m, w3 = qkv.shape
    ng = len(ATT_GROUPS)
    dils = tuple(dil for _, dil in ATT_GROUPS)
    for v, (win, dil) in zip(cache_views, ATT_GROUPS):
        assert v.shape[3] == win and win % LANE == 0 and dil & (dil - 1) == 0
    c_specs = [pl.BlockSpec((None, 2, ATT_GW, v.shape[3]), lambda b: (layer * m + b, 0, 0, 0)) for v in cache_views]
    o_sd = jax.ShapeDtypeStruct((m, 1, ATT_GW), F32)
    o_spec = pl.BlockSpec((None, 1, ATT_GW), lambda b: (b, 0, 0))
    n_in = 1 + ng
    res = pl.pallas_call(
        functools.partial(_attn_sample_body, dils=dils, n_prev=len(prev)),
        grid=(m,),
        in_specs=[pl.BlockSpec((None, 1, w3), lambda b: (b, 0, 0))] + c_specs
                 + [pl.BlockSpec(memory_space=pl.ANY)] * len(prev),
        out_specs=[o_spec] * (2 * ng) + c_specs,
        out_shape=[o_sd] * (2 * ng) + [jax.ShapeDtypeStruct(v.shape, v.dtype) for v in cache_views],
        input_output_aliases={n_in + k: 2 * ng + k for k in range(len(prev))},
        compiler_params=_cparams(("parallel",)),
        name="attn_sample",
    )(qkv.reshape(m, 1, w3), *cache_views, *prev)
    os_ = [r.reshape(m, ATT_GW) for r in res[0:ng]]
    ls_ = [r.reshape(m, ATT_GW) for r in res[ng:2 * ng]]
    return os_, ls_, list(res[2 * ng:])


def _prep_weights(norms, ssm_w_in, ssm_conv_b, ssm_dt_bias, ssm_a_log, ssm_d, ssm_norm_w, ssm_w_out, att_w_qkv,
                  att_w_o, mem_norm, xa_w_q, xa_w_kv, xa_w_o, ffn_w_gu, ffn_conv_b, ffn_w_down):
    n_ssm, d_model, in_dim = ssm_w_in.shape
    n_heads = ssm_dt_bias.shape[1]
    d_inner = n_heads * SSM_HEAD_DIM
    d_ff = ffn_w_down.shape[1]
    pad_heads = LANE - n_heads
    w = {}
    w['n_heads'], w['d_inner'] = n_heads, d_inner
    w['norms'] = norms[:, :, None, :]
    w['ssm_w_in'] = jnp.pad(ssm_w_in, ((0, 0), (0, 0), (0, pad_heads))).astype(BF16)
    w['ssm_conv_b'] = ssm_conv_b[:, None, :]
    w['ssm_dt_bias'] = jnp.pad(ssm_dt_bias, ((0, 0), (0, pad_heads)))[:, None, :]
    w['ssm_a_log'] = jnp.pad(ssm_a_log, ((0, 0), (0, pad_heads)))[:, None, :]
    w['ssm_dx'] = jnp.repeat(ssm_d, SSM_HEAD_DIM, axis=1)[:, None, :]
    w['ssm_norm_w'] = ssm_norm_w[:, None, :]
    w['ssm_w_out'] = ssm_w_out.astype(BF16)
    w['att_w_qkv'] = att_w_qkv.astype(BF16)
    w['att_w_o'] = att_w_o.astype(BF16)
    w['mem_norm'] = mem_norm[:, None, :]
    w['xa_w_q'] = xa_w_q.astype(BF16)
    w['xa_w_kv'] = xa_w_kv.astype(BF16)
    w['xa_w_o'] = xa_w_o.astype(BF16)
    w['ffn_w_g'] = ffn_w_gu[:, :, :d_ff].astype(BF16)
    w['ffn_w_u'] = ffn_w_gu[:, :, d_ff:].astype(BF16)
    w['ffn_conv_b'] = ffn_conv_b[:, None, :]
    w['ffn_w_down'] = ffn_w_down.astype(BF16)
    e = np.zeros((LANE, d_inner), np.float32)
    for h in range(n_heads):
        e[h, h * SSM_HEAD_DIM:(h + 1) * SSM_HEAD_DIM] = 1.0
    w['expand'] = jnp.asarray(e, BF16)
    w['tril'] = jnp.asarray(np.tril(np.ones((SSM_CHUNK, SSM_CHUNK), np.float32)), BF16)
    return w


def _prompt_trunk(x3, mem3, w, ssm_conv_w, ffn_conv_w):
    n, t_len, d = x3.shape
    assert t_len % SSM_CHUNK == 0 and all(t_len % win == 0 for win, _ in ATT_GROUPS)
    depth = w['norms'].shape[0]
    n_heads, d_inner = w['n_heads'], w['d_inner']
    conv_dim = ssm_conv_w.shape[2]
    d_ff = ffn_conv_w.shape[2]
    tm = PROMPT_ROWS_SSD
    tm_big = PROMPT_ROWS
    x = x3.reshape(n * t_len, d)
    mem = mem3.reshape(-1, d)
    kv_all, kv_rows = _mem_kv(mem, w['mem_norm'], w['xa_w_kv'], tm)
    tabs = _rope_tables(jnp.arange(t_len, dtype=jnp.int32))
    zero_hist = jnp.zeros((n, SSM_CONV - 1, conv_dim), F32)
    zero_h = jnp.zeros((n, n_heads * SSM_HEAD_DIM, D_STATE), F32)
    zero_fh = jnp.zeros((n, FFN_CONV - 1, d_ff), F32)
    n_ssm, n_att = (depth + 1) // 2, depth // 2
    new = {'ssm': [], 'ssm_conv': [], 'swa': [], 'ffn_conv': []}
    for i in range(depth):
        j = i // 2
        g = w['norms'][i]
        if i % 2 == 0:
            x, h_last, c_last = _ssd_layer(x, g[0], g[1], w['ssm_w_in'][j], ssm_conv_w[j], w['ssm_conv_b'][j],
                                           w['ssm_dt_bias'][j], w['ssm_a_log'][j], w['ssm_dx'][j],
                                           w['ssm_norm_w'][j], w['expand'], w['tril'], w['ssm_w_out'][j],
                                           zero_hist, zero_h, new['ssm'], j, n_ssm, n, d_inner, n_heads, tm)
            new['ssm'] = [h_last]
            new['ssm_conv'].append(c_last)
        else:
            qkv_c, new['swa'] = _norm_qkv_prompt(x, g[0], w['att_w_qkv'][j], tabs, new['swa'], j, n_att, n, tm_big)
            os_, ls_ = [], []
            for gi, (win, dil) in enumerate(ATT_GROUPS):
                o, l = _band_attn(qkv_c, n, gi, win, dil)
                os_.append(o)
                ls_.append(l)
            x = _attn_out(os_, ls_, w['att_w_o'][j], x, g[1], tm_big)
        x = _xattn_prompt(x, g[2], g[3], w['xa_w_q'][i], kv_all[i], w['xa_w_o'][i], n, tm_big)
        x, f_hist = _ffn_prompt(x, g[4], g[5], w['ffn_w_g'][i], w['ffn_w_u'][i], ffn_conv_w[i],
                                w['ffn_conv_b'][i], w['ffn_w_down'][i], zero_fh, n, tm_big)
        new['ffn_conv'].append(f_hist)
    n_mem = mem3.shape[1]
    hd = d // MEM_HEADS
    swa = [jnp.transpose(c.reshape(n_att, n, 2, ATT_HPG, ATT_HEAD_DIM, c.shape[3]), (0, 1, 5, 2, 3, 4))
           for c in new['swa']]
    p_mem = jnp.transpose(kv_rows.reshape(depth, n, n_mem, 2, hd // LANE, MEM_HEADS, LANE),
                          (0, 1, 2, 3, 5, 4, 6)).reshape(depth, n, n_mem, 2, MEM_HEADS, hd)
    return (x.reshape(n, t_len, d), new['ssm'][0].reshape(n_ssm, n, n_heads, SSM_HEAD_DIM, D_STATE),
            jnp.stack(new['ssm_conv']), swa, p_mem, jnp.stack(new['ffn_conv']))


def _sample_trunk(x3, w, ssm_conv_w, ffn_conv_w, state_ssm, state_ssm_conv, caches, cache_mem_kv, state_ffn_conv):
    m, t_len, d = x3.shape
    assert t_len == 1
    depth = w['norms'].shape[0]
    n_heads, d_inner = w['n_heads'], w['d_inner']
    n_ssm = state_ssm.shape[0]
    x = x3.reshape(m, d)
    tabs = _rope_tables(jnp.full((m,), PAST_LEN, jnp.int32))
    h_all = state_ssm.reshape(n_ssm * m, n_heads, SSM_HEAD_DIM, D_STATE)
    n_mem, hd = cache_mem_kv.shape[2], d // MEM_HEADS
    kv_all = jnp.transpose(cache_mem_kv.reshape(depth, m, n_mem, 2, MEM_HEADS, hd // LANE, LANE),
                           (0, 1, 2, 3, 5, 4, 6)).reshape(depth * m, n_mem * 2 * MEM_HEADS * (hd // LANE), LANE)
    cache_views = [jnp.transpose(c, (0, 1, 3, 4, 5, 2)).reshape(c.shape[0] * m, 2, ATT_GW, c.shape[2])
                   for c in caches]
    new_caches = []
    new = {'ssm': [], 'ssm_conv': [], 'ffn_conv': []}
    for i in range(depth):
        j = i // 2
        g = w['norms'][i]
        if i % 2 == 0:
            proj = _norm_proj(x, g[0], w['ssm_w_in'][j], m)
            hist_t = jnp.swapaxes(state_ssm_conv[j], 0, 1)
            xs, bm, cm, xdt_t, dec_t, c_last_t = _ssd_step_pre(proj, hist_t, ssm_conv_w[j], w['ssm_conv_b'][j],
                                                                w['ssm_dt_bias'][j], w['ssm_a_log'][j],
                                                                w['expand'], d_inner)
            h_new, y_t = _ssd_step_state(h_all, new['ssm'], j, xdt_t, dec_t, bm, cm, n_heads)
            y = _ssd_step_post(y_t, xs, proj, w['ssm_dx'][j], w['ssm_norm_w'][j])
            x = _proj_res(y, w['ssm_w_out'][j], x, g[1], m)
            new['ssm'] = [h_new]
            new['ssm_conv'].append(jnp.swapaxes(c_last_t, 0, 1))
        else:
            qkv = _norm_qkv_rope(x, g[0], w['att_w_qkv'][j], tabs, m, 1)
            os_, ls_, new_caches = _attn_sample(qkv, cache_views, new_caches, j)
            x = _attn_out(os_, ls_, w['att_w_o'][j], x, g[1], m)
        x = _xattn_sample(x, g[2], g[3], w['xa_w_q'][i], kv_all, i, w['xa_w_o'][i])
        hist_t = jnp.swapaxes(state_ffn_conv[i], 0, 1)
        x, f_hist_t = _ffn_sample(x, g[4], g[5], w['ffn_w_g'][i], w['ffn_w_u'][i], ffn_conv_w[i],
                                  w['ffn_conv_b'][i], w['ffn_w_down'][i], hist_t, 256)
        new['ffn_conv'].append(jnp.swapaxes(f_hist_t, 0, 1))
    new_caches = [jnp.transpose(nc.reshape(c.shape[0], m, 2, ATT_HPG, ATT_HEAD_DIM, c.shape[2]), (0, 1, 5, 2, 3, 4))
                  for nc, c in zip(new_caches, caches)]
    return (x.reshape(m, 1, d), new['ssm'][0].reshape(state_ssm.shape), jnp.stack(new['ssm_conv']), new_caches,
            jnp.stack(new['ffn_conv']))


def kernel(x_prompt, x_sample, mem_prompt, state_ssm, state_ssm_conv, cache_swa_kv_w128, cache_swa_kv_w512,
           cache_swa_kv_w2048, cache_mem_kv, state_ffn_conv, norms, ssm_w_in, ssm_conv_w, ssm_conv_b,
           ssm_dt_bias, ssm_a_log, ssm_d, ssm_norm_w, ssm_w_out, att_w_qkv, att_w_o, mem_norm, xa_w_q,
           xa_w_kv, xa_w_o, ffn_w_gu, ffn_conv_w, ffn_conv_b, ffn_w_down):
    w = _prep_weights(norms, ssm_w_in, ssm_conv_b, ssm_dt_bias, ssm_a_log, ssm_d, ssm_norm_w, ssm_w_out,
                      att_w_qkv, att_w_o, mem_norm, xa_w_q, xa_w_kv, xa_w_o, ffn_w_gu, ffn_conv_b, ffn_w_down)
    caches = [cache_swa_kv_w128, cache_swa_kv_w512, cache_swa_kv_w2048]
    yp, p_ssm, p_conv, p_swa, p_mem, p_ffn = _prompt_trunk(x_prompt, mem_prompt, w, ssm_conv_w, ffn_conv_w)
    ys, s_ssm, s_conv, s_swa, s_ffn = _sample_trunk(x_sample, w, ssm_conv_w, ffn_conv_w, state_ssm,
                                                    state_ssm_conv, caches, cache_mem_kv, state_ffn_conv)
    return (yp, ys, p_ssm, p_conv, p_swa[0], p_swa[1], p_swa[2], p_mem, p_ffn,
            s_ssm, s_conv, s_swa[0], s_swa[1], s_swa[2], s_ffn)
```

```python
import functools
import math

import numpy as np
import jax
import jax.numpy as jnp
from jax import lax
from jax.experimental import pallas as pl
from jax.experimental.pallas import tpu as pltpu

F32 = jnp.float32
BF16 = jnp.bfloat16

EPS = 1e-6
PAST_LEN = 8192
SSM_HEAD_DIM = 64
SSM_GROUPS = 4
D_STATE = 128
SSM_CONV = 4
SSM_CHUNK = 128
CONV_COLS = 1024
ATT_GROUPS = ((128, 1), (512, 4), (2048, 16))
ATT_HPG = 4
ATT_HEAD_DIM = 64
ATT_GW = ATT_HPG * ATT_HEAD_DIM
BAND_BLOCK = 512
ROT_DIM = ATT_HEAD_DIM // 4
ROPE_THETA = 500000.0
MEM_HEADS = 4
FFN_CONV = 3

LANE = 128
SUBLANE = 8
VMEM_LIMIT = 56 * 1024 * 1024
PROMPT_ROWS = 512
PROMPT_ROWS_SSD = 256


def _cparams(sem):
    return pltpu.CompilerParams(dimension_semantics=sem, vmem_limit_bytes=VMEM_LIMIT)


def _rms(x, g):
    return x * lax.rsqrt(jnp.mean(x * x, axis=-1, keepdims=True) + EPS) * g


def _silu(x):
    return x / (1.0 + jnp.exp(-x))


def _softplus(x):
    return jnp.maximum(x, 0.0) + jnp.log(1.0 + jnp.exp(-jnp.abs(x)))


def _dot(a, b):
    return jnp.dot(a, b, preferred_element_type=F32)


def _dot_nt(a, b):
    return lax.dot_general(a, b, (((1,), (1,)), ((), ())), preferred_element_type=F32)


def _split3(v):
    hi = v.astype(BF16)
    r = v - hi.astype(F32)
    mid = r.astype(BF16)
    lo = (r - mid.astype(F32)).astype(BF16)
    return hi, mid, lo


def _expand(v, e, n_heads):
    hi = v.astype(BF16).astype(F32)
    r1 = v - hi
    mid = r1.astype(BF16).astype(F32)
    lo = r1 - mid
    lane = lax.broadcasted_iota(jnp.int32, v.shape, 1)
    packed = jnp.where(lane < n_heads, hi,
                       jnp.where(lane < 2 * n_heads, pltpu.roll(mid, n_heads, 1),
                                 jnp.where(lane < 3 * n_heads, pltpu.roll(lo, 2 * n_heads, 1), 0.0)))
    return _dot(packed.astype(BF16), e)


def _full(shape):
    return pl.BlockSpec(shape, lambda *_: (0,) * len(shape))


def _resident(shape):
    return pl.BlockSpec(shape, lambda *_: (0,) * len(shape), pipeline_mode=pl.Buffered(1))


def _norm_proj_body(x_ref, g_ref, w_ref, o_ref):
    u = _rms(x_ref[...], g_ref[...]).astype(BF16)
    o_ref[...] = _dot(u, w_ref[...])


def _norm_proj(x, g, w, tm):
    m, d = x.shape
    n = w.shape[1]
    return pl.pallas_call(
        _norm_proj_body,
        grid=(m // tm,),
        in_specs=[pl.BlockSpec((tm, d), lambda i: (i, 0)), _full((1, d)), _full((d, n))],
        out_specs=pl.BlockSpec((tm, n), lambda i: (i, 0)),
        out_shape=jax.ShapeDtypeStruct((m, n), F32),
        compiler_params=_cparams(("parallel",)),
        name="norm_proj",
    )(x, g, w)


def _norm_qkv_rope_body(x_ref, g_ref, w_ref, cos_ref, sa_ref, sb_ref, o_ref, *, n_rot):
    u = _rms(x_ref[...], g_ref[...]).astype(BF16)
    y = _dot(u, w_ref[...])
    cos, sa, sb = cos_ref[...], sa_ref[...], sb_ref[...]
    for c in range(n_rot // LANE):
        t = y[:, c * LANE:(c + 1) * LANE]
        o_ref[:, c * LANE:(c + 1) * LANE] = (t * cos + pltpu.roll(t, LANE - ROT_DIM // 2, 1) * sa
                                             + pltpu.roll(t, ROT_DIM // 2, 1) * sb)
    o_ref[:, n_rot:] = y[:, n_rot:]


def _norm_qkv_rope(x, g, w, tabs, tm, seq_blocks):
    m, d = x.shape
    n = w.shape[1]
    tab_spec = pl.BlockSpec((tm, LANE), lambda i: (i % seq_blocks, 0))
    return pl.pallas_call(
        functools.partial(_norm_qkv_rope_body, n_rot=2 * n // 3),
        grid=(m // tm,),
        in_specs=[pl.BlockSpec((tm, d), lambda i: (i, 0)), _full((1, d)), _full((d, n)),
                  tab_spec, tab_spec, tab_spec],
        out_specs=pl.BlockSpec((tm, n), lambda i: (i, 0)),
        out_shape=jax.ShapeDtypeStruct((m, n), F32),
        compiler_params=_cparams(("parallel",)),
        name="norm_qkv_rope",
    )(x, g, w, *tabs)


def _rope_chunk(t, cos, sa, sb):
    return t * cos + pltpu.roll(t, LANE - ROT_DIM // 2, 1) * sa + pltpu.roll(t, ROT_DIM // 2, 1) * sb


def _norm_qkv_prompt_body(*refs, n_rot, n_prev, nblks):
    x_ref, g_ref, w_ref, cos_ref, sa_ref, sb_ref = refs[:6]
    o_ref = refs[6 + n_prev]
    c_refs = refs[7 + n_prev:]
    ng = len(c_refs)
    t = pl.program_id(1)
    tb = pl.num_programs(1)
    tm = x_ref.shape[0]
    u = _rms(x_ref[...], g_ref[...]).astype(BF16)
    y = _dot(u, w_ref[...])
    cos, sa, sb = cos_ref[...], sa_ref[...], sb_ref[...]
    vals = []
    for c in range(y.shape[1] // LANE):
        v = y[:, c * LANE:(c + 1) * LANE]
        if c * LANE < n_rot:
            v = _rope_chunk(v, cos, sa, sb)
        o_ref[c] = v
        vals.append(v)
    per_sec = len(vals) // 3
    per_grp = ATT_GW // LANE
    for gi, c_ref in enumerate(c_refs):
        kw = c_ref.shape[2]
        nblk = nblks[gi]

        @pl.when(t >= tb - nblk)
        def _(gi=gi, c_ref=c_ref, kw=kw):
            for kv in range(2):
                for hp in range(per_grp):
                    v = vals[(1 + kv) * per_sec + gi * per_grp + hp]
                    c_ref[kv, hp * LANE:(hp + 1) * LANE, :] = v[tm - kw:, :].T


def _norm_qkv_prompt(x, g, w, tabs, prev, layer, n_layers, n_seq, tm):
    m, d = x.shape
    n = w.shape[1]
    t_len = m // n_seq
    tb = t_len // tm
    tab_spec = pl.BlockSpec((tm, LANE), lambda b, t: (t, 0))
    c_specs, c_shapes, nblks = [], [], []
    for gi, (win, _) in enumerate(ATT_GROUPS):
        keep = min(win, t_len)
        kw = min(keep, tm)
        nblk = keep // kw
        assert keep % kw == 0
        nblks.append(nblk)
        c_specs.append(pl.BlockSpec((None, 2, ATT_GW, kw),
                                    lambda b, t, nblk=nblk: (layer * n_seq + b, 0, 0, jnp.maximum(t - (tb - nblk), 0))))
        c_shapes.append(jax.ShapeDtypeStruct((n_layers * n_seq, 2, ATT_GW, keep), F32))
    res = pl.pallas_call(
        functools.partial(_norm_qkv_prompt_body, n_rot=2 * n // 3, n_prev=len(prev), nblks=tuple(nblks)),
        grid=(n_seq, tb),
        in_specs=[pl.BlockSpec((tm, d), lambda b, t: (b * tb + t, 0)), _full((1, d)), _resident((d, n)),
                  tab_spec, tab_spec, tab_spec] + [pl.BlockSpec(memory_space=pl.ANY)] * len(prev),
        out_specs=[pl.BlockSpec((n // LANE, tm, LANE), lambda b, t: (0, b * tb + t, 0))] + c_specs,
        out_shape=[jax.ShapeDtypeStruct((n // LANE, m, LANE), F32)] + c_shapes,
        input_output_aliases={6 + k: 1 + k for k in range(len(prev))},
        compiler_params=_cparams(("parallel", "arbitrary")),
        name="norm_qkv_prompt",
    )(x, g, w, *tabs, *prev)
    return res[0], list(res[1:])


def _rope_tables(pos):
    half = ROT_DIM // 2
    inv = ROPE_THETA ** (-jnp.arange(half, dtype=F32) / half)
    ang = pos.astype(F32)[:, None] * inv[None, :]
    cos, sin = jnp.cos(ang), jnp.sin(ang)
    p = pos.shape[0]
    rest = ATT_HEAD_DIM - ROT_DIM
    c = jnp.concatenate([cos, cos, jnp.ones((p, rest), F32)], axis=1)
    sa = jnp.concatenate([-sin, jnp.zeros((p, half + rest), F32)], axis=1)
    sb = jnp.concatenate([jnp.zeros((p, half), F32), sin, jnp.zeros((p, rest), F32)], axis=1)
    rep = LANE // ATT_HEAD_DIM
    return tuple(jnp.tile(t, (1, rep)) for t in (c, sa, sb))


def _proj_res_body(y_ref, w_ref, x_ref, g_ref, o_ref):
    f = _dot(y_ref[...].astype(BF16), w_ref[...])
    o_ref[...] = x_ref[...] + _rms(f, g_ref[...])


def _proj_res(y, w, x, g, tm):
    m, k = y.shape
    d = w.shape[1]
    return pl.pallas_call(
        _proj_res_body,
        grid=(m // tm,),
        in_specs=[pl.BlockSpec((tm, k), lambda i: (i, 0)), _full((k, d)),
                  pl.BlockSpec((tm, d), lambda i: (i, 0)), _full((1, d))],
        out_specs=pl.BlockSpec((tm, d), lambda i: (i, 0)),
        out_shape=jax.ShapeDtypeStruct((m, d), F32),
        compiler_params=_cparams(("parallel",)),
        name="proj_res",
    )(y, w, x, g)


def _attn_out_body(o0, o1, o2, l0, l1, l2, w_ref, x_ref, g_ref, out_ref):
    def load(ref):
        if len(ref.shape) == 2:
            return ref[...]
        return jnp.concatenate([ref[c] for c in range(ref.shape[0])], axis=1)

    ls = [load(l0), load(l1), load(l2)]
    mx = jnp.maximum(jnp.maximum(ls[0], ls[1]), ls[2])
    es = [jnp.exp(l - mx) for l in ls]
    den = es[0] + es[1] + es[2]
    og = jnp.concatenate([(es[gi] / den * load(o_ref)).astype(BF16) for gi, o_ref in enumerate((o0, o1, o2))], axis=1)
    out_ref[...] = x_ref[...] + _rms(_dot(og, w_ref[...]), g_ref[...])


def _attn_out(os_, ls_, w, x, g, tm):
    m, d = x.shape
    if os_[0].ndim == 2:
        blk = pl.BlockSpec((tm, ATT_GW), lambda i: (i, 0))
    else:
        blk = pl.BlockSpec((ATT_GW // LANE, tm, LANE), lambda i: (0, i, 0))
    return pl.pallas_call(
        _attn_out_body,
        grid=(m // tm,),
        in_specs=[blk] * 6 + [_full(w.shape), pl.BlockSpec((tm, d), lambda i: (i, 0)), _full((1, d))],
        out_specs=pl.BlockSpec((tm, d), lambda i: (i, 0)),
        out_shape=jax.ShapeDtypeStruct((m, d), F32),
        compiler_params=_cparams(("parallel",)),
        name="attn_out",
    )(*os_, *ls_, w, x, g)


def _mem_kv_body(x_ref, g_ref, w_ref, o_ref, t_ref):
    tm = x_ref.shape[0]
    u = _rms(x_ref[...], g_ref[...]).astype(BF16)
    y = _dot(u, w_ref[...])
    o_ref[...] = y
    hd = y.shape[1] // (2 * MEM_HEADS)
    nch = hd // LANE
    tok_rows = 2 * nch * MEM_HEADS
    for kv in range(2):
        for h in range(MEM_HEADS):
            for c in range(nch):
                col = (kv * MEM_HEADS + h) * hd + c * LANE
                t_ref[pl.ds((kv * nch + c) * MEM_HEADS + h, tm, stride=tok_rows), :] = y[:, col:col + LANE]


def _mem_kv(mem, g, w, tm):
    m, d = mem.shape
    depth, _, n = w.shape
    tok_rows = n // LANE
    return pl.pallas_call(
        _mem_kv_body,
        grid=(depth, m // tm),
        in_specs=[pl.BlockSpec((tm, d), lambda l, i: (i, 0)),
                  pl.BlockSpec((None, 1, d), lambda l, i: (l, 0, 0)),
                  pl.BlockSpec((None, d, n), lambda l, i: (l, 0, 0))],
        out_specs=[pl.BlockSpec((None, tm, n), lambda l, i: (l, i, 0)),
                   pl.BlockSpec((None, tm * tok_rows, LANE), lambda l, i: (l, i, 0))],
        out_shape=[jax.ShapeDtypeStruct((depth, m, n), F32),
                   jax.ShapeDtypeStruct((depth, m * tok_rows, LANE), F32)],
        compiler_params=_cparams(("parallel", "parallel")),
        name="mem_kv",
    )(mem, g, w)


def _xattn_prompt_body(x_ref, gpre_ref, gpost_ref, wq_ref, kv_ref, wo_ref, o_ref, obuf):
    x = x_ref[...]
    d = x.shape[1]
    hd = d // MEM_HEADS
    u = _rms(x, gpre_ref[...]).astype(BF16)
    q = _dot(u, wq_ref[...]).astype(BF16)
    scale = hd ** -0.5
    for h in range(MEM_HEADS):
        kh = kv_ref[:, h * hd:(h + 1) * hd].astype(BF16)
        vh = kv_ref[:, d + h * hd:d + (h + 1) * hd].astype(BF16)
        s = _dot_nt(q[:, h * hd:(h + 1) * hd], kh) * scale
        mx = jnp.max(s, axis=-1, keepdims=True)
        p = jnp.exp(s - mx)
        den = jnp.sum(p, axis=-1, keepdims=True)
        obuf[:, h * hd:(h + 1) * hd] = (_dot(p.astype(BF16), vh) / den).astype(BF16)
    f = _dot(obuf[...], wo_ref[...])
    o_ref[...] = x + _rms(f, gpost_ref[...])


def _xattn_prompt(x, gpre, gpost, wq, kv, layer, wo, n_seq, tm):
    m, d = x.shape
    tb = m // n_seq // tm
    n_mem = kv.shape[1] // n_seq
    return pl.pallas_call(
        _xattn_prompt_body,
        grid=(n_seq, tb),
        in_specs=[pl.BlockSpec((tm, d), lambda b, t: (b * tb + t, 0)), _full((1, d)), _full((1, d)),
                  _resident((d, d)), pl.BlockSpec((None, n_mem, 2 * d), lambda b, t: (layer, b, 0)),
                  _resident((d, d))],
        out_specs=pl.BlockSpec((tm, d), lambda b, t: (b * tb + t, 0)),
        out_shape=jax.ShapeDtypeStruct((m, d), F32),
        scratch_shapes=[pltpu.VMEM((tm, d), BF16)],
        compiler_params=_cparams(("parallel", "parallel")),
        name="xattn_prompt",
    )(x, gpre, gpost, wq, kv, wo)


def _xattn_sample_body(x_ref, gpre_ref, gpost_ref, wq_ref, kv_ref, wo_ref, o_ref, q_scr, o_scr):
    b = pl.program_id(0)
    d = x_ref.shape[1]
    hd = d // MEM_HEADS

    @pl.when(b == 0)
    def _():
        u = _rms(x_ref[...], gpre_ref[...]).astype(BF16)
        q_scr[...] = _dot(u, wq_ref[...]) * (hd ** -0.5)

    qb = q_scr[pl.ds(b, 1), :]
    nch = hd // LANE
    tok_rows = 2 * nch * MEM_HEADS
    n_mem = kv_ref.shape[0] // tok_rows
    o_parts = []
    for h in range(MEM_HEADS):
        s = None
        for c in range(nch):
            kc = kv_ref[pl.ds(c * MEM_HEADS + h, n_mem, stride=tok_rows), :]
            part = jnp.sum(kc * qb[:, h * hd + c * LANE:h * hd + (c + 1) * LANE], axis=1, keepdims=True)
            s = part if s is None else s + part
        mx = jnp.max(s, axis=0, keepdims=True)
        p = jnp.exp(s - mx)
        den = jnp.sum(p, axis=0, keepdims=True)
        for c in range(nch):
            vc = kv_ref[pl.ds((nch + c) * MEM_HEADS + h, n_mem, stride=tok_rows), :]
            o_parts.append(jnp.sum(p * vc, axis=0, keepdims=True) / den)
    o_scr[pl.ds(b, 1), :] = jnp.concatenate(o_parts, axis=1)

    @pl.when(b == pl.num_programs(0) - 1)
    def _():
        f = _dot(o_scr[...].astype(BF16), wo_ref[...])
        o_ref[...] = x_ref[...] + _rms(f, gpost_ref[...])


def _xattn_sample(x, gpre, gpost, wq, kv_all, layer, wo):
    m, d = x.shape
    return pl.pallas_call(
        _xattn_sample_body,
        grid=(m,),
        in_specs=[_full((m, d)), _full((1, d)), _full((1, d)), _full((d, d)),
                  pl.BlockSpec((None, kv_all.shape[1], LANE), lambda b: (layer * m + b, 0, 0)), _full((d, d))],
        out_specs=_full((m, d)),
        out_shape=jax.ShapeDtypeStruct((m, d), F32),
        scratch_shapes=[pltpu.VMEM((m, d), F32), pltpu.VMEM((m, d), F32)],
        compiler_params=_cparams(("arbitrary",)),
        name="xattn_sample",
    )(x, gpre, gpost, wq, kv_all, wo)


def _ffn_prompt_body(x_ref, gpre_ref, gpost_ref, wg_ref, wu_ref, cw_ref, cb_ref, wd_ref, hist_ref,
                     o_ref, nh_ref, gbuf):
    t = pl.program_id(1)
    tm = x_ref.shape[0]
    k = FFN_CONV - 1
    base = SUBLANE - k

    @pl.when(t == 0)
    def _():
        gbuf[base:SUBLANE, :] = hist_ref[...]

    x = x_ref[...]
    u = _rms(x, gpre_ref[...]).astype(BF16)
    gbuf[SUBLANE:SUBLANE + tm, :] = _dot(u, wg_ref[...])
    up = _dot(u, wu_ref[...])
    gc = gbuf[base:base + tm, :] * cw_ref[0:1, :] + cb_ref[...]
    for j in range(1, FFN_CONV):
        gc = gc + gbuf[base + j:base + j + tm, :] * cw_ref[j:j + 1, :]
    hmid = (_silu(gc) * up).astype(BF16)
    f = _dot(hmid, wd_ref[...])
    o_ref[...] = x + _rms(f, gpost_ref[...])
    last = gbuf[tm + base:tm + SUBLANE, :]
    nh_ref[...] = last
    gbuf[base:SUBLANE, :] = last


def _ffn_prompt(x, gpre, gpost, wg, wu, cw, cb, wd, hist, n_seq, tm):
    m, d = x.shape
    f = wg.shape[1]
    tb = m // n_seq // tm
    k = FFN_CONV - 1
    return pl.pallas_call(
        _ffn_prompt_body,
        grid=(n_seq, tb),
        in_specs=[pl.BlockSpec((tm, d), lambda b, t: (b * tb + t, 0)), _full((1, d)), _full((1, d)),
                  _resident((d, f)), _resident((d, f)), _full((FFN_CONV, f)), _full((1, f)), _resident((f, d)),
                  pl.BlockSpec((None, k, f), lambda b, t: (b, 0, 0))],
        out_specs=[pl.BlockSpec((tm, d), lambda b, t: (b * tb + t, 0)),
                   pl.BlockSpec((None, k, f), lambda b, t: (b, 0, 0))],
        out_shape=[jax.ShapeDtypeStruct((m, d), F32), jax.ShapeDtypeStruct((n_seq, k, f), F32)],
        scratch_shapes=[pltpu.VMEM((tm + SUBLANE, f), F32)],
        compiler_params=_cparams(("parallel", "arbitrary")),
        name="ffn_prompt",
    )(x, gpre, gpost, wg, wu, cw, cb, wd, hist)


def _ffn_sample_body(x_ref, gpre_ref, gpost_ref, wg_ref, wu_ref, cw_ref, cb_ref, wd_ref, hist_ref,
                     o_ref, nh_ref, u_scr, acc):
    j = pl.program_id(0)

    @pl.when(j == 0)
    def _():
        u_scr[...] = _rms(x_ref[...], gpre_ref[...]).astype(BF16)
        acc[...] = jnp.zeros_like(acc)

    u = u_scr[...]
    gate = _dot(u, wg_ref[...])
    up = _dot(u, wu_ref[...])
    gc = hist_ref[0] * cw_ref[0:1, :] + cb_ref[...]
    for k in range(1, FFN_CONV - 1):
        gc = gc + hist_ref[k] * cw_ref[k:k + 1, :]
        nh_ref[k - 1] = hist_ref[k]
    gc = gc + gate * cw_ref[FFN_CONV - 1:FFN_CONV, :]
    nh_ref[FFN_CONV - 2] = gate
    hmid = (_silu(gc) * up).astype(BF16)
    acc[...] += _dot(hmid, wd_ref[...])

    @pl.when(j == pl.num_programs(0) - 1)
    def _():
        o_ref[...] = x_ref[...] + _rms(acc[...], gpost_ref[...])


def _ffn_sample(x, gpre, gpost, wg, wu, cw, cb, wd, hist_t, tn):
    m, d = x.shape
    f = wg.shape[1]
    k = FFN_CONV - 1
    return pl.pallas_call(
        _ffn_sample_body,
        grid=(f // tn,),
        in_specs=[_full((m, d)), _full((1, d)), _full((1, d)),
                  pl.BlockSpec((d, tn), lambda j: (0, j)), pl.BlockSpec((d, tn), lambda j: (0, j)),
                  pl.BlockSpec((FFN_CONV, tn), lambda j: (0, j)), pl.BlockSpec((1, tn), lambda j: (0, j)),
                  pl.BlockSpec((tn, d), lambda j: (j, 0)),
                  pl.BlockSpec((k, m, tn), lambda j: (0, 0, j))],
        out_specs=[_full((m, d)), pl.BlockSpec((k, m, tn), lambda j: (0, 0, j))],
        out_shape=[jax.ShapeDtypeStruct((m, d), F32), jax.ShapeDtypeStruct((k, m, f), F32)],
        scratch_shapes=[pltpu.VMEM((m, d), BF16), pltpu.VMEM((m, d), F32)],
        compiler_params=_cparams(("arbitrary",)),
        name="ffn_sample",
    )(x, gpre, gpost, wg, wu, cw, cb, wd, hist_t)


def _ssd_activations(u, w_ref, cw_ref, cb_ref, dtb_ref, act, cbuf, clast_ref, d_inner):
    tm = u.shape[0]
    k = SSM_CONV - 1
    base = SUBLANE - k
    conv_dim = cw_ref.shape[1]
    act[:, :d_inner] = _silu(_dot(u, w_ref[:, :d_inner]))
    act[:, d_inner + conv_dim:] = _softplus(_dot(u, w_ref[:, d_inner + conv_dim:]) + dtb_ref[...])
    for c0 in range(0, conv_dim, CONV_COLS):
        cs = slice(c0, c0 + CONV_COLS)
        cbuf[SUBLANE:SUBLANE + tm, cs] = _dot(u, w_ref[:, d_inner + c0:d_inner + c0 + CONV_COLS])
        xc = cbuf[base:base + tm, cs] * cw_ref[0:1, cs] + cb_ref[:, cs]
        for j in range(1, SSM_CONV):
            xc = xc + cbuf[base + j:base + j + tm, cs] * cw_ref[j:j + 1, cs]
        act[:, d_inner + c0:d_inner + c0 + CONV_COLS] = _silu(xc)
    last = cbuf[tm + base:tm + SUBLANE, :]
    clast_ref[...] = last
    cbuf[base:SUBLANE, :] = last


def _ssd_chunk(act_ref, alog_ref, dx_ref, nw_ref, e_ref, tril_ref, state, ybuf, y_ref, d_inner, n_heads):
    L = SSM_CHUNK
    gn = SSM_GROUPS * D_STATE
    conv_dim = d_inner + 2 * gn
    hpg = n_heads // SSM_GROUPS
    gw = d_inner // SSM_GROUPS
    xs = act_ref[:, d_inner:2 * d_inner]
    xs_b = xs.astype(BF16)
    e = e_ref[...]
    dt = act_ref[:, d_inner + conv_dim:]
    a = dt * (-jnp.exp(alog_ref[...]))
    tril = tril_ref[...]
    a_hi, a_mid, a_lo = _split3(a)
    acum = _dot(tril, a_hi) + _dot(tril, a_mid) + _dot(tril, a_lo)
    acum_t = acum.T
    dt_t = dt.T
    a_last = acum[L - 1:L, :]
    d_acc = _expand(jnp.exp(acum), e, n_heads)
    xde = (xs * _expand(dt * jnp.exp(a_last - acum), e, n_heads)).astype(BF16)
    row = lax.broadcasted_iota(jnp.int32, (L, L), 0)
    col = lax.broadcasted_iota(jnp.int32, (L, L), 1)
    causal = row >= col

    for g in range(SSM_GROUPS):
        b0 = 2 * d_inner + g * D_STATE
        gs = slice(g * gw, (g + 1) * gw)
        b_f = act_ref[:, b0:b0 + D_STATE]
        bg = b_f.astype(BF16)
        cg = act_ref[:, b0 + gn:b0 + gn + D_STATE].astype(BF16)
        cb = _dot_nt(cg, bg)
        y_inter = _dot(cg, state[:, gs].astype(BF16)) * d_acc[:, gs]
        for j in range(hpg):
            hd = g * hpg + j
            seg = acum[:, hd:hd + 1] - acum_t[hd:hd + 1, :]
            w = cb * jnp.exp(jnp.where(causal, seg, -jnp.inf)) * dt_t[hd:hd + 1, :]
            sl = slice(hd * SSM_HEAD_DIM, (hd + 1) * SSM_HEAD_DIM)
            ybuf[:, sl] = _dot(w.astype(BF16), xs_b[:, sl])
        ybuf[:, gs] += y_inter
        state[:, gs] = state[:, gs] * d_acc[L - 1:L, gs] + _dot(b_f.T.astype(BF16), xde[:, gs])

    y = ybuf[...] + xs * dx_ref[...]
    gated = y * act_ref[:, :d_inner]
    for g in range(SSM_GROUPS):
        seg = gated[:, g * gw:(g + 1) * gw]
        y_ref[:, g * gw:(g + 1) * gw] = _rms(seg, nw_ref[:, g * gw:(g + 1) * gw]).astype(BF16)


N_SSD_IN = 15


def _ssd_layer_body(x_ref, gpre_ref, gpost_ref, w_ref, cw_ref, cb_ref, dtb_ref, alog_ref, dx_ref, nw_ref,
                    e_ref, tril_ref, wo_ref, hist_ref, h0_ref, *rest, d_inner, n_heads):
    o_ref, hlast_ref, clast_ref, act, cbuf, state, ybuf, ybf = rest[-8:]
    t = pl.program_id(1)
    tm = x_ref.shape[0]
    L = SSM_CHUNK
    base = SUBLANE - (SSM_CONV - 1)

    @pl.when(t == 0)
    def _():
        cbuf[base:SUBLANE, :] = hist_ref[...]
        state[...] = h0_ref[...].T

    x = x_ref[...]
    u = _rms(x, gpre_ref[...]).astype(BF16)
    _ssd_activations(u, w_ref, cw_ref, cb_ref, dtb_ref, act, cbuf, clast_ref, d_inner)
    for ci in range(tm // L):
        _ssd_chunk(act.at[ci * L:(ci + 1) * L, :], alog_ref, dx_ref, nw_ref, e_ref, tril_ref, state, ybuf,
                   ybf.at[ci * L:(ci + 1) * L, :], d_inner, n_heads)
    o_ref[...] = x + _rms(_dot(ybf[...], wo_ref[...]), gpost_ref[...])

    @pl.when(t == pl.num_programs(1) - 1)
    def _():
        hlast_ref[...] = state[...].T


def _ssd_layer(x, gpre, gpost, w, cw, cb, dtb, alog, dx, nw, e, tril, wo, hist, h0, prev, layer, n_layers,
               n_seq, d_inner, n_heads, tm):
    m, d = x.shape
    n = w.shape[1]
    tb = m // n_seq // tm
    conv_dim = cw.shape[1]
    k = SSM_CONV - 1
    rows = n_heads * SSM_HEAD_DIM
    assert conv_dim % CONV_COLS == 0 and tm % SSM_CHUNK == 0
    ins = (x, gpre, gpost, w, cw, cb, dtb, alog, dx, nw, e, tril, wo, hist, h0)
    assert len(ins) == N_SSD_IN
    x_spec = pl.BlockSpec((tm, d), lambda b, t: (b * tb + t, 0))
    return pl.pallas_call(
        functools.partial(_ssd_layer_body, d_inner=d_inner, n_heads=n_heads),
        grid=(n_seq, tb),
        in_specs=[x_spec, _full((1, d)), _full((1, d)), _resident(w.shape), _full(cw.shape), _full(cb.shape),
                  _full(dtb.shape), _full(alog.shape), _full(dx.shape), _full(nw.shape), _full(e.shape),
                  _full(tril.shape), _resident(wo.shape),
                  pl.BlockSpec((None, k, conv_dim), lambda b, t: (b, 0, 0)),
                  pl.BlockSpec((None, rows, D_STATE), lambda b, t: (b, 0, 0))]
                 + [pl.BlockSpec(memory_space=pl.ANY)] * len(prev),
        out_specs=[x_spec,
                   pl.BlockSpec((None, rows, D_STATE), lambda b, t: (layer * n_seq + b, 0, 0)),
                   pl.BlockSpec((None, k, conv_dim), lambda b, t: (b, 0, 0))],
        out_shape=[jax.ShapeDtypeStruct((m, d), F32),
                   jax.ShapeDtypeStruct((n_layers * n_seq, rows, D_STATE), F32),
                   jax.ShapeDtypeStruct((n_seq, k, conv_dim), F32)],
        input_output_aliases={N_SSD_IN: 1} if prev else {},
        scratch_shapes=[pltpu.VMEM((tm, n), F32), pltpu.VMEM((tm + SUBLANE, conv_dim), F32),
                        pltpu.VMEM((D_STATE, rows), F32), pltpu.VMEM((SSM_CHUNK, d_inner), F32),
                        pltpu.VMEM((tm, d_inner), BF16)],
        compiler_params=_cparams(("parallel", "arbitrary")),
        name="ssd_layer",
    )(*ins, *prev)


def _ssd_step_pre_body(proj_ref, hist_ref, cw_ref, cb_ref, dtb_ref, alog_ref, e_ref,
                       xs_ref, b_ref, c_ref, xdt_t_ref, dec_b_ref, clast_ref, *, d_inner):
    gn = SSM_GROUPS * D_STATE
    conv_dim = d_inner + 2 * gn
    n_heads = d_inner // SSM_HEAD_DIM
    m = proj_ref.shape[0]
    xbc = proj_ref[:, d_inner:d_inner + conv_dim]
    xc = hist_ref[0] * cw_ref[0:1, :] + cb_ref[...]
    for j in range(1, SSM_CONV - 1):
        xc = xc + hist_ref[j] * cw_ref[j:j + 1, :]
        clast_ref[j - 1] = hist_ref[j]
    xc = _silu(xc + xbc * cw_ref[SSM_CONV - 1:SSM_CONV, :])
    clast_ref[SSM_CONV - 2] = xbc
    xs = xc[:, :d_inner]
    xs_ref[...] = xs
    b_ref[...] = xc[:, d_inner:d_inner + gn]
    c_ref[...] = xc[:, d_inner + gn:]
    dt = _softplus(proj_ref[:, d_inner + conv_dim:] + dtb_ref[...])
    dec = jnp.exp(dt * (-jnp.exp(alog_ref[...])))
    pad = jnp.zeros((LANE - m, d_inner), F32)
    xdt_t_ref[...] = jnp.concatenate([xs * _expand(dt, e_ref[...], n_heads), pad], axis=0).T
    for j in range(n_heads):
        dec_b_ref[j] = jnp.broadcast_to(dec[:, j:j + 1], (m, LANE))


def _ssd_step_pre(proj, hist_t, cw, cb, dtb, alog, e, d_inner):
    m = proj.shape[0]
    gn = SSM_GROUPS * D_STATE
    conv_dim = cw.shape[1]
    k = SSM_CONV - 1
    n_heads = d_inner // SSM_HEAD_DIM
    args = (proj, hist_t, cw, cb, dtb, alog, e)
    return pl.pallas_call(
        functools.partial(_ssd_step_pre_body, d_inner=d_inner),
        grid=(1,),
        in_specs=[_full(a.shape) for a in args],
        out_specs=[_full((m, d_inner)), _full((m, gn)), _full((m, gn)), _full((d_inner, LANE)),
                   _full((n_heads, m, LANE)), _full((k, m, conv_dim))],
        out_shape=[jax.ShapeDtypeStruct((m, d_inner), F32), jax.ShapeDtypeStruct((m, gn), F32),
                   jax.ShapeDtypeStruct((m, gn), F32), jax.ShapeDtypeStruct((d_inner, LANE), F32),
                   jax.ShapeDtypeStruct((n_heads, m, LANE), F32), jax.ShapeDtypeStruct((k, m, conv_dim), F32)],
        compiler_params=_cparams(("arbitrary",)),
        name="ssd_step_pre",
    )(*args)


def _hi_lo(v):
    hi = v.astype(BF16)
    return hi, (v - hi.astype(F32)).astype(BF16)


def _ssd_step_state_body(h0_ref, xdt_t_ref, dec_b_ref, b_ref, c_ref, *rest):
    hn_ref, y_t_ref = rest[-2:]
    m = h0_ref.shape[0]
    wide = jnp.concatenate([b_ref[...]] * m, axis=1)
    own = (lax.broadcasted_iota(jnp.int32, wide.shape, 1) // D_STATE
           == lax.broadcasted_iota(jnp.int32, wide.shape, 0))
    bbd = jnp.concatenate([jnp.where(own, wide, 0.0), jnp.zeros((LANE - m, m * D_STATE), F32)], axis=0)
    b_hi, b_lo = _hi_lo(bbd)
    x_hi, x_lo = _hi_lo(xdt_t_ref[...])
    upd = _dot(jnp.concatenate([x_hi, x_lo, x_hi], axis=1), jnp.concatenate([b_hi, b_hi, b_lo], axis=0))
    prods = []
    for b in range(m):
        h = h0_ref[b] * dec_b_ref[b:b + 1, :] + upd[:, b * D_STATE:(b + 1) * D_STATE]
        hn_ref[b] = h
        prods.append(h * c_ref[b:b + 1, :])
    p_hi, p_lo = _hi_lo(jnp.concatenate(prods, axis=0))
    ones = jnp.ones((D_STATE, LANE), BF16)
    sums = _dot(p_hi, ones) + _dot(p_lo, ones)
    lane = lax.broadcasted_iota(jnp.int32, (SSM_HEAD_DIM, LANE), 1)
    acc = jnp.zeros((SSM_HEAD_DIM, LANE), F32)
    for b in range(m):
        acc = jnp.where(lane == b, sums[b * SSM_HEAD_DIM:(b + 1) * SSM_HEAD_DIM, :], acc)
    y_t_ref[...] = acc


def _ssd_step_state(h_all, prev, layer, xdt_t, dec_t, bm, cm, n_heads):
    m = bm.shape[0]
    hpg = n_heads // SSM_GROUPS
    h_spec = pl.BlockSpec((m, None, SSM_HEAD_DIM, D_STATE), lambda j: (layer, j, 0, 0))
    return pl.pallas_call(
        _ssd_step_state_body,
        grid=(n_heads,),
        in_specs=[h_spec,
                  pl.BlockSpec((SSM_HEAD_DIM, LANE), lambda j: (j, 0)),
                  pl.BlockSpec((None, m, LANE), lambda j: (j, 0, 0)),
                  pl.BlockSpec((m, D_STATE), lambda j: (0, j // hpg)),
                  pl.BlockSpec((m, D_STATE), lambda j: (0, j // hpg))]
                 + [pl.BlockSpec(memory_space=pl.ANY)] * len(prev),
        out_specs=[h_spec, pl.BlockSpec((SSM_HEAD_DIM, LANE), lambda j: (j, 0))],
        out_shape=[jax.ShapeDtypeStruct(h_all.shape, F32),
                   jax.ShapeDtypeStruct((n_heads * SSM_HEAD_DIM, LANE), F32)],
        input_output_aliases={5: 0} if prev else {},
        compiler_params=_cparams(("parallel",)),
        name="ssd_step_state",
    )(h_all, xdt_t, dec_t, bm, cm, *prev)


def _ssd_step_post_body(y_t_ref, xs_ref, z_ref, dx_ref, nw_ref, o_ref):
    m, d_inner = xs_ref.shape
    gw = d_inner // SSM_GROUPS
    y = y_t_ref[...].T[:m, :] + xs_ref[...] * dx_ref[...]
    gated = y * _silu(z_ref[...])
    for g in range(SSM_GROUPS):
        seg = gated[:, g * gw:(g + 1) * gw]
        o_ref[:, g * gw:(g + 1) * gw] = _rms(seg, nw_ref[:, g * gw:(g + 1) * gw]).astype(BF16)


def _ssd_step_post(y_t, xs, proj, dx, nw):
    m, d_inner = xs.shape
    return pl.pallas_call(
        _ssd_step_post_body,
        grid=(1,),
        in_specs=[_full(y_t.shape), _full(xs.shape), pl.BlockSpec((m, d_inner), lambda i: (0, 0)),
                  _full(dx.shape), _full(nw.shape)],
        out_specs=_full((m, d_inner)),
        out_shape=jax.ShapeDtypeStruct((m, d_inner), BF16),
        compiler_params=_cparams(("arbitrary",)),
        name="ssd_step_post",
    )(y_t, xs, proj, dx, nw)


def _band_attn_body(q_ref, kc_ref, kp_ref, vc_ref, vp_ref, o_ref, l_ref, *, win, dil):
    blk = pl.program_id(1)
    n_hp, bt, _ = q_ref.shape
    span = win // dil
    heads = LANE // ATT_HEAD_DIM
    qi = lax.broadcasted_iota(jnp.int32, (span, 2 * span), 0)
    ki = lax.broadcasted_iota(jnp.int32, (span, 2 * span), 1)
    band = (ki >= qi) & (ki <= qi + span)
    band_first = band & ((blk > 0) | (ki >= span))
    lane_q = lax.broadcasted_iota(jnp.int32, (span, LANE), 1)
    lane_k = lax.broadcasted_iota(jnp.int32, (2 * span, LANE), 1)

    def rows(ref, hp, start, r):
        if dil == 1:
            return ref[hp, start:start + span, :]
        return ref[hp, pl.ds(start + r, span, stride=dil), :]

    for hp in range(n_hp):
        for wi in range(bt // win):
            for r in range(dil):
                q = (rows(q_ref, hp, wi * win, r) * (ATT_HEAD_DIM ** -0.5)).astype(BF16)
                if wi == 0:
                    k_prev = rows(kp_ref, hp, kp_ref.shape[1] - win, r)
                    v_prev = rows(vp_ref, hp, vp_ref.shape[1] - win, r)
                else:
                    k_prev = rows(kc_ref, hp, (wi - 1) * win, r)
                    v_prev = rows(vc_ref, hp, (wi - 1) * win, r)
                kk = jnp.concatenate([k_prev, rows(kc_ref, hp, wi * win, r)], axis=0).astype(BF16)
                vv = jnp.concatenate([v_prev, rows(vc_ref, hp, wi * win, r)], axis=0).astype(BF16)
                valid = band_first if wi == 0 else band
                o_acc, l_acc = None, None
                for h in range(heads):
                    in_q = (lane_q >= h * ATT_HEAD_DIM) & (lane_q < (h + 1) * ATT_HEAD_DIM)
                    in_k = (lane_k >= h * ATT_HEAD_DIM) & (lane_k < (h + 1) * ATT_HEAD_DIM)
                    s = jnp.where(valid, _dot_nt(jnp.where(in_q, q, jnp.zeros_like(q)), kk), -jnp.inf)
                    mx = jnp.max(s, axis=-1, keepdims=True)
                    p = jnp.exp(s - mx)
                    den = jnp.sum(p, axis=-1, keepdims=True)
                    o_h = _dot(p.astype(BF16), jnp.where(in_k, vv, jnp.zeros_like(vv))) / den
                    l_h = jnp.where(in_q, mx + jnp.log(den), 0.0)
                    o_acc = o_h if o_acc is None else o_acc + o_h
                    l_acc = l_h if l_acc is None else l_acc + l_h
                if dil == 1:
                    o_ref[hp, wi * win:wi * win + span, :] = o_acc
                    l_ref[hp, wi * win:wi * win + span, :] = l_acc
                else:
                    o_ref[hp, pl.ds(wi * win + r, span, stride=dil), :] = o_acc
                    l_ref[hp, pl.ds(wi * win + r, span, stride=dil), :] = l_acc


def _band_attn(qkv_c, n_seq, gi, win, dil):
    nch, m, _ = qkv_c.shape
    t_len = m // n_seq
    bt = max(win, min(BAND_BLOCK, t_len))
    assert bt % win == 0 and t_len % bt == 0
    nblk = t_len // bt
    wpb = bt // win
    per_grp = ATT_GW // LANE
    ng = len(ATT_GROUPS)

    def cur(sec):
        return pl.BlockSpec((per_grp, bt, LANE), lambda b, k: (sec * ng + gi, b * nblk + k, 0))

    def prev(sec):
        return pl.BlockSpec((per_grp, win, LANE),
                            lambda b, k: (sec * ng + gi, jnp.maximum((b * nblk + k) * wpb - 1, 0), 0))

    out_spec = pl.BlockSpec((per_grp, bt, LANE), lambda b, k: (0, b * nblk + k, 0))
    out_sd = jax.ShapeDtypeStruct((per_grp, m, LANE), F32)
    return pl.pallas_call(
        functools.partial(_band_attn_body, win=win, dil=dil),
        grid=(n_seq, nblk),
        in_specs=[cur(0), cur(1), prev(1), cur(2), prev(2)],
        out_specs=[out_spec, out_spec],
        out_shape=[out_sd, out_sd],
        compiler_params=_cparams(("parallel", "arbitrary")),
        name="band_attn_w%d" % win,
    )(qkv_c, qkv_c, qkv_c, qkv_c, qkv_c)


def _row_to_cols(v):
    return jnp.concatenate([jnp.broadcast_to(v[:, c * LANE:(c + 1) * LANE], (LANE, LANE)).T
                            for c in range(v.shape[1] // LANE)], axis=0)


def _col_to_row(v):
    return jnp.concatenate([jnp.broadcast_to(v[c * LANE:(c + 1) * LANE, :], (LANE, LANE)).T[0:1, :]
                            for c in range(v.shape[0] // LANE)], axis=1)


def _attn_sample_body(*refs, dils, n_prev):
    ng = len(dils)
    qkv_ref = refs[0]
    bufs = refs[1:1 + ng]
    outs = refs[1 + ng + n_prev:]
    o_refs, l_refs, c_refs = outs[0:ng], outs[ng:2 * ng], outs[2 * ng:3 * ng]
    scale = ATT_HEAD_DIM ** -0.5
    for gi in range(ng):
        buf, o_ref, l_ref, c_ref = bufs[gi], o_refs[gi], l_refs[gi], c_refs[gi]
        wb = buf.shape[2]
        nch = wb // LANE
        q_c = _row_to_cols(qkv_ref[:, gi * ATT_GW:(gi + 1) * ATT_GW] * scale)
        kn_c = _row_to_cols(qkv_ref[:, (ng + gi) * ATT_GW:(ng + gi + 1) * ATT_GW])
        vn_c = _row_to_cols(qkv_ref[:, (2 * ng + gi) * ATT_GW:(2 * ng + gi + 1) * ATT_GW])
        lane = lax.broadcasted_iota(jnp.int32, (1, wb), 1)
        valid = (lane & (dils[gi] - 1)) == 0
        o_cols = []
        for h in range(ATT_HPG):
            rs = slice(h * ATT_HEAD_DIM, (h + 1) * ATT_HEAD_DIM)
            qh = q_c[rs, :]
            s = jnp.concatenate([jnp.sum(buf[0, rs, c * LANE:(c + 1) * LANE] * qh, axis=0, keepdims=True)
                                 for c in range(nch)], axis=1)
            s = jnp.where(valid, s, -jnp.inf)
            sn = jnp.sum(qh[:, 0:1] * kn_c[rs, 0:1], axis=0, keepdims=True)
            mx = jnp.maximum(jnp.max(s, axis=1, keepdims=True), sn)
            p = jnp.exp(s - mx)
            p_new = jnp.exp(sn - mx)
            den = jnp.sum(p, axis=1, keepdims=True) + p_new
            acc = buf[1, rs, 0:LANE] * p[:, 0:LANE]
            for c in range(1, nch):
                acc = acc + buf[1, rs, c * LANE:(c + 1) * LANE] * p[:, c * LANE:(c + 1) * LANE]
            o_cols.append((jnp.sum(acc, axis=1, keepdims=True) + p_new * vn_c[rs, 0:1]) / den)
            l_ref[:, rs] = jnp.broadcast_to(mx + jnp.log(den), (1, ATT_HEAD_DIM))
        o_ref[...] = _col_to_row(jnp.concatenate(o_cols, axis=0))
        last = lax.broadcasted_iota(jnp.int32, (ATT_HEAD_DIM, LANE), 1) == LANE - 1
        for kv, new_c in ((0, kn_c), (1, vn_c)):
            for h in range(ATT_HPG):
                rs = slice(h * ATT_HEAD_DIM, (h + 1) * ATT_HEAD_DIM)
                rolled = pltpu.roll(buf[kv, rs, :], wb - 1, 1)
                if nch > 1:
                    c_ref[kv, rs, 0:wb - LANE] = rolled[:, 0:wb - LANE]
                c_ref[kv, rs, wb - LANE:wb] = jnp.where(last, new_c[rs, :], rolled[:, wb - LANE:wb])


def _attn_sample(qkv, cache_views, prev, layer):
    m, w3 = qkv.shape
    ng = len(ATT_GROUPS)
    dils = tuple(dil for _, dil in ATT_GROUPS)
    for v, (win, dil) in zip(cache_views, ATT_GROUPS):
        assert v.shape[3] == win and win % LANE == 0 and dil & (dil - 1) == 0
    c_specs = [pl.BlockSpec((None, 2, ATT_GW, v.shape[3]), lambda b: (layer * m + b, 0, 0, 0)) for v in cache_views]
    o_sd = jax.ShapeDtypeStruct((m, 1, ATT_GW), F32)
    o_spec = pl.BlockSpec((None, 1, ATT_GW), lambda b: (b, 0, 0))
    n_in = 1 + ng
    res = pl.pallas_call(
        functools.partial(_attn_sample_body, dils=dils, n_prev=len(prev)),
        grid=(m,),
        in_specs=[pl.BlockSpec((None, 1, w3), lambda b: (b, 0, 0))] + c_specs
                 + [pl.BlockSpec(memory_space=pl.ANY)] * len(prev),
        out_specs=[o_spec] * (2 * ng) + c_specs,
        out_shape=[o_sd] * (2 * ng) + [jax.ShapeDtypeStruct(v.shape, v.dtype) for v in cache_views],
        input_output_aliases={n_in + k: 2 * ng + k for k in range(len(prev))},
        compiler_params=_cparams(("parallel",)),
        name="attn_sample",
    )(qkv.reshape(m, 1, w3), *cache_views, *prev)
    os_ = [r.reshape(m, ATT_GW) for r in res[0:ng]]
    ls_ = [r.reshape(m, ATT_GW) for r in res[ng:2 * ng]]
    return os_, ls_, list(res[2 * ng:])


def _prep_weights(norms, ssm_w_in, ssm_conv_b, ssm_dt_bias, ssm_a_log, ssm_d, ssm_norm_w, ssm_w_out, att_w_qkv,
                  att_w_o, mem_norm, xa_w_q, xa_w_kv, xa_w_o, ffn_w_gu, ffn_conv_b, ffn_w_down):
    n_ssm, d_model, in_dim = ssm_w_in.shape
    n_heads = ssm_dt_bias.shape[1]
    d_inner = n_heads * SSM_HEAD_DIM
    d_ff = ffn_w_down.shape[1]
    pad_heads = LANE - n_heads
    w = {}
    w['n_heads'], w['d_inner'] = n_heads, d_inner
    w['norms'] = norms[:, :, None, :]
    def per_layer(a, cols=slice(None)):
        return [a[i, :, cols].astype(BF16) for i in range(a.shape[0])]

    w['ssm_w_in'] = [jnp.pad(ssm_w_in[i], ((0, 0), (0, pad_heads))).astype(BF16) for i in range(n_ssm)]
    w['ssm_conv_b'] = ssm_conv_b[:, None, :]
    w['ssm_dt_bias'] = jnp.pad(ssm_dt_bias, ((0, 0), (0, pad_heads)))[:, None, :]
    w['ssm_a_log'] = jnp.pad(ssm_a_log, ((0, 0), (0, pad_heads)))[:, None, :]
    w['ssm_dx'] = jnp.repeat(ssm_d, SSM_HEAD_DIM, axis=1)[:, None, :]
    w['ssm_norm_w'] = ssm_norm_w[:, None, :]
    w['ssm_w_out'] = per_layer(ssm_w_out)
    w['att_w_qkv'] = per_layer(att_w_qkv)
    w['att_w_o'] = per_layer(att_w_o)
    w['mem_norm'] = mem_norm[:, None, :]
    w['xa_w_q'] = per_layer(xa_w_q)
    w['xa_w_kv'] = xa_w_kv.astype(BF16)
    w['xa_w_o'] = per_layer(xa_w_o)
    w['ffn_w_g'] = per_layer(ffn_w_gu, slice(0, d_ff))
    w['ffn_w_u'] = per_layer(ffn_w_gu, slice(d_ff, None))
    w['ffn_conv_b'] = ffn_conv_b[:, None, :]
    w['ffn_w_down'] = per_layer(ffn_w_down)
    e = np.zeros((LANE, d_inner), np.float32)
    assert 3 * n_heads <= LANE
    for h in range(n_heads):
        for part in range(3):
            e[part * n_heads + h, h * SSM_HEAD_DIM:(h + 1) * SSM_HEAD_DIM] = 1.0
    w['expand'] = jnp.asarray(e, BF16)
    w['tril'] = jnp.asarray(np.tril(np.ones((SSM_CHUNK, SSM_CHUNK), np.float32)), BF16)
    return w


def _prompt_trunk(x3, mem3, w, ssm_conv_w, ffn_conv_w):
    n, t_len, d = x3.shape
    assert t_len % SSM_CHUNK == 0 and all(t_len % win == 0 for win, _ in ATT_GROUPS)
    depth = w['norms'].shape[0]
    n_heads, d_inner = w['n_heads'], w['d_inner']
    conv_dim = ssm_conv_w.shape[2]
    d_ff = ffn_conv_w.shape[2]
    tm = PROMPT_ROWS_SSD
    tm_big = PROMPT_ROWS
    x = x3.reshape(n * t_len, d)
    mem = mem3.reshape(-1, d)
    kv_all, kv_rows = _mem_kv(mem, w['mem_norm'], w['xa_w_kv'], tm)
    tabs = _rope_tables(jnp.arange(t_len, dtype=jnp.int32))
    zero_hist = jnp.zeros((n, SSM_CONV - 1, conv_dim), F32)
    zero_h = jnp.zeros((n, n_heads * SSM_HEAD_DIM, D_STATE), F32)
    zero_fh = jnp.zeros((n, FFN_CONV - 1, d_ff), F32)
    n_ssm, n_att = (depth + 1) // 2, depth // 2
    new = {'ssm': [], 'ssm_conv': [], 'swa': [], 'ffn_conv': []}
    for i in range(depth):
        j = i // 2
        g = w['norms'][i]
        if i % 2 == 0:
            x, h_last, c_last = _ssd_layer(x, g[0], g[1], w['ssm_w_in'][j], ssm_conv_w[j], w['ssm_conv_b'][j],
                                           w['ssm_dt_bias'][j], w['ssm_a_log'][j], w['ssm_dx'][j],
                                           w['ssm_norm_w'][j], w['expand'], w['tril'], w['ssm_w_out'][j],
                                           zero_hist, zero_h, new['ssm'], j, n_ssm, n, d_inner, n_heads, tm)
            new['ssm'] = [h_last]
            new['ssm_conv'].append(c_last)
        else:
            qkv_c, new['swa'] = _norm_qkv_prompt(x, g[0], w['att_w_qkv'][j], tabs, new['swa'], j, n_att, n, tm_big)
            os_, ls_ = [], []
            for gi, (win, dil) in enumerate(ATT_GROUPS):
                o, l = _band_attn(qkv_c, n, gi, win, dil)
                os_.append(o)
                ls_.append(l)
            x = _attn_out(os_, ls_, w['att_w_o'][j], x, g[1], tm_big)
        x = _xattn_prompt(x, g[2], g[3], w['xa_w_q'][i], kv_all, i, w['xa_w_o'][i], n, tm_big)
        x, f_hist = _ffn_prompt(x, g[4], g[5], w['ffn_w_g'][i], w['ffn_w_u'][i], ffn_conv_w[i],
                                w['ffn_conv_b'][i], w['ffn_w_down'][i], zero_fh, n, tm_big)
        new['ffn_conv'].append(f_hist)
    n_mem = mem3.shape[1]
    hd = d // MEM_HEADS
    swa = [jnp.transpose(c.reshape(n_att, n, 2, ATT_HPG, ATT_HEAD_DIM, c.shape[3]), (0, 1, 5, 2, 3, 4))
           for c in new['swa']]
    p_mem = jnp.transpose(kv_rows.reshape(depth, n, n_mem, 2, hd // LANE, MEM_HEADS, LANE),
                          (0, 1, 2, 3, 5, 4, 6)).reshape(depth, n, n_mem, 2, MEM_HEADS, hd)
    return (x.reshape(n, t_len, d), new['ssm'][0].reshape(n_ssm, n, n_heads, SSM_HEAD_DIM, D_STATE),
            jnp.stack(new['ssm_conv']), swa, p_mem, jnp.stack(new['ffn_conv']))


def _sample_trunk(x3, w, ssm_conv_w, ffn_conv_w, state_ssm, state_ssm_conv, caches, cache_mem_kv, state_ffn_conv):
    m, t_len, d = x3.shape
    assert t_len == 1
    depth = w['norms'].shape[0]
    n_heads, d_inner = w['n_heads'], w['d_inner']
    n_ssm = state_ssm.shape[0]
    x = x3.reshape(m, d)
    tabs = _rope_tables(jnp.full((m,), PAST_LEN, jnp.int32))
    h_all = state_ssm.reshape(n_ssm * m, n_heads, SSM_HEAD_DIM, D_STATE)
    n_mem, hd = cache_mem_kv.shape[2], d // MEM_HEADS
    kv_all = jnp.transpose(cache_mem_kv.reshape(depth, m, n_mem, 2, MEM_HEADS, hd // LANE, LANE),
                           (0, 1, 2, 3, 5, 4, 6)).reshape(depth * m, n_mem * 2 * MEM_HEADS * (hd // LANE), LANE)
    cache_views = [jnp.transpose(c, (0, 1, 3, 4, 5, 2)).reshape(c.shape[0] * m, 2, ATT_GW, c.shape[2])
                   for c in caches]
    new_caches = []
    new = {'ssm': [], 'ssm_conv': [], 'ffn_conv': []}
    for i in range(depth):
        j = i // 2
        g = w['norms'][i]
        if i % 2 == 0:
            proj = _norm_proj(x, g[0], w['ssm_w_in'][j], m)
            hist_t = jnp.swapaxes(state_ssm_conv[j], 0, 1)
            xs, bm, cm, xdt_t, dec_t, c_last_t = _ssd_step_pre(proj, hist_t, ssm_conv_w[j], w['ssm_conv_b'][j],
                                                                w['ssm_dt_bias'][j], w['ssm_a_log'][j],
                                                                w['expand'], d_inner)
            h_new, y_t = _ssd_step_state(h_all, new['ssm'], j, xdt_t, dec_t, bm, cm, n_heads)
            y = _ssd_step_post(y_t, xs, proj, w['ssm_dx'][j], w['ssm_norm_w'][j])
            x = _proj_res(y, w['ssm_w_out'][j], x, g[1], m)
            new['ssm'] = [h_new]
            new['ssm_conv'].append(jnp.swapaxes(c_last_t, 0, 1))
        else:
            qkv = _norm_qkv_rope(x, g[0], w['att_w_qkv'][j], tabs, m, 1)
            os_, ls_, new_caches = _attn_sample(qkv, cache_views, new_caches, j)
            x = _attn_out(os_, ls_, w['att_w_o'][j], x, g[1], m)
        x = _xattn_sample(x, g[2], g[3], w['xa_w_q'][i], kv_all, i, w['xa_w_o'][i])
        hist_t = jnp.swapaxes(state_ffn_conv[i], 0, 1)
        x, f_hist_t = _ffn_sample(x, g[4], g[5], w['ffn_w_g'][i], w['ffn_w_u'][i], ffn_conv_w[i],
                                  w['ffn_conv_b'][i], w['ffn_w_down'][i], hist_t, 256)
        new['ffn_conv'].append(jnp.swapaxes(f_hist_t, 0, 1))
    new_caches = [jnp.transpose(nc.reshape(c.shape[0], m, 2, ATT_HPG, ATT_HEAD_DIM, c.shape[2]), (0, 1, 5, 2, 3, 4))
                  for nc, c in zip(new_caches, caches)]
    return (x.reshape(m, 1, d), new['ssm'][0].reshape(state_ssm.shape), jnp.stack(new['ssm_conv']), new_caches,
            jnp.stack(new['ffn_conv']))


def kernel(x_prompt, x_sample, mem_prompt, state_ssm, state_ssm_conv, cache_swa_kv_w128, cache_swa_kv_w512,
           cache_swa_kv_w2048, cache_mem_kv, state_ffn_conv, norms, ssm_w_in, ssm_conv_w, ssm_conv_b,
           ssm_dt_bias, ssm_a_log, ssm_d, ssm_norm_w, ssm_w_out, att_w_qkv, att_w_o, mem_norm, xa_w_q,
           xa_w_kv, xa_w_o, ffn_w_gu, ffn_conv_w, ffn_conv_b, ffn_w_down):
    w = _prep_weights(norms, ssm_w_in, ssm_conv_b, ssm_dt_bias, ssm_a_log, ssm_d, ssm_norm_w, ssm_w_out,
                      att_w_qkv, att_w_o, mem_norm, xa_w_q, xa_w_kv, xa_w_o, ffn_w_gu, ffn_conv_b, ffn_w_down)
    caches = [cache_swa_kv_w128, cache_swa_kv_w512, cache_swa_kv_w2048]
    yp, p_ssm, p_conv, p_swa, p_mem, p_ffn = _prompt_trunk(x_prompt, mem_prompt, w, ssm_conv_w, ffn_conv_w)
    ys, s_ssm, s_conv, s_swa, s_ffn = _sample_trunk(x_sample, w, ssm_conv_w, ffn_conv_w, state_ssm,
                                                    state_ssm_conv, caches, cache_mem_kv, state_ffn_conv)
    return (yp, ys, p_ssm, p_conv, p_swa[0], p_swa[1], p_swa[2], p_mem, p_ffn,
            s_ssm, s_conv, s_swa[0], s_swa[1], s_swa[2], s_ffn)
```

```python
import functools
import math

import numpy as np
import jax
import jax.numpy as jnp
from jax import lax
from jax.experimental import pallas as pl
from jax.experimental.pallas import tpu as pltpu

F32 = jnp.float32
BF16 = jnp.bfloat16

EPS = 1e-6
PAST_LEN = 8192
SSM_HEAD_DIM = 64
SSM_GROUPS = 4
D_STATE = 128
SSM_CONV = 4
SSM_CHUNK = 128
CONV_COLS = 1024
ATT_GROUPS = ((128, 1), (512, 4), (2048, 16))
ATT_HPG = 4
ATT_HEAD_DIM = 64
ATT_GW = ATT_HPG * ATT_HEAD_DIM
BAND_BLOCK = 512
ROT_DIM = ATT_HEAD_DIM // 4
ROPE_THETA = 500000.0
MEM_HEADS = 4
FFN_CONV = 3

LANE = 128
SUBLANE = 8
VMEM_LIMIT = 56 * 1024 * 1024
PROMPT_ROWS = 512
PROMPT_ROWS_SSD = 256


def _cparams(sem):
    return pltpu.CompilerParams(dimension_semantics=sem, vmem_limit_bytes=VMEM_LIMIT)


def _rms(x, g):
    return x * lax.rsqrt(jnp.mean(x * x, axis=-1, keepdims=True) + EPS) * g


def _silu(x):
    return x / (1.0 + jnp.exp(-x))


def _softplus(x):
    return jnp.maximum(x, 0.0) + jnp.log(1.0 + jnp.exp(-jnp.abs(x)))


def _dot(a, b):
    return jnp.dot(a, b, preferred_element_type=F32)


def _dot_nt(a, b):
    return lax.dot_general(a, b, (((1,), (1,)), ((), ())), preferred_element_type=F32)


def _split3(v):
    hi = v.astype(BF16)
    r = v - hi.astype(F32)
    mid = r.astype(BF16)
    lo = (r - mid.astype(F32)).astype(BF16)
    return hi, mid, lo


def _expand(v, e, n_heads):
    hi = v.astype(BF16).astype(F32)
    r1 = v - hi
    mid = r1.astype(BF16).astype(F32)
    lo = r1 - mid
    lane = lax.broadcasted_iota(jnp.int32, v.shape, 1)
    packed = jnp.where(lane < n_heads, hi,
                       jnp.where(lane < 2 * n_heads, pltpu.roll(mid, n_heads, 1),
                                 jnp.where(lane < 3 * n_heads, pltpu.roll(lo, 2 * n_heads, 1), 0.0)))
    return _dot(packed.astype(BF16), e)


def _full(shape):
    return pl.BlockSpec(shape, lambda *_: (0,) * len(shape))


def _resident(shape):
    return pl.BlockSpec(shape, lambda *_: (0,) * len(shape), pipeline_mode=pl.Buffered(1))


def _norm_proj_body(x_ref, g_ref, w_ref, o_ref):
    u = _rms(x_ref[...], g_ref[...]).astype(BF16)
    o_ref[...] = _dot(u, w_ref[...])


def _norm_proj(x, g, w, tm):
    m, d = x.shape
    n = w.shape[1]
    return pl.pallas_call(
        _norm_proj_body,
        grid=(m // tm,),
        in_specs=[pl.BlockSpec((tm, d), lambda i: (i, 0)), _full((1, d)), _full((d, n))],
        out_specs=pl.BlockSpec((tm, n), lambda i: (i, 0)),
        out_shape=jax.ShapeDtypeStruct((m, n), F32),
        compiler_params=_cparams(("parallel",)),
        name="norm_proj",
    )(x, g, w)


def _norm_qkv_rope_body(x_ref, g_ref, w_ref, cos_ref, sa_ref, sb_ref, o_ref, *, n_rot):
    u = _rms(x_ref[...], g_ref[...]).astype(BF16)
    y = _dot(u, w_ref[...])
    cos, sa, sb = cos_ref[...], sa_ref[...], sb_ref[...]
    for c in range(n_rot // LANE):
        t = y[:, c * LANE:(c + 1) * LANE]
        o_ref[:, c * LANE:(c + 1) * LANE] = (t * cos + pltpu.roll(t, LANE - ROT_DIM // 2, 1) * sa
                                             + pltpu.roll(t, ROT_DIM // 2, 1) * sb)
    o_ref[:, n_rot:] = y[:, n_rot:]


def _norm_qkv_rope(x, g, w, tabs, tm, seq_blocks):
    m, d = x.shape
    n = w.shape[1]
    tab_spec = pl.BlockSpec((tm, LANE), lambda i: (i % seq_blocks, 0))
    return pl.pallas_call(
        functools.partial(_norm_qkv_rope_body, n_rot=2 * n // 3),
        grid=(m // tm,),
        in_specs=[pl.BlockSpec((tm, d), lambda i: (i, 0)), _full((1, d)), _full((d, n)),
                  tab_spec, tab_spec, tab_spec],
        out_specs=pl.BlockSpec((tm, n), lambda i: (i, 0)),
        out_shape=jax.ShapeDtypeStruct((m, n), F32),
        compiler_params=_cparams(("parallel",)),
        name="norm_qkv_rope",
    )(x, g, w, *tabs)


def _rope_chunk(t, cos, sa, sb):
    return t * cos + pltpu.roll(t, LANE - ROT_DIM // 2, 1) * sa + pltpu.roll(t, ROT_DIM // 2, 1) * sb


def _norm_qkv_prompt_body(*refs, n_rot, n_prev, nblks):
    x_ref, g_ref, w_ref, cos_ref, sa_ref, sb_ref = refs[:6]
    o_ref = refs[6 + n_prev]
    c_refs = refs[7 + n_prev:]
    ng = len(c_refs)
    t = pl.program_id(1)
    tb = pl.num_programs(1)
    tm = x_ref.shape[0]
    u = _rms(x_ref[...], g_ref[...]).astype(BF16)
    y = _dot(u, w_ref[...])
    cos, sa, sb = cos_ref[...], sa_ref[...], sb_ref[...]
    vals = []
    for c in range(y.shape[1] // LANE):
        v = y[:, c * LANE:(c + 1) * LANE]
        if c * LANE < n_rot:
            v = _rope_chunk(v, cos, sa, sb)
        o_ref[c] = v
        vals.append(v)
    per_sec = len(vals) // 3
    per_grp = ATT_GW // LANE
    for gi, c_ref in enumerate(c_refs):
        kw = c_ref.shape[2]
        nblk = nblks[gi]

        @pl.when(t >= tb - nblk)
        def _(gi=gi, c_ref=c_ref, kw=kw):
            for kv in range(2):
                for hp in range(per_grp):
                    v = vals[(1 + kv) * per_sec + gi * per_grp + hp]
                    c_ref[kv, hp * LANE:(hp + 1) * LANE, :] = v[tm - kw:, :].T


def _norm_qkv_prompt(x, g, w, tabs, prev, layer, n_layers, n_seq, tm):
    m, d = x.shape
    n = w.shape[1]
    t_len = m // n_seq
    tb = t_len // tm
    tab_spec = pl.BlockSpec((tm, LANE), lambda b, t: (t, 0))
    c_specs, c_shapes, nblks = [], [], []
    for gi, (win, _) in enumerate(ATT_GROUPS):
        keep = min(win, t_len)
        kw = min(keep, tm)
        nblk = keep // kw
        assert keep % kw == 0
        nblks.append(nblk)
        c_specs.append(pl.BlockSpec((None, 2, ATT_GW, kw),
                                    lambda b, t, nblk=nblk: (layer * n_seq + b, 0, 0, jnp.maximum(t - (tb - nblk), 0))))
        c_shapes.append(jax.ShapeDtypeStruct((n_layers * n_seq, 2, ATT_GW, keep), F32))
    res = pl.pallas_call(
        functools.partial(_norm_qkv_prompt_body, n_rot=2 * n // 3, n_prev=len(prev), nblks=tuple(nblks)),
        grid=(n_seq, tb),
        in_specs=[pl.BlockSpec((tm, d), lambda b, t: (b * tb + t, 0)), _full((1, d)), _resident((d, n)),
                  tab_spec, tab_spec, tab_spec] + [pl.BlockSpec(memory_space=pl.ANY)] * len(prev),
        out_specs=[pl.BlockSpec((n // LANE, tm, LANE), lambda b, t: (0, b * tb + t, 0))] + c_specs,
        out_shape=[jax.ShapeDtypeStruct((n // LANE, m, LANE), F32)] + c_shapes,
        input_output_aliases={6 + k: 1 + k for k in range(len(prev))},
        compiler_params=_cparams(("parallel", "arbitrary")),
        name="norm_qkv_prompt",
    )(x, g, w, *tabs, *prev)
    return res[0], list(res[1:])


def _rope_tables(pos):
    half = ROT_DIM // 2
    inv = ROPE_THETA ** (-jnp.arange(half, dtype=F32) / half)
    ang = pos.astype(F32)[:, None] * inv[None, :]
    cos, sin = jnp.cos(ang), jnp.sin(ang)
    p = pos.shape[0]
    rest = ATT_HEAD_DIM - ROT_DIM
    c = jnp.concatenate([cos, cos, jnp.ones((p, rest), F32)], axis=1)
    sa = jnp.concatenate([-sin, jnp.zeros((p, half + rest), F32)], axis=1)
    sb = jnp.concatenate([jnp.zeros((p, half), F32), sin, jnp.zeros((p, rest), F32)], axis=1)
    rep = LANE // ATT_HEAD_DIM
    return tuple(jnp.tile(t, (1, rep)) for t in (c, sa, sb))


def _proj_res_body(y_ref, w_ref, x_ref, g_ref, o_ref):
    f = _dot(y_ref[...].astype(BF16), w_ref[...])
    o_ref[...] = x_ref[...] + _rms(f, g_ref[...])


def _proj_res(y, w, x, g, tm):
    m, k = y.shape
    d = w.shape[1]
    return pl.pallas_call(
        _proj_res_body,
        grid=(m // tm,),
        in_specs=[pl.BlockSpec((tm, k), lambda i: (i, 0)), _full((k, d)),
                  pl.BlockSpec((tm, d), lambda i: (i, 0)), _full((1, d))],
        out_specs=pl.BlockSpec((tm, d), lambda i: (i, 0)),
        out_shape=jax.ShapeDtypeStruct((m, d), F32),
        compiler_params=_cparams(("parallel",)),
        name="proj_res",
    )(y, w, x, g)


def _attn_out_body(o0, o1, o2, l0, l1, l2, w_ref, x_ref, g_ref, out_ref):
    def load(ref):
        if len(ref.shape) == 2:
            return ref[...]
        return jnp.concatenate([ref[c] for c in range(ref.shape[0])], axis=1)

    ls = [load(l0), load(l1), load(l2)]
    mx = jnp.maximum(jnp.maximum(ls[0], ls[1]), ls[2])
    es = [jnp.exp(l - mx) for l in ls]
    den = es[0] + es[1] + es[2]
    og = jnp.concatenate([(es[gi] / den * load(o_ref)).astype(BF16) for gi, o_ref in enumerate((o0, o1, o2))], axis=1)
    out_ref[...] = x_ref[...] + _rms(_dot(og, w_ref[...]), g_ref[...])


def _attn_out(os_, ls_, w, x, g, tm):
    m, d = x.shape
    if os_[0].ndim == 2:
        blk = pl.BlockSpec((tm, ATT_GW), lambda i: (i, 0))
    else:
        blk = pl.BlockSpec((ATT_GW // LANE, tm, LANE), lambda i: (0, i, 0))
    return pl.pallas_call(
        _attn_out_body,
        grid=(m // tm,),
        in_specs=[blk] * 6 + [_full(w.shape), pl.BlockSpec((tm, d), lambda i: (i, 0)), _full((1, d))],
        out_specs=pl.BlockSpec((tm, d), lambda i: (i, 0)),
        out_shape=jax.ShapeDtypeStruct((m, d), F32),
        compiler_params=_cparams(("parallel",)),
        name="attn_out",
    )(*os_, *ls_, w, x, g)


def _mem_kv_body(x_ref, g_ref, w_ref, o_ref, t_ref):
    tm = x_ref.shape[0]
    u = _rms(x_ref[...], g_ref[...]).astype(BF16)
    y = _dot(u, w_ref[...])
    o_ref[...] = y
    hd = y.shape[1] // (2 * MEM_HEADS)
    nch = hd // LANE
    tok_rows = 2 * nch * MEM_HEADS
    for kv in range(2):
        for h in range(MEM_HEADS):
            for c in range(nch):
                col = (kv * MEM_HEADS + h) * hd + c * LANE
                t_ref[pl.ds((kv * nch + c) * MEM_HEADS + h, tm, stride=tok_rows), :] = y[:, col:col + LANE]


def _mem_kv(mem, g, w, tm):
    m, d = mem.shape
    depth, _, n = w.shape
    tok_rows = n // LANE
    return pl.pallas_call(
        _mem_kv_body,
        grid=(depth, m // tm),
        in_specs=[pl.BlockSpec((tm, d), lambda l, i: (i, 0)),
                  pl.BlockSpec((None, 1, d), lambda l, i: (l, 0, 0)),
                  pl.BlockSpec((None, d, n), lambda l, i: (l, 0, 0))],
        out_specs=[pl.BlockSpec((None, tm, n), lambda l, i: (l, i, 0)),
                   pl.BlockSpec((None, tm * tok_rows, LANE), lambda l, i: (l, i, 0))],
        out_shape=[jax.ShapeDtypeStruct((depth, m, n), F32),
                   jax.ShapeDtypeStruct((depth, m * tok_rows, LANE), F32)],
        compiler_params=_cparams(("parallel", "parallel")),
        name="mem_kv",
    )(mem, g, w)


def _xattn_prompt_body(x_ref, gpre_ref, gpost_ref, wq_ref, kv_ref, wo_ref, o_ref, obuf):
    x = x_ref[...]
    d = x.shape[1]
    hd = d // MEM_HEADS
    u = _rms(x, gpre_ref[...]).astype(BF16)
    q = _dot(u, wq_ref[...]).astype(BF16)
    scale = hd ** -0.5
    for h in range(MEM_HEADS):
        kh = kv_ref[:, h * hd:(h + 1) * hd].astype(BF16)
        vh = kv_ref[:, d + h * hd:d + (h + 1) * hd].astype(BF16)
        s = _dot_nt(q[:, h * hd:(h + 1) * hd], kh) * scale
        mx = jnp.max(s, axis=-1, keepdims=True)
        p = jnp.exp(s - mx)
        den = jnp.sum(p, axis=-1, keepdims=True)
        obuf[:, h * hd:(h + 1) * hd] = (_dot(p.astype(BF16), vh) / den).astype(BF16)
    f = _dot(obuf[...], wo_ref[...])
    o_ref[...] = x + _rms(f, gpost_ref[...])


def _xattn_prompt(x, gpre, gpost, wq, kv, layer, wo, n_seq, tm):
    m, d = x.shape
    tb = m // n_seq // tm
    n_mem = kv.shape[1] // n_seq
    return pl.pallas_call(
        _xattn_prompt_body,
        grid=(n_seq, tb),
        in_specs=[pl.BlockSpec((tm, d), lambda b, t: (b * tb + t, 0)), _full((1, d)), _full((1, d)),
                  _resident((d, d)), pl.BlockSpec((None, n_mem, 2 * d), lambda b, t: (layer, b, 0)),
                  _resident((d, d))],
        out_specs=pl.BlockSpec((tm, d), lambda b, t: (b * tb + t, 0)),
        out_shape=jax.ShapeDtypeStruct((m, d), F32),
        scratch_shapes=[pltpu.VMEM((tm, d), BF16)],
        compiler_params=_cparams(("parallel", "parallel")),
        name="xattn_prompt",
    )(x, gpre, gpost, wq, kv, wo)


def _xattn_sample_body(x_ref, gpre_ref, gpost_ref, wq_ref, kv_ref, wo_ref, o_ref, q_scr, o_scr):
    b = pl.program_id(0)
    d = x_ref.shape[1]
    hd = d // MEM_HEADS

    @pl.when(b == 0)
    def _():
        u = _rms(x_ref[...], gpre_ref[...]).astype(BF16)
        q_scr[...] = _dot(u, wq_ref[...]) * (hd ** -0.5)

    qb = q_scr[pl.ds(b, 1), :]
    nch = hd // LANE
    n_mem, _, rows, _ = kv_ref.shape
    row_id = lax.broadcasted_iota(jnp.int32, (rows, LANE), 0)
    q_tile = jnp.zeros((rows, LANE), F32)
    for c in range(nch):
        for h in range(MEM_HEADS):
            piece = qb[:, h * hd + c * LANE:h * hd + (c + 1) * LANE]
            q_tile = jnp.where(row_id == c * MEM_HEADS + h, piece, q_tile)
    prod = (kv_ref[:, 0] * q_tile).reshape(n_mem * rows, LANE)
    p_hi, p_lo = _hi_lo(prod)
    ones = jnp.ones((LANE, LANE), BF16)
    s = (_dot(p_hi, ones) + _dot(p_lo, ones)).reshape(n_mem, rows, LANE)
    part = s
    for c in range(1, nch):
        s = s + pltpu.roll(part, c * MEM_HEADS, 1)
    mx = jnp.max(s, axis=0, keepdims=True)
    p = jnp.exp(s - mx)
    den = jnp.sum(p, axis=0)
    o_tile = jnp.sum(p * kv_ref[:, 1], axis=0) / den
    o_scr[pl.ds(b, 1), :] = jnp.concatenate(
        [o_tile[c * MEM_HEADS + h:c * MEM_HEADS + h + 1, :] for h in range(MEM_HEADS) for c in range(nch)], axis=1)

    @pl.when(b == pl.num_programs(0) - 1)
    def _():
        f = _dot(o_scr[...].astype(BF16), wo_ref[...])
        o_ref[...] = x_ref[...] + _rms(f, gpost_ref[...])


def _xattn_sample(x, gpre, gpost, wq, kv_all, layer, wo):
    m, d = x.shape
    return pl.pallas_call(
        _xattn_sample_body,
        grid=(m,),
        in_specs=[_full((m, d)), _full((1, d)), _full((1, d)), _full((d, d)),
                  pl.BlockSpec((None,) + kv_all.shape[1:], lambda b: (layer * m + b, 0, 0, 0, 0)), _full((d, d))],
        out_specs=_full((m, d)),
        out_shape=jax.ShapeDtypeStruct((m, d), F32),
        scratch_shapes=[pltpu.VMEM((m, d), F32), pltpu.VMEM((m, d), F32)],
        compiler_params=_cparams(("arbitrary",)),
        name="xattn_sample",
    )(x, gpre, gpost, wq, kv_all, wo)


def _ffn_prompt_body(x_ref, gpre_ref, gpost_ref, wg_ref, wu_ref, cw_ref, cb_ref, wd_ref, hist_ref,
                     o_ref, nh_ref, gbuf):
    t = pl.program_id(1)
    tm = x_ref.shape[0]
    k = FFN_CONV - 1
    base = SUBLANE - k

    @pl.when(t == 0)
    def _():
        gbuf[base:SUBLANE, :] = hist_ref[...]

    x = x_ref[...]
    u = _rms(x, gpre_ref[...]).astype(BF16)
    gbuf[SUBLANE:SUBLANE + tm, :] = _dot(u, wg_ref[...])
    up = _dot(u, wu_ref[...])
    gc = gbuf[base:base + tm, :] * cw_ref[0:1, :] + cb_ref[...]
    for j in range(1, FFN_CONV):
        gc = gc + gbuf[base + j:base + j + tm, :] * cw_ref[j:j + 1, :]
    hmid = (_silu(gc) * up).astype(BF16)
    f = _dot(hmid, wd_ref[...])
    o_ref[...] = x + _rms(f, gpost_ref[...])
    last = gbuf[tm + base:tm + SUBLANE, :]
    nh_ref[...] = last
    gbuf[base:SUBLANE, :] = last


def _ffn_prompt(x, gpre, gpost, w_gu, layer, cw, cb, w_down, hist, n_seq, tm):
    m, d = x.shape
    f = w_down.shape[1]
    tb = m // n_seq // tm
    k = FFN_CONV - 1
    one = pl.Buffered(1)
    return pl.pallas_call(
        _ffn_prompt_body,
        grid=(n_seq, tb),
        in_specs=[pl.BlockSpec((tm, d), lambda b, t: (b * tb + t, 0)), _full((1, d)), _full((1, d)),
                  pl.BlockSpec((None, d, f), lambda b, t: (layer, 0, 0), pipeline_mode=one),
                  pl.BlockSpec((None, d, f), lambda b, t: (layer, 0, 1), pipeline_mode=one),
                  _full((FFN_CONV, f)), _full((1, f)),
                  pl.BlockSpec((None, f, d), lambda b, t: (layer, 0, 0), pipeline_mode=one),
                  pl.BlockSpec((None, k, f), lambda b, t: (b, 0, 0))],
        out_specs=[pl.BlockSpec((tm, d), lambda b, t: (b * tb + t, 0)),
                   pl.BlockSpec((None, k, f), lambda b, t: (b, 0, 0))],
        out_shape=[jax.ShapeDtypeStruct((m, d), F32), jax.ShapeDtypeStruct((n_seq, k, f), F32)],
        scratch_shapes=[pltpu.VMEM((tm + SUBLANE, f), F32)],
        compiler_params=_cparams(("parallel", "arbitrary")),
        name="ffn_prompt",
    )(x, gpre, gpost, w_gu, w_gu, cw, cb, w_down, hist)


def _ffn_sample_body(x_ref, gpre_ref, gpost_ref, wg_ref, wu_ref, cw_ref, cb_ref, wd_ref, hist_ref,
                     o_ref, nh_ref, u_scr, acc):
    j = pl.program_id(0)

    @pl.when(j == 0)
    def _():
        u_scr[...] = _rms(x_ref[...], gpre_ref[...]).astype(BF16)
        acc[...] = jnp.zeros_like(acc)

    u = u_scr[...]
    gate = _dot(u, wg_ref[...])
    up = _dot(u, wu_ref[...])
    gc = hist_ref[0] * cw_ref[0:1, :] + cb_ref[...]
    for k in range(1, FFN_CONV - 1):
        gc = gc + hist_ref[k] * cw_ref[k:k + 1, :]
        nh_ref[k - 1] = hist_ref[k]
    gc = gc + gate * cw_ref[FFN_CONV - 1:FFN_CONV, :]
    nh_ref[FFN_CONV - 2] = gate
    hmid = (_silu(gc) * up).astype(BF16)
    acc[...] += _dot(hmid, wd_ref[...])

    @pl.when(j == pl.num_programs(0) - 1)
    def _():
        o_ref[...] = x_ref[...] + _rms(acc[...], gpost_ref[...])


def _ffn_sample(x, gpre, gpost, w_gu, layer, cw, cb, w_down, hist_t, tn):
    m, d = x.shape
    f = w_down.shape[1]
    k = FFN_CONV - 1
    nb = f // tn
    return pl.pallas_call(
        _ffn_sample_body,
        grid=(nb,),
        in_specs=[_full((m, d)), _full((1, d)), _full((1, d)),
                  pl.BlockSpec((None, d, tn), lambda j: (layer, 0, j)),
                  pl.BlockSpec((None, d, tn), lambda j: (layer, 0, nb + j)),
                  pl.BlockSpec((FFN_CONV, tn), lambda j: (0, j)), pl.BlockSpec((1, tn), lambda j: (0, j)),
                  pl.BlockSpec((None, tn, d), lambda j: (layer, j, 0)),
                  pl.BlockSpec((k, m, tn), lambda j: (0, 0, j))],
        out_specs=[_full((m, d)), pl.BlockSpec((k, m, tn), lambda j: (0, 0, j))],
        out_shape=[jax.ShapeDtypeStruct((m, d), F32), jax.ShapeDtypeStruct((k, m, f), F32)],
        scratch_shapes=[pltpu.VMEM((m, d), BF16), pltpu.VMEM((m, d), F32)],
        compiler_params=_cparams(("arbitrary",)),
        name="ffn_sample",
    )(x, gpre, gpost, w_gu, w_gu, cw, cb, w_down, hist_t)


def _ssd_activations(u, w_ref, cw_ref, cb_ref, dtb_ref, act, cbuf, clast_ref, d_inner):
    tm = u.shape[0]
    k = SSM_CONV - 1
    base = SUBLANE - k
    conv_dim = cw_ref.shape[1]
    act[:, :d_inner] = _silu(_dot(u, w_ref[:, :d_inner]))
    act[:, d_inner + conv_dim:] = _softplus(_dot(u, w_ref[:, d_inner + conv_dim:]) + dtb_ref[...])
    for c0 in range(0, conv_dim, CONV_COLS):
        cs = slice(c0, c0 + CONV_COLS)
        cbuf[SUBLANE:SUBLANE + tm, cs] = _dot(u, w_ref[:, d_inner + c0:d_inner + c0 + CONV_COLS])
        full = cbuf[:, cs]
        xc = full[SUBLANE:, :] * cw_ref[k:k + 1, cs] + cb_ref[:, cs]
        for j in range(k):
            xc = xc + pltpu.roll(full, k - j, 0)[SUBLANE:, :] * cw_ref[j:j + 1, cs]
        act[:, d_inner + c0:d_inner + c0 + CONV_COLS] = _silu(xc)
    last = cbuf[tm + base:tm + SUBLANE, :]
    clast_ref[...] = last
    cbuf[base:SUBLANE, :] = last


def _ssd_chunk(act_ref, alog_ref, dx_ref, nw_ref, e_ref, tril_ref, state, ybuf, y_ref, d_inner, n_heads):
    L = SSM_CHUNK
    gn = SSM_GROUPS * D_STATE
    conv_dim = d_inner + 2 * gn
    hpg = n_heads // SSM_GROUPS
    gw = d_inner // SSM_GROUPS
    xs = act_ref[:, d_inner:2 * d_inner]
    xs_b = xs.astype(BF16)
    e = e_ref[...]
    dt = act_ref[:, d_inner + conv_dim:]
    a = dt * (-jnp.exp(alog_ref[...]))
    tril = tril_ref[...]
    a_hi, a_mid, a_lo = _split3(a)
    acum = _dot(tril, a_hi) + _dot(tril, a_mid) + _dot(tril, a_lo)
    acum_t = acum.T
    dt_t = dt.T
    a_last = acum[L - 1:L, :]
    d_acc = _expand(jnp.exp(acum), e, n_heads)
    xde = (xs * _expand(dt * jnp.exp(a_last - acum), e, n_heads)).astype(BF16)
    row = lax.broadcasted_iota(jnp.int32, (L, L), 0)
    col = lax.broadcasted_iota(jnp.int32, (L, L), 1)
    causal = row >= col

    for g in range(SSM_GROUPS):
        b0 = 2 * d_inner + g * D_STATE
        gs = slice(g * gw, (g + 1) * gw)
        b_f = act_ref[:, b0:b0 + D_STATE]
        bg = b_f.astype(BF16)
        cg = act_ref[:, b0 + gn:b0 + gn + D_STATE].astype(BF16)
        cb = _dot_nt(cg, bg)
        y_inter = _dot(cg, state[:, gs].astype(BF16)) * d_acc[:, gs]
        for j in range(hpg):
            hd = g * hpg + j
            seg = acum[:, hd:hd + 1] - acum_t[hd:hd + 1, :]
            w = cb * jnp.exp(jnp.where(causal, seg, -jnp.inf)) * dt_t[hd:hd + 1, :]
            sl = slice(hd * SSM_HEAD_DIM, (hd + 1) * SSM_HEAD_DIM)
            ybuf[:, sl] = _dot(w.astype(BF16), xs_b[:, sl])
        ybuf[:, gs] += y_inter
        state[:, gs] = state[:, gs] * d_acc[L - 1:L, gs] + _dot(b_f.T.astype(BF16), xde[:, gs])

    y = ybuf[...] + xs * dx_ref[...]
    gated = y * act_ref[:, :d_inner]
    for g in range(SSM_GROUPS):
        seg = gated[:, g * gw:(g + 1) * gw]
        y_ref[:, g * gw:(g + 1) * gw] = _rms(seg, nw_ref[:, g * gw:(g + 1) * gw]).astype(BF16)


N_SSD_IN = 15


def _ssd_layer_body(x_ref, gpre_ref, gpost_ref, w_ref, cw_ref, cb_ref, dtb_ref, alog_ref, dx_ref, nw_ref,
                    e_ref, tril_ref, wo_ref, hist_ref, h0_ref, *rest, d_inner, n_heads):
    o_ref, hlast_ref, clast_ref, act, cbuf, state, ybuf, ybf = rest[-8:]
    t = pl.program_id(1)
    tm = x_ref.shape[0]
    L = SSM_CHUNK
    base = SUBLANE - (SSM_CONV - 1)

    @pl.when(t == 0)
    def _():
        cbuf[base:SUBLANE, :] = hist_ref[...]
        state[...] = h0_ref[...].T

    x = x_ref[...]
    u = _rms(x, gpre_ref[...]).astype(BF16)
    _ssd_activations(u, w_ref, cw_ref, cb_ref, dtb_ref, act, cbuf, clast_ref, d_inner)
    for ci in range(tm // L):
        _ssd_chunk(act.at[ci * L:(ci + 1) * L, :], alog_ref, dx_ref, nw_ref, e_ref, tril_ref, state, ybuf,
                   ybf.at[ci * L:(ci + 1) * L, :], d_inner, n_heads)
    o_ref[...] = x + _rms(_dot(ybf[...], wo_ref[...]), gpost_ref[...])

    @pl.when(t == pl.num_programs(1) - 1)
    def _():
        hlast_ref[...] = state[...].T


def _ssd_layer(x, gpre, gpost, w, cw, cb, dtb, alog, dx, nw, e, tril, wo, hist, h0, prev, layer, n_layers,
               n_seq, d_inner, n_heads, tm):
    m, d = x.shape
    n = w.shape[1]
    tb = m // n_seq // tm
    conv_dim = cw.shape[1]
    k = SSM_CONV - 1
    rows = n_heads * SSM_HEAD_DIM
    assert conv_dim % CONV_COLS == 0 and tm % SSM_CHUNK == 0
    ins = (x, gpre, gpost, w, cw, cb, dtb, alog, dx, nw, e, tril, wo, hist, h0)
    assert len(ins) == N_SSD_IN
    x_spec = pl.BlockSpec((tm, d), lambda b, t: (b * tb + t, 0))
    return pl.pallas_call(
        functools.partial(_ssd_layer_body, d_inner=d_inner, n_heads=n_heads),
        grid=(n_seq, tb),
        in_specs=[x_spec, _full((1, d)), _full((1, d)), _resident(w.shape), _full(cw.shape), _full(cb.shape),
                  _full(dtb.shape), _full(alog.shape), _full(dx.shape), _full(nw.shape), _full(e.shape),
                  _full(tril.shape), _resident(wo.shape),
                  pl.BlockSpec((None, k, conv_dim), lambda b, t: (b, 0, 0)),
                  pl.BlockSpec((None, rows, D_STATE), lambda b, t: (b, 0, 0))]
                 + [pl.BlockSpec(memory_space=pl.ANY)] * len(prev),
        out_specs=[x_spec,
                   pl.BlockSpec((None, rows, D_STATE), lambda b, t: (layer * n_seq + b, 0, 0)),
                   pl.BlockSpec((None, k, conv_dim), lambda b, t: (b, 0, 0))],
        out_shape=[jax.ShapeDtypeStruct((m, d), F32),
                   jax.ShapeDtypeStruct((n_layers * n_seq, rows, D_STATE), F32),
                   jax.ShapeDtypeStruct((n_seq, k, conv_dim), F32)],
        input_output_aliases={N_SSD_IN: 1} if prev else {},
        scratch_shapes=[pltpu.VMEM((tm, n), F32), pltpu.VMEM((tm + SUBLANE, conv_dim), F32),
                        pltpu.VMEM((D_STATE, rows), F32), pltpu.VMEM((SSM_CHUNK, d_inner), F32),
                        pltpu.VMEM((tm, d_inner), BF16)],
        compiler_params=_cparams(("parallel", "arbitrary")),
        name="ssd_layer",
    )(*ins, *prev)


def _ssd_step_pre_body(proj_ref, hist_ref, cw_ref, cb_ref, dtb_ref, alog_ref, e_ref,
                       xs_ref, b_ref, c_ref, xdt_t_ref, dec_b_ref, clast_ref, *, d_inner):
    gn = SSM_GROUPS * D_STATE
    conv_dim = d_inner + 2 * gn
    n_heads = d_inner // SSM_HEAD_DIM
    m = proj_ref.shape[0]
    xbc = proj_ref[:, d_inner:d_inner + conv_dim]
    xc = hist_ref[0] * cw_ref[0:1, :] + cb_ref[...]
    for j in range(1, SSM_CONV - 1):
        xc = xc + hist_ref[j] * cw_ref[j:j + 1, :]
        clast_ref[j - 1] = hist_ref[j]
    xc = _silu(xc + xbc * cw_ref[SSM_CONV - 1:SSM_CONV, :])
    clast_ref[SSM_CONV - 2] = xbc
    xs = xc[:, :d_inner]
    xs_ref[...] = xs
    b_ref[...] = xc[:, d_inner:d_inner + gn]
    c_ref[...] = xc[:, d_inner + gn:]
    dt = _softplus(proj_ref[:, d_inner + conv_dim:] + dtb_ref[...])
    dec = jnp.exp(dt * (-jnp.exp(alog_ref[...])))
    pad = jnp.zeros((LANE - m, d_inner), F32)
    xdt_t_ref[...] = jnp.concatenate([xs * _expand(dt, e_ref[...], n_heads), pad], axis=0).T
    for j in range(n_heads):
        dec_b_ref[j] = jnp.broadcast_to(dec[:, j:j + 1], (m, LANE))


def _ssd_step_pre(proj, hist_t, cw, cb, dtb, alog, e, d_inner):
    m = proj.shape[0]
    gn = SSM_GROUPS * D_STATE
    conv_dim = cw.shape[1]
    k = SSM_CONV - 1
    n_heads = d_inner // SSM_HEAD_DIM
    args = (proj, hist_t, cw, cb, dtb, alog, e)
    return pl.pallas_call(
        functools.partial(_ssd_step_pre_body, d_inner=d_inner),
        grid=(1,),
        in_specs=[_full(a.shape) for a in args],
        out_specs=[_full((m, d_inner)), _full((m, gn)), _full((m, gn)), _full((d_inner, LANE)),
                   _full((n_heads, m, LANE)), _full((k, m, conv_dim))],
        out_shape=[jax.ShapeDtypeStruct((m, d_inner), F32), jax.ShapeDtypeStruct((m, gn), F32),
                   jax.ShapeDtypeStruct((m, gn), F32), jax.ShapeDtypeStruct((d_inner, LANE), F32),
                   jax.ShapeDtypeStruct((n_heads, m, LANE), F32), jax.ShapeDtypeStruct((k, m, conv_dim), F32)],
        compiler_params=_cparams(("arbitrary",)),
        name="ssd_step_pre",
    )(*args)


def _hi_lo(v):
    hi = v.astype(BF16)
    return hi, (v - hi.astype(F32)).astype(BF16)


def _ssd_step_state_body(h0_ref, xdt_t_ref, dec_b_ref, b_ref, c_ref, *rest):
    hn_ref, y_t_ref = rest[-2:]
    m = h0_ref.shape[0]
    wide = jnp.concatenate([b_ref[...]] * m, axis=1)
    own = (lax.broadcasted_iota(jnp.int32, wide.shape, 1) // D_STATE
           == lax.broadcasted_iota(jnp.int32, wide.shape, 0))
    bbd = jnp.concatenate([jnp.where(own, wide, 0.0), jnp.zeros((LANE - m, m * D_STATE), F32)], axis=0)
    b_hi, b_lo = _hi_lo(bbd)
    x_hi, x_lo = _hi_lo(xdt_t_ref[...])
    upd = _dot(jnp.concatenate([x_hi, x_lo, x_hi], axis=1), jnp.concatenate([b_hi, b_hi, b_lo], axis=0))
    prods = []
    for b in range(m):
        h = h0_ref[b] * dec_b_ref[b:b + 1, :] + upd[:, b * D_STATE:(b + 1) * D_STATE]
        hn_ref[b] = h
        prods.append(h * c_ref[b:b + 1, :])
    p_hi, p_lo = _hi_lo(jnp.concatenate(prods, axis=0))
    ones = jnp.ones((D_STATE, LANE), BF16)
    sums = _dot(p_hi, ones) + _dot(p_lo, ones)
    lane = lax.broadcasted_iota(jnp.int32, (SSM_HEAD_DIM, LANE), 1)
    acc = jnp.zeros((SSM_HEAD_DIM, LANE), F32)
    for b in range(m):
        acc = jnp.where(lane == b, sums[b * SSM_HEAD_DIM:(b + 1) * SSM_HEAD_DIM, :], acc)
    y_t_ref[...] = acc


def _ssd_step_state(h_all, prev, layer, xdt_t, dec_t, bm, cm, n_heads):
    m = bm.shape[0]
    hpg = n_heads // SSM_GROUPS
    h_spec = pl.BlockSpec((m, None, SSM_HEAD_DIM, D_STATE), lambda j: (layer, j, 0, 0))
    return pl.pallas_call(
        _ssd_step_state_body,
        grid=(n_heads,),
        in_specs=[h_spec,
                  pl.BlockSpec((SSM_HEAD_DIM, LANE), lambda j: (j, 0)),
                  pl.BlockSpec((None, m, LANE), lambda j: (j, 0, 0)),
                  pl.BlockSpec((m, D_STATE), lambda j: (0, j // hpg)),
                  pl.BlockSpec((m, D_STATE), lambda j: (0, j // hpg))]
                 + [pl.BlockSpec(memory_space=pl.ANY)] * len(prev),
        out_specs=[h_spec, pl.BlockSpec((SSM_HEAD_DIM, LANE), lambda j: (j, 0))],
        out_shape=[jax.ShapeDtypeStruct(h_all.shape, F32),
                   jax.ShapeDtypeStruct((n_heads * SSM_HEAD_DIM, LANE), F32)],
        input_output_aliases={5: 0} if prev else {},
        compiler_params=_cparams(("parallel",)),
        name="ssd_step_state",
    )(h_all, xdt_t, dec_t, bm, cm, *prev)


def _ssd_step_post_body(y_t_ref, xs_ref, z_ref, dx_ref, nw_ref, o_ref):
    m, d_inner = xs_ref.shape
    gw = d_inner // SSM_GROUPS
    y = y_t_ref[...].T[:m, :] + xs_ref[...] * dx_ref[...]
    gated = y * _silu(z_ref[...])
    for g in range(SSM_GROUPS):
        seg = gated[:, g * gw:(g + 1) * gw]
        o_ref[:, g * gw:(g + 1) * gw] = _rms(seg, nw_ref[:, g * gw:(g + 1) * gw]).astype(BF16)


def _ssd_step_post(y_t, xs, proj, dx, nw):
    m, d_inner = xs.shape
    return pl.pallas_call(
        _ssd_step_post_body,
        grid=(1,),
        in_specs=[_full(y_t.shape), _full(xs.shape), pl.BlockSpec((m, d_inner), lambda i: (0, 0)),
                  _full(dx.shape), _full(nw.shape)],
        out_specs=_full((m, d_inner)),
        out_shape=jax.ShapeDtypeStruct((m, d_inner), BF16),
        compiler_params=_cparams(("arbitrary",)),
        name="ssd_step_post",
    )(y_t, xs, proj, dx, nw)


def _band_attn_body(q_ref, kc_ref, kp_ref, vc_ref, vp_ref, o_ref, l_ref, *, win, dil):
    blk = pl.program_id(1)
    n_hp, bt, _ = q_ref.shape
    span = win // dil
    heads = LANE // ATT_HEAD_DIM
    qi = lax.broadcasted_iota(jnp.int32, (span, 2 * span), 0)
    ki = lax.broadcasted_iota(jnp.int32, (span, 2 * span), 1)
    band = (ki >= qi) & (ki <= qi + span)
    band_first = band & ((blk > 0) | (ki >= span))
    lane_q = lax.broadcasted_iota(jnp.int32, (span, LANE), 1)
    lane_k = lax.broadcasted_iota(jnp.int32, (2 * span, LANE), 1)

    def rows(ref, hp, start, r):
        if dil == 1:
            return ref[hp, start:start + span, :]
        return ref[hp, pl.ds(start + r, span, stride=dil), :]

    for hp in range(n_hp):
        for wi in range(bt // win):
            for r in range(dil):
                q = (rows(q_ref, hp, wi * win, r) * (ATT_HEAD_DIM ** -0.5)).astype(BF16)
                if wi == 0:
                    k_prev = rows(kp_ref, hp, kp_ref.shape[1] - win, r)
                    v_prev = rows(vp_ref, hp, vp_ref.shape[1] - win, r)
                else:
                    k_prev = rows(kc_ref, hp, (wi - 1) * win, r)
                    v_prev = rows(vc_ref, hp, (wi - 1) * win, r)
                kk = jnp.concatenate([k_prev, rows(kc_ref, hp, wi * win, r)], axis=0).astype(BF16)
                vv = jnp.concatenate([v_prev, rows(vc_ref, hp, wi * win, r)], axis=0).astype(BF16)
                valid = band_first if wi == 0 else band
                o_acc, l_acc = None, None
                for h in range(heads):
                    in_q = (lane_q >= h * ATT_HEAD_DIM) & (lane_q < (h + 1) * ATT_HEAD_DIM)
                    in_k = (lane_k >= h * ATT_HEAD_DIM) & (lane_k < (h + 1) * ATT_HEAD_DIM)
                    s = jnp.where(valid, _dot_nt(jnp.where(in_q, q, jnp.zeros_like(q)), kk), -jnp.inf)
                    mx = jnp.max(s, axis=-1, keepdims=True)
                    p = jnp.exp(s - mx)
                    den = jnp.sum(p, axis=-1, keepdims=True)
                    o_h = _dot(p.astype(BF16), jnp.where(in_k, vv, jnp.zeros_like(vv))) / den
                    l_h = jnp.where(in_q, mx + jnp.log(den), 0.0)
                    o_acc = o_h if o_acc is None else o_acc + o_h
                    l_acc = l_h if l_acc is None else l_acc + l_h
                if dil == 1:
                    o_ref[hp, wi * win:wi * win + span, :] = o_acc
                    l_ref[hp, wi * win:wi * win + span, :] = l_acc
                else:
                    o_ref[hp, pl.ds(wi * win + r, span, stride=dil), :] = o_acc
                    l_ref[hp, pl.ds(wi * win + r, span, stride=dil), :] = l_acc


def _band_attn(qkv_c, n_seq, gi, win, dil):
    nch, m, _ = qkv_c.shape
    t_len = m // n_seq
    bt = max(win, min(BAND_BLOCK, t_len))
    assert bt % win == 0 and t_len % bt == 0
    nblk = t_len // bt
    wpb = bt // win
    per_grp = ATT_GW // LANE
    ng = len(ATT_GROUPS)

    def cur(sec):
        return pl.BlockSpec((per_grp, bt, LANE), lambda b, k: (sec * ng + gi, b * nblk + k, 0))

    def prev(sec):
        return pl.BlockSpec((per_grp, win, LANE),
                            lambda b, k: (sec * ng + gi, jnp.maximum((b * nblk + k) * wpb - 1, 0), 0))

    out_spec = pl.BlockSpec((per_grp, bt, LANE), lambda b, k: (0, b * nblk + k, 0))
    out_sd = jax.ShapeDtypeStruct((per_grp, m, LANE), F32)
    return pl.pallas_call(
        functools.partial(_band_attn_body, win=win, dil=dil),
        grid=(n_seq, nblk),
        in_specs=[cur(0), cur(1), prev(1), cur(2), prev(2)],
        out_specs=[out_spec, out_spec],
        out_shape=[out_sd, out_sd],
        compiler_params=_cparams(("parallel", "arbitrary")),
        name="band_attn_w%d" % win,
    )(qkv_c, qkv_c, qkv_c, qkv_c, qkv_c)


def _row_to_cols(v):
    return jnp.concatenate([jnp.broadcast_to(v[:, c * LANE:(c + 1) * LANE], (LANE, LANE)).T
                            for c in range(v.shape[1] // LANE)], axis=0)


def _col_to_row(v):
    return jnp.concatenate([jnp.broadcast_to(v[c * LANE:(c + 1) * LANE, :], (LANE, LANE)).T[0:1, :]
                            for c in range(v.shape[0] // LANE)], axis=1)


def _attn_sample_body(*refs, dils, n_prev):
    ng = len(dils)
    qkv_ref = refs[0]
    bufs = refs[1:1 + ng]
    outs = refs[1 + ng + n_prev:]
    o_refs, l_refs, c_refs = outs[0:ng], outs[ng:2 * ng], outs[2 * ng:3 * ng]
    scale = ATT_HEAD_DIM ** -0.5
    for gi in range(ng):
        buf, o_ref, l_ref, c_ref = bufs[gi], o_refs[gi], l_refs[gi], c_refs[gi]
        wb = buf.shape[2]
        nch = wb // LANE
        q_c = _row_to_cols(qkv_ref[:, gi * ATT_GW:(gi + 1) * ATT_GW] * scale)
        kn_c = _row_to_cols(qkv_ref[:, (ng + gi) * ATT_GW:(ng + gi + 1) * ATT_GW])
        vn_c = _row_to_cols(qkv_ref[:, (2 * ng + gi) * ATT_GW:(2 * ng + gi + 1) * ATT_GW])
        lane = lax.broadcasted_iota(jnp.int32, (1, wb), 1)
        valid = (lane & (dils[gi] - 1)) == 0
        o_cols = []
        for h in range(ATT_HPG):
            rs = slice(h * ATT_HEAD_DIM, (h + 1) * ATT_HEAD_DIM)
            qh = q_c[rs, :]
            s = jnp.concatenate([jnp.sum(buf[0, rs, c * LANE:(c + 1) * LANE] * qh, axis=0, keepdims=True)
                                 for c in range(nch)], axis=1)
            s = jnp.where(valid, s, -jnp.inf)
            sn = jnp.sum(qh[:, 0:1] * kn_c[rs, 0:1], axis=0, keepdims=True)
            mx = jnp.maximum(jnp.max(s, axis=1, keepdims=True), sn)
            p = jnp.exp(s - mx)
            p_new = jnp.exp(sn - mx)
            den = jnp.sum(p, axis=1, keepdims=True) + p_new
            acc = buf[1, rs, 0:LANE] * p[:, 0:LANE]
            for c in range(1, nch):
                acc = acc + buf[1, rs, c * LANE:(c + 1) * LANE] * p[:, c * LANE:(c + 1) * LANE]
            o_cols.append((jnp.sum(acc, axis=1, keepdims=True) + p_new * vn_c[rs, 0:1]) / den)
            l_ref[:, rs] = jnp.broadcast_to(mx + jnp.log(den), (1, ATT_HEAD_DIM))
        o_ref[...] = _col_to_row(jnp.concatenate(o_cols, axis=0))
        last = lax.broadcasted_iota(jnp.int32, (ATT_HEAD_DIM, LANE), 1) == LANE - 1
        for kv, new_c in ((0, kn_c), (1, vn_c)):
            for h in range(ATT_HPG):
                rs = slice(h * ATT_HEAD_DIM, (h + 1) * ATT_HEAD_DIM)
                rolled = pltpu.roll(buf[kv, rs, :], wb - 1, 1)
                if nch > 1:
                    c_ref[kv, rs, 0:wb - LANE] = rolled[:, 0:wb - LANE]
                c_ref[kv, rs, wb - LANE:wb] = jnp.where(last, new_c[rs, :], rolled[:, wb - LANE:wb])


def _attn_sample(qkv, cache_views, prev, layer):
    m, w3 = qkv.shape
    ng = len(ATT_GROUPS)
    dils = tuple(dil for _, dil in ATT_GROUPS)
    for v, (win, dil) in zip(cache_views, ATT_GROUPS):
        assert v.shape[3] == win and win % LANE == 0 and dil & (dil - 1) == 0
    c_specs = [pl.BlockSpec((None, 2, ATT_GW, v.shape[3]), lambda b: (layer * m + b, 0, 0, 0)) for v in cache_views]
    o_sd = jax.ShapeDtypeStruct((m, 1, ATT_GW), F32)
    o_spec = pl.BlockSpec((None, 1, ATT_GW), lambda b: (b, 0, 0))
    n_in = 1 + ng
    res = pl.pallas_call(
        functools.partial(_attn_sample_body, dils=dils, n_prev=len(prev)),
        grid=(m,),
        in_specs=[pl.BlockSpec((None, 1, w3), lambda b: (b, 0, 0))] + c_specs
                 + [pl.BlockSpec(memory_space=pl.ANY)] * len(prev),
        out_specs=[o_spec] * (2 * ng) + c_specs,
        out_shape=[o_sd] * (2 * ng) + [jax.ShapeDtypeStruct(v.shape, v.dtype) for v in cache_views],
        input_output_aliases={n_in + k: 2 * ng + k for k in range(len(prev))},
        compiler_params=_cparams(("parallel",)),
        name="attn_sample",
    )(qkv.reshape(m, 1, w3), *cache_views, *prev)
    os_ = [r.reshape(m, ATT_GW) for r in res[0:ng]]
    ls_ = [r.reshape(m, ATT_GW) for r in res[ng:2 * ng]]
    return os_, ls_, list(res[2 * ng:])


def _prep_weights(norms, ssm_w_in, ssm_conv_b, ssm_dt_bias, ssm_a_log, ssm_d, ssm_norm_w, ssm_w_out, att_w_qkv,
                  att_w_o, mem_norm, xa_w_q, xa_w_kv, xa_w_o, ffn_w_gu, ffn_conv_b, ffn_w_down):
    n_ssm, d_model, in_dim = ssm_w_in.shape
    n_heads = ssm_dt_bias.shape[1]
    d_inner = n_heads * SSM_HEAD_DIM
    d_ff = ffn_w_down.shape[1]
    pad_heads = LANE - n_heads
    w = {}
    w['n_heads'], w['d_inner'] = n_heads, d_inner
    w['norms'] = norms[:, :, None, :]
    def per_layer(a, cols=slice(None)):
        return [a[i, :, cols].astype(BF16) for i in range(a.shape[0])]

    w['ssm_w_in'] = [jnp.pad(ssm_w_in[i], ((0, 0), (0, pad_heads))).astype(BF16) for i in range(n_ssm)]
    w['ssm_conv_b'] = ssm_conv_b[:, None, :]
    w['ssm_dt_bias'] = jnp.pad(ssm_dt_bias, ((0, 0), (0, pad_heads)))[:, None, :]
    w['ssm_a_log'] = jnp.pad(ssm_a_log, ((0, 0), (0, pad_heads)))[:, None, :]
    w['ssm_dx'] = jnp.repeat(ssm_d, SSM_HEAD_DIM, axis=1)[:, None, :]
    w['ssm_norm_w'] = ssm_norm_w[:, None, :]
    w['ssm_w_out'] = per_layer(ssm_w_out)
    w['att_w_qkv'] = per_layer(att_w_qkv)
    w['att_w_o'] = per_layer(att_w_o)
    w['mem_norm'] = mem_norm[:, None, :]
    w['xa_w_q'] = per_layer(xa_w_q)
    w['xa_w_kv'] = xa_w_kv.astype(BF16)
    w['xa_w_o'] = per_layer(xa_w_o)
    w['ffn_w_gu'] = ffn_w_gu.astype(BF16)
    w['ffn_conv_b'] = ffn_conv_b[:, None, :]
    w['ffn_w_down'] = ffn_w_down.astype(BF16)
    e = np.zeros((LANE, d_inner), np.float32)
    assert 3 * n_heads <= LANE
    for h in range(n_heads):
        for part in range(3):
            e[part * n_heads + h, h * SSM_HEAD_DIM:(h + 1) * SSM_HEAD_DIM] = 1.0
    w['expand'] = jnp.asarray(e, BF16)
    w['tril'] = jnp.asarray(np.tril(np.ones((SSM_CHUNK, SSM_CHUNK), np.float32)), BF16)
    return w


def _prompt_trunk(x3, mem3, w, ssm_conv_w, ffn_conv_w):
    n, t_len, d = x3.shape
    assert t_len % SSM_CHUNK == 0 and all(t_len % win == 0 for win, _ in ATT_GROUPS)
    depth = w['norms'].shape[0]
    n_heads, d_inner = w['n_heads'], w['d_inner']
    conv_dim = ssm_conv_w.shape[2]
    d_ff = ffn_conv_w.shape[2]
    tm = PROMPT_ROWS_SSD
    tm_big = PROMPT_ROWS
    x = x3.reshape(n * t_len, d)
    mem = mem3.reshape(-1, d)
    kv_all, kv_rows = _mem_kv(mem, w['mem_norm'], w['xa_w_kv'], tm)
    tabs = _rope_tables(jnp.arange(t_len, dtype=jnp.int32))
    zero_hist = jnp.zeros((n, SSM_CONV - 1, conv_dim), F32)
    zero_h = jnp.zeros((n, n_heads * SSM_HEAD_DIM, D_STATE), F32)
    zero_fh = jnp.zeros((n, FFN_CONV - 1, d_ff), F32)
    n_ssm, n_att = (depth + 1) // 2, depth // 2
    new = {'ssm': [], 'ssm_conv': [], 'swa': [], 'ffn_conv': []}
    for i in range(depth):
        j = i // 2
        g = w['norms'][i]
        if i % 2 == 0:
            x, h_last, c_last = _ssd_layer(x, g[0], g[1], w['ssm_w_in'][j], ssm_conv_w[j], w['ssm_conv_b'][j],
                                           w['ssm_dt_bias'][j], w['ssm_a_log'][j], w['ssm_dx'][j],
                                           w['ssm_norm_w'][j], w['expand'], w['tril'], w['ssm_w_out'][j],
                                           zero_hist, zero_h, new['ssm'], j, n_ssm, n, d_inner, n_heads, tm)
            new['ssm'] = [h_last]
            new['ssm_conv'].append(c_last)
        else:
            qkv_c, new['swa'] = _norm_qkv_prompt(x, g[0], w['att_w_qkv'][j], tabs, new['swa'], j, n_att, n, tm_big)
            os_, ls_ = [], []
            for gi, (win, dil) in enumerate(ATT_GROUPS):
                o, l = _band_attn(qkv_c, n, gi, win, dil)
                os_.append(o)
                ls_.append(l)
            x = _attn_out(os_, ls_, w['att_w_o'][j], x, g[1], tm_big)
        x = _xattn_prompt(x, g[2], g[3], w['xa_w_q'][i], kv_all, i, w['xa_w_o'][i], n, tm_big)
        x, f_hist = _ffn_prompt(x, g[4], g[5], w['ffn_w_gu'], i, ffn_conv_w[i],
                                w['ffn_conv_b'][i], w['ffn_w_down'], zero_fh, n, tm_big)
        new['ffn_conv'].append(f_hist)
    n_mem = mem3.shape[1]
    hd = d // MEM_HEADS
    swa = [jnp.transpose(c.reshape(n_att, n, 2, ATT_HPG, ATT_HEAD_DIM, c.shape[3]), (0, 1, 5, 2, 3, 4))
           for c in new['swa']]
    p_mem = jnp.transpose(kv_rows.reshape(depth, n, n_mem, 2, hd // LANE, MEM_HEADS, LANE),
                          (0, 1, 2, 3, 5, 4, 6)).reshape(depth, n, n_mem, 2, MEM_HEADS, hd)
    return (x.reshape(n, t_len, d), new['ssm'][0].reshape(n_ssm, n, n_heads, SSM_HEAD_DIM, D_STATE),
            jnp.stack(new['ssm_conv']), swa, p_mem, jnp.stack(new['ffn_conv']))


def _sample_trunk(x3, w, ssm_conv_w, ffn_conv_w, state_ssm, state_ssm_conv, caches, cache_mem_kv, state_ffn_conv):
    m, t_len, d = x3.shape
    assert t_len == 1
    depth = w['norms'].shape[0]
    n_heads, d_inner = w['n_heads'], w['d_inner']
    n_ssm = state_ssm.shape[0]
    x = x3.reshape(m, d)
    tabs = _rope_tables(jnp.full((m,), PAST_LEN, jnp.int32))
    h_all = state_ssm.reshape(n_ssm * m, n_heads, SSM_HEAD_DIM, D_STATE)
    n_mem, hd = cache_mem_kv.shape[2], d // MEM_HEADS
    kv_all = jnp.transpose(cache_mem_kv.reshape(depth, m, n_mem, 2, MEM_HEADS, hd // LANE, LANE),
                           (0, 1, 2, 3, 5, 4, 6)).reshape(depth * m, n_mem, 2, MEM_HEADS * (hd // LANE), LANE)
    cache_views = [jnp.transpose(c, (0, 1, 3, 4, 5, 2)).reshape(c.shape[0] * m, 2, ATT_GW, c.shape[2])
                   for c in caches]
    new_caches = []
    new = {'ssm': [], 'ssm_conv': [], 'ffn_conv': []}
    for i in range(depth):
        j = i // 2
        g = w['norms'][i]
        if i % 2 == 0:
            proj = _norm_proj(x, g[0], w['ssm_w_in'][j], m)
            hist_t = jnp.swapaxes(state_ssm_conv[j], 0, 1)
            xs, bm, cm, xdt_t, dec_t, c_last_t = _ssd_step_pre(proj, hist_t, ssm_conv_w[j], w['ssm_conv_b'][j],
                                                                w['ssm_dt_bias'][j], w['ssm_a_log'][j],
                                                                w['expand'], d_inner)
            h_new, y_t = _ssd_step_state(h_all, new['ssm'], j, xdt_t, dec_t, bm, cm, n_heads)
            y = _ssd_step_post(y_t, xs, proj, w['ssm_dx'][j], w['ssm_norm_w'][j])
            x = _proj_res(y, w['ssm_w_out'][j], x, g[1], m)
            new['ssm'] = [h_new]
            new['ssm_conv'].append(jnp.swapaxes(c_last_t, 0, 1))
        else:
            qkv = _norm_qkv_rope(x, g[0], w['att_w_qkv'][j], tabs, m, 1)
            os_, ls_, new_caches = _attn_sample(qkv, cache_views, new_caches, j)
            x = _attn_out(os_, ls_, w['att_w_o'][j], x, g[1], m)
        x = _xattn_sample(x, g[2], g[3], w['xa_w_q'][i], kv_all, i, w['xa_w_o'][i])
        hist_t = jnp.swapaxes(state_ffn_conv[i], 0, 1)
        x, f_hist_t = _ffn_sample(x, g[4], g[5], w['ffn_w_gu'], i, ffn_conv_w[i],
                                  w['ffn_conv_b'][i], w['ffn_w_down'], hist_t, 256)
        new['ffn_conv'].append(jnp.swapaxes(f_hist_t, 0, 1))
    new_caches = [jnp.transpose(nc.reshape(c.shape[0], m, 2, ATT_HPG, ATT_HEAD_DIM, c.shape[2]), (0, 1, 5, 2, 3, 4))
                  for nc, c in zip(new_caches, caches)]
    return (x.reshape(m, 1, d), new['ssm'][0].reshape(state_ssm.shape), jnp.stack(new['ssm_conv']), new_caches,
            jnp.stack(new['ffn_conv']))


def kernel(x_prompt, x_sample, mem_prompt, state_ssm, state_ssm_conv, cache_swa_kv_w128, cache_swa_kv_w512,
           cache_swa_kv_w2048, cache_mem_kv, state_ffn_conv, norms, ssm_w_in, ssm_conv_w, ssm_conv_b,
           ssm_dt_bias, ssm_a_log, ssm_d, ssm_norm_w, ssm_w_out, att_w_qkv, att_w_o, mem_norm, xa_w_q,
           xa_w_kv, xa_w_o, ffn_w_gu, ffn_conv_w, ffn_conv_b, ffn_w_down):
    w = _prep_weights(norms, ssm_w_in, ssm_conv_b, ssm_dt_bias, ssm_a_log, ssm_d, ssm_norm_w, ssm_w_out,
                      att_w_qkv, att_w_o, mem_norm, xa_w_q, xa_w_kv, xa_w_o, ffn_w_gu, ffn_conv_b, ffn_w_down)
    caches = [cache_swa_kv_w128, cache_swa_kv_w512, cache_swa_kv_w2048]
    yp, p_ssm, p_conv, p_swa, p_mem, p_ffn = _prompt_trunk(x_prompt, mem_prompt, w, ssm_conv_w, ffn_conv_w)
    ys, s_ssm, s_conv, s_swa, s_ffn = _sample_trunk(x_sample, w, ssm_conv_w, ffn_conv_w, state_ssm,
                                                    state_ssm_conv, caches, cache_mem_kv, state_ffn_conv)
    return (yp, ys, p_ssm, p_conv, p_swa[0], p_swa[1], p_swa[2], p_mem, p_ffn,
            s_ssm, s_conv, s_swa[0], s_swa[1], s_swa[2], s_ffn)
```

```python
import functools
import math

import numpy as np
import jax
import jax.numpy as jnp
from jax import lax
from jax.experimental import pallas as pl
from jax.experimental.pallas import tpu as pltpu

F32 = jnp.float32
BF16 = jnp.bfloat16

EPS = 1e-6
PAST_LEN = 8192
SSM_HEAD_DIM = 64
SSM_GROUPS = 4
D_STATE = 128
SSM_CONV = 4
SSM_CHUNK = 128
CONV_COLS = 1024
ATT_GROUPS = ((128, 1), (512, 4), (2048, 16))
ATT_HPG = 4
ATT_HEAD_DIM = 64
ATT_GW = ATT_HPG * ATT_HEAD_DIM
BAND_BLOCK = 512
ROT_DIM = ATT_HEAD_DIM // 4
ROPE_THETA = 500000.0
MEM_HEADS = 4
FFN_CONV = 3

LANE = 128
SUBLANE = 8
VMEM_LIMIT = 56 * 1024 * 1024
PROMPT_ROWS = 512
PROMPT_ROWS_SSD = 256


def _cparams(sem):
    return pltpu.CompilerParams(dimension_semantics=sem, vmem_limit_bytes=VMEM_LIMIT)


def _rms(x, g):
    return x * lax.rsqrt(jnp.mean(x * x, axis=-1, keepdims=True) + EPS) * g


def _silu(x):
    return x / (1.0 + jnp.exp(-x))


def _softplus(x):
    return jnp.maximum(x, 0.0) + jnp.log(1.0 + jnp.exp(-jnp.abs(x)))


def _dot(a, b):
    return jnp.dot(a, b, preferred_element_type=F32)


def _dot_nt(a, b):
    return lax.dot_general(a, b, (((1,), (1,)), ((), ())), preferred_element_type=F32)


def _split3(v):
    hi = v.astype(BF16)
    r = v - hi.astype(F32)
    mid = r.astype(BF16)
    lo = (r - mid.astype(F32)).astype(BF16)
    return hi, mid, lo


def _expand(v, e, n_heads):
    return _dot(_pack3(v, n_heads), e)


def _pack3(v, n_heads):
    hi = v.astype(BF16).astype(F32)
    r1 = v - hi
    mid = r1.astype(BF16).astype(F32)
    lo = r1 - mid
    lane = lax.broadcasted_iota(jnp.int32, v.shape, 1)
    packed = jnp.where(lane < n_heads, hi,
                       jnp.where(lane < 2 * n_heads, pltpu.roll(mid, n_heads, 1),
                                 jnp.where(lane < 3 * n_heads, pltpu.roll(lo, 2 * n_heads, 1), 0.0)))
    return packed.astype(BF16)


def _full(shape):
    return pl.BlockSpec(shape, lambda *_: (0,) * len(shape))


def _resident(shape):
    return pl.BlockSpec(shape, lambda *_: (0,) * len(shape), pipeline_mode=pl.Buffered(1))


def _norm_proj_body(x_ref, g_ref, w_ref, o_ref):
    u = _rms(x_ref[...], g_ref[...]).astype(BF16)
    o_ref[...] = _dot(u, w_ref[...])


def _norm_proj(x, g, w, tm):
    m, d = x.shape
    n = w.shape[1]
    return pl.pallas_call(
        _norm_proj_body,
        grid=(m // tm,),
        in_specs=[pl.BlockSpec((tm, d), lambda i: (i, 0)), _full((1, d)), _full((d, n))],
        out_specs=pl.BlockSpec((tm, n), lambda i: (i, 0)),
        out_shape=jax.ShapeDtypeStruct((m, n), F32),
        compiler_params=_cparams(("parallel",)),
        name="norm_proj",
    )(x, g, w)


def _norm_qkv_rope_body(x_ref, g_ref, w_ref, cos_ref, sa_ref, sb_ref, o_ref, *, n_rot):
    u = _rms(x_ref[...], g_ref[...]).astype(BF16)
    y = _dot(u, w_ref[...])
    cos, sa, sb = cos_ref[...], sa_ref[...], sb_ref[...]
    for c in range(n_rot // LANE):
        t = y[:, c * LANE:(c + 1) * LANE]
        o_ref[:, c * LANE:(c + 1) * LANE] = (t * cos + pltpu.roll(t, LANE - ROT_DIM // 2, 1) * sa
                                             + pltpu.roll(t, ROT_DIM // 2, 1) * sb)
    o_ref[:, n_rot:] = y[:, n_rot:]


def _norm_qkv_rope(x, g, w, tabs, tm, seq_blocks):
    m, d = x.shape
    n = w.shape[1]
    tab_spec = pl.BlockSpec((tm, LANE), lambda i: (i % seq_blocks, 0))
    return pl.pallas_call(
        functools.partial(_norm_qkv_rope_body, n_rot=2 * n // 3),
        grid=(m // tm,),
        in_specs=[pl.BlockSpec((tm, d), lambda i: (i, 0)), _full((1, d)), _full((d, n)),
                  tab_spec, tab_spec, tab_spec],
        out_specs=pl.BlockSpec((tm, n), lambda i: (i, 0)),
        out_shape=jax.ShapeDtypeStruct((m, n), F32),
        compiler_params=_cparams(("parallel",)),
        name="norm_qkv_rope",
    )(x, g, w, *tabs)


def _rope_chunk(t, cos, sa, sb):
    return t * cos + pltpu.roll(t, LANE - ROT_DIM // 2, 1) * sa + pltpu.roll(t, ROT_DIM // 2, 1) * sb


def _norm_qkv_prompt_body(*refs, n_rot, n_prev, nblks):
    x_ref, g_ref, w_ref, cos_ref, sa_ref, sb_ref = refs[:6]
    o_ref = refs[6 + n_prev]
    c_refs = refs[7 + n_prev:]
    ng = len(c_refs)
    t = pl.program_id(1)
    tb = pl.num_programs(1)
    tm = x_ref.shape[0]
    u = _rms(x_ref[...], g_ref[...]).astype(BF16)
    y = _dot(u, w_ref[...])
    cos, sa, sb = cos_ref[...], sa_ref[...], sb_ref[...]
    vals = []
    for c in range(y.shape[1] // LANE):
        v = y[:, c * LANE:(c + 1) * LANE]
        if c * LANE < n_rot:
            v = _rope_chunk(v, cos, sa, sb)
        o_ref[c] = v
        vals.append(v)
    per_sec = len(vals) // 3
    per_grp = ATT_GW // LANE
    for gi, c_ref in enumerate(c_refs):
        kw = c_ref.shape[2]
        nblk = nblks[gi]

        @pl.when(t >= tb - nblk)
        def _(gi=gi, c_ref=c_ref, kw=kw):
            for kv in range(2):
                for hp in range(per_grp):
                    v = vals[(1 + kv) * per_sec + gi * per_grp + hp]
                    c_ref[kv, hp * LANE:(hp + 1) * LANE, :] = v[tm - kw:, :].T


def _norm_qkv_prompt(x, g, w, tabs, prev, layer, n_layers, n_seq, tm):
    m, d = x.shape
    n = w.shape[1]
    t_len = m // n_seq
    tb = t_len // tm
    tab_spec = pl.BlockSpec((tm, LANE), lambda b, t: (t, 0))
    c_specs, c_shapes, nblks = [], [], []
    for gi, (win, _) in enumerate(ATT_GROUPS):
        keep = min(win, t_len)
        kw = min(keep, tm)
        nblk = keep // kw
        assert keep % kw == 0
        nblks.append(nblk)
        c_specs.append(pl.BlockSpec((None, 2, ATT_GW, kw),
                                    lambda b, t, nblk=nblk: (layer * n_seq + b, 0, 0, jnp.maximum(t - (tb - nblk), 0))))
        c_shapes.append(jax.ShapeDtypeStruct((n_layers * n_seq, 2, ATT_GW, keep), F32))
    res = pl.pallas_call(
        functools.partial(_norm_qkv_prompt_body, n_rot=2 * n // 3, n_prev=len(prev), nblks=tuple(nblks)),
        grid=(n_seq, tb),
        in_specs=[pl.BlockSpec((tm, d), lambda b, t: (b * tb + t, 0)), _full((1, d)), _resident((d, n)),
                  tab_spec, tab_spec, tab_spec] + [pl.BlockSpec(memory_space=pl.ANY)] * len(prev),
        out_specs=[pl.BlockSpec((n // LANE, tm, LANE), lambda b, t: (0, b * tb + t, 0))] + c_specs,
        out_shape=[jax.ShapeDtypeStruct((n // LANE, m, LANE), F32)] + c_shapes,
        input_output_aliases={6 + k: 1 + k for k in range(len(prev))},
        compiler_params=_cparams(("parallel", "arbitrary")),
        name="norm_qkv_prompt",
    )(x, g, w, *tabs, *prev)
    return res[0], list(res[1:])


def _rope_tables(pos):
    half = ROT_DIM // 2
    inv = ROPE_THETA ** (-jnp.arange(half, dtype=F32) / half)
    ang = pos.astype(F32)[:, None] * inv[None, :]
    cos, sin = jnp.cos(ang), jnp.sin(ang)
    p = pos.shape[0]
    rest = ATT_HEAD_DIM - ROT_DIM
    c = jnp.concatenate([cos, cos, jnp.ones((p, rest), F32)], axis=1)
    sa = jnp.concatenate([-sin, jnp.zeros((p, half + rest), F32)], axis=1)
    sb = jnp.concatenate([jnp.zeros((p, half), F32), sin, jnp.zeros((p, rest), F32)], axis=1)
    rep = LANE // ATT_HEAD_DIM
    return tuple(jnp.tile(t, (1, rep)) for t in (c, sa, sb))


def _proj_res_body(y_ref, w_ref, x_ref, g_ref, o_ref):
    f = _dot(y_ref[...].astype(BF16), w_ref[...])
    o_ref[...] = x_ref[...] + _rms(f, g_ref[...])


def _proj_res(y, w, x, g, tm):
    m, k = y.shape
    d = w.shape[1]
    return pl.pallas_call(
        _proj_res_body,
        grid=(m // tm,),
        in_specs=[pl.BlockSpec((tm, k), lambda i: (i, 0)), _full((k, d)),
                  pl.BlockSpec((tm, d), lambda i: (i, 0)), _full((1, d))],
        out_specs=pl.BlockSpec((tm, d), lambda i: (i, 0)),
        out_shape=jax.ShapeDtypeStruct((m, d), F32),
        compiler_params=_cparams(("parallel",)),
        name="proj_res",
    )(y, w, x, g)


def _attn_out_body(o0, o1, o2, l0, l1, l2, w_ref, x_ref, g_ref, out_ref):
    def load(ref):
        if len(ref.shape) == 2:
            return ref[...]
        return jnp.concatenate([ref[c] for c in range(ref.shape[0])], axis=1)

    ls = [load(l0), load(l1), load(l2)]
    mx = jnp.maximum(jnp.maximum(ls[0], ls[1]), ls[2])
    es = [jnp.exp(l - mx) for l in ls]
    den = es[0] + es[1] + es[2]
    og = jnp.concatenate([(es[gi] / den * load(o_ref)).astype(BF16) for gi, o_ref in enumerate((o0, o1, o2))], axis=1)
    out_ref[...] = x_ref[...] + _rms(_dot(og, w_ref[...]), g_ref[...])


def _attn_out(os_, ls_, w, x, g, tm):
    m, d = x.shape
    if os_[0].ndim == 2:
        blk = pl.BlockSpec((tm, ATT_GW), lambda i: (i, 0))
    else:
        blk = pl.BlockSpec((ATT_GW // LANE, tm, LANE), lambda i: (0, i, 0))
    return pl.pallas_call(
        _attn_out_body,
        grid=(m // tm,),
        in_specs=[blk] * 6 + [_full(w.shape), pl.BlockSpec((tm, d), lambda i: (i, 0)), _full((1, d))],
        out_specs=pl.BlockSpec((tm, d), lambda i: (i, 0)),
        out_shape=jax.ShapeDtypeStruct((m, d), F32),
        compiler_params=_cparams(("parallel",)),
        name="attn_out",
    )(*os_, *ls_, w, x, g)


def _mem_kv_body(x_ref, g_ref, w_ref, o_ref, t_ref):
    tm = x_ref.shape[0]
    u = _rms(x_ref[...], g_ref[...]).astype(BF16)
    y = _dot(u, w_ref[...])
    o_ref[...] = y
    hd = y.shape[1] // (2 * MEM_HEADS)
    nch = hd // LANE
    tok_rows = 2 * nch * MEM_HEADS
    for kv in range(2):
        for h in range(MEM_HEADS):
            for c in range(nch):
                col = (kv * MEM_HEADS + h) * hd + c * LANE
                t_ref[pl.ds((kv * nch + c) * MEM_HEADS + h, tm, stride=tok_rows), :] = y[:, col:col + LANE]


def _mem_kv(mem, g, w, tm):
    m, d = mem.shape
    depth, _, n = w.shape
    tok_rows = n // LANE
    return pl.pallas_call(
        _mem_kv_body,
        grid=(depth, m // tm),
        in_specs=[pl.BlockSpec((tm, d), lambda l, i: (i, 0)),
                  pl.BlockSpec((None, 1, d), lambda l, i: (l, 0, 0)),
                  pl.BlockSpec((None, d, n), lambda l, i: (l, 0, 0))],
        out_specs=[pl.BlockSpec((None, tm, n), lambda l, i: (l, i, 0)),
                   pl.BlockSpec((None, tm * tok_rows, LANE), lambda l, i: (l, i, 0))],
        out_shape=[jax.ShapeDtypeStruct((depth, m, n), F32),
                   jax.ShapeDtypeStruct((depth, m * tok_rows, LANE), F32)],
        compiler_params=_cparams(("parallel", "parallel")),
        name="mem_kv",
    )(mem, g, w)


def _xattn_prompt_body(x_ref, gpre_ref, gpost_ref, wq_ref, kv_ref, wo_ref, o_ref, obuf):
    x = x_ref[...]
    d = x.shape[1]
    hd = d // MEM_HEADS
    u = _rms(x, gpre_ref[...]).astype(BF16)
    q = _dot(u, wq_ref[...]).astype(BF16)
    scale = hd ** -0.5
    for h in range(MEM_HEADS):
        kh = kv_ref[:, h * hd:(h + 1) * hd].astype(BF16)
        vh = kv_ref[:, d + h * hd:d + (h + 1) * hd].astype(BF16)
        s = _dot_nt(q[:, h * hd:(h + 1) * hd], kh) * scale
        mx = jnp.max(s, axis=-1, keepdims=True)
        p = jnp.exp(s - mx)
        den = jnp.sum(p, axis=-1, keepdims=True)
        obuf[:, h * hd:(h + 1) * hd] = (_dot(p.astype(BF16), vh) / den).astype(BF16)
    f = _dot(obuf[...], wo_ref[...])
    o_ref[...] = x + _rms(f, gpost_ref[...])


def _xattn_prompt(x, gpre, gpost, wq, kv, layer, wo, n_seq, tm):
    m, d = x.shape
    tb = m // n_seq // tm
    n_mem = kv.shape[1] // n_seq
    return pl.pallas_call(
        _xattn_prompt_body,
        grid=(n_seq, tb),
        in_specs=[pl.BlockSpec((tm, d), lambda b, t: (b * tb + t, 0)), _full((1, d)), _full((1, d)),
                  _resident((d, d)), pl.BlockSpec((None, n_mem, 2 * d), lambda b, t: (layer, b, 0)),
                  _resident((d, d))],
        out_specs=pl.BlockSpec((tm, d), lambda b, t: (b * tb + t, 0)),
        out_shape=jax.ShapeDtypeStruct((m, d), F32),
        scratch_shapes=[pltpu.VMEM((tm, d), BF16)],
        compiler_params=_cparams(("parallel", "parallel")),
        name="xattn_prompt",
    )(x, gpre, gpost, wq, kv, wo)


def _xattn_sample_body(x_ref, gpre_ref, gpost_ref, wq_ref, kv_ref, wo_ref, o_ref, q_scr, o_scr):
    b = pl.program_id(0)
    d = x_ref.shape[1]
    hd = d // MEM_HEADS

    @pl.when(b == 0)
    def _():
        u = _rms(x_ref[...], gpre_ref[...]).astype(BF16)
        q_scr[...] = _dot(u, wq_ref[...]) * (hd ** -0.5)

    qb = q_scr[pl.ds(b, 1), :]
    nch = hd // LANE
    n_mem, _, rows, _ = kv_ref.shape
    row_id = lax.broadcasted_iota(jnp.int32, (rows, LANE), 0)
    q_tile = jnp.zeros((rows, LANE), F32)
    for c in range(nch):
        for h in range(MEM_HEADS):
            piece = qb[:, h * hd + c * LANE:h * hd + (c + 1) * LANE]
            q_tile = jnp.where(row_id == c * MEM_HEADS + h, piece, q_tile)
    prod = (kv_ref[:, 0] * q_tile).reshape(n_mem * rows, LANE)
    p_hi, p_lo = _hi_lo(prod)
    ones = jnp.ones((LANE, LANE), BF16)
    s = (_dot(p_hi, ones) + _dot(p_lo, ones)).reshape(n_mem, rows, LANE)
    part = s
    for c in range(1, nch):
        s = s + pltpu.roll(part, c * MEM_HEADS, 1)
    mx = jnp.max(s, axis=0, keepdims=True)
    p = jnp.exp(s - mx)
    den = jnp.sum(p, axis=0)
    o_tile = jnp.sum(p * kv_ref[:, 1], axis=0) / den
    o_scr[pl.ds(b, 1), :] = jnp.concatenate(
        [o_tile[c * MEM_HEADS + h:c * MEM_HEADS + h + 1, :] for h in range(MEM_HEADS) for c in range(nch)], axis=1)

    @pl.when(b == pl.num_programs(0) - 1)
    def _():
        f = _dot(o_scr[...].astype(BF16), wo_ref[...])
        o_ref[...] = x_ref[...] + _rms(f, gpost_ref[...])


def _xattn_sample(x, gpre, gpost, wq, kv_all, layer, wo):
    m, d = x.shape
    return pl.pallas_call(
        _xattn_sample_body,
        grid=(m,),
        in_specs=[_full((m, d)), _full((1, d)), _full((1, d)), _full((d, d)),
                  pl.BlockSpec((None,) + kv_all.shape[1:], lambda b: (layer * m + b, 0, 0, 0, 0)), _full((d, d))],
        out_specs=_full((m, d)),
        out_shape=jax.ShapeDtypeStruct((m, d), F32),
        scratch_shapes=[pltpu.VMEM((m, d), F32), pltpu.VMEM((m, d), F32)],
        compiler_params=_cparams(("arbitrary",)),
        name="xattn_sample",
    )(x, gpre, gpost, wq, kv_all, wo)


def _ffn_prompt_body(x_ref, gpre_ref, gpost_ref, wg_ref, wu_ref, cw_ref, cb_ref, wd_ref, hist_ref,
                     o_ref, nh_ref, gbuf):
    t = pl.program_id(1)
    tm = x_ref.shape[0]
    k = FFN_CONV - 1
    base = SUBLANE - k

    @pl.when(t == 0)
    def _():
        gbuf[base:SUBLANE, :] = hist_ref[...]

    x = x_ref[...]
    u = _rms(x, gpre_ref[...]).astype(BF16)
    gbuf[SUBLANE:SUBLANE + tm, :] = _dot(u, wg_ref[...])
    up = _dot(u, wu_ref[...])
    gc = gbuf[base:base + tm, :] * cw_ref[0:1, :] + cb_ref[...]
    for j in range(1, FFN_CONV):
        gc = gc + gbuf[base + j:base + j + tm, :] * cw_ref[j:j + 1, :]
    hmid = (_silu(gc) * up).astype(BF16)
    f = _dot(hmid, wd_ref[...])
    o_ref[...] = x + _rms(f, gpost_ref[...])
    last = gbuf[tm + base:tm + SUBLANE, :]
    nh_ref[...] = last
    gbuf[base:SUBLANE, :] = last


def _ffn_prompt(x, gpre, gpost, w_gu, layer, cw, cb, w_down, hist, n_seq, tm):
    m, d = x.shape
    f = w_down.shape[1]
    tb = m // n_seq // tm
    k = FFN_CONV - 1
    one = pl.Buffered(1)
    return pl.pallas_call(
        _ffn_prompt_body,
        grid=(n_seq, tb),
        in_specs=[pl.BlockSpec((tm, d), lambda b, t: (b * tb + t, 0)), _full((1, d)), _full((1, d)),
                  pl.BlockSpec((None, d, f), lambda b, t: (layer, 0, 0), pipeline_mode=one),
                  pl.BlockSpec((None, d, f), lambda b, t: (layer, 0, 1), pipeline_mode=one),
                  _full((FFN_CONV, f)), _full((1, f)),
                  pl.BlockSpec((None, f, d), lambda b, t: (layer, 0, 0), pipeline_mode=one),
                  pl.BlockSpec((None, k, f), lambda b, t: (b, 0, 0))],
        out_specs=[pl.BlockSpec((tm, d), lambda b, t: (b * tb + t, 0)),
                   pl.BlockSpec((None, k, f), lambda b, t: (b, 0, 0))],
        out_shape=[jax.ShapeDtypeStruct((m, d), F32), jax.ShapeDtypeStruct((n_seq, k, f), F32)],
        scratch_shapes=[pltpu.VMEM((tm + SUBLANE, f), F32)],
        compiler_params=_cparams(("parallel", "arbitrary")),
        name="ffn_prompt",
    )(x, gpre, gpost, w_gu, w_gu, cw, cb, w_down, hist)


def _ffn_sample_body(x_ref, gpre_ref, gpost_ref, wg_ref, wu_ref, cw_ref, cb_ref, wd_ref, hist_ref,
                     o_ref, nh_ref, u_scr, acc):
    j = pl.program_id(0)

    @pl.when(j == 0)
    def _():
        u_scr[...] = _rms(x_ref[...], gpre_ref[...]).astype(BF16)
        acc[...] = jnp.zeros_like(acc)

    u = u_scr[...]
    gate = _dot(u, wg_ref[...])
    up = _dot(u, wu_ref[...])
    gc = hist_ref[0] * cw_ref[0:1, :] + cb_ref[...]
    for k in range(1, FFN_CONV - 1):
        gc = gc + hist_ref[k] * cw_ref[k:k + 1, :]
        nh_ref[k - 1] = hist_ref[k]
    gc = gc + gate * cw_ref[FFN_CONV - 1:FFN_CONV, :]
    nh_ref[FFN_CONV - 2] = gate
    hmid = (_silu(gc) * up).astype(BF16)
    acc[...] += _dot(hmid, wd_ref[...])

    @pl.when(j == pl.num_programs(0) - 1)
    def _():
        o_ref[...] = x_ref[...] + _rms(acc[...], gpost_ref[...])


def _ffn_sample(x, gpre, gpost, w_gu, layer, cw, cb, w_down, hist_t, tn):
    m, d = x.shape
    f = w_down.shape[1]
    k = FFN_CONV - 1
    nb = f // tn
    return pl.pallas_call(
        _ffn_sample_body,
        grid=(nb,),
        in_specs=[_full((m, d)), _full((1, d)), _full((1, d)),
                  pl.BlockSpec((None, d, tn), lambda j: (layer, 0, j)),
                  pl.BlockSpec((None, d, tn), lambda j: (layer, 0, nb + j)),
                  pl.BlockSpec((FFN_CONV, tn), lambda j: (0, j)), pl.BlockSpec((1, tn), lambda j: (0, j)),
                  pl.BlockSpec((None, tn, d), lambda j: (layer, j, 0)),
                  pl.BlockSpec((k, m, tn), lambda j: (0, 0, j))],
        out_specs=[_full((m, d)), pl.BlockSpec((k, m, tn), lambda j: (0, 0, j))],
        out_shape=[jax.ShapeDtypeStruct((m, d), F32), jax.ShapeDtypeStruct((k, m, f), F32)],
        scratch_shapes=[pltpu.VMEM((m, d), BF16), pltpu.VMEM((m, d), F32)],
        compiler_params=_cparams(("arbitrary",)),
        name="ffn_sample",
    )(x, gpre, gpost, w_gu, w_gu, cw, cb, w_down, hist_t)


def _ssd_activations(u, w_ref, cw_ref, cb_ref, dtb_ref, act, cbuf, clast_ref, d_inner):
    tm = u.shape[0]
    k = SSM_CONV - 1
    base = SUBLANE - k
    conv_dim = cw_ref.shape[1]
    act[:, :d_inner] = _silu(_dot(u, w_ref[:, :d_inner]))
    act[:, d_inner + conv_dim:] = _softplus(_dot(u, w_ref[:, d_inner + conv_dim:]) + dtb_ref[...])
    for c0 in range(0, conv_dim, CONV_COLS):
        cs = slice(c0, c0 + CONV_COLS)
        cbuf[SUBLANE:SUBLANE + tm, cs] = _dot(u, w_ref[:, d_inner + c0:d_inner + c0 + CONV_COLS])
        full = cbuf[:, cs]
        xc = full[SUBLANE:, :] * cw_ref[k:k + 1, cs] + cb_ref[:, cs]
        for j in range(k):
            xc = xc + pltpu.roll(full, k - j, 0)[SUBLANE:, :] * cw_ref[j:j + 1, cs]
        act[:, d_inner + c0:d_inner + c0 + CONV_COLS] = _silu(xc)
    last = cbuf[tm + base:tm + SUBLANE, :]
    clast_ref[...] = last
    cbuf[base:SUBLANE, :] = last


def _ssd_chunk(act_ref, alog_ref, dx_ref, nw_ref, e_ref, tril_ref, state, y_ref, d_inner, n_heads):
    L = SSM_CHUNK
    gn = SSM_GROUPS * D_STATE
    conv_dim = d_inner + 2 * gn
    hpg = n_heads // SSM_GROUPS
    gw = d_inner // SSM_GROUPS
    xs = act_ref[:, d_inner:2 * d_inner]
    xs_b = xs.astype(BF16)
    e = e_ref[...]
    dt = act_ref[:, d_inner + conv_dim:]
    a = dt * (-jnp.exp(alog_ref[...]))
    tril = tril_ref[...]
    a_hi, a_mid, a_lo = _split3(a)
    acum = _dot(tril, a_hi) + _dot(tril, a_mid) + _dot(tril, a_lo)
    acum_t = acum.T
    dt_t = dt.T
    a_last = acum[L - 1:L, :]
    d_acc = _expand(jnp.exp(acum), e, n_heads)
    xde = (xs * _expand(dt * jnp.exp(a_last - acum), e, n_heads)).astype(BF16)
    row = lax.broadcasted_iota(jnp.int32, (L, L), 0)
    col = lax.broadcasted_iota(jnp.int32, (L, L), 1)
    causal = row >= col

    for g in range(SSM_GROUPS):
        b0 = 2 * d_inner + g * D_STATE
        gs = slice(g * gw, (g + 1) * gw)
        b_f = act_ref[:, b0:b0 + D_STATE]
        bg = b_f.astype(BF16)
        cg = act_ref[:, b0 + gn:b0 + gn + D_STATE].astype(BF16)
        cb = _dot_nt(cg, bg)
        y_inter = _dot(cg, state[:, gs].astype(BF16)) * d_acc[:, gs]
        tiles = []
        for t0 in range(g * gw, (g + 1) * gw, LANE):
            xt = xs_b[:, t0:t0 + LANE]
            lane = lax.broadcasted_iota(jnp.int32, xt.shape, 1)
            tile = None
            for k in range(LANE // SSM_HEAD_DIM):
                hd = t0 // SSM_HEAD_DIM + k
                seg = acum[:, hd:hd + 1] - acum_t[hd:hd + 1, :]
                w = cb * jnp.exp(jnp.where(causal, seg, -jnp.inf)) * dt_t[hd:hd + 1, :]
                mine = (lane >= k * SSM_HEAD_DIM) & (lane < (k + 1) * SSM_HEAD_DIM)
                part = _dot(w.astype(BF16), jnp.where(mine, xt, jnp.zeros_like(xt)))
                tile = part if tile is None else tile + part
            tiles.append(tile)
        y = jnp.concatenate(tiles, axis=1) + y_inter + xs[:, gs] * dx_ref[:, gs]
        y_ref[:, gs] = _rms(y * act_ref[:, gs], nw_ref[:, gs]).astype(BF16)
        state[:, gs] = state[:, gs] * d_acc[L - 1:L, gs] + _dot(b_f.T.astype(BF16), xde[:, gs])


N_SSD_IN = 15


def _ssd_layer_body(x_ref, gpre_ref, gpost_ref, w_ref, cw_ref, cb_ref, dtb_ref, alog_ref, dx_ref, nw_ref,
                    e_ref, tril_ref, wo_ref, hist_ref, h0_ref, *rest, d_inner, n_heads):
    o_ref, hlast_ref, clast_ref, act, cbuf, state, ybf = rest[-7:]
    t = pl.program_id(1)
    tm = x_ref.shape[0]
    L = SSM_CHUNK
    base = SUBLANE - (SSM_CONV - 1)

    @pl.when(t == 0)
    def _():
        cbuf[base:SUBLANE, :] = hist_ref[...]
        state[...] = h0_ref[...].T

    x = x_ref[...]
    u = _rms(x, gpre_ref[...]).astype(BF16)
    _ssd_activations(u, w_ref, cw_ref, cb_ref, dtb_ref, act, cbuf, clast_ref, d_inner)
    for ci in range(tm // L):
        _ssd_chunk(act.at[ci * L:(ci + 1) * L, :], alog_ref, dx_ref, nw_ref, e_ref, tril_ref, state,
                   ybf.at[ci * L:(ci + 1) * L, :], d_inner, n_heads)
    o_ref[...] = x + _rms(_dot(ybf[...], wo_ref[...]), gpost_ref[...])

    @pl.when(t == pl.num_programs(1) - 1)
    def _():
        hlast_ref[...] = state[...].T


def _ssd_layer(x, gpre, gpost, w, cw, cb, dtb, alog, dx, nw, e, tril, wo, hist, h0, prev, layer, n_layers,
               n_seq, d_inner, n_heads, tm):
    m, d = x.shape
    n = w.shape[1]
    tb = m // n_seq // tm
    conv_dim = cw.shape[1]
    k = SSM_CONV - 1
    rows = n_heads * SSM_HEAD_DIM
    assert conv_dim % CONV_COLS == 0 and tm % SSM_CHUNK == 0
    ins = (x, gpre, gpost, w, cw, cb, dtb, alog, dx, nw, e, tril, wo, hist, h0)
    assert len(ins) == N_SSD_IN
    x_spec = pl.BlockSpec((tm, d), lambda b, t: (b * tb + t, 0))
    return pl.pallas_call(
        functools.partial(_ssd_layer_body, d_inner=d_inner, n_heads=n_heads),
        grid=(n_seq, tb),
        in_specs=[x_spec, _full((1, d)), _full((1, d)), _resident(w.shape), _full(cw.shape), _full(cb.shape),
                  _full(dtb.shape), _full(alog.shape), _full(dx.shape), _full(nw.shape), _full(e.shape),
                  _full(tril.shape), _resident(wo.shape),
                  pl.BlockSpec((None, k, conv_dim), lambda b, t: (b, 0, 0)),
                  pl.BlockSpec((None, rows, D_STATE), lambda b, t: (b, 0, 0))]
                 + [pl.BlockSpec(memory_space=pl.ANY)] * len(prev),
        out_specs=[x_spec,
                   pl.BlockSpec((None, rows, D_STATE), lambda b, t: (layer * n_seq + b, 0, 0)),
                   pl.BlockSpec((None, k, conv_dim), lambda b, t: (b, 0, 0))],
        out_shape=[jax.ShapeDtypeStruct((m, d), F32),
                   jax.ShapeDtypeStruct((n_layers * n_seq, rows, D_STATE), F32),
                   jax.ShapeDtypeStruct((n_seq, k, conv_dim), F32)],
        input_output_aliases={N_SSD_IN: 1} if prev else {},
        scratch_shapes=[pltpu.VMEM((tm, n), F32), pltpu.VMEM((tm + SUBLANE, conv_dim), F32),
                        pltpu.VMEM((D_STATE, rows), F32),
                        pltpu.VMEM((tm, d_inner), BF16)],
        compiler_params=_cparams(("parallel", "arbitrary")),
        name="ssd_layer",
    )(*ins, *prev)


def _ssd_step_pre_body(proj_ref, hist_ref, cw_ref, cb_ref, dtb_ref, alog_ref, e_ref,
                       xs_ref, b_ref, c_ref, xdt_t_ref, dec_b_ref, clast_ref, *, d_inner):
    gn = SSM_GROUPS * D_STATE
    conv_dim = d_inner + 2 * gn
    n_heads = d_inner // SSM_HEAD_DIM
    m = proj_ref.shape[0]
    xbc = proj_ref[:, d_inner:d_inner + conv_dim]
    xc = hist_ref[0] * cw_ref[0:1, :] + cb_ref[...]
    for j in range(1, SSM_CONV - 1):
        xc = xc + hist_ref[j] * cw_ref[j:j + 1, :]
        clast_ref[j - 1] = hist_ref[j]
    xc = _silu(xc + xbc * cw_ref[SSM_CONV - 1:SSM_CONV, :])
    clast_ref[SSM_CONV - 2] = xbc
    xs = xc[:, :d_inner]
    xs_ref[...] = xs
    b_ref[...] = xc[:, d_inner:d_inner + gn]
    c_ref[...] = xc[:, d_inner + gn:]
    dt = _softplus(proj_ref[:, d_inner + conv_dim:] + dtb_ref[...])
    dec = jnp.exp(dt * (-jnp.exp(alog_ref[...])))
    pad = jnp.zeros((LANE - m, d_inner), F32)
    xdt_t_ref[...] = jnp.concatenate([xs * _expand(dt, e_ref[...], n_heads), pad], axis=0).T
    for j in range(n_heads):
        dec_b_ref[j] = jnp.broadcast_to(dec[:, j:j + 1], (m, LANE))


def _ssd_step_pre(proj, hist_t, cw, cb, dtb, alog, e, d_inner):
    m = proj.shape[0]
    gn = SSM_GROUPS * D_STATE
    conv_dim = cw.shape[1]
    k = SSM_CONV - 1
    n_heads = d_inner // SSM_HEAD_DIM
    args = (proj, hist_t, cw, cb, dtb, alog, e)
    return pl.pallas_call(
        functools.partial(_ssd_step_pre_body, d_inner=d_inner),
        grid=(1,),
        in_specs=[_full(a.shape) for a in args],
        out_specs=[_full((m, d_inner)), _full((m, gn)), _full((m, gn)), _full((d_inner, LANE)),
                   _full((n_heads, m, LANE)), _full((k, m, conv_dim))],
        out_shape=[jax.ShapeDtypeStruct((m, d_inner), F32), jax.ShapeDtypeStruct((m, gn), F32),
                   jax.ShapeDtypeStruct((m, gn), F32), jax.ShapeDtypeStruct((d_inner, LANE), F32),
                   jax.ShapeDtypeStruct((n_heads, m, LANE), F32), jax.ShapeDtypeStruct((k, m, conv_dim), F32)],
        compiler_params=_cparams(("arbitrary",)),
        name="ssd_step_pre",
    )(*args)


def _hi_lo(v):
    hi = v.astype(BF16)
    return hi, (v - hi.astype(F32)).astype(BF16)


def _ssd_step_state_body(h0_ref, xdt_t_ref, dec_b_ref, b_ref, c_ref, *rest):
    hn_ref, y_t_ref = rest[-2:]
    m = h0_ref.shape[0]
    wide = jnp.concatenate([b_ref[...]] * m, axis=1)
    own = (lax.broadcasted_iota(jnp.int32, wide.shape, 1) // D_STATE
           == lax.broadcasted_iota(jnp.int32, wide.shape, 0))
    bbd = jnp.concatenate([jnp.where(own, wide, 0.0), jnp.zeros((LANE - m, m * D_STATE), F32)], axis=0)
    b_hi, b_lo = _hi_lo(bbd)
    x_hi, x_lo = _hi_lo(xdt_t_ref[...])
    upd = _dot(jnp.concatenate([x_hi, x_lo, x_hi], axis=1), jnp.concatenate([b_hi, b_hi, b_lo], axis=0))
    prods = []
    for b in range(m):
        h = h0_ref[b] * dec_b_ref[b:b + 1, :] + upd[:, b * D_STATE:(b + 1) * D_STATE]
        hn_ref[b] = h
        prods.append(h * c_ref[b:b + 1, :])
    p_hi, p_lo = _hi_lo(jnp.concatenate(prods, axis=0))
    ones = jnp.ones((D_STATE, LANE), BF16)
    sums = _dot(p_hi, ones) + _dot(p_lo, ones)
    lane = lax.broadcasted_iota(jnp.int32, (SSM_HEAD_DIM, LANE), 1)
    acc = jnp.zeros((SSM_HEAD_DIM, LANE), F32)
    for b in range(m):
        acc = jnp.where(lane == b, sums[b * SSM_HEAD_DIM:(b + 1) * SSM_HEAD_DIM, :], acc)
    y_t_ref[...] = acc


def _ssd_step_state(h_all, prev, layer, xdt_t, dec_t, bm, cm, n_heads):
    m = bm.shape[0]
    hpg = n_heads // SSM_GROUPS
    h_spec = pl.BlockSpec((m, None, SSM_HEAD_DIM, D_STATE), lambda j: (layer, j, 0, 0))
    return pl.pallas_call(
        _ssd_step_state_body,
        grid=(n_heads,),
        in_specs=[h_spec,
                  pl.BlockSpec((SSM_HEAD_DIM, LANE), lambda j: (j, 0)),
                  pl.BlockSpec((None, m, LANE), lambda j: (j, 0, 0)),
                  pl.BlockSpec((m, D_STATE), lambda j: (0, j // hpg)),
                  pl.BlockSpec((m, D_STATE), lambda j: (0, j // hpg))]
                 + [pl.BlockSpec(memory_space=pl.ANY)] * len(prev),
        out_specs=[h_spec, pl.BlockSpec((SSM_HEAD_DIM, LANE), lambda j: (j, 0))],
        out_shape=[jax.ShapeDtypeStruct(h_all.shape, F32),
                   jax.ShapeDtypeStruct((n_heads * SSM_HEAD_DIM, LANE), F32)],
        input_output_aliases={5: 0} if prev else {},
        compiler_params=_cparams(("parallel",)),
        name="ssd_step_state",
    )(h_all, xdt_t, dec_t, bm, cm, *prev)


def _ssd_step_post_body(y_t_ref, xs_ref, z_ref, dx_ref, nw_ref, o_ref):
    m, d_inner = xs_ref.shape
    gw = d_inner // SSM_GROUPS
    y = y_t_ref[...].T[:m, :] + xs_ref[...] * dx_ref[...]
    gated = y * _silu(z_ref[...])
    for g in range(SSM_GROUPS):
        seg = gated[:, g * gw:(g + 1) * gw]
        o_ref[:, g * gw:(g + 1) * gw] = _rms(seg, nw_ref[:, g * gw:(g + 1) * gw]).astype(BF16)


def _ssd_step_post(y_t, xs, proj, dx, nw):
    m, d_inner = xs.shape
    return pl.pallas_call(
        _ssd_step_post_body,
        grid=(1,),
        in_specs=[_full(y_t.shape), _full(xs.shape), pl.BlockSpec((m, d_inner), lambda i: (0, 0)),
                  _full(dx.shape), _full(nw.shape)],
        out_specs=_full((m, d_inner)),
        out_shape=jax.ShapeDtypeStruct((m, d_inner), BF16),
        compiler_params=_cparams(("arbitrary",)),
        name="ssd_step_post",
    )(y_t, xs, proj, dx, nw)


def _band_attn_body(q_ref, kc_ref, kp_ref, vc_ref, vp_ref, o_ref, l_ref, *, win, dil):
    blk = pl.program_id(1)
    n_hp, bt, _ = q_ref.shape
    span = win // dil
    heads = LANE // ATT_HEAD_DIM
    qi = lax.broadcasted_iota(jnp.int32, (span, 2 * span), 0)
    ki = lax.broadcasted_iota(jnp.int32, (span, 2 * span), 1)
    band = (ki >= qi) & (ki <= qi + span)
    band_first = band & ((blk > 0) | (ki >= span))
    lane_q = lax.broadcasted_iota(jnp.int32, (span, LANE), 1)
    lane_k = lax.broadcasted_iota(jnp.int32, (2 * span, LANE), 1)

    def rows(ref, hp, start, r):
        if dil == 1:
            return ref[hp, start:start + span, :]
        return ref[hp, pl.ds(start + r, span, stride=dil), :]

    for hp in range(n_hp):
        for wi in range(bt // win):
            for r in range(dil):
                q = (rows(q_ref, hp, wi * win, r) * (ATT_HEAD_DIM ** -0.5)).astype(BF16)
                if wi == 0:
                    k_prev = rows(kp_ref, hp, kp_ref.shape[1] - win, r)
                    v_prev = rows(vp_ref, hp, vp_ref.shape[1] - win, r)
                else:
                    k_prev = rows(kc_ref, hp, (wi - 1) * win, r)
                    v_prev = rows(vc_ref, hp, (wi - 1) * win, r)
                kk = jnp.concatenate([k_prev, rows(kc_ref, hp, wi * win, r)], axis=0).astype(BF16)
                vv = jnp.concatenate([v_prev, rows(vc_ref, hp, wi * win, r)], axis=0).astype(BF16)
                valid = band_first if wi == 0 else band
                o_acc, l_acc = None, None
                for h in range(heads):
                    in_q = (lane_q >= h * ATT_HEAD_DIM) & (lane_q < (h + 1) * ATT_HEAD_DIM)
                    in_k = (lane_k >= h * ATT_HEAD_DIM) & (lane_k < (h + 1) * ATT_HEAD_DIM)
                    s = jnp.where(valid, _dot_nt(jnp.where(in_q, q, jnp.zeros_like(q)), kk), -jnp.inf)
                    mx = jnp.max(s, axis=-1, keepdims=True)
                    p = jnp.exp(s - mx)
                    den = jnp.sum(p, axis=-1, keepdims=True)
                    o_h = _dot(p.astype(BF16), jnp.where(in_k, vv, jnp.zeros_like(vv))) / den
                    l_h = jnp.where(in_q, mx + jnp.log(den), 0.0)
                    o_acc = o_h if o_acc is None else o_acc + o_h
                    l_acc = l_h if l_acc is None else l_acc + l_h
                if dil == 1:
                    o_ref[hp, wi * win:wi * win + span, :] = o_acc
                    l_ref[hp, wi * win:wi * win + span, :] = l_acc
                else:
                    o_ref[hp, pl.ds(wi * win + r, span, stride=dil), :] = o_acc
                    l_ref[hp, pl.ds(wi * win + r, span, stride=dil), :] = l_acc


def _band_attn(qkv_c, n_seq, gi, win, dil):
    nch, m, _ = qkv_c.shape
    t_len = m // n_seq
    bt = max(win, min(BAND_BLOCK, t_len))
    assert bt % win == 0 and t_len % bt == 0
    nblk = t_len // bt
    wpb = bt // win
    per_grp = ATT_GW // LANE
    ng = len(ATT_GROUPS)

    def cur(sec):
        return pl.BlockSpec((per_grp, bt, LANE), lambda b, k: (sec * ng + gi, b * nblk + k, 0))

    def prev(sec):
        return pl.BlockSpec((per_grp, win, LANE),
                            lambda b, k: (sec * ng + gi, jnp.maximum((b * nblk + k) * wpb - 1, 0), 0))

    out_spec = pl.BlockSpec((per_grp, bt, LANE), lambda b, k: (0, b * nblk + k, 0))
    out_sd = jax.ShapeDtypeStruct((per_grp, m, LANE), F32)
    return pl.pallas_call(
        functools.partial(_band_attn_body, win=win, dil=dil),
        grid=(n_seq, nblk),
        in_specs=[cur(0), cur(1), prev(1), cur(2), prev(2)],
        out_specs=[out_spec, out_spec],
        out_shape=[out_sd, out_sd],
        compiler_params=_cparams(("parallel", "arbitrary")),
        name="band_attn_w%d" % win,
    )(qkv_c, qkv_c, qkv_c, qkv_c, qkv_c)


def _row_to_cols(v):
    return jnp.concatenate([jnp.broadcast_to(v[:, c * LANE:(c + 1) * LANE], (LANE, LANE)).T
                            for c in range(v.shape[1] // LANE)], axis=0)


def _col_to_row(v):
    return jnp.concatenate([jnp.broadcast_to(v[c * LANE:(c + 1) * LANE, :], (LANE, LANE)).T[0:1, :]
                            for c in range(v.shape[0] // LANE)], axis=1)


def _attn_sample_body(*refs, dils, n_prev):
    ng = len(dils)
    qkv_ref = refs[0]
    bufs = refs[1:1 + ng]
    outs = refs[1 + ng + n_prev:]
    o_refs, l_refs, c_refs = outs[0:ng], outs[ng:2 * ng], outs[2 * ng:3 * ng]
    scale = ATT_HEAD_DIM ** -0.5
    for gi in range(ng):
        buf, o_ref, l_ref, c_ref = bufs[gi], o_refs[gi], l_refs[gi], c_refs[gi]
        wb = buf.shape[2]
        nch = wb // LANE
        q_c = _row_to_cols(qkv_ref[:, gi * ATT_GW:(gi + 1) * ATT_GW] * scale)
        kn_c = _row_to_cols(qkv_ref[:, (ng + gi) * ATT_GW:(ng + gi + 1) * ATT_GW])
        vn_c = _row_to_cols(qkv_ref[:, (2 * ng + gi) * ATT_GW:(2 * ng + gi + 1) * ATT_GW])
        lane = lax.broadcasted_iota(jnp.int32, (1, wb), 1)
        valid = (lane & (dils[gi] - 1)) == 0
        o_cols = []
        for h in range(ATT_HPG):
            rs = slice(h * ATT_HEAD_DIM, (h + 1) * ATT_HEAD_DIM)
            qh = q_c[rs, :]
            s = jnp.concatenate([jnp.sum(buf[0, rs, c * LANE:(c + 1) * LANE] * qh, axis=0, keepdims=True)
                                 for c in range(nch)], axis=1)
            s = jnp.where(valid, s, -jnp.inf)
            sn = jnp.sum(qh[:, 0:1] * kn_c[rs, 0:1], axis=0, keepdims=True)
            mx = jnp.maximum(jnp.max(s, axis=1, keepdims=True), sn)
            p = jnp.exp(s - mx)
            p_new = jnp.exp(sn - mx)
            den = jnp.sum(p, axis=1, keepdims=True) + p_new
            acc = buf[1, rs, 0:LANE] * p[:, 0:LANE]
            for c in range(1, nch):
                acc = acc + buf[1, rs, c * LANE:(c + 1) * LANE] * p[:, c * LANE:(c + 1) * LANE]
            o_cols.append((jnp.sum(acc, axis=1, keepdims=True) + p_new * vn_c[rs, 0:1]) / den)
            l_ref[:, rs] = jnp.broadcast_to(mx + jnp.log(den), (1, ATT_HEAD_DIM))
        o_ref[...] = _col_to_row(jnp.concatenate(o_cols, axis=0))
        last = lax.broadcasted_iota(jnp.int32, (ATT_HEAD_DIM, LANE), 1) == LANE - 1
        for kv, new_c in ((0, kn_c), (1, vn_c)):
            for h in range(ATT_HPG):
                rs = slice(h * ATT_HEAD_DIM, (h + 1) * ATT_HEAD_DIM)
                rolled = pltpu.roll(buf[kv, rs, :], wb - 1, 1)
                if nch > 1:
                    c_ref[kv, rs, 0:wb - LANE] = rolled[:, 0:wb - LANE]
                c_ref[kv, rs, wb - LANE:wb] = jnp.where(last, new_c[rs, :], rolled[:, wb - LANE:wb])


def _attn_sample(qkv, cache_views, prev, layer):
    m, w3 = qkv.shape
    ng = len(ATT_GROUPS)
    dils = tuple(dil for _, dil in ATT_GROUPS)
    for v, (win, dil) in zip(cache_views, ATT_GROUPS):
        assert v.shape[3] == win and win % LANE == 0 and dil & (dil - 1) == 0
    c_specs = [pl.BlockSpec((None, 2, ATT_GW, v.shape[3]), lambda b: (layer * m + b, 0, 0, 0)) for v in cache_views]
    o_sd = jax.ShapeDtypeStruct((m, 1, ATT_GW), F32)
    o_spec = pl.BlockSpec((None, 1, ATT_GW), lambda b: (b, 0, 0))
    n_in = 1 + ng
    res = pl.pallas_call(
        functools.partial(_attn_sample_body, dils=dils, n_prev=len(prev)),
        grid=(m,),
        in_specs=[pl.BlockSpec((None, 1, w3), lambda b: (b, 0, 0))] + c_specs
                 + [pl.BlockSpec(memory_space=pl.ANY)] * len(prev),
        out_specs=[o_spec] * (2 * ng) + c_specs,
        out_shape=[o_sd] * (2 * ng) + [jax.ShapeDtypeStruct(v.shape, v.dtype) for v in cache_views],
        input_output_aliases={n_in + k: 2 * ng + k for k in range(len(prev))},
        compiler_params=_cparams(("parallel",)),
        name="attn_sample",
    )(qkv.reshape(m, 1, w3), *cache_views, *prev)
    os_ = [r.reshape(m, ATT_GW) for r in res[0:ng]]
    ls_ = [r.reshape(m, ATT_GW) for r in res[ng:2 * ng]]
    return os_, ls_, list(res[2 * ng:])


def _prep_weights(norms, ssm_w_in, ssm_conv_b, ssm_dt_bias, ssm_a_log, ssm_d, ssm_norm_w, ssm_w_out, att_w_qkv,
                  att_w_o, mem_norm, xa_w_q, xa_w_kv, xa_w_o, ffn_w_gu, ffn_conv_b, ffn_w_down):
    n_ssm, d_model, in_dim = ssm_w_in.shape
    n_heads = ssm_dt_bias.shape[1]
    d_inner = n_heads * SSM_HEAD_DIM
    d_ff = ffn_w_down.shape[1]
    pad_heads = LANE - n_heads
    w = {}
    w['n_heads'], w['d_inner'] = n_heads, d_inner
    w['norms'] = norms[:, :, None, :]
    def per_layer(a, cols=slice(None)):
        return [a[i, :, cols].astype(BF16) for i in range(a.shape[0])]

    w['ssm_w_in'] = [jnp.pad(ssm_w_in[i], ((0, 0), (0, pad_heads))).astype(BF16) for i in range(n_ssm)]
    w['ssm_conv_b'] = ssm_conv_b[:, None, :]
    w['ssm_dt_bias'] = jnp.pad(ssm_dt_bias, ((0, 0), (0, pad_heads)))[:, None, :]
    w['ssm_a_log'] = jnp.pad(ssm_a_log, ((0, 0), (0, pad_heads)))[:, None, :]
    w['ssm_dx'] = jnp.repeat(ssm_d, SSM_HEAD_DIM, axis=1)[:, None, :]
    w['ssm_norm_w'] = ssm_norm_w[:, None, :]
    w['ssm_w_out'] = per_layer(ssm_w_out)
    w['att_w_qkv'] = per_layer(att_w_qkv)
    w['att_w_o'] = per_layer(att_w_o)
    w['mem_norm'] = mem_norm[:, None, :]
    w['xa_w_q'] = per_layer(xa_w_q)
    w['xa_w_kv'] = xa_w_kv.astype(BF16)
    w['xa_w_o'] = per_layer(xa_w_o)
    w['ffn_w_gu'] = ffn_w_gu.astype(BF16)
    w['ffn_conv_b'] = ffn_conv_b[:, None, :]
    w['ffn_w_down'] = ffn_w_down.astype(BF16)
    e = np.zeros((LANE, d_inner), np.float32)
    assert 3 * n_heads <= LANE
    for h in range(n_heads):
        for part in range(3):
            e[part * n_heads + h, h * SSM_HEAD_DIM:(h + 1) * SSM_HEAD_DIM] = 1.0
    w['expand'] = jnp.asarray(e, BF16)
    w['tril'] = jnp.asarray(np.tril(np.ones((SSM_CHUNK, SSM_CHUNK), np.float32)), BF16)
    return w


def _prompt_trunk(x3, mem3, w, ssm_conv_w, ffn_conv_w):
    n, t_len, d = x3.shape
    assert t_len % SSM_CHUNK == 0 and all(t_len % win == 0 for win, _ in ATT_GROUPS)
    depth = w['norms'].shape[0]
    n_heads, d_inner = w['n_heads'], w['d_inner']
    conv_dim = ssm_conv_w.shape[2]
    d_ff = ffn_conv_w.shape[2]
    tm = PROMPT_ROWS_SSD
    tm_big = PROMPT_ROWS
    x = x3.reshape(n * t_len, d)
    mem = mem3.reshape(-1, d)
    kv_all, kv_rows = _mem_kv(mem, w['mem_norm'], w['xa_w_kv'], tm)
    tabs = _rope_tables(jnp.arange(t_len, dtype=jnp.int32))
    zero_hist = jnp.zeros((n, SSM_CONV - 1, conv_dim), F32)
    zero_h = jnp.zeros((n, n_heads * SSM_HEAD_DIM, D_STATE), F32)
    zero_fh = jnp.zeros((n, FFN_CONV - 1, d_ff), F32)
    n_ssm, n_att = (depth + 1) // 2, depth // 2
    new = {'ssm': [], 'ssm_conv': [], 'swa': [], 'ffn_conv': []}
    for i in range(depth):
        j = i // 2
        g = w['norms'][i]
        if i % 2 == 0:
            x, h_last, c_last = _ssd_layer(x, g[0], g[1], w['ssm_w_in'][j], ssm_conv_w[j], w['ssm_conv_b'][j],
                                           w['ssm_dt_bias'][j], w['ssm_a_log'][j], w['ssm_dx'][j],
                                           w['ssm_norm_w'][j], w['expand'], w['tril'], w['ssm_w_out'][j],
                                           zero_hist, zero_h, new['ssm'], j, n_ssm, n, d_inner, n_heads, tm)
            new['ssm'] = [h_last]
            new['ssm_conv'].append(c_last)
        else:
            qkv_c, new['swa'] = _norm_qkv_prompt(x, g[0], w['att_w_qkv'][j], tabs, new['swa'], j, n_att, n, tm_big)
            os_, ls_ = [], []
            for gi, (win, dil) in enumerate(ATT_GROUPS):
                o, l = _band_attn(qkv_c, n, gi, win, dil)
                os_.append(o)
                ls_.append(l)
            x = _attn_out(os_, ls_, w['att_w_o'][j], x, g[1], tm_big)
        x = _xattn_prompt(x, g[2], g[3], w['xa_w_q'][i], kv_all, i, w['xa_w_o'][i], n, tm_big)
        x, f_hist = _ffn_prompt(x, g[4], g[5], w['ffn_w_gu'], i, ffn_conv_w[i],
                                w['ffn_conv_b'][i], w['ffn_w_down'], zero_fh, n, tm_big)
        new['ffn_conv'].append(f_hist)
    n_mem = mem3.shape[1]
    hd = d // MEM_HEADS
    swa = [jnp.transpose(c.reshape(n_att, n, 2, ATT_HPG, ATT_HEAD_DIM, c.shape[3]), (0, 1, 5, 2, 3, 4))
           for c in new['swa']]
    p_mem = jnp.transpose(kv_rows.reshape(depth, n, n_mem, 2, hd // LANE, MEM_HEADS, LANE),
                          (0, 1, 2, 3, 5, 4, 6)).reshape(depth, n, n_mem, 2, MEM_HEADS, hd)
    return (x.reshape(n, t_len, d), new['ssm'][0].reshape(n_ssm, n, n_heads, SSM_HEAD_DIM, D_STATE),
            jnp.stack(new['ssm_conv']), swa, p_mem, jnp.stack(new['ffn_conv']))


def _sample_trunk(x3, w, ssm_conv_w, ffn_conv_w, state_ssm, state_ssm_conv, caches, cache_mem_kv, state_ffn_conv):
    m, t_len, d = x3.shape
    assert t_len == 1
    depth = w['norms'].shape[0]
    n_heads, d_inner = w['n_heads'], w['d_inner']
    n_ssm = state_ssm.shape[0]
    x = x3.reshape(m, d)
    tabs = _rope_tables(jnp.full((m,), PAST_LEN, jnp.int32))
    h_all = state_ssm.reshape(n_ssm * m, n_heads, SSM_HEAD_DIM, D_STATE)
    n_mem, hd = cache_mem_kv.shape[2], d // MEM_HEADS
    kv_all = jnp.transpose(cache_mem_kv.reshape(depth, m, n_mem, 2, MEM_HEADS, hd // LANE, LANE),
                           (0, 1, 2, 3, 5, 4, 6)).reshape(depth * m, n_mem, 2, MEM_HEADS * (hd // LANE), LANE)
    cache_views = [jnp.transpose(c, (0, 1, 3, 4, 5, 2)).reshape(c.shape[0] * m, 2, ATT_GW, c.shape[2])
                   for c in caches]
    new_caches = []
    new = {'ssm': [], 'ssm_conv': [], 'ffn_conv': []}
    for i in range(depth):
        j = i // 2
        g = w['norms'][i]
        if i % 2 == 0:
            proj = _norm_proj(x, g[0], w['ssm_w_in'][j], m)
            hist_t = jnp.swapaxes(state_ssm_conv[j], 0, 1)
            xs, bm, cm, xdt_t, dec_t, c_last_t = _ssd_step_pre(proj, hist_t, ssm_conv_w[j], w['ssm_conv_b'][j],
                                                                w['ssm_dt_bias'][j], w['ssm_a_log'][j],
                                                                w['expand'], d_inner)
            h_new, y_t = _ssd_step_state(h_all, new['ssm'], j, xdt_t, dec_t, bm, cm, n_heads)
            y = _ssd_step_post(y_t, xs, proj, w['ssm_dx'][j], w['ssm_norm_w'][j])
            x = _proj_res(y, w['ssm_w_out'][j], x, g[1], m)
            new['ssm'] = [h_new]
            new['ssm_conv'].append(jnp.swapaxes(c_last_t, 0, 1))
        else:
            qkv = _norm_qkv_rope(x, g[0], w['att_w_qkv'][j], tabs, m, 1)
            os_, ls_, new_caches = _attn_sample(qkv, cache_views, new_caches, j)
            x = _attn_out(os_, ls_, w['att_w_o'][j], x, g[1], m)
        x = _xattn_sample(x, g[2], g[3], w['xa_w_q'][i], kv_all, i, w['xa_w_o'][i])
        hist_t = jnp.swapaxes(state_ffn_conv[i], 0, 1)
        x, f_hist_t = _ffn_sample(x, g[4], g[5], w['ffn_w_gu'], i, ffn_conv_w[i],
                                  w['ffn_conv_b'][i], w['ffn_w_down'], hist_t, 256)
        new['ffn_conv'].append(jnp.swapaxes(f_hist_t, 0, 1))
    new_caches = [jnp.transpose(nc.reshape(c.shape[0], m, 2, ATT_HPG, ATT_HEAD_DIM, c.shape[2]), (0, 1, 5, 2, 3, 4))
                  for nc, c in zip(new_caches, caches)]
    return (x.reshape(m, 1, d), new['ssm'][0].reshape(state_ssm.shape), jnp.stack(new['ssm_conv']), new_caches,
            jnp.stack(new['ffn_conv']))


def kernel(x_prompt, x_sample, mem_prompt, state_ssm, state_ssm_conv, cache_swa_kv_w128, cache_swa_kv_w512,
           cache_swa_kv_w2048, cache_mem_kv, state_ffn_conv, norms, ssm_w_in, ssm_conv_w, ssm_conv_b,
           ssm_dt_bias, ssm_a_log, ssm_d, ssm_norm_w, ssm_w_out, att_w_qkv, att_w_o, mem_norm, xa_w_q,
           xa_w_kv, xa_w_o, ffn_w_gu, ffn_conv_w, ffn_conv_b, ffn_w_down):
    w = _prep_weights(norms, ssm_w_in, ssm_conv_b, ssm_dt_bias, ssm_a_log, ssm_d, ssm_norm_w, ssm_w_out,
                      att_w_qkv, att_w_o, mem_norm, xa_w_q, xa_w_kv, xa_w_o, ffn_w_gu, ffn_conv_b, ffn_w_down)
    caches = [cache_swa_kv_w128, cache_swa_kv_w512, cache_swa_kv_w2048]
    yp, p_ssm, p_conv, p_swa, p_mem, p_ffn = _prompt_trunk(x_prompt, mem_prompt, w, ssm_conv_w, ffn_conv_w)
    ys, s_ssm, s_conv, s_swa, s_ffn = _sample_trunk(x_sample, w, ssm_conv_w, ffn_conv_w, state_ssm,
                                                    state_ssm_conv, caches, cache_mem_kv, state_ffn_conv)
    return (yp, ys, p_ssm, p_conv, p_swa[0], p_swa[1], p_swa[2], p_mem, p_ffn,
            s_ssm, s_conv, s_swa[0], s_swa[1], s_swa[2], s_ffn)
```

```python
import functools
import math

import numpy as np
import jax
import jax.numpy as jnp
from jax import lax
from jax.experimental import pallas as pl
from jax.experimental.pallas import tpu as pltpu

F32 = jnp.float32
BF16 = jnp.bfloat16

EPS = 1e-6
PAST_LEN = 8192
SSM_HEAD_DIM = 64
SSM_GROUPS = 4
D_STATE = 128
SSM_CONV = 4
SSM_CHUNK = 128
CONV_COLS = 1024
ATT_GROUPS = ((128, 1), (512, 4), (2048, 16))
ATT_HPG = 4
ATT_HEAD_DIM = 64
ATT_GW = ATT_HPG * ATT_HEAD_DIM
BAND_BLOCK = 512
ROT_DIM = ATT_HEAD_DIM // 4
ROPE_THETA = 500000.0
MEM_HEADS = 4
FFN_CONV = 3

LANE = 128
SUBLANE = 8
VMEM_LIMIT = 56 * 1024 * 1024
PROMPT_ROWS = 512
PROMPT_ROWS_SSD = 256


def _cparams(sem):
    return pltpu.CompilerParams(dimension_semantics=sem, vmem_limit_bytes=VMEM_LIMIT)


def _rms(x, g):
    return x * lax.rsqrt(jnp.mean(x * x, axis=-1, keepdims=True) + EPS) * g


def _silu(x):
    return x / (1.0 + jnp.exp(-x))


def _softplus(x):
    return jnp.maximum(x, 0.0) + jnp.log(1.0 + jnp.exp(-jnp.abs(x)))


def _dot(a, b):
    return jnp.dot(a, b, preferred_element_type=F32)


def _dot_nt(a, b):
    return lax.dot_general(a, b, (((1,), (1,)), ((), ())), preferred_element_type=F32)


def _split3(v):
    hi = v.astype(BF16)
    r = v - hi.astype(F32)
    mid = r.astype(BF16)
    lo = (r - mid.astype(F32)).astype(BF16)
    return hi, mid, lo


def _expand(v, e, n_heads):
    return _dot(_pack3(v, n_heads), e)


def _pack3(v, n_heads):
    hi = v.astype(BF16).astype(F32)
    r1 = v - hi
    mid = r1.astype(BF16).astype(F32)
    lo = r1 - mid
    lane = lax.broadcasted_iota(jnp.int32, v.shape, 1)
    packed = jnp.where(lane < n_heads, hi,
                       jnp.where(lane < 2 * n_heads, pltpu.roll(mid, n_heads, 1),
                                 jnp.where(lane < 3 * n_heads, pltpu.roll(lo, 2 * n_heads, 1), 0.0)))
    return packed.astype(BF16)


def _full(shape):
    return pl.BlockSpec(shape, lambda *_: (0,) * len(shape))


def _resident(shape):
    return pl.BlockSpec(shape, lambda *_: (0,) * len(shape), pipeline_mode=pl.Buffered(1))


def _norm_proj_body(x_ref, g_ref, w_ref, o_ref):
    u = _rms(x_ref[...], g_ref[...]).astype(BF16)
    o_ref[...] = _dot(u, w_ref[...])


def _norm_proj(x, g, w, tm):
    m, d = x.shape
    n = w.shape[1]
    return pl.pallas_call(
        _norm_proj_body,
        grid=(m // tm,),
        in_specs=[pl.BlockSpec((tm, d), lambda i: (i, 0)), _full((1, d)), _full((d, n))],
        out_specs=pl.BlockSpec((tm, n), lambda i: (i, 0)),
        out_shape=jax.ShapeDtypeStruct((m, n), F32),
        compiler_params=_cparams(("parallel",)),
        name="norm_proj",
    )(x, g, w)


def _norm_qkv_rope_body(x_ref, g_ref, w_ref, cos_ref, sa_ref, sb_ref, o_ref, *, n_rot):
    u = _rms(x_ref[...], g_ref[...]).astype(BF16)
    y = _dot(u, w_ref[...])
    cos, sa, sb = cos_ref[...], sa_ref[...], sb_ref[...]
    for c in range(n_rot // LANE):
        t = y[:, c * LANE:(c + 1) * LANE]
        o_ref[:, c * LANE:(c + 1) * LANE] = (t * cos + pltpu.roll(t, LANE - ROT_DIM // 2, 1) * sa
                                             + pltpu.roll(t, ROT_DIM // 2, 1) * sb)
    o_ref[:, n_rot:] = y[:, n_rot:]


def _norm_qkv_rope(x, g, w, tabs, tm, seq_blocks):
    m, d = x.shape
    n = w.shape[1]
    tab_spec = pl.BlockSpec((tm, LANE), lambda i: (i % seq_blocks, 0))
    return pl.pallas_call(
        functools.partial(_norm_qkv_rope_body, n_rot=2 * n // 3),
        grid=(m // tm,),
        in_specs=[pl.BlockSpec((tm, d), lambda i: (i, 0)), _full((1, d)), _full((d, n)),
                  tab_spec, tab_spec, tab_spec],
        out_specs=pl.BlockSpec((tm, n), lambda i: (i, 0)),
        out_shape=jax.ShapeDtypeStruct((m, n), F32),
        compiler_params=_cparams(("parallel",)),
        name="norm_qkv_rope",
    )(x, g, w, *tabs)


def _rope_chunk(t, cos, sa, sb):
    return t * cos + pltpu.roll(t, LANE - ROT_DIM // 2, 1) * sa + pltpu.roll(t, ROT_DIM // 2, 1) * sb


def _norm_qkv_prompt_body(*refs, n_rot, n_prev, nblks):
    x_ref, g_ref, w_ref, cos_ref, sa_ref, sb_ref = refs[:6]
    o_ref = refs[6 + n_prev]
    c_refs = refs[7 + n_prev:]
    ng = len(c_refs)
    t = pl.program_id(1)
    tb = pl.num_programs(1)
    tm = x_ref.shape[0]
    u = _rms(x_ref[...], g_ref[...]).astype(BF16)
    y = _dot(u, w_ref[...])
    cos, sa, sb = cos_ref[...], sa_ref[...], sb_ref[...]
    vals = []
    for c in range(y.shape[1] // LANE):
        v = y[:, c * LANE:(c + 1) * LANE]
        if c * LANE < n_rot:
            v = _rope_chunk(v, cos, sa, sb)
        o_ref[c] = v
        vals.append(v)
    per_sec = len(vals) // 3
    per_grp = ATT_GW // LANE
    for gi, c_ref in enumerate(c_refs):
        kw = c_ref.shape[2]
        nblk = nblks[gi]

        @pl.when(t >= tb - nblk)
        def _(gi=gi, c_ref=c_ref, kw=kw):
            for kv in range(2):
                for hp in range(per_grp):
                    v = vals[(1 + kv) * per_sec + gi * per_grp + hp]
                    c_ref[kv, hp * LANE:(hp + 1) * LANE, :] = v[tm - kw:, :].T


def _norm_qkv_prompt(x, g, w, tabs, prev, layer, n_layers, n_seq, tm):
    m, d = x.shape
    n = w.shape[1]
    t_len = m // n_seq
    tb = t_len // tm
    tab_spec = pl.BlockSpec((tm, LANE), lambda b, t: (t, 0))
    c_specs, c_shapes, nblks = [], [], []
    for gi, (win, _) in enumerate(ATT_GROUPS):
        keep = min(win, t_len)
        kw = min(keep, tm)
        nblk = keep // kw
        assert keep % kw == 0
        nblks.append(nblk)
        c_specs.append(pl.BlockSpec((None, 2, ATT_GW, kw),
                                    lambda b, t, nblk=nblk: (layer * n_seq + b, 0, 0, jnp.maximum(t - (tb - nblk), 0))))
        c_shapes.append(jax.ShapeDtypeStruct((n_layers * n_seq, 2, ATT_GW, keep), F32))
    res = pl.pallas_call(
        functools.partial(_norm_qkv_prompt_body, n_rot=2 * n // 3, n_prev=len(prev), nblks=tuple(nblks)),
        grid=(n_seq, tb),
        in_specs=[pl.BlockSpec((tm, d), lambda b, t: (b * tb + t, 0)), _full((1, d)), _resident((d, n)),
                  tab_spec, tab_spec, tab_spec] + [pl.BlockSpec(memory_space=pl.ANY)] * len(prev),
        out_specs=[pl.BlockSpec((n // LANE, tm, LANE), lambda b, t: (0, b * tb + t, 0))] + c_specs,
        out_shape=[jax.ShapeDtypeStruct((n // LANE, m, LANE), F32)] + c_shapes,
        input_output_aliases={6 + k: 1 + k for k in range(len(prev))},
        compiler_params=_cparams(("parallel", "arbitrary")),
        name="norm_qkv_prompt",
    )(x, g, w, *tabs, *prev)
    return res[0], list(res[1:])


def _rope_tables(pos):
    half = ROT_DIM // 2
    inv = ROPE_THETA ** (-jnp.arange(half, dtype=F32) / half)
    ang = pos.astype(F32)[:, None] * inv[None, :]
    cos, sin = jnp.cos(ang), jnp.sin(ang)
    p = pos.shape[0]
    rest = ATT_HEAD_DIM - ROT_DIM
    c = jnp.concatenate([cos, cos, jnp.ones((p, rest), F32)], axis=1)
    sa = jnp.concatenate([-sin, jnp.zeros((p, half + rest), F32)], axis=1)
    sb = jnp.concatenate([jnp.zeros((p, half), F32), sin, jnp.zeros((p, rest), F32)], axis=1)
    rep = LANE // ATT_HEAD_DIM
    return tuple(jnp.tile(t, (1, rep)) for t in (c, sa, sb))


def _proj_res_body(y_ref, w_ref, x_ref, g_ref, o_ref):
    f = _dot(y_ref[...].astype(BF16), w_ref[...])
    o_ref[...] = x_ref[...] + _rms(f, g_ref[...])


def _proj_res(y, w, x, g, tm):
    m, k = y.shape
    d = w.shape[1]
    return pl.pallas_call(
        _proj_res_body,
        grid=(m // tm,),
        in_specs=[pl.BlockSpec((tm, k), lambda i: (i, 0)), _full((k, d)),
                  pl.BlockSpec((tm, d), lambda i: (i, 0)), _full((1, d))],
        out_specs=pl.BlockSpec((tm, d), lambda i: (i, 0)),
        out_shape=jax.ShapeDtypeStruct((m, d), F32),
        compiler_params=_cparams(("parallel",)),
        name="proj_res",
    )(y, w, x, g)


def _attn_out_body(o0, o1, o2, l0, l1, l2, w_ref, x_ref, g_ref, out_ref):
    def load(ref):
        if len(ref.shape) == 2:
            return ref[...]
        return jnp.concatenate([ref[c] for c in range(ref.shape[0])], axis=1)

    ls = [load(l0), load(l1), load(l2)]
    mx = jnp.maximum(jnp.maximum(ls[0], ls[1]), ls[2])
    es = [jnp.exp(l - mx) for l in ls]
    den = es[0] + es[1] + es[2]
    og = jnp.concatenate([(es[gi] / den * load(o_ref)).astype(BF16) for gi, o_ref in enumerate((o0, o1, o2))], axis=1)
    out_ref[...] = x_ref[...] + _rms(_dot(og, w_ref[...]), g_ref[...])


def _attn_out(os_, ls_, w, x, g, tm):
    m, d = x.shape
    if os_[0].ndim == 2:
        blk = pl.BlockSpec((tm, ATT_GW), lambda i: (i, 0))
    else:
        blk = pl.BlockSpec((ATT_GW // LANE, tm, LANE), lambda i: (0, i, 0))
    return pl.pallas_call(
        _attn_out_body,
        grid=(m // tm,),
        in_specs=[blk] * 6 + [_full(w.shape), pl.BlockSpec((tm, d), lambda i: (i, 0)), _full((1, d))],
        out_specs=pl.BlockSpec((tm, d), lambda i: (i, 0)),
        out_shape=jax.ShapeDtypeStruct((m, d), F32),
        compiler_params=_cparams(("parallel",)),
        name="attn_out",
    )(*os_, *ls_, w, x, g)


def _mem_kv_body(x_ref, g_ref, w_ref, o_ref, t_ref):
    tm = x_ref.shape[0]
    u = _rms(x_ref[...], g_ref[...]).astype(BF16)
    y = _dot(u, w_ref[...])
    o_ref[...] = y
    hd = y.shape[1] // (2 * MEM_HEADS)
    nch = hd // LANE
    tok_rows = 2 * nch * MEM_HEADS
    for kv in range(2):
        for h in range(MEM_HEADS):
            for c in range(nch):
                col = (kv * MEM_HEADS + h) * hd + c * LANE
                t_ref[pl.ds((kv * nch + c) * MEM_HEADS + h, tm, stride=tok_rows), :] = y[:, col:col + LANE]


def _mem_kv(mem, g, w, tm):
    m, d = mem.shape
    depth, _, n = w.shape
    tok_rows = n // LANE
    return pl.pallas_call(
        _mem_kv_body,
        grid=(depth, m // tm),
        in_specs=[pl.BlockSpec((tm, d), lambda l, i: (i, 0)),
                  pl.BlockSpec((None, 1, d), lambda l, i: (l, 0, 0)),
                  pl.BlockSpec((None, d, n), lambda l, i: (l, 0, 0))],
        out_specs=[pl.BlockSpec((None, tm, n), lambda l, i: (l, i, 0)),
                   pl.BlockSpec((None, tm * tok_rows, LANE), lambda l, i: (l, i, 0))],
        out_shape=[jax.ShapeDtypeStruct((depth, m, n), F32),
                   jax.ShapeDtypeStruct((depth, m * tok_rows, LANE), F32)],
        compiler_params=_cparams(("parallel", "parallel")),
        name="mem_kv",
    )(mem, g, w)


def _xattn_prompt_body(x_ref, gpre_ref, gpost_ref, wq_ref, kv_ref, wo_ref, o_ref, obuf):
    x = x_ref[...]
    d = x.shape[1]
    hd = d // MEM_HEADS
    u = _rms(x, gpre_ref[...]).astype(BF16)
    q = _dot(u, wq_ref[...]).astype(BF16)
    scale = hd ** -0.5
    for h in range(MEM_HEADS):
        kh = kv_ref[:, h * hd:(h + 1) * hd].astype(BF16)
        vh = kv_ref[:, d + h * hd:d + (h + 1) * hd].astype(BF16)
        s = _dot_nt(q[:, h * hd:(h + 1) * hd], kh) * scale
        mx = jnp.max(s, axis=-1, keepdims=True)
        p = jnp.exp(s - mx)
        den = jnp.sum(p, axis=-1, keepdims=True)
        obuf[:, h * hd:(h + 1) * hd] = (_dot(p.astype(BF16), vh) / den).astype(BF16)
    f = _dot(obuf[...], wo_ref[...])
    o_ref[...] = x + _rms(f, gpost_ref[...])


def _xattn_prompt(x, gpre, gpost, wq, kv, layer, wo, n_seq, tm):
    m, d = x.shape
    tb = m // n_seq // tm
    n_mem = kv.shape[1] // n_seq
    return pl.pallas_call(
        _xattn_prompt_body,
        grid=(n_seq, tb),
        in_specs=[pl.BlockSpec((tm, d), lambda b, t: (b * tb + t, 0)), _full((1, d)), _full((1, d)),
                  _resident((d, d)), pl.BlockSpec((None, n_mem, 2 * d), lambda b, t: (layer, b, 0)),
                  _resident((d, d))],
        out_specs=pl.BlockSpec((tm, d), lambda b, t: (b * tb + t, 0)),
        out_shape=jax.ShapeDtypeStruct((m, d), F32),
        scratch_shapes=[pltpu.VMEM((tm, d), BF16)],
        compiler_params=_cparams(("parallel", "parallel")),
        name="xattn_prompt",
    )(x, gpre, gpost, wq, kv, wo)


def _xattn_sample_body(x_ref, gpre_ref, gpost_ref, wq_ref, kv_ref, wo_ref, o_ref, q_scr, o_scr):
    b = pl.program_id(0)
    d = x_ref.shape[1]
    hd = d // MEM_HEADS

    @pl.when(b == 0)
    def _():
        u = _rms(x_ref[...], gpre_ref[...]).astype(BF16)
        q_scr[...] = _dot(u, wq_ref[...]) * (hd ** -0.5)

    qb = q_scr[pl.ds(b, 1), :]
    nch = hd // LANE
    n_mem, _, rows, _ = kv_ref.shape
    row_id = lax.broadcasted_iota(jnp.int32, (rows, LANE), 0)
    q_tile = jnp.zeros((rows, LANE), F32)
    for c in range(nch):
        for h in range(MEM_HEADS):
            piece = qb[:, h * hd + c * LANE:h * hd + (c + 1) * LANE]
            q_tile = jnp.where(row_id == c * MEM_HEADS + h, piece, q_tile)
    prod = (kv_ref[:, 0] * q_tile).reshape(n_mem * rows, LANE)
    p_hi, p_lo = _hi_lo(prod)
    ones = jnp.ones((LANE, LANE), BF16)
    s = (_dot(p_hi, ones) + _dot(p_lo, ones)).reshape(n_mem, rows, LANE)
    part = s
    for c in range(1, nch):
        s = s + pltpu.roll(part, c * MEM_HEADS, 1)
    mx = jnp.max(s, axis=0, keepdims=True)
    p = jnp.exp(s - mx)
    den = jnp.sum(p, axis=0)
    o_tile = jnp.sum(p * kv_ref[:, 1], axis=0) / den
    o_scr[pl.ds(b, 1), :] = jnp.concatenate(
        [o_tile[c * MEM_HEADS + h:c * MEM_HEADS + h + 1, :] for h in range(MEM_HEADS) for c in range(nch)], axis=1)

    @pl.when(b == pl.num_programs(0) - 1)
    def _():
        f = _dot(o_scr[...].astype(BF16), wo_ref[...])
        o_ref[...] = x_ref[...] + _rms(f, gpost_ref[...])


def _xattn_sample(x, gpre, gpost, wq, kv_all, layer, wo):
    m, d = x.shape
    return pl.pallas_call(
        _xattn_sample_body,
        grid=(m,),
        in_specs=[_full((m, d)), _full((1, d)), _full((1, d)), _full((d, d)),
                  pl.BlockSpec((None,) + kv_all.shape[1:], lambda b: (layer * m + b, 0, 0, 0, 0)), _full((d, d))],
        out_specs=_full((m, d)),
        out_shape=jax.ShapeDtypeStruct((m, d), F32),
        scratch_shapes=[pltpu.VMEM((m, d), F32), pltpu.VMEM((m, d), F32)],
        compiler_params=_cparams(("arbitrary",)),
        name="xattn_sample",
    )(x, gpre, gpost, wq, kv_all, wo)


def _ffn_prompt_body(x_ref, gpre_ref, gpost_ref, wg_ref, wu_ref, cw_ref, cb_ref, wd_ref, hist_ref,
                     o_ref, nh_ref, gbuf):
    t = pl.program_id(1)
    tm = x_ref.shape[0]
    k = FFN_CONV - 1
    base = SUBLANE - k

    @pl.when(t == 0)
    def _():
        gbuf[...] = jnp.zeros(gbuf.shape, F32)
        gbuf[base:SUBLANE, :] = hist_ref[...]

    x = x_ref[...]
    u = _rms(x, gpre_ref[...]).astype(BF16)
    gate = _dot(u, wg_ref[...])
    up = _dot(u, wu_ref[...])
    full = jnp.concatenate([gbuf[...], gate], axis=0)
    gc = gate * cw_ref[k:k + 1, :] + cb_ref[...]
    for j in range(k):
        gc = gc + pltpu.roll(full, k - j, 0)[SUBLANE:, :] * cw_ref[j:j + 1, :]
    hmid = (_silu(gc) * up).astype(BF16)
    f = _dot(hmid, wd_ref[...])
    o_ref[...] = x + _rms(f, gpost_ref[...])
    gbuf[...] = gate[tm - SUBLANE:, :]
    nh_ref[...] = gbuf[base:SUBLANE, :]


def _ffn_prompt(x, gpre, gpost, w_gu, layer, cw, cb, w_down, hist, n_seq, tm):
    m, d = x.shape
    f = w_down.shape[1]
    tb = m // n_seq // tm
    k = FFN_CONV - 1
    one = pl.Buffered(1)
    return pl.pallas_call(
        _ffn_prompt_body,
        grid=(n_seq, tb),
        in_specs=[pl.BlockSpec((tm, d), lambda b, t: (b * tb + t, 0)), _full((1, d)), _full((1, d)),
                  pl.BlockSpec((None, d, f), lambda b, t: (layer, 0, 0), pipeline_mode=one),
                  pl.BlockSpec((None, d, f), lambda b, t: (layer, 0, 1), pipeline_mode=one),
                  _full((FFN_CONV, f)), _full((1, f)),
                  pl.BlockSpec((None, f, d), lambda b, t: (layer, 0, 0), pipeline_mode=one),
                  pl.BlockSpec((None, k, f), lambda b, t: (b, 0, 0))],
        out_specs=[pl.BlockSpec((tm, d), lambda b, t: (b * tb + t, 0)),
                   pl.BlockSpec((None, k, f), lambda b, t: (b, 0, 0))],
        out_shape=[jax.ShapeDtypeStruct((m, d), F32), jax.ShapeDtypeStruct((n_seq, k, f), F32)],
        scratch_shapes=[pltpu.VMEM((SUBLANE, f), F32)],
        compiler_params=_cparams(("parallel", "arbitrary")),
        name="ffn_prompt",
    )(x, gpre, gpost, w_gu, w_gu, cw, cb, w_down, hist)


def _ffn_sample_body(x_ref, gpre_ref, gpost_ref, wg_ref, wu_ref, cw_ref, cb_ref, wd_ref, hist_ref,
                     o_ref, nh_ref, u_scr, acc):
    j = pl.program_id(0)

    @pl.when(j == 0)
    def _():
        u_scr[...] = _rms(x_ref[...], gpre_ref[...]).astype(BF16)
        acc[...] = jnp.zeros_like(acc)

    u = u_scr[...]
    gate = _dot(u, wg_ref[...])
    up = _dot(u, wu_ref[...])
    gc = hist_ref[0] * cw_ref[0:1, :] + cb_ref[...]
    for k in range(1, FFN_CONV - 1):
        gc = gc + hist_ref[k] * cw_ref[k:k + 1, :]
        nh_ref[k - 1] = hist_ref[k]
    gc = gc + gate * cw_ref[FFN_CONV - 1:FFN_CONV, :]
    nh_ref[FFN_CONV - 2] = gate
    hmid = (_silu(gc) * up).astype(BF16)
    acc[...] += _dot(hmid, wd_ref[...])

    @pl.when(j == pl.num_programs(0) - 1)
    def _():
        o_ref[...] = x_ref[...] + _rms(acc[...], gpost_ref[...])


def _ffn_sample(x, gpre, gpost, w_gu, layer, cw, cb, w_down, hist_t, tn):
    m, d = x.shape
    f = w_down.shape[1]
    k = FFN_CONV - 1
    nb = f // tn
    return pl.pallas_call(
        _ffn_sample_body,
        grid=(nb,),
        in_specs=[_full((m, d)), _full((1, d)), _full((1, d)),
                  pl.BlockSpec((None, d, tn), lambda j: (layer, 0, j)),
                  pl.BlockSpec((None, d, tn), lambda j: (layer, 0, nb + j)),
                  pl.BlockSpec((FFN_CONV, tn), lambda j: (0, j)), pl.BlockSpec((1, tn), lambda j: (0, j)),
                  pl.BlockSpec((None, tn, d), lambda j: (layer, j, 0)),
                  pl.BlockSpec((k, m, tn), lambda j: (0, 0, j))],
        out_specs=[_full((m, d)), pl.BlockSpec((k, m, tn), lambda j: (0, 0, j))],
        out_shape=[jax.ShapeDtypeStruct((m, d), F32), jax.ShapeDtypeStruct((k, m, f), F32)],
        scratch_shapes=[pltpu.VMEM((m, d), BF16), pltpu.VMEM((m, d), F32)],
        compiler_params=_cparams(("arbitrary",)),
        name="ffn_sample",
    )(x, gpre, gpost, w_gu, w_gu, cw, cb, w_down, hist_t)


def _ssd_activations(u, w_ref, cw_ref, cb_ref, dtb_ref, act, cbuf, clast_ref, d_inner):
    tm = u.shape[0]
    k = SSM_CONV - 1
    base = SUBLANE - k
    conv_dim = cw_ref.shape[1]
    act[:, :d_inner] = _silu(_dot(u, w_ref[:, :d_inner]))
    act[:, d_inner + conv_dim:] = _softplus(_dot(u, w_ref[:, d_inner + conv_dim:]) + dtb_ref[...])
    for c0 in range(0, conv_dim, CONV_COLS):
        cs = slice(c0, c0 + CONV_COLS)
        cbuf[SUBLANE:SUBLANE + tm, cs] = _dot(u, w_ref[:, d_inner + c0:d_inner + c0 + CONV_COLS])
        full = cbuf[:, cs]
        xc = full[SUBLANE:, :] * cw_ref[k:k + 1, cs] + cb_ref[:, cs]
        for j in range(k):
            xc = xc + pltpu.roll(full, k - j, 0)[SUBLANE:, :] * cw_ref[j:j + 1, cs]
        act[:, d_inner + c0:d_inner + c0 + CONV_COLS] = _silu(xc)
    last = cbuf[tm + base:tm + SUBLANE, :]
    clast_ref[...] = last
    cbuf[base:SUBLANE, :] = last


def _ssd_chunk(act_ref, alog_ref, dx_ref, nw_ref, e_ref, tril_ref, state, y_ref, d_inner, n_heads):
    L = SSM_CHUNK
    gn = SSM_GROUPS * D_STATE
    conv_dim = d_inner + 2 * gn
    hpg = n_heads // SSM_GROUPS
    gw = d_inner // SSM_GROUPS
    xs = act_ref[:, d_inner:2 * d_inner]
    xs_b = xs.astype(BF16)
    e = e_ref[...]
    dt = act_ref[:, d_inner + conv_dim:]
    a = dt * (-jnp.exp(alog_ref[...]))
    tril = tril_ref[...]
    a_hi, a_mid, a_lo = _split3(a)
    acum = _dot(tril, a_hi) + _dot(tril, a_mid) + _dot(tril, a_lo)
    acum_t = acum.T
    dt_t = dt.T
    a_last = acum[L - 1:L, :]
    d_acc = _expand(jnp.exp(acum), e, n_heads)
    xde = (xs * _expand(dt * jnp.exp(a_last - acum), e, n_heads)).astype(BF16)
    row = lax.broadcasted_iota(jnp.int32, (L, L), 0)
    col = lax.broadcasted_iota(jnp.int32, (L, L), 1)
    causal = row >= col

    for g in range(SSM_GROUPS):
        b0 = 2 * d_inner + g * D_STATE
        gs = slice(g * gw, (g + 1) * gw)
        b_f = act_ref[:, b0:b0 + D_STATE]
        bg = b_f.astype(BF16)
        cg = act_ref[:, b0 + gn:b0 + gn + D_STATE].astype(BF16)
        cb = _dot_nt(cg, bg)
        y_inter = _dot(cg, state[:, gs].astype(BF16)) * d_acc[:, gs]
        tiles = []
        for t0 in range(g * gw, (g + 1) * gw, LANE):
            xt = xs_b[:, t0:t0 + LANE]
            lane = lax.broadcasted_iota(jnp.int32, xt.shape, 1)
            tile = None
            for k in range(LANE // SSM_HEAD_DIM):
                hd = t0 // SSM_HEAD_DIM + k
                seg = acum[:, hd:hd + 1] - acum_t[hd:hd + 1, :]
                w = cb * jnp.exp(jnp.where(causal, seg, -jnp.inf)) * dt_t[hd:hd + 1, :]
                mine = (lane >= k * SSM_HEAD_DIM) & (lane < (k + 1) * SSM_HEAD_DIM)
                part = _dot(w.astype(BF16), jnp.where(mine, xt, jnp.zeros_like(xt)))
                tile = part if tile is None else tile + part
            tiles.append(tile)
        y = jnp.concatenate(tiles, axis=1) + y_inter + xs[:, gs] * dx_ref[:, gs]
        y_ref[:, gs] = _rms(y * act_ref[:, gs], nw_ref[:, gs]).astype(BF16)
        state[:, gs] = state[:, gs] * d_acc[L - 1:L, gs] + _dot(b_f.T.astype(BF16), xde[:, gs])


N_SSD_IN = 15


def _ssd_layer_body(x_ref, gpre_ref, gpost_ref, w_ref, cw_ref, cb_ref, dtb_ref, alog_ref, dx_ref, nw_ref,
                    e_ref, tril_ref, wo_ref, hist_ref, h0_ref, *rest, d_inner, n_heads):
    o_ref, hlast_ref, clast_ref, act, cbuf, state, ybf = rest[-7:]
    t = pl.program_id(1)
    tm = x_ref.shape[0]
    L = SSM_CHUNK
    base = SUBLANE - (SSM_CONV - 1)

    @pl.when(t == 0)
    def _():
        cbuf[0:SUBLANE, :] = jnp.zeros((SUBLANE, cbuf.shape[1]), F32)
        cbuf[base:SUBLANE, :] = hist_ref[...]
        state[...] = h0_ref[...].T

    x = x_ref[...]
    u = _rms(x, gpre_ref[...]).astype(BF16)
    _ssd_activations(u, w_ref, cw_ref, cb_ref, dtb_ref, act, cbuf, clast_ref, d_inner)
    for ci in range(tm // L):
        _ssd_chunk(act.at[ci * L:(ci + 1) * L, :], alog_ref, dx_ref, nw_ref, e_ref, tril_ref, state,
                   ybf.at[ci * L:(ci + 1) * L, :], d_inner, n_heads)
    o_ref[...] = x + _rms(_dot(ybf[...], wo_ref[...]), gpost_ref[...])

    @pl.when(t == pl.num_programs(1) - 1)
    def _():
        hlast_ref[...] = state[...].T


def _ssd_layer(x, gpre, gpost, w, cw, cb, dtb, alog, dx, nw, e, tril, wo, hist, h0, prev, layer, n_layers,
               n_seq, d_inner, n_heads, tm):
    m, d = x.shape
    n = w.shape[1]
    tb = m // n_seq // tm
    conv_dim = cw.shape[1]
    k = SSM_CONV - 1
    rows = n_heads * SSM_HEAD_DIM
    assert conv_dim % CONV_COLS == 0 and tm % SSM_CHUNK == 0
    ins = (x, gpre, gpost, w, cw, cb, dtb, alog, dx, nw, e, tril, wo, hist, h0)
    assert len(ins) == N_SSD_IN
    x_spec = pl.BlockSpec((tm, d), lambda b, t: (b * tb + t, 0))
    return pl.pallas_call(
        functools.partial(_ssd_layer_body, d_inner=d_inner, n_heads=n_heads),
        grid=(n_seq, tb),
        in_specs=[x_spec, _full((1, d)), _full((1, d)), _resident(w.shape), _full(cw.shape), _full(cb.shape),
                  _full(dtb.shape), _full(alog.shape), _full(dx.shape), _full(nw.shape), _full(e.shape),
                  _full(tril.shape), _resident(wo.shape),
                  pl.BlockSpec((None, k, conv_dim), lambda b, t: (b, 0, 0)),
                  pl.BlockSpec((None, rows, D_STATE), lambda b, t: (b, 0, 0))]
                 + [pl.BlockSpec(memory_space=pl.ANY)] * len(prev),
        out_specs=[x_spec,
                   pl.BlockSpec((None, rows, D_STATE), lambda b, t: (layer * n_seq + b, 0, 0)),
                   pl.BlockSpec((None, k, conv_dim), lambda b, t: (b, 0, 0))],
        out_shape=[jax.ShapeDtypeStruct((m, d), F32),
                   jax.ShapeDtypeStruct((n_layers * n_seq, rows, D_STATE), F32),
                   jax.ShapeDtypeStruct((n_seq, k, conv_dim), F32)],
        input_output_aliases={N_SSD_IN: 1} if prev else {},
        scratch_shapes=[pltpu.VMEM((tm, n), F32), pltpu.VMEM((tm + SUBLANE, conv_dim), F32),
                        pltpu.VMEM((D_STATE, rows), F32),
                        pltpu.VMEM((tm, d_inner), BF16)],
        compiler_params=_cparams(("parallel", "arbitrary")),
        name="ssd_layer",
    )(*ins, *prev)


def _ssd_step_pre_body(proj_ref, hist_ref, cw_ref, cb_ref, dtb_ref, alog_ref, e_ref,
                       xs_ref, b_ref, c_ref, xdt_t_ref, dec_b_ref, clast_ref, *, d_inner):
    gn = SSM_GROUPS * D_STATE
    conv_dim = d_inner + 2 * gn
    n_heads = d_inner // SSM_HEAD_DIM
    m = proj_ref.shape[0]
    xbc = proj_ref[:, d_inner:d_inner + conv_dim]
    xc = hist_ref[0] * cw_ref[0:1, :] + cb_ref[...]
    for j in range(1, SSM_CONV - 1):
        xc = xc + hist_ref[j] * cw_ref[j:j + 1, :]
        clast_ref[j - 1] = hist_ref[j]
    xc = _silu(xc + xbc * cw_ref[SSM_CONV - 1:SSM_CONV, :])
    clast_ref[SSM_CONV - 2] = xbc
    xs = xc[:, :d_inner]
    xs_ref[...] = xs
    b_ref[...] = xc[:, d_inner:d_inner + gn]
    c_ref[...] = xc[:, d_inner + gn:]
    dt = _softplus(proj_ref[:, d_inner + conv_dim:] + dtb_ref[...])
    dec = jnp.exp(dt * (-jnp.exp(alog_ref[...])))
    pad = jnp.zeros((LANE - m, d_inner), F32)
    xdt_t_ref[...] = jnp.concatenate([xs * _expand(dt, e_ref[...], n_heads), pad], axis=0).T
    for j in range(n_heads):
        dec_b_ref[j] = jnp.broadcast_to(dec[:, j:j + 1], (m, LANE))


def _ssd_step_pre(proj, hist_t, cw, cb, dtb, alog, e, d_inner):
    m = proj.shape[0]
    gn = SSM_GROUPS * D_STATE
    conv_dim = cw.shape[1]
    k = SSM_CONV - 1
    n_heads = d_inner // SSM_HEAD_DIM
    args = (proj, hist_t, cw, cb, dtb, alog, e)
    return pl.pallas_call(
        functools.partial(_ssd_step_pre_body, d_inner=d_inner),
        grid=(1,),
        in_specs=[_full(a.shape) for a in args],
        out_specs=[_full((m, d_inner)), _full((m, gn)), _full((m, gn)), _full((d_inner, LANE)),
                   _full((n_heads, m, LANE)), _full((k, m, conv_dim))],
        out_shape=[jax.ShapeDtypeStruct((m, d_inner), F32), jax.ShapeDtypeStruct((m, gn), F32),
                   jax.ShapeDtypeStruct((m, gn), F32), jax.ShapeDtypeStruct((d_inner, LANE), F32),
                   jax.ShapeDtypeStruct((n_heads, m, LANE), F32), jax.ShapeDtypeStruct((k, m, conv_dim), F32)],
        compiler_params=_cparams(("arbitrary",)),
        name="ssd_step_pre",
    )(*args)


def _hi_lo(v):
    hi = v.astype(BF16)
    return hi, (v - hi.astype(F32)).astype(BF16)


def _ssd_step_state_body(h0_ref, xdt_t_ref, dec_b_ref, b_ref, c_ref, *rest):
    hn_ref, y_t_ref = rest[-2:]
    m = h0_ref.shape[0]
    wide = jnp.concatenate([b_ref[...]] * m, axis=1)
    own = (lax.broadcasted_iota(jnp.int32, wide.shape, 1) // D_STATE
           == lax.broadcasted_iota(jnp.int32, wide.shape, 0))
    bbd = jnp.concatenate([jnp.where(own, wide, 0.0), jnp.zeros((LANE - m, m * D_STATE), F32)], axis=0)
    b_hi, b_lo = _hi_lo(bbd)
    x_hi, x_lo = _hi_lo(xdt_t_ref[...])
    upd = _dot(jnp.concatenate([x_hi, x_lo, x_hi], axis=1), jnp.concatenate([b_hi, b_hi, b_lo], axis=0))
    prods = []
    for b in range(m):
        h = h0_ref[b] * dec_b_ref[b:b + 1, :] + upd[:, b * D_STATE:(b + 1) * D_STATE]
        hn_ref[b] = h
        prods.append(h * c_ref[b:b + 1, :])
    p_hi, p_lo = _hi_lo(jnp.concatenate(prods, axis=0))
    ones = jnp.ones((D_STATE, LANE), BF16)
    sums = _dot(p_hi, ones) + _dot(p_lo, ones)
    lane = lax.broadcasted_iota(jnp.int32, (SSM_HEAD_DIM, LANE), 1)
    acc = jnp.zeros((SSM_HEAD_DIM, LANE), F32)
    for b in range(m):
        acc = jnp.where(lane == b, sums[b * SSM_HEAD_DIM:(b + 1) * SSM_HEAD_DIM, :], acc)
    y_t_ref[...] = acc


def _ssd_step_state(h_all, prev, layer, xdt_t, dec_t, bm, cm, n_heads):
    m = bm.shape[0]
    hpg = n_heads // SSM_GROUPS
    h_spec = pl.BlockSpec((m, None, SSM_HEAD_DIM, D_STATE), lambda j: (layer, j, 0, 0))
    return pl.pallas_call(
        _ssd_step_state_body,
        grid=(n_heads,),
        in_specs=[h_spec,
                  pl.BlockSpec((SSM_HEAD_DIM, LANE), lambda j: (j, 0)),
                  pl.BlockSpec((None, m, LANE), lambda j: (j, 0, 0)),
                  pl.BlockSpec((m, D_STATE), lambda j: (0, j // hpg)),
                  pl.BlockSpec((m, D_STATE), lambda j: (0, j // hpg))]
                 + [pl.BlockSpec(memory_space=pl.ANY)] * len(prev),
        out_specs=[h_spec, pl.BlockSpec((SSM_HEAD_DIM, LANE), lambda j: (j, 0))],
        out_shape=[jax.ShapeDtypeStruct(h_all.shape, F32),
                   jax.ShapeDtypeStruct((n_heads * SSM_HEAD_DIM, LANE), F32)],
        input_output_aliases={5: 0} if prev else {},
        compiler_params=_cparams(("parallel",)),
        name="ssd_step_state",
    )(h_all, xdt_t, dec_t, bm, cm, *prev)


def _ssd_step_post_body(y_t_ref, xs_ref, z_ref, dx_ref, nw_ref, o_ref):
    m, d_inner = xs_ref.shape
    gw = d_inner // SSM_GROUPS
    y = y_t_ref[...].T[:m, :] + xs_ref[...] * dx_ref[...]
    gated = y * _silu(z_ref[...])
    for g in range(SSM_GROUPS):
        seg = gated[:, g * gw:(g + 1) * gw]
        o_ref[:, g * gw:(g + 1) * gw] = _rms(seg, nw_ref[:, g * gw:(g + 1) * gw]).astype(BF16)


def _ssd_step_post(y_t, xs, proj, dx, nw):
    m, d_inner = xs.shape
    return pl.pallas_call(
        _ssd_step_post_body,
        grid=(1,),
        in_specs=[_full(y_t.shape), _full(xs.shape), pl.BlockSpec((m, d_inner), lambda i: (0, 0)),
                  _full(dx.shape), _full(nw.shape)],
        out_specs=_full((m, d_inner)),
        out_shape=jax.ShapeDtypeStruct((m, d_inner), BF16),
        compiler_params=_cparams(("arbitrary",)),
        name="ssd_step_post",
    )(y_t, xs, proj, dx, nw)


def _band_attn_body(q_ref, kc_ref, kp_ref, vc_ref, vp_ref, o_ref, l_ref, *, win, dil):
    blk = pl.program_id(1)
    n_hp, bt, _ = q_ref.shape
    span = win // dil
    heads = LANE // ATT_HEAD_DIM
    qi = lax.broadcasted_iota(jnp.int32, (span, 2 * span), 0)
    ki = lax.broadcasted_iota(jnp.int32, (span, 2 * span), 1)
    band = (ki >= qi) & (ki <= qi + span)
    band_first = band & ((blk > 0) | (ki >= span))
    lane_q = lax.broadcasted_iota(jnp.int32, (span, LANE), 1)
    lane_k = lax.broadcasted_iota(jnp.int32, (2 * span, LANE), 1)

    def rows(ref, hp, start, r):
        if dil == 1:
            return ref[hp, start:start + span, :]
        return ref[hp, pl.ds(start + r, span, stride=dil), :]

    for hp in range(n_hp):
        for wi in range(bt // win):
            for r in range(dil):
                q = (rows(q_ref, hp, wi * win, r) * (ATT_HEAD_DIM ** -0.5)).astype(BF16)
                if wi == 0:
                    k_prev = rows(kp_ref, hp, kp_ref.shape[1] - win, r)
                    v_prev = rows(vp_ref, hp, vp_ref.shape[1] - win, r)
                else:
                    k_prev = rows(kc_ref, hp, (wi - 1) * win, r)
                    v_prev = rows(vc_ref, hp, (wi - 1) * win, r)
                kk = jnp.concatenate([k_prev, rows(kc_ref, hp, wi * win, r)], axis=0).astype(BF16)
                vv = jnp.concatenate([v_prev, rows(vc_ref, hp, wi * win, r)], axis=0).astype(BF16)
                valid = band_first if wi == 0 else band
                o_acc, l_acc = None, None
                for h in range(heads):
                    in_q = (lane_q >= h * ATT_HEAD_DIM) & (lane_q < (h + 1) * ATT_HEAD_DIM)
                    in_k = (lane_k >= h * ATT_HEAD_DIM) & (lane_k < (h + 1) * ATT_HEAD_DIM)
                    s = jnp.where(valid, _dot_nt(jnp.where(in_q, q, jnp.zeros_like(q)), kk), -jnp.inf)
                    mx = jnp.max(s, axis=-1, keepdims=True)
                    p = jnp.exp(s - mx)
                    den = jnp.sum(p, axis=-1, keepdims=True)
                    o_h = _dot(p.astype(BF16), jnp.where(in_k, vv, jnp.zeros_like(vv))) / den
                    l_h = jnp.where(in_q, mx + jnp.log(den), 0.0)
                    o_acc = o_h if o_acc is None else o_acc + o_h
                    l_acc = l_h if l_acc is None else l_acc + l_h
                if dil == 1:
                    o_ref[hp, wi * win:wi * win + span, :] = o_acc
                    l_ref[hp, wi * win:wi * win + span, :] = l_acc
                else:
                    o_ref[hp, pl.ds(wi * win + r, span, stride=dil), :] = o_acc
                    l_ref[hp, pl.ds(wi * win + r, span, stride=dil), :] = l_acc


def _band_attn(qkv_c, n_seq, gi, win, dil):
    nch, m, _ = qkv_c.shape
    t_len = m // n_seq
    bt = max(win, min(BAND_BLOCK, t_len))
    assert bt % win == 0 and t_len % bt == 0
    nblk = t_len // bt
    wpb = bt // win
    per_grp = ATT_GW // LANE
    ng = len(ATT_GROUPS)

    def cur(sec):
        return pl.BlockSpec((per_grp, bt, LANE), lambda b, k: (sec * ng + gi, b * nblk + k, 0))

    def prev(sec):
        return pl.BlockSpec((per_grp, win, LANE),
                            lambda b, k: (sec * ng + gi, jnp.maximum((b * nblk + k) * wpb - 1, 0), 0))

    out_spec = pl.BlockSpec((per_grp, bt, LANE), lambda b, k: (0, b * nblk + k, 0))
    out_sd = jax.ShapeDtypeStruct((per_grp, m, LANE), F32)
    return pl.pallas_call(
        functools.partial(_band_attn_body, win=win, dil=dil),
        grid=(n_seq, nblk),
        in_specs=[cur(0), cur(1), prev(1), cur(2), prev(2)],
        out_specs=[out_spec, out_spec],
        out_shape=[out_sd, out_sd],
        compiler_params=_cparams(("parallel", "arbitrary")),
        name="band_attn_w%d" % win,
    )(qkv_c, qkv_c, qkv_c, qkv_c, qkv_c)


def _row_to_cols(v):
    return jnp.concatenate([jnp.broadcast_to(v[:, c * LANE:(c + 1) * LANE], (LANE, LANE)).T
                            for c in range(v.shape[1] // LANE)], axis=0)


def _col_to_row(v):
    return jnp.concatenate([jnp.broadcast_to(v[c * LANE:(c + 1) * LANE, :], (LANE, LANE)).T[0:1, :]
                            for c in range(v.shape[0] // LANE)], axis=1)


def _attn_sample_body(*refs, dils, n_prev):
    ng = len(dils)
    qkv_ref = refs[0]
    bufs = refs[1:1 + ng]
    outs = refs[1 + ng + n_prev:]
    o_refs, l_refs, c_refs = outs[0:ng], outs[ng:2 * ng], outs[2 * ng:3 * ng]
    scale = ATT_HEAD_DIM ** -0.5
    for gi in range(ng):
        buf, o_ref, l_ref, c_ref = bufs[gi], o_refs[gi], l_refs[gi], c_refs[gi]
        wb = buf.shape[2]
        nch = wb // LANE
        q_c = _row_to_cols(qkv_ref[:, gi * ATT_GW:(gi + 1) * ATT_GW] * scale)
        kn_c = _row_to_cols(qkv_ref[:, (ng + gi) * ATT_GW:(ng + gi + 1) * ATT_GW])
        vn_c = _row_to_cols(qkv_ref[:, (2 * ng + gi) * ATT_GW:(2 * ng + gi + 1) * ATT_GW])
        lane = lax.broadcasted_iota(jnp.int32, (1, wb), 1)
        valid = (lane & (dils[gi] - 1)) == 0
        o_cols = []
        for h in range(ATT_HPG):
            rs = slice(h * ATT_HEAD_DIM, (h + 1) * ATT_HEAD_DIM)
            qh = q_c[rs, :]
            s = jnp.concatenate([jnp.sum(buf[0, rs, c * LANE:(c + 1) * LANE] * qh, axis=0, keepdims=True)
                                 for c in range(nch)], axis=1)
            s = jnp.where(valid, s, -jnp.inf)
            sn = jnp.sum(qh[:, 0:1] * kn_c[rs, 0:1], axis=0, keepdims=True)
            mx = jnp.maximum(jnp.max(s, axis=1, keepdims=True), sn)
            p = jnp.exp(s - mx)
            p_new = jnp.exp(sn - mx)
            den = jnp.sum(p, axis=1, keepdims=True) + p_new
            acc = buf[1, rs, 0:LANE] * p[:, 0:LANE]
            for c in range(1, nch):
                acc = acc + buf[1, rs, c * LANE:(c + 1) * LANE] * p[:, c * LANE:(c + 1) * LANE]
            o_cols.append((jnp.sum(acc, axis=1, keepdims=True) + p_new * vn_c[rs, 0:1]) / den)
            l_ref[:, rs] = jnp.broadcast_to(mx + jnp.log(den), (1, ATT_HEAD_DIM))
        o_ref[...] = _col_to_row(jnp.concatenate(o_cols, axis=0))
        last = lax.broadcasted_iota(jnp.int32, (ATT_HEAD_DIM, LANE), 1) == LANE - 1
        for kv, new_c in ((0, kn_c), (1, vn_c)):
            for h in range(ATT_HPG):
                rs = slice(h * ATT_HEAD_DIM, (h + 1) * ATT_HEAD_DIM)
                rolled = pltpu.roll(buf[kv, rs, :], wb - 1, 1)
                if nch > 1:
                    c_ref[kv, rs, 0:wb - LANE] = rolled[:, 0:wb - LANE]
                c_ref[kv, rs, wb - LANE:wb] = jnp.where(last, new_c[rs, :], rolled[:, wb - LANE:wb])


def _attn_sample(qkv, cache_views, prev, layer):
    m, w3 = qkv.shape
    ng = len(ATT_GROUPS)
    dils = tuple(dil for _, dil in ATT_GROUPS)
    for v, (win, dil) in zip(cache_views, ATT_GROUPS):
        assert v.shape[3] == win and win % LANE == 0 and dil & (dil - 1) == 0
    c_specs = [pl.BlockSpec((None, 2, ATT_GW, v.shape[3]), lambda b: (layer * m + b, 0, 0, 0)) for v in cache_views]
    o_sd = jax.ShapeDtypeStruct((m, 1, ATT_GW), F32)
    o_spec = pl.BlockSpec((None, 1, ATT_GW), lambda b: (b, 0, 0))
    n_in = 1 + ng
    res = pl.pallas_call(
        functools.partial(_attn_sample_body, dils=dils, n_prev=len(prev)),
        grid=(m,),
        in_specs=[pl.BlockSpec((None, 1, w3), lambda b: (b, 0, 0))] + c_specs
                 + [pl.BlockSpec(memory_space=pl.ANY)] * len(prev),
        out_specs=[o_spec] * (2 * ng) + c_specs,
        out_shape=[o_sd] * (2 * ng) + [jax.ShapeDtypeStruct(v.shape, v.dtype) for v in cache_views],
        input_output_aliases={n_in + k: 2 * ng + k for k in range(len(prev))},
        compiler_params=_cparams(("parallel",)),
        name="attn_sample",
    )(qkv.reshape(m, 1, w3), *cache_views, *prev)
    os_ = [r.reshape(m, ATT_GW) for r in res[0:ng]]
    ls_ = [r.reshape(m, ATT_GW) for r in res[ng:2 * ng]]
    return os_, ls_, list(res[2 * ng:])


def _prep_weights(norms, ssm_w_in, ssm_conv_b, ssm_dt_bias, ssm_a_log, ssm_d, ssm_norm_w, ssm_w_out, att_w_qkv,
                  att_w_o, mem_norm, xa_w_q, xa_w_kv, xa_w_o, ffn_w_gu, ffn_conv_b, ffn_w_down):
    n_ssm, d_model, in_dim = ssm_w_in.shape
    n_heads = ssm_dt_bias.shape[1]
    d_inner = n_heads * SSM_HEAD_DIM
    d_ff = ffn_w_down.shape[1]
    pad_heads = LANE - n_heads
    w = {}
    w['n_heads'], w['d_inner'] = n_heads, d_inner
    w['norms'] = norms[:, :, None, :]
    def per_layer(a, cols=slice(None)):
        return [a[i, :, cols].astype(BF16) for i in range(a.shape[0])]

    w['ssm_w_in'] = [jnp.pad(ssm_w_in[i], ((0, 0), (0, pad_heads))).astype(BF16) for i in range(n_ssm)]
    w['ssm_conv_b'] = ssm_conv_b[:, None, :]
    w['ssm_dt_bias'] = jnp.pad(ssm_dt_bias, ((0, 0), (0, pad_heads)))[:, None, :]
    w['ssm_a_log'] = jnp.pad(ssm_a_log, ((0, 0), (0, pad_heads)))[:, None, :]
    w['ssm_dx'] = jnp.repeat(ssm_d, SSM_HEAD_DIM, axis=1)[:, None, :]
    w['ssm_norm_w'] = ssm_norm_w[:, None, :]
    w['ssm_w_out'] = per_layer(ssm_w_out)
    w['att_w_qkv'] = per_layer(att_w_qkv)
    w['att_w_o'] = per_layer(att_w_o)
    w['mem_norm'] = mem_norm[:, None, :]
    w['xa_w_q'] = per_layer(xa_w_q)
    w['xa_w_kv'] = xa_w_kv.astype(BF16)
    w['xa_w_o'] = per_layer(xa_w_o)
    w['ffn_w_gu'] = ffn_w_gu.astype(BF16)
    w['ffn_conv_b'] = ffn_conv_b[:, None, :]
    w['ffn_w_down'] = ffn_w_down.astype(BF16)
    e = np.zeros((LANE, d_inner), np.float32)
    assert 3 * n_heads <= LANE
    for h in range(n_heads):
        for part in range(3):
            e[part * n_heads + h, h * SSM_HEAD_DIM:(h + 1) * SSM_HEAD_DIM] = 1.0
    w['expand'] = jnp.asarray(e, BF16)
    w['tril'] = jnp.asarray(np.tril(np.ones((SSM_CHUNK, SSM_CHUNK), np.float32)), BF16)
    return w


def _prompt_trunk(x3, mem3, w, ssm_conv_w, ffn_conv_w):
    n, t_len, d = x3.shape
    assert t_len % SSM_CHUNK == 0 and all(t_len % win == 0 for win, _ in ATT_GROUPS)
    depth = w['norms'].shape[0]
    n_heads, d_inner = w['n_heads'], w['d_inner']
    conv_dim = ssm_conv_w.shape[2]
    d_ff = ffn_conv_w.shape[2]
    tm = PROMPT_ROWS_SSD
    tm_big = PROMPT_ROWS
    x = x3.reshape(n * t_len, d)
    mem = mem3.reshape(-1, d)
    kv_all, kv_rows = _mem_kv(mem, w['mem_norm'], w['xa_w_kv'], tm)
    tabs = _rope_tables(jnp.arange(t_len, dtype=jnp.int32))
    zero_hist = jnp.zeros((n, SSM_CONV - 1, conv_dim), F32)
    zero_h = jnp.zeros((n, n_heads * SSM_HEAD_DIM, D_STATE), F32)
    zero_fh = jnp.zeros((n, FFN_CONV - 1, d_ff), F32)
    n_ssm, n_att = (depth + 1) // 2, depth // 2
    new = {'ssm': [], 'ssm_conv': [], 'swa': [], 'ffn_conv': []}
    for i in range(depth):
        j = i // 2
        g = w['norms'][i]
        if i % 2 == 0:
            x, h_last, c_last = _ssd_layer(x, g[0], g[1], w['ssm_w_in'][j], ssm_conv_w[j], w['ssm_conv_b'][j],
                                           w['ssm_dt_bias'][j], w['ssm_a_log'][j], w['ssm_dx'][j],
                                           w['ssm_norm_w'][j], w['expand'], w['tril'], w['ssm_w_out'][j],
                                           zero_hist, zero_h, new['ssm'], j, n_ssm, n, d_inner, n_heads, tm)
            new['ssm'] = [h_last]
            new['ssm_conv'].append(c_last)
        else:
            qkv_c, new['swa'] = _norm_qkv_prompt(x, g[0], w['att_w_qkv'][j], tabs, new['swa'], j, n_att, n, tm_big)
            os_, ls_ = [], []
            for gi, (win, dil) in enumerate(ATT_GROUPS):
                o, l = _band_attn(qkv_c, n, gi, win, dil)
                os_.append(o)
                ls_.append(l)
            x = _attn_out(os_, ls_, w['att_w_o'][j], x, g[1], tm_big)
        x = _xattn_prompt(x, g[2], g[3], w['xa_w_q'][i], kv_all, i, w['xa_w_o'][i], n, tm_big)
        x, f_hist = _ffn_prompt(x, g[4], g[5], w['ffn_w_gu'], i, ffn_conv_w[i],
                                w['ffn_conv_b'][i], w['ffn_w_down'], zero_fh, n, tm_big)
        new['ffn_conv'].append(f_hist)
    n_mem = mem3.shape[1]
    hd = d // MEM_HEADS
    swa = [jnp.transpose(c.reshape(n_att, n, 2, ATT_HPG, ATT_HEAD_DIM, c.shape[3]), (0, 1, 5, 2, 3, 4))
           for c in new['swa']]
    p_mem = jnp.transpose(kv_rows.reshape(depth, n, n_mem, 2, hd // LANE, MEM_HEADS, LANE),
                          (0, 1, 2, 3, 5, 4, 6)).reshape(depth, n, n_mem, 2, MEM_HEADS, hd)
    return (x.reshape(n, t_len, d), new['ssm'][0].reshape(n_ssm, n, n_heads, SSM_HEAD_DIM, D_STATE),
            jnp.stack(new['ssm_conv']), swa, p_mem, jnp.stack(new['ffn_conv']))


def _sample_trunk(x3, w, ssm_conv_w, ffn_conv_w, state_ssm, state_ssm_conv, caches, cache_mem_kv, state_ffn_conv):
    m, t_len, d = x3.shape
    assert t_len == 1
    depth = w['norms'].shape[0]
    n_heads, d_inner = w['n_heads'], w['d_inner']
    n_ssm = state_ssm.shape[0]
    x = x3.reshape(m, d)
    tabs = _rope_tables(jnp.full((m,), PAST_LEN, jnp.int32))
    h_all = state_ssm.reshape(n_ssm * m, n_heads, SSM_HEAD_DIM, D_STATE)
    n_mem, hd = cache_mem_kv.shape[2], d // MEM_HEADS
    kv_all = jnp.transpose(cache_mem_kv.reshape(depth, m, n_mem, 2, MEM_HEADS, hd // LANE, LANE),
                           (0, 1, 2, 3, 5, 4, 6)).reshape(depth * m, n_mem, 2, MEM_HEADS * (hd // LANE), LANE)
    cache_views = [jnp.transpose(c, (0, 1, 3, 4, 5, 2)).reshape(c.shape[0] * m, 2, ATT_GW, c.shape[2])
                   for c in caches]
    new_caches = []
    new = {'ssm': [], 'ssm_conv': [], 'ffn_conv': []}
    for i in range(depth):
        j = i // 2
        g = w['norms'][i]
        if i % 2 == 0:
            proj = _norm_proj(x, g[0], w['ssm_w_in'][j], m)
            hist_t = jnp.swapaxes(state_ssm_conv[j], 0, 1)
            xs, bm, cm, xdt_t, dec_t, c_last_t = _ssd_step_pre(proj, hist_t, ssm_conv_w[j], w['ssm_conv_b'][j],
                                                                w['ssm_dt_bias'][j], w['ssm_a_log'][j],
                                                                w['expand'], d_inner)
            h_new, y_t = _ssd_step_state(h_all, new['ssm'], j, xdt_t, dec_t, bm, cm, n_heads)
            y = _ssd_step_post(y_t, xs, proj, w['ssm_dx'][j], w['ssm_norm_w'][j])
            x = _proj_res(y, w['ssm_w_out'][j], x, g[1], m)
            new['ssm'] = [h_new]
            new['ssm_conv'].append(jnp.swapaxes(c_last_t, 0, 1))
        else:
            qkv = _norm_qkv_rope(x, g[0], w['att_w_qkv'][j], tabs, m, 1)
            os_, ls_, new_caches = _attn_sample(qkv, cache_views, new_caches, j)
            x = _attn_out(os_, ls_, w['att_w_o'][j], x, g[1], m)
        x = _xattn_sample(x, g[2], g[3], w['xa_w_q'][i], kv_all, i, w['xa_w_o'][i])
        hist_t = jnp.swapaxes(state_ffn_conv[i], 0, 1)
        x, f_hist_t = _ffn_sample(x, g[4], g[5], w['ffn_w_gu'], i, ffn_conv_w[i],
                                  w['ffn_conv_b'][i], w['ffn_w_down'], hist_t, 256)
        new['ffn_conv'].append(jnp.swapaxes(f_hist_t, 0, 1))
    new_caches = [jnp.transpose(nc.reshape(c.shape[0], m, 2, ATT_HPG, ATT_HEAD_DIM, c.shape[2]), (0, 1, 5, 2, 3, 4))
                  for nc, c in zip(new_caches, caches)]
    return (x.reshape(m, 1, d), new['ssm'][0].reshape(state_ssm.shape), jnp.stack(new['ssm_conv']), new_caches,
            jnp.stack(new['ffn_conv']))


def kernel(x_prompt, x_sample, mem_prompt, state_ssm, state_ssm_conv, cache_swa_kv_w128, cache_swa_kv_w512,
           cache_swa_kv_w2048, cache_mem_kv, state_ffn_conv, norms, ssm_w_in, ssm_conv_w, ssm_conv_b,
           ssm_dt_bias, ssm_a_log, ssm_d, ssm_norm_w, ssm_w_out, att_w_qkv, att_w_o, mem_norm, xa_w_q,
           xa_w_kv, xa_w_o, ffn_w_gu, ffn_conv_w, ffn_conv_b, ffn_w_down):
    w = _prep_weights(norms, ssm_w_in, ssm_conv_b, ssm_dt_bias, ssm_a_log, ssm_d, ssm_norm_w, ssm_w_out,
                      att_w_qkv, att_w_o, mem_norm, xa_w_q, xa_w_kv, xa_w_o, ffn_w_gu, ffn_conv_b, ffn_w_down)
    caches = [cache_swa_kv_w128, cache_swa_kv_w512, cache_swa_kv_w2048]
    yp, p_ssm, p_conv, p_swa, p_mem, p_ffn = _prompt_trunk(x_prompt, mem_prompt, w, ssm_conv_w, ffn_conv_w)
    ys, s_ssm, s_conv, s_swa, s_ffn = _sample_trunk(x_sample, w, ssm_conv_w, ffn_conv_w, state_ssm,
                                                    state_ssm_conv, caches, cache_mem_kv, state_ffn_conv)
    return (yp, ys, p_ssm, p_conv, p_swa[0], p_swa[1], p_swa[2], p_mem, p_ffn,
            s_ssm, s_conv, s_swa[0], s_swa[1], s_swa[2], s_ffn)
```

```python
import functools

import numpy as np
import jax
import jax.numpy as jnp
from jax import lax
from jax.experimental import pallas as pl
from jax.experimental.pallas import tpu as pltpu

F32 = jnp.float32
BF16 = jnp.bfloat16

EPS = 1e-6
PAST_LEN = 8192
SSM_HEAD_DIM = 64
SSM_GROUPS = 4
D_STATE = 128
SSM_CONV = 4
SSM_CHUNK = 128
CONV_COLS = 1024
ATT_GROUPS = ((128, 1), (512, 4), (2048, 16))
ATT_HPG = 4
ATT_HEAD_DIM = 64
ATT_GW = ATT_HPG * ATT_HEAD_DIM
BAND_BLOCK = 512
ROT_DIM = ATT_HEAD_DIM // 4
ROPE_THETA = 500000.0
MEM_HEADS = 4
FFN_CONV = 3

LANE = 128
SUBLANE = 8
VMEM_LIMIT = 56 * 1024 * 1024
PROMPT_ROWS = 512
PROMPT_ROWS_SSD = 256


def _cparams(sem):
    return pltpu.CompilerParams(dimension_semantics=sem, vmem_limit_bytes=VMEM_LIMIT)


def _rms(x, g):
    return x * lax.rsqrt(jnp.mean(x * x, axis=-1, keepdims=True) + EPS) * g


def _silu(x):
    return x / (1.0 + jnp.exp(-x))


def _softplus(x):
    return jnp.maximum(x, 0.0) + jnp.log(1.0 + jnp.exp(-jnp.abs(x)))


def _dot(a, b):
    return jnp.dot(a, b, preferred_element_type=F32)


def _dot_nt(a, b):
    return lax.dot_general(a, b, (((1,), (1,)), ((), ())), preferred_element_type=F32)


def _split3(v):
    hi = v.astype(BF16)
    r = v - hi.astype(F32)
    mid = r.astype(BF16)
    lo = (r - mid.astype(F32)).astype(BF16)
    return hi, mid, lo


def _hi_lo(v):
    hi = v.astype(BF16)
    return hi, (v - hi.astype(F32)).astype(BF16)


def _expand(v, e, n_heads):
    return _dot(_pack3(v, n_heads), e)


def _pack3(v, n_heads):
    hi = v.astype(BF16).astype(F32)
    r1 = v - hi
    mid = r1.astype(BF16).astype(F32)
    lo = r1 - mid
    lane = lax.broadcasted_iota(jnp.int32, v.shape, 1)
    packed = jnp.where(lane < n_heads, hi,
                       jnp.where(lane < 2 * n_heads, pltpu.roll(mid, n_heads, 1),
                                 jnp.where(lane < 3 * n_heads, pltpu.roll(lo, 2 * n_heads, 1), 0.0)))
    return packed.astype(BF16)


def _full(shape):
    return pl.BlockSpec(shape, lambda *_: (0,) * len(shape))


def _resident(shape):
    return pl.BlockSpec(shape, lambda *_: (0,) * len(shape), pipeline_mode=pl.Buffered(1))


def _norm_proj_body(x_ref, g_ref, w_ref, o_ref):
    u = _rms(x_ref[...], g_ref[...]).astype(BF16)
    o_ref[...] = _dot(u, w_ref[...])


def _norm_proj(x, g, w, tm):
    m, d = x.shape
    n = w.shape[1]
    return pl.pallas_call(
        _norm_proj_body,
        grid=(m // tm,),
        in_specs=[pl.BlockSpec((tm, d), lambda i: (i, 0)), _full((1, d)), _full((d, n))],
        out_specs=pl.BlockSpec((tm, n), lambda i: (i, 0)),
        out_shape=jax.ShapeDtypeStruct((m, n), F32),
        compiler_params=_cparams(("parallel",)),
        name="norm_proj",
    )(x, g, w)


def _norm_qkv_rope_body(x_ref, g_ref, w_ref, cos_ref, sa_ref, sb_ref, o_ref, *, n_rot):
    u = _rms(x_ref[...], g_ref[...]).astype(BF16)
    y = _dot(u, w_ref[...])
    cos, sa, sb = cos_ref[...], sa_ref[...], sb_ref[...]
    for c in range(n_rot // LANE):
        t = y[:, c * LANE:(c + 1) * LANE]
        o_ref[:, c * LANE:(c + 1) * LANE] = (t * cos + pltpu.roll(t, LANE - ROT_DIM // 2, 1) * sa
                                             + pltpu.roll(t, ROT_DIM // 2, 1) * sb)
    o_ref[:, n_rot:] = y[:, n_rot:]


def _norm_qkv_rope(x, g, w, tabs, tm, seq_blocks):
    m, d = x.shape
    n = w.shape[1]
    tab_spec = pl.BlockSpec((tm, LANE), lambda i: (i % seq_blocks, 0))
    return pl.pallas_call(
        functools.partial(_norm_qkv_rope_body, n_rot=2 * n // 3),
        grid=(m // tm,),
        in_specs=[pl.BlockSpec((tm, d), lambda i: (i, 0)), _full((1, d)), _full((d, n)),
                  tab_spec, tab_spec, tab_spec],
        out_specs=pl.BlockSpec((tm, n), lambda i: (i, 0)),
        out_shape=jax.ShapeDtypeStruct((m, n), F32),
        compiler_params=_cparams(("parallel",)),
        name="norm_qkv_rope",
    )(x, g, w, *tabs)


def _rope_chunk(t, cos, sa, sb):
    return t * cos + pltpu.roll(t, LANE - ROT_DIM // 2, 1) * sa + pltpu.roll(t, ROT_DIM // 2, 1) * sb


def _norm_qkv_prompt_body(*refs, n_rot, n_prev, nblks):
    x_ref, g_ref, w_ref, cos_ref, sa_ref, sb_ref = refs[:6]
    o_ref = refs[6 + n_prev]
    c_refs = refs[7 + n_prev:]
    ng = len(c_refs)
    t = pl.program_id(1)
    tb = pl.num_programs(1)
    tm = x_ref.shape[0]
    u = _rms(x_ref[...], g_ref[...]).astype(BF16)
    y = _dot(u, w_ref[...])
    cos, sa, sb = cos_ref[...], sa_ref[...], sb_ref[...]
    vals = []
    for c in range(y.shape[1] // LANE):
        v = y[:, c * LANE:(c + 1) * LANE]
        if c * LANE < n_rot:
            v = _rope_chunk(v, cos, sa, sb)
        o_ref[c] = v
        vals.append(v)
    per_sec = len(vals) // 3
    per_grp = ATT_GW // LANE
    for gi, c_ref in enumerate(c_refs):
        kw = c_ref.shape[2]
        nblk = nblks[gi]

        @pl.when(t >= tb - nblk)
        def _(gi=gi, c_ref=c_ref, kw=kw):
            for kv in range(2):
                for hp in range(per_grp):
                    v = vals[(1 + kv) * per_sec + gi * per_grp + hp]
                    c_ref[kv, hp * LANE:(hp + 1) * LANE, :] = v[tm - kw:, :].T


def _norm_qkv_prompt(x, g, w, tabs, prev, layer, n_layers, n_seq, tm):
    m, d = x.shape
    n = w.shape[1]
    t_len = m // n_seq
    tb = t_len // tm
    tab_spec = pl.BlockSpec((tm, LANE), lambda b, t: (t, 0))
    c_specs, c_shapes, nblks = [], [], []
    for gi, (win, _) in enumerate(ATT_GROUPS):
        keep = min(win, t_len)
        kw = min(keep, tm)
        nblk = keep // kw
        assert keep % kw == 0
        nblks.append(nblk)
        c_specs.append(pl.BlockSpec((None, 2, ATT_GW, kw),
                                    lambda b, t, nblk=nblk: (layer * n_seq + b, 0, 0, jnp.maximum(t - (tb - nblk), 0))))
        c_shapes.append(jax.ShapeDtypeStruct((n_layers * n_seq, 2, ATT_GW, keep), F32))
    res = pl.pallas_call(
        functools.partial(_norm_qkv_prompt_body, n_rot=2 * n // 3, n_prev=len(prev), nblks=tuple(nblks)),
        grid=(n_seq, tb),
        in_specs=[pl.BlockSpec((tm, d), lambda b, t: (b * tb + t, 0)), _full((1, d)), _resident((d, n)),
                  tab_spec, tab_spec, tab_spec] + [pl.BlockSpec(memory_space=pl.ANY)] * len(prev),
        out_specs=[pl.BlockSpec((n // LANE, tm, LANE), lambda b, t: (0, b * tb + t, 0))] + c_specs,
        out_shape=[jax.ShapeDtypeStruct((n // LANE, m, LANE), F32)] + c_shapes,
        input_output_aliases={6 + k: 1 + k for k in range(len(prev))},
        compiler_params=_cparams(("parallel", "arbitrary")),
        name="norm_qkv_prompt",
    )(x, g, w, *tabs, *prev)
    return res[0], list(res[1:])


def _rope_tables(pos):
    half = ROT_DIM // 2
    inv = ROPE_THETA ** (-jnp.arange(half, dtype=F32) / half)
    ang = pos.astype(F32)[:, None] * inv[None, :]
    cos, sin = jnp.cos(ang), jnp.sin(ang)
    p = pos.shape[0]
    rest = ATT_HEAD_DIM - ROT_DIM
    c = jnp.concatenate([cos, cos, jnp.ones((p, rest), F32)], axis=1)
    sa = jnp.concatenate([-sin, jnp.zeros((p, half + rest), F32)], axis=1)
    sb = jnp.concatenate([jnp.zeros((p, half), F32), sin, jnp.zeros((p, rest), F32)], axis=1)
    rep = LANE // ATT_HEAD_DIM
    return tuple(jnp.tile(t, (1, rep)) for t in (c, sa, sb))


def _proj_res_body(y_ref, w_ref, x_ref, g_ref, o_ref):
    f = _dot(y_ref[...].astype(BF16), w_ref[...])
    o_ref[...] = x_ref[...] + _rms(f, g_ref[...])


def _proj_res(y, w, x, g, tm):
    m, k = y.shape
    d = w.shape[1]
    return pl.pallas_call(
        _proj_res_body,
        grid=(m // tm,),
        in_specs=[pl.BlockSpec((tm, k), lambda i: (i, 0)), _full((k, d)),
                  pl.BlockSpec((tm, d), lambda i: (i, 0)), _full((1, d))],
        out_specs=pl.BlockSpec((tm, d), lambda i: (i, 0)),
        out_shape=jax.ShapeDtypeStruct((m, d), F32),
        compiler_params=_cparams(("parallel",)),
        name="proj_res",
    )(y, w, x, g)


def _attn_out_body(o0, o1, o2, l0, l1, l2, w_ref, x_ref, g_ref, out_ref):
    def load(ref):
        if len(ref.shape) == 2:
            return ref[...]
        return jnp.concatenate([ref[c] for c in range(ref.shape[0])], axis=1)

    ls = [load(l0), load(l1), load(l2)]
    mx = jnp.maximum(jnp.maximum(ls[0], ls[1]), ls[2])
    es = [jnp.exp(l - mx) for l in ls]
    den = es[0] + es[1] + es[2]
    og = jnp.concatenate([(es[gi] / den * load(o_ref)).astype(BF16) for gi, o_ref in enumerate((o0, o1, o2))], axis=1)
    out_ref[...] = x_ref[...] + _rms(_dot(og, w_ref[...]), g_ref[...])


def _attn_out(os_, ls_, w, x, g, tm):
    m, d = x.shape
    if os_[0].ndim == 2:
        blk = pl.BlockSpec((tm, ATT_GW), lambda i: (i, 0))
    else:
        blk = pl.BlockSpec((ATT_GW // LANE, tm, LANE), lambda i: (0, i, 0))
    return pl.pallas_call(
        _attn_out_body,
        grid=(m // tm,),
        in_specs=[blk] * 6 + [_full(w.shape), pl.BlockSpec((tm, d), lambda i: (i, 0)), _full((1, d))],
        out_specs=pl.BlockSpec((tm, d), lambda i: (i, 0)),
        out_shape=jax.ShapeDtypeStruct((m, d), F32),
        compiler_params=_cparams(("parallel",)),
        name="attn_out",
    )(*os_, *ls_, w, x, g)


def _mem_kv_body(x_ref, g_ref, w_ref, o_ref, t_ref):
    tm = x_ref.shape[0]
    u = _rms(x_ref[...], g_ref[...]).astype(BF16)
    y = _dot(u, w_ref[...])
    o_ref[...] = y
    hd = y.shape[1] // (2 * MEM_HEADS)
    nch = hd // LANE
    tok_rows = 2 * nch * MEM_HEADS
    for kv in range(2):
        for h in range(MEM_HEADS):
            for c in range(nch):
                col = (kv * MEM_HEADS + h) * hd + c * LANE
                t_ref[pl.ds((kv * nch + c) * MEM_HEADS + h, tm, stride=tok_rows), :] = y[:, col:col + LANE]


def _mem_kv(mem, g, w, tm):
    m, d = mem.shape
    depth, _, n = w.shape
    tok_rows = n // LANE
    return pl.pallas_call(
        _mem_kv_body,
        grid=(depth, m // tm),
        in_specs=[pl.BlockSpec((tm, d), lambda l, i: (i, 0)),
                  pl.BlockSpec((None, 1, d), lambda l, i: (l, 0, 0)),
                  pl.BlockSpec((None, d, n), lambda l, i: (l, 0, 0))],
        out_specs=[pl.BlockSpec((None, tm, n), lambda l, i: (l, i, 0)),
                   pl.BlockSpec((None, tm * tok_rows, LANE), lambda l, i: (l, i, 0))],
        out_shape=[jax.ShapeDtypeStruct((depth, m, n), F32),
                   jax.ShapeDtypeStruct((depth, m * tok_rows, LANE), F32)],
        compiler_params=_cparams(("parallel", "parallel")),
        name="mem_kv",
    )(mem, g, w)


def _xattn_prompt_body(x_ref, gpre_ref, gpost_ref, wq_ref, kv_ref, wo_ref, o_ref, obuf):
    x = x_ref[...]
    d = x.shape[1]
    hd = d // MEM_HEADS
    u = _rms(x, gpre_ref[...]).astype(BF16)
    q = _dot(u, wq_ref[...]).astype(BF16)
    scale = hd ** -0.5
    for h in range(MEM_HEADS):
        kh = kv_ref[:, h * hd:(h + 1) * hd].astype(BF16)
        vh = kv_ref[:, d + h * hd:d + (h + 1) * hd].astype(BF16)
        s = _dot_nt(q[:, h * hd:(h + 1) * hd], kh) * scale
        mx = jnp.max(s, axis=-1, keepdims=True)
        p = jnp.exp(s - mx)
        den = jnp.sum(p, axis=-1, keepdims=True)
        obuf[:, h * hd:(h + 1) * hd] = (_dot(p.astype(BF16), vh) / den).astype(BF16)
    f = _dot(obuf[...], wo_ref[...])
    o_ref[...] = x + _rms(f, gpost_ref[...])


def _xattn_prompt(x, gpre, gpost, wq, kv, layer, wo, n_seq, tm):
    m, d = x.shape
    tb = m // n_seq // tm
    n_mem = kv.shape[1] // n_seq
    return pl.pallas_call(
        _xattn_prompt_body,
        grid=(n_seq, tb),
        in_specs=[pl.BlockSpec((tm, d), lambda b, t: (b * tb + t, 0)), _full((1, d)), _full((1, d)),
                  _resident((d, d)), pl.BlockSpec((None, n_mem, 2 * d), lambda b, t: (layer, b, 0)),
                  _resident((d, d))],
        out_specs=pl.BlockSpec((tm, d), lambda b, t: (b * tb + t, 0)),
        out_shape=jax.ShapeDtypeStruct((m, d), F32),
        scratch_shapes=[pltpu.VMEM((tm, d), BF16)],
        compiler_params=_cparams(("parallel", "parallel")),
        name="xattn_prompt",
    )(x, gpre, gpost, wq, kv, wo)


def _xattn_sample_body(x_ref, gpre_ref, gpost_ref, wq_ref, kv_ref, wo_ref, o_ref, q_scr, o_scr):
    b = pl.program_id(0)
    d = x_ref.shape[1]
    hd = d // MEM_HEADS

    @pl.when(b == 0)
    def _():
        u = _rms(x_ref[...], gpre_ref[...]).astype(BF16)
        q_scr[...] = _dot(u, wq_ref[...]) * (hd ** -0.5)

    qb = q_scr[pl.ds(b, 1), :]
    nch = hd // LANE
    n_mem, _, rows, _ = kv_ref.shape
    row_id = lax.broadcasted_iota(jnp.int32, (rows, LANE), 0)
    q_tile = jnp.zeros((rows, LANE), F32)
    for c in range(nch):
        for h in range(MEM_HEADS):
            piece = qb[:, h * hd + c * LANE:h * hd + (c + 1) * LANE]
            q_tile = jnp.where(row_id == c * MEM_HEADS + h, piece, q_tile)
    prod = (kv_ref[:, 0] * q_tile).reshape(n_mem * rows, LANE)
    p_hi, p_lo = _hi_lo(prod)
    ones = jnp.ones((LANE, LANE), BF16)
    s = (_dot(p_hi, ones) + _dot(p_lo, ones)).reshape(n_mem, rows, LANE)
    part = s
    for c in range(1, nch):
        s = s + pltpu.roll(part, c * MEM_HEADS, 1)
    mx = jnp.max(s, axis=0, keepdims=True)
    p = jnp.exp(s - mx)
    den = jnp.sum(p, axis=0)
    o_tile = jnp.sum(p * kv_ref[:, 1], axis=0) / den
    o_scr[pl.ds(b, 1), :] = jnp.concatenate(
        [o_tile[c * MEM_HEADS + h:c * MEM_HEADS + h + 1, :] for h in range(MEM_HEADS) for c in range(nch)], axis=1)

    @pl.when(b == pl.num_programs(0) - 1)
    def _():
        f = _dot(o_scr[...].astype(BF16), wo_ref[...])
        o_ref[...] = x_ref[...] + _rms(f, gpost_ref[...])


def _xattn_sample(x, gpre, gpost, wq, kv_all, layer, wo):
    m, d = x.shape
    return pl.pallas_call(
        _xattn_sample_body,
        grid=(m,),
        in_specs=[_full((m, d)), _full((1, d)), _full((1, d)), _full((d, d)),
                  pl.BlockSpec((None,) + kv_all.shape[1:], lambda b: (layer * m + b, 0, 0, 0, 0)), _full((d, d))],
        out_specs=_full((m, d)),
        out_shape=jax.ShapeDtypeStruct((m, d), F32),
        scratch_shapes=[pltpu.VMEM((m, d), F32), pltpu.VMEM((m, d), F32)],
        compiler_params=_cparams(("arbitrary",)),
        name="xattn_sample",
    )(x, gpre, gpost, wq, kv_all, wo)


def _ffn_prompt_body(x_ref, gpre_ref, gpost_ref, wg_ref, wu_ref, cw_ref, cb_ref, wd_ref, hist_ref,
                     o_ref, nh_ref, gbuf):
    t = pl.program_id(1)
    tm = x_ref.shape[0]
    k = FFN_CONV - 1
    base = SUBLANE - k

    @pl.when(t == 0)
    def _():
        gbuf[...] = jnp.zeros(gbuf.shape, F32)
        gbuf[base:SUBLANE, :] = hist_ref[...]

    x = x_ref[...]
    u = _rms(x, gpre_ref[...]).astype(BF16)
    gate = _dot(u, wg_ref[...])
    up = _dot(u, wu_ref[...])
    full = jnp.concatenate([gbuf[...], gate], axis=0)
    gc = gate * cw_ref[k:k + 1, :] + cb_ref[...]
    for j in range(k):
        gc = gc + pltpu.roll(full, k - j, 0)[SUBLANE:, :] * cw_ref[j:j + 1, :]
    hmid = (_silu(gc) * up).astype(BF16)
    f = _dot(hmid, wd_ref[...])
    o_ref[...] = x + _rms(f, gpost_ref[...])
    gbuf[...] = gate[tm - SUBLANE:, :]
    nh_ref[...] = gbuf[base:SUBLANE, :]


def _ffn_prompt(x, gpre, gpost, w_gu, layer, cw, cb, w_down, hist, n_seq, tm):
    m, d = x.shape
    f = w_down.shape[1]
    tb = m // n_seq // tm
    k = FFN_CONV - 1
    one = pl.Buffered(1)
    return pl.pallas_call(
        _ffn_prompt_body,
        grid=(n_seq, tb),
        in_specs=[pl.BlockSpec((tm, d), lambda b, t: (b * tb + t, 0)), _full((1, d)), _full((1, d)),
                  pl.BlockSpec((None, d, f), lambda b, t: (layer, 0, 0), pipeline_mode=one),
                  pl.BlockSpec((None, d, f), lambda b, t: (layer, 0, 1), pipeline_mode=one),
                  _full((FFN_CONV, f)), _full((1, f)),
                  pl.BlockSpec((None, f, d), lambda b, t: (layer, 0, 0), pipeline_mode=one),
                  pl.BlockSpec((None, k, f), lambda b, t: (b, 0, 0))],
        out_specs=[pl.BlockSpec((tm, d), lambda b, t: (b * tb + t, 0)),
                   pl.BlockSpec((None, k, f), lambda b, t: (b, 0, 0))],
        out_shape=[jax.ShapeDtypeStruct((m, d), F32), jax.ShapeDtypeStruct((n_seq, k, f), F32)],
        scratch_shapes=[pltpu.VMEM((SUBLANE, f), F32)],
        compiler_params=_cparams(("parallel", "arbitrary")),
        name="ffn_prompt",
    )(x, gpre, gpost, w_gu, w_gu, cw, cb, w_down, hist)


def _ffn_sample_body(x_ref, gpre_ref, gpost_ref, wg_ref, wu_ref, cw_ref, cb_ref, wd_ref, hist_ref,
                     o_ref, nh_ref, u_scr, acc):
    j = pl.program_id(0)

    @pl.when(j == 0)
    def _():
        u_scr[...] = _rms(x_ref[...], gpre_ref[...]).astype(BF16)
        acc[...] = jnp.zeros_like(acc)

    u = u_scr[...]
    gate = _dot(u, wg_ref[...])
    up = _dot(u, wu_ref[...])
    gc = hist_ref[0] * cw_ref[0:1, :] + cb_ref[...]
    for k in range(1, FFN_CONV - 1):
        gc = gc + hist_ref[k] * cw_ref[k:k + 1, :]
        nh_ref[k - 1] = hist_ref[k]
    gc = gc + gate * cw_ref[FFN_CONV - 1:FFN_CONV, :]
    nh_ref[FFN_CONV - 2] = gate
    hmid = (_silu(gc) * up).astype(BF16)
    acc[...] += _dot(hmid, wd_ref[...])

    @pl.when(j == pl.num_programs(0) - 1)
    def _():
        o_ref[...] = x_ref[...] + _rms(acc[...], gpost_ref[...])


def _ffn_sample(x, gpre, gpost, w_gu, layer, cw, cb, w_down, hist_t, tn):
    m, d = x.shape
    f = w_down.shape[1]
    k = FFN_CONV - 1
    nb = f // tn
    return pl.pallas_call(
        _ffn_sample_body,
        grid=(nb,),
        in_specs=[_full((m, d)), _full((1, d)), _full((1, d)),
                  pl.BlockSpec((None, d, tn), lambda j: (layer, 0, j)),
                  pl.BlockSpec((None, d, tn), lambda j: (layer, 0, nb + j)),
                  pl.BlockSpec((FFN_CONV, tn), lambda j: (0, j)), pl.BlockSpec((1, tn), lambda j: (0, j)),
                  pl.BlockSpec((None, tn, d), lambda j: (layer, j, 0)),
                  pl.BlockSpec((k, m, tn), lambda j: (0, 0, j))],
        out_specs=[_full((m, d)), pl.BlockSpec((k, m, tn), lambda j: (0, 0, j))],
        out_shape=[jax.ShapeDtypeStruct((m, d), F32), jax.ShapeDtypeStruct((k, m, f), F32)],
        scratch_shapes=[pltpu.VMEM((m, d), BF16), pltpu.VMEM((m, d), F32)],
        compiler_params=_cparams(("arbitrary",)),
        name="ffn_sample",
    )(x, gpre, gpost, w_gu, w_gu, cw, cb, w_down, hist_t)


def _ssd_activations(u, w_ref, cw_ref, cb_ref, dtb_ref, act, cbuf, clast_ref, d_inner):
    tm = u.shape[0]
    k = SSM_CONV - 1
    base = SUBLANE - k
    conv_dim = cw_ref.shape[1]
    act[:, :d_inner] = _silu(_dot(u, w_ref[:, :d_inner]))
    act[:, d_inner + conv_dim:] = _softplus(_dot(u, w_ref[:, d_inner + conv_dim:]) + dtb_ref[...])
    for c0 in range(0, conv_dim, CONV_COLS):
        cs = slice(c0, c0 + CONV_COLS)
        cbuf[SUBLANE:SUBLANE + tm, cs] = _dot(u, w_ref[:, d_inner + c0:d_inner + c0 + CONV_COLS])
        full = cbuf[:, cs]
        xc = full[SUBLANE:, :] * cw_ref[k:k + 1, cs] + cb_ref[:, cs]
        for j in range(k):
            xc = xc + pltpu.roll(full, k - j, 0)[SUBLANE:, :] * cw_ref[j:j + 1, cs]
        act[:, d_inner + c0:d_inner + c0 + CONV_COLS] = _silu(xc)
    last = cbuf[tm + base:tm + SUBLANE, :]
    clast_ref[...] = last
    cbuf[base:SUBLANE, :] = last


def _ssd_chunk(act_ref, alog_ref, dx_ref, nw_ref, e_ref, tril_ref, state, y_ref, d_inner, n_heads):
    L = SSM_CHUNK
    gn = SSM_GROUPS * D_STATE
    conv_dim = d_inner + 2 * gn
    hpg = n_heads // SSM_GROUPS
    gw = d_inner // SSM_GROUPS
    xs = act_ref[:, d_inner:2 * d_inner]
    xs_b = xs.astype(BF16)
    e = e_ref[...]
    dt = act_ref[:, d_inner + conv_dim:]
    a = dt * (-jnp.exp(alog_ref[...]))
    tril = tril_ref[...]
    a_hi, a_mid, a_lo = _split3(a)
    acum = _dot(tril, a_hi) + _dot(tril, a_mid) + _dot(tril, a_lo)
    acum_t = acum.T
    dt_t = dt.T
    a_last = acum[L - 1:L, :]
    d_acc = _expand(jnp.exp(acum), e, n_heads)
    xde = (xs * _expand(dt * jnp.exp(a_last - acum), e, n_heads)).astype(BF16)
    row = lax.broadcasted_iota(jnp.int32, (L, L), 0)
    col = lax.broadcasted_iota(jnp.int32, (L, L), 1)
    causal = row >= col

    for g in range(SSM_GROUPS):
        b0 = 2 * d_inner + g * D_STATE
        gs = slice(g * gw, (g + 1) * gw)
        b_f = act_ref[:, b0:b0 + D_STATE]
        bg = b_f.astype(BF16)
        cg = act_ref[:, b0 + gn:b0 + gn + D_STATE].astype(BF16)
        cb = _dot_nt(cg, bg)
        y_inter = _dot(cg, state[:, gs].astype(BF16)) * d_acc[:, gs]
        tiles = []
        for t0 in range(g * gw, (g + 1) * gw, LANE):
            xt = xs_b[:, t0:t0 + LANE]
            lane = lax.broadcasted_iota(jnp.int32, xt.shape, 1)
            tile = None
            for k in range(LANE // SSM_HEAD_DIM):
                hd = t0 // SSM_HEAD_DIM + k
                seg = acum[:, hd:hd + 1] - acum_t[hd:hd + 1, :]
                w = cb * jnp.exp(jnp.where(causal, seg, -jnp.inf)) * dt_t[hd:hd + 1, :]
                mine = (lane >= k * SSM_HEAD_DIM) & (lane < (k + 1) * SSM_HEAD_DIM)
                part = _dot(w.astype(BF16), jnp.where(mine, xt, jnp.zeros_like(xt)))
                tile = part if tile is None else tile + part
            tiles.append(tile)
        y = jnp.concatenate(tiles, axis=1) + y_inter + xs[:, gs] * dx_ref[:, gs]
        y_ref[:, gs] = _rms(y * act_ref[:, gs], nw_ref[:, gs]).astype(BF16)
        state[:, gs] = state[:, gs] * d_acc[L - 1:L, gs] + _dot(b_f.T.astype(BF16), xde[:, gs])


N_SSD_IN = 15


def _ssd_layer_body(x_ref, gpre_ref, gpost_ref, w_ref, cw_ref, cb_ref, dtb_ref, alog_ref, dx_ref, nw_ref,
                    e_ref, tril_ref, wo_ref, hist_ref, h0_ref, *rest, d_inner, n_heads):
    o_ref, hlast_ref, clast_ref, act, cbuf, state, ybf = rest[-7:]
    t = pl.program_id(1)
    tm = x_ref.shape[0]
    L = SSM_CHUNK
    base = SUBLANE - (SSM_CONV - 1)

    @pl.when(t == 0)
    def _():
        cbuf[0:SUBLANE, :] = jnp.zeros((SUBLANE, cbuf.shape[1]), F32)
        cbuf[base:SUBLANE, :] = hist_ref[...]
        state[...] = h0_ref[...].T

    x = x_ref[...]
    u = _rms(x, gpre_ref[...]).astype(BF16)
    _ssd_activations(u, w_ref, cw_ref, cb_ref, dtb_ref, act, cbuf, clast_ref, d_inner)
    for ci in range(tm // L):
        _ssd_chunk(act.at[ci * L:(ci + 1) * L, :], alog_ref, dx_ref, nw_ref, e_ref, tril_ref, state,
                   ybf.at[ci * L:(ci + 1) * L, :], d_inner, n_heads)
    o_ref[...] = x + _rms(_dot(ybf[...], wo_ref[...]), gpost_ref[...])

    @pl.when(t == pl.num_programs(1) - 1)
    def _():
        hlast_ref[...] = state[...].T


def _ssd_layer(x, gpre, gpost, w, cw, cb, dtb, alog, dx, nw, e, tril, wo, hist, h0, prev, layer, n_layers,
               n_seq, d_inner, n_heads, tm):
    m, d = x.shape
    n = w.shape[1]
    tb = m // n_seq // tm
    conv_dim = cw.shape[1]
    k = SSM_CONV - 1
    rows = n_heads * SSM_HEAD_DIM
    assert conv_dim % CONV_COLS == 0 and tm % SSM_CHUNK == 0
    ins = (x, gpre, gpost, w, cw, cb, dtb, alog, dx, nw, e, tril, wo, hist, h0)
    assert len(ins) == N_SSD_IN
    x_spec = pl.BlockSpec((tm, d), lambda b, t: (b * tb + t, 0))
    return pl.pallas_call(
        functools.partial(_ssd_layer_body, d_inner=d_inner, n_heads=n_heads),
        grid=(n_seq, tb),
        in_specs=[x_spec, _full((1, d)), _full((1, d)), _resident(w.shape), _full(cw.shape), _full(cb.shape),
                  _full(dtb.shape), _full(alog.shape), _full(dx.shape), _full(nw.shape), _full(e.shape),
                  _full(tril.shape), _resident(wo.shape),
                  pl.BlockSpec((None, k, conv_dim), lambda b, t: (b, 0, 0)),
                  pl.BlockSpec((None, rows, D_STATE), lambda b, t: (b, 0, 0))]
                 + [pl.BlockSpec(memory_space=pl.ANY)] * len(prev),
        out_specs=[x_spec,
                   pl.BlockSpec((None, rows, D_STATE), lambda b, t: (layer * n_seq + b, 0, 0)),
                   pl.BlockSpec((None, k, conv_dim), lambda b, t: (b, 0, 0))],
        out_shape=[jax.ShapeDtypeStruct((m, d), F32),
                   jax.ShapeDtypeStruct((n_layers * n_seq, rows, D_STATE), F32),
                   jax.ShapeDtypeStruct((n_seq, k, conv_dim), F32)],
        input_output_aliases={N_SSD_IN: 1} if prev else {},
        scratch_shapes=[pltpu.VMEM((tm, n), F32), pltpu.VMEM((tm + SUBLANE, conv_dim), F32),
                        pltpu.VMEM((D_STATE, rows), F32),
                        pltpu.VMEM((tm, d_inner), BF16)],
        compiler_params=_cparams(("parallel", "arbitrary")),
        name="ssd_layer",
    )(*ins, *prev)


def _ssd_step_pre_body(proj_ref, hist_ref, cw_ref, cb_ref, dtb_ref, alog_ref, e_ref,
                       xs_ref, b_ref, c_ref, xdt_t_ref, dec_b_ref, clast_ref, *, d_inner):
    gn = SSM_GROUPS * D_STATE
    conv_dim = d_inner + 2 * gn
    n_heads = d_inner // SSM_HEAD_DIM
    m = proj_ref.shape[0]
    xbc = proj_ref[:, d_inner:d_inner + conv_dim]
    xc = hist_ref[0] * cw_ref[0:1, :] + cb_ref[...]
    for j in range(1, SSM_CONV - 1):
        xc = xc + hist_ref[j] * cw_ref[j:j + 1, :]
        clast_ref[j - 1] = hist_ref[j]
    xc = _silu(xc + xbc * cw_ref[SSM_CONV - 1:SSM_CONV, :])
    clast_ref[SSM_CONV - 2] = xbc
    xs = xc[:, :d_inner]
    xs_ref[...] = xs
    b_ref[...] = xc[:, d_inner:d_inner + gn]
    c_ref[...] = xc[:, d_inner + gn:]
    dt = _softplus(proj_ref[:, d_inner + conv_dim:] + dtb_ref[...])
    dec = jnp.exp(dt * (-jnp.exp(alog_ref[...])))
    pad = jnp.zeros((LANE - m, d_inner), F32)
    xdt_t_ref[...] = jnp.concatenate([xs * _expand(dt, e_ref[...], n_heads), pad], axis=0).T
    for j in range(n_heads):
        dec_b_ref[j] = jnp.broadcast_to(dec[:, j:j + 1], (m, LANE))


def _ssd_step_pre(proj, hist_t, cw, cb, dtb, alog, e, d_inner):
    m = proj.shape[0]
    gn = SSM_GROUPS * D_STATE
    conv_dim = cw.shape[1]
    k = SSM_CONV - 1
    n_heads = d_inner // SSM_HEAD_DIM
    args = (proj, hist_t, cw, cb, dtb, alog, e)
    return pl.pallas_call(
        functools.partial(_ssd_step_pre_body, d_inner=d_inner),
        grid=(1,),
        in_specs=[_full(a.shape) for a in args],
        out_specs=[_full((m, d_inner)), _full((m, gn)), _full((m, gn)), _full((d_inner, LANE)),
                   _full((n_heads, m, LANE)), _full((k, m, conv_dim))],
        out_shape=[jax.ShapeDtypeStruct((m, d_inner), F32), jax.ShapeDtypeStruct((m, gn), F32),
                   jax.ShapeDtypeStruct((m, gn), F32), jax.ShapeDtypeStruct((d_inner, LANE), F32),
                   jax.ShapeDtypeStruct((n_heads, m, LANE), F32), jax.ShapeDtypeStruct((k, m, conv_dim), F32)],
        compiler_params=_cparams(("arbitrary",)),
        name="ssd_step_pre",
    )(*args)


def _ssd_step_state_body(h0_ref, xdt_t_ref, dec_b_ref, b_ref, c_ref, *rest):
    hn_ref, y_t_ref = rest[-2:]
    m = h0_ref.shape[0]
    wide = jnp.concatenate([b_ref[...]] * m, axis=1)
    own = (lax.broadcasted_iota(jnp.int32, wide.shape, 1) // D_STATE
           == lax.broadcasted_iota(jnp.int32, wide.shape, 0))
    bbd = jnp.concatenate([jnp.where(own, wide, 0.0), jnp.zeros((LANE - m, m * D_STATE), F32)], axis=0)
    b_hi, b_lo = _hi_lo(bbd)
    x_hi, x_lo = _hi_lo(xdt_t_ref[...])
    upd = _dot(jnp.concatenate([x_hi, x_lo, x_hi], axis=1), jnp.concatenate([b_hi, b_hi, b_lo], axis=0))
    prods = []
    for b in range(m):
        h = h0_ref[b] * dec_b_ref[b:b + 1, :] + upd[:, b * D_STATE:(b + 1) * D_STATE]
        hn_ref[b] = h
        prods.append(h * c_ref[b:b + 1, :])
    p_hi, p_lo = _hi_lo(jnp.concatenate(prods, axis=0))
    ones = jnp.ones((D_STATE, LANE), BF16)
    sums = _dot(p_hi, ones) + _dot(p_lo, ones)
    lane = lax.broadcasted_iota(jnp.int32, (SSM_HEAD_DIM, LANE), 1)
    acc = jnp.zeros((SSM_HEAD_DIM, LANE), F32)
    for b in range(m):
        acc = jnp.where(lane == b, sums[b * SSM_HEAD_DIM:(b + 1) * SSM_HEAD_DIM, :], acc)
    y_t_ref[...] = acc


def _ssd_step_state(h_all, prev, layer, xdt_t, dec_t, bm, cm, n_heads):
    m = bm.shape[0]
    hpg = n_heads // SSM_GROUPS
    h_spec = pl.BlockSpec((m, None, SSM_HEAD_DIM, D_STATE), lambda j: (layer, j, 0, 0))
    return pl.pallas_call(
        _ssd_step_state_body,
        grid=(n_heads,),
        in_specs=[h_spec,
                  pl.BlockSpec((SSM_HEAD_DIM, LANE), lambda j: (j, 0)),
                  pl.BlockSpec((None, m, LANE), lambda j: (j, 0, 0)),
                  pl.BlockSpec((m, D_STATE), lambda j: (0, j // hpg)),
                  pl.BlockSpec((m, D_STATE), lambda j: (0, j // hpg))]
                 + [pl.BlockSpec(memory_space=pl.ANY)] * len(prev),
        out_specs=[h_spec, pl.BlockSpec((SSM_HEAD_DIM, LANE), lambda j: (j, 0))],
        out_shape=[jax.ShapeDtypeStruct(h_all.shape, F32),
                   jax.ShapeDtypeStruct((n_heads * SSM_HEAD_DIM, LANE), F32)],
        input_output_aliases={5: 0} if prev else {},
        compiler_params=_cparams(("parallel",)),
        name="ssd_step_state",
    )(h_all, xdt_t, dec_t, bm, cm, *prev)


def _ssd_step_post_body(y_t_ref, xs_ref, z_ref, dx_ref, nw_ref, o_ref):
    m, d_inner = xs_ref.shape
    gw = d_inner // SSM_GROUPS
    y = y_t_ref[...].T[:m, :] + xs_ref[...] * dx_ref[...]
    gated = y * _silu(z_ref[...])
    for g in range(SSM_GROUPS):
        seg = gated[:, g * gw:(g + 1) * gw]
        o_ref[:, g * gw:(g + 1) * gw] = _rms(seg, nw_ref[:, g * gw:(g + 1) * gw]).astype(BF16)


def _ssd_step_post(y_t, xs, proj, dx, nw):
    m, d_inner = xs.shape
    return pl.pallas_call(
        _ssd_step_post_body,
        grid=(1,),
        in_specs=[_full(y_t.shape), _full(xs.shape), pl.BlockSpec((m, d_inner), lambda i: (0, 0)),
                  _full(dx.shape), _full(nw.shape)],
        out_specs=_full((m, d_inner)),
        out_shape=jax.ShapeDtypeStruct((m, d_inner), BF16),
        compiler_params=_cparams(("arbitrary",)),
        name="ssd_step_post",
    )(y_t, xs, proj, dx, nw)


def _band_attn_body(q_ref, kc_ref, kp_ref, vc_ref, vp_ref, o_ref, l_ref, *, win, dil):
    blk = pl.program_id(1)
    n_hp, bt, _ = q_ref.shape
    span = win // dil
    heads = LANE // ATT_HEAD_DIM
    qi = lax.broadcasted_iota(jnp.int32, (span, 2 * span), 0)
    ki = lax.broadcasted_iota(jnp.int32, (span, 2 * span), 1)
    band = (ki >= qi) & (ki <= qi + span)
    band_first = band & ((blk > 0) | (ki >= span))
    lane_q = lax.broadcasted_iota(jnp.int32, (span, LANE), 1)
    lane_k = lax.broadcasted_iota(jnp.int32, (2 * span, LANE), 1)

    def rows(ref, hp, start, r):
        if dil == 1:
            return ref[hp, start:start + span, :]
        return ref[hp, pl.ds(start + r, span, stride=dil), :]

    for hp in range(n_hp):
        for wi in range(bt // win):
            for r in range(dil):
                q = (rows(q_ref, hp, wi * win, r) * (ATT_HEAD_DIM ** -0.5)).astype(BF16)
                if wi == 0:
                    k_prev = rows(kp_ref, hp, kp_ref.shape[1] - win, r)
                    v_prev = rows(vp_ref, hp, vp_ref.shape[1] - win, r)
                else:
                    k_prev = rows(kc_ref, hp, (wi - 1) * win, r)
                    v_prev = rows(vc_ref, hp, (wi - 1) * win, r)
                kk = jnp.concatenate([k_prev, rows(kc_ref, hp, wi * win, r)], axis=0).astype(BF16)
                vv = jnp.concatenate([v_prev, rows(vc_ref, hp, wi * win, r)], axis=0).astype(BF16)
                valid = band_first if wi == 0 else band
                o_acc, l_acc = None, None
                for h in range(heads):
                    in_q = (lane_q >= h * ATT_HEAD_DIM) & (lane_q < (h + 1) * ATT_HEAD_DIM)
                    in_k = (lane_k >= h * ATT_HEAD_DIM) & (lane_k < (h + 1) * ATT_HEAD_DIM)
                    s = jnp.where(valid, _dot_nt(jnp.where(in_q, q, jnp.zeros_like(q)), kk), -jnp.inf)
                    mx = jnp.max(s, axis=-1, keepdims=True)
                    p = jnp.exp(s - mx)
                    den = jnp.sum(p, axis=-1, keepdims=True)
                    o_h = _dot(p.astype(BF16), jnp.where(in_k, vv, jnp.zeros_like(vv))) / den
                    l_h = jnp.where(in_q, mx + jnp.log(den), 0.0)
                    o_acc = o_h if o_acc is None else o_acc + o_h
                    l_acc = l_h if l_acc is None else l_acc + l_h
                if dil == 1:
                    o_ref[hp, wi * win:wi * win + span, :] = o_acc
                    l_ref[hp, wi * win:wi * win + span, :] = l_acc
                else:
                    o_ref[hp, pl.ds(wi * win + r, span, stride=dil), :] = o_acc
                    l_ref[hp, pl.ds(wi * win + r, span, stride=dil), :] = l_acc


def _band_attn(qkv_c, n_seq, gi, win, dil):
    nch, m, _ = qkv_c.shape
    t_len = m // n_seq
    bt = max(win, min(BAND_BLOCK, t_len))
    assert bt % win == 0 and t_len % bt == 0
    nblk = t_len // bt
    wpb = bt // win
    per_grp = ATT_GW // LANE
    ng = len(ATT_GROUPS)

    def cur(sec):
        return pl.BlockSpec((per_grp, bt, LANE), lambda b, k: (sec * ng + gi, b * nblk + k, 0))

    def prev(sec):
        return pl.BlockSpec((per_grp, win, LANE),
                            lambda b, k: (sec * ng + gi, jnp.maximum((b * nblk + k) * wpb - 1, 0), 0))

    out_spec = pl.BlockSpec((per_grp, bt, LANE), lambda b, k: (0, b * nblk + k, 0))
    out_sd = jax.ShapeDtypeStruct((per_grp, m, LANE), F32)
    return pl.pallas_call(
        functools.partial(_band_attn_body, win=win, dil=dil),
        grid=(n_seq, nblk),
        in_specs=[cur(0), cur(1), prev(1), cur(2), prev(2)],
        out_specs=[out_spec, out_spec],
        out_shape=[out_sd, out_sd],
        compiler_params=_cparams(("parallel", "arbitrary")),
        name="band_attn_w%d" % win,
    )(qkv_c, qkv_c, qkv_c, qkv_c, qkv_c)


def _row_to_cols(v):
    return jnp.concatenate([jnp.broadcast_to(v[:, c * LANE:(c + 1) * LANE], (LANE, LANE)).T
                            for c in range(v.shape[1] // LANE)], axis=0)


def _col_to_row(v):
    return jnp.concatenate([jnp.broadcast_to(v[c * LANE:(c + 1) * LANE, :], (LANE, LANE)).T[0:1, :]
                            for c in range(v.shape[0] // LANE)], axis=1)


def _attn_sample_body(*refs, dils, n_prev):
    ng = len(dils)
    qkv_ref = refs[0]
    bufs = refs[1:1 + ng]
    outs = refs[1 + ng + n_prev:]
    o_refs, l_refs, c_refs = outs[0:ng], outs[ng:2 * ng], outs[2 * ng:3 * ng]
    scale = ATT_HEAD_DIM ** -0.5
    for gi in range(ng):
        buf, o_ref, l_ref, c_ref = bufs[gi], o_refs[gi], l_refs[gi], c_refs[gi]
        wb = buf.shape[2]
        nch = wb // LANE
        q_c = _row_to_cols(qkv_ref[:, gi * ATT_GW:(gi + 1) * ATT_GW] * scale)
        kn_c = _row_to_cols(qkv_ref[:, (ng + gi) * ATT_GW:(ng + gi + 1) * ATT_GW])
        vn_c = _row_to_cols(qkv_ref[:, (2 * ng + gi) * ATT_GW:(2 * ng + gi + 1) * ATT_GW])
        lane = lax.broadcasted_iota(jnp.int32, (1, wb), 1)
        valid = (lane & (dils[gi] - 1)) == 0
        o_cols = []
        for h in range(ATT_HPG):
            rs = slice(h * ATT_HEAD_DIM, (h + 1) * ATT_HEAD_DIM)
            qh = q_c[rs, :]
            s = jnp.concatenate([jnp.sum(buf[0, rs, c * LANE:(c + 1) * LANE] * qh, axis=0, keepdims=True)
                                 for c in range(nch)], axis=1)
            s = jnp.where(valid, s, -jnp.inf)
            sn = jnp.sum(qh[:, 0:1] * kn_c[rs, 0:1], axis=0, keepdims=True)
            mx = jnp.maximum(jnp.max(s, axis=1, keepdims=True), sn)
            p = jnp.exp(s - mx)
            p_new = jnp.exp(sn - mx)
            den = jnp.sum(p, axis=1, keepdims=True) + p_new
            acc = buf[1, rs, 0:LANE] * p[:, 0:LANE]
            for c in range(1, nch):
                acc = acc + buf[1, rs, c * LANE:(c + 1) * LANE] * p[:, c * LANE:(c + 1) * LANE]
            o_cols.append((jnp.sum(acc, axis=1, keepdims=True) + p_new * vn_c[rs, 0:1]) / den)
            l_ref[:, rs] = jnp.broadcast_to(mx + jnp.log(den), (1, ATT_HEAD_DIM))
        o_ref[...] = _col_to_row(jnp.concatenate(o_cols, axis=0))
        last = lax.broadcasted_iota(jnp.int32, (ATT_HEAD_DIM, LANE), 1) == LANE - 1
        for kv, new_c in ((0, kn_c), (1, vn_c)):
            for h in range(ATT_HPG):
                rs = slice(h * ATT_HEAD_DIM, (h + 1) * ATT_HEAD_DIM)
                rolled = pltpu.roll(buf[kv, rs, :], wb - 1, 1)
                if nch > 1:
                    c_ref[kv, rs, 0:wb - LANE] = rolled[:, 0:wb - LANE]
                c_ref[kv, rs, wb - LANE:wb] = jnp.where(last, new_c[rs, :], rolled[:, wb - LANE:wb])


def _attn_sample(qkv, cache_views, prev, layer):
    m, w3 = qkv.shape
    ng = len(ATT_GROUPS)
    dils = tuple(dil for _, dil in ATT_GROUPS)
    for v, (win, dil) in zip(cache_views, ATT_GROUPS):
        assert v.shape[3] == win and win % LANE == 0 and dil & (dil - 1) == 0
    c_specs = [pl.BlockSpec((None, 2, ATT_GW, v.shape[3]), lambda b: (layer * m + b, 0, 0, 0)) for v in cache_views]
    o_sd = jax.ShapeDtypeStruct((m, 1, ATT_GW), F32)
    o_spec = pl.BlockSpec((None, 1, ATT_GW), lambda b: (b, 0, 0))
    n_in = 1 + ng
    res = pl.pallas_call(
        functools.partial(_attn_sample_body, dils=dils, n_prev=len(prev)),
        grid=(m,),
        in_specs=[pl.BlockSpec((None, 1, w3), lambda b: (b, 0, 0))] + c_specs
                 + [pl.BlockSpec(memory_space=pl.ANY)] * len(prev),
        out_specs=[o_spec] * (2 * ng) + c_specs,
        out_shape=[o_sd] * (2 * ng) + [jax.ShapeDtypeStruct(v.shape, v.dtype) for v in cache_views],
        input_output_aliases={n_in + k: 2 * ng + k for k in range(len(prev))},
        compiler_params=_cparams(("parallel",)),
        name="attn_sample",
    )(qkv.reshape(m, 1, w3), *cache_views, *prev)
    os_ = [r.reshape(m, ATT_GW) for r in res[0:ng]]
    ls_ = [r.reshape(m, ATT_GW) for r in res[ng:2 * ng]]
    return os_, ls_, list(res[2 * ng:])


def _prep_weights(norms, ssm_w_in, ssm_conv_b, ssm_dt_bias, ssm_a_log, ssm_d, ssm_norm_w, ssm_w_out, att_w_qkv,
                  att_w_o, mem_norm, xa_w_q, xa_w_kv, xa_w_o, ffn_w_gu, ffn_conv_b, ffn_w_down):
    n_ssm = ssm_w_in.shape[0]
    n_heads = ssm_dt_bias.shape[1]
    d_inner = n_heads * SSM_HEAD_DIM
    pad_heads = LANE - n_heads
    w = {}
    w['n_heads'], w['d_inner'] = n_heads, d_inner
    w['norms'] = norms[:, :, None, :]
    def per_layer(a, cols=slice(None)):
        return [a[i, :, cols].astype(BF16) for i in range(a.shape[0])]

    w['ssm_w_in'] = [jnp.pad(ssm_w_in[i], ((0, 0), (0, pad_heads))).astype(BF16) for i in range(n_ssm)]
    w['ssm_conv_b'] = ssm_conv_b[:, None, :]
    w['ssm_dt_bias'] = jnp.pad(ssm_dt_bias, ((0, 0), (0, pad_heads)))[:, None, :]
    w['ssm_a_log'] = jnp.pad(ssm_a_log, ((0, 0), (0, pad_heads)))[:, None, :]
    w['ssm_dx'] = jnp.repeat(ssm_d, SSM_HEAD_DIM, axis=1)[:, None, :]
    w['ssm_norm_w'] = ssm_norm_w[:, None, :]
    w['ssm_w_out'] = per_layer(ssm_w_out)
    w['att_w_qkv'] = per_layer(att_w_qkv)
    w['att_w_o'] = per_layer(att_w_o)
    w['mem_norm'] = mem_norm[:, None, :]
    w['xa_w_q'] = per_layer(xa_w_q)
    w['xa_w_kv'] = xa_w_kv.astype(BF16)
    w['xa_w_o'] = per_layer(xa_w_o)
    w['ffn_w_gu'] = ffn_w_gu.astype(BF16)
    w['ffn_conv_b'] = ffn_conv_b[:, None, :]
    w['ffn_w_down'] = ffn_w_down.astype(BF16)
    e = np.zeros((LANE, d_inner), np.float32)
    assert 3 * n_heads <= LANE
    for h in range(n_heads):
        for part in range(3):
            e[part * n_heads + h, h * SSM_HEAD_DIM:(h + 1) * SSM_HEAD_DIM] = 1.0
    w['expand'] = jnp.asarray(e, BF16)
    w['tril'] = jnp.asarray(np.tril(np.ones((SSM_CHUNK, SSM_CHUNK), np.float32)), BF16)
    return w


def _prompt_trunk(x3, mem3, w, ssm_conv_w, ffn_conv_w):
    n, t_len, d = x3.shape
    assert t_len % SSM_CHUNK == 0 and all(t_len % win == 0 for win, _ in ATT_GROUPS)
    depth = w['norms'].shape[0]
    n_heads, d_inner = w['n_heads'], w['d_inner']
    conv_dim = ssm_conv_w.shape[2]
    d_ff = ffn_conv_w.shape[2]
    tm = PROMPT_ROWS_SSD
    tm_big = PROMPT_ROWS
    x = x3.reshape(n * t_len, d)
    mem = mem3.reshape(-1, d)
    kv_all, kv_rows = _mem_kv(mem, w['mem_norm'], w['xa_w_kv'], tm)
    tabs = _rope_tables(jnp.arange(t_len, dtype=jnp.int32))
    zero_hist = jnp.zeros((n, SSM_CONV - 1, conv_dim), F32)
    zero_h = jnp.zeros((n, n_heads * SSM_HEAD_DIM, D_STATE), F32)
    zero_fh = jnp.zeros((n, FFN_CONV - 1, d_ff), F32)
    n_ssm, n_att = (depth + 1) // 2, depth // 2
    new = {'ssm': [], 'ssm_conv': [], 'swa': [], 'ffn_conv': []}
    for i in range(depth):
        j = i // 2
        g = w['norms'][i]
        if i % 2 == 0:
            x, h_last, c_last = _ssd_layer(x, g[0], g[1], w['ssm_w_in'][j], ssm_conv_w[j], w['ssm_conv_b'][j],
                                           w['ssm_dt_bias'][j], w['ssm_a_log'][j], w['ssm_dx'][j],
                                           w['ssm_norm_w'][j], w['expand'], w['tril'], w['ssm_w_out'][j],
                                           zero_hist, zero_h, new['ssm'], j, n_ssm, n, d_inner, n_heads, tm)
            new['ssm'] = [h_last]
            new['ssm_conv'].append(c_last)
        else:
            qkv_c, new['swa'] = _norm_qkv_prompt(x, g[0], w['att_w_qkv'][j], tabs, new['swa'], j, n_att, n, tm_big)
            os_, ls_ = [], []
            for gi, (win, dil) in enumerate(ATT_GROUPS):
                o, l = _band_attn(qkv_c, n, gi, win, dil)
                os_.append(o)
                ls_.append(l)
            x = _attn_out(os_, ls_, w['att_w_o'][j], x, g[1], tm_big)
        x = _xattn_prompt(x, g[2], g[3], w['xa_w_q'][i], kv_all, i, w['xa_w_o'][i], n, tm_big)
        x, f_hist = _ffn_prompt(x, g[4], g[5], w['ffn_w_gu'], i, ffn_conv_w[i],
                                w['ffn_conv_b'][i], w['ffn_w_down'], zero_fh, n, tm_big)
        new['ffn_conv'].append(f_hist)
    n_mem = mem3.shape[1]
    hd = d // MEM_HEADS
    swa = [jnp.transpose(c.reshape(n_att, n, 2, ATT_HPG, ATT_HEAD_DIM, c.shape[3]), (0, 1, 5, 2, 3, 4))
           for c in new['swa']]
    p_mem = jnp.transpose(kv_rows.reshape(depth, n, n_mem, 2, hd // LANE, MEM_HEADS, LANE),
                          (0, 1, 2, 3, 5, 4, 6)).reshape(depth, n, n_mem, 2, MEM_HEADS, hd)
    return (x.reshape(n, t_len, d), new['ssm'][0].reshape(n_ssm, n, n_heads, SSM_HEAD_DIM, D_STATE),
            jnp.stack(new['ssm_conv']), swa, p_mem, jnp.stack(new['ffn_conv']))


def _sample_trunk(x3, w, ssm_conv_w, ffn_conv_w, state_ssm, state_ssm_conv, caches, cache_mem_kv, state_ffn_conv):
    m, t_len, d = x3.shape
    assert t_len == 1
    depth = w['norms'].shape[0]
    n_heads, d_inner = w['n_heads'], w['d_inner']
    n_ssm = state_ssm.shape[0]
    x = x3.reshape(m, d)
    tabs = _rope_tables(jnp.full((m,), PAST_LEN, jnp.int32))
    h_all = state_ssm.reshape(n_ssm * m, n_heads, SSM_HEAD_DIM, D_STATE)
    n_mem, hd = cache_mem_kv.shape[2], d // MEM_HEADS
    kv_all = jnp.transpose(cache_mem_kv.reshape(depth, m, n_mem, 2, MEM_HEADS, hd // LANE, LANE),
                           (0, 1, 2, 3, 5, 4, 6)).reshape(depth * m, n_mem, 2, MEM_HEADS * (hd // LANE), LANE)
    cache_views = [jnp.transpose(c, (0, 1, 3, 4, 5, 2)).reshape(c.shape[0] * m, 2, ATT_GW, c.shape[2])
                   for c in caches]
    new_caches = []
    new = {'ssm': [], 'ssm_conv': [], 'ffn_conv': []}
    for i in range(depth):
        j = i // 2
        g = w['norms'][i]
        if i % 2 == 0:
            proj = _norm_proj(x, g[0], w['ssm_w_in'][j], m)
            hist_t = jnp.swapaxes(state_ssm_conv[j], 0, 1)
            xs, bm, cm, xdt_t, dec_t, c_last_t = _ssd_step_pre(proj, hist_t, ssm_conv_w[j], w['ssm_conv_b'][j],
                                                                w['ssm_dt_bias'][j], w['ssm_a_log'][j],
                                                                w['expand'], d_inner)
            h_new, y_t = _ssd_step_state(h_all, new['ssm'], j, xdt_t, dec_t, bm, cm, n_heads)
            y = _ssd_step_post(y_t, xs, proj, w['ssm_dx'][j], w['ssm_norm_w'][j])
            x = _proj_res(y, w['ssm_w_out'][j], x, g[1], m)
            new['ssm'] = [h_new]
            new['ssm_conv'].append(jnp.swapaxes(c_last_t, 0, 1))
        else:
            qkv = _norm_qkv_rope(x, g[0], w['att_w_qkv'][j], tabs, m, 1)
            os_, ls_, new_caches = _attn_sample(qkv, cache_views, new_caches, j)
            x = _attn_out(os_, ls_, w['att_w_o'][j], x, g[1], m)
        x = _xattn_sample(x, g[2], g[3], w['xa_w_q'][i], kv_all, i, w['xa_w_o'][i])
        hist_t = jnp.swapaxes(state_ffn_conv[i], 0, 1)
        x, f_hist_t = _ffn_sample(x, g[4], g[5], w['ffn_w_gu'], i, ffn_conv_w[i],
                                  w['ffn_conv_b'][i], w['ffn_w_down'], hist_t, 256)
        new['ffn_conv'].append(jnp.swapaxes(f_hist_t, 0, 1))
    new_caches = [jnp.transpose(nc.reshape(c.shape[0], m, 2, ATT_HPG, ATT_HEAD_DIM, c.shape[2]), (0, 1, 5, 2, 3, 4))
                  for nc, c in zip(new_caches, caches)]
    return (x.reshape(m, 1, d), new['ssm'][0].reshape(state_ssm.shape), jnp.stack(new['ssm_conv']), new_caches,
            jnp.stack(new['ffn_conv']))


def kernel(x_prompt, x_sample, mem_prompt, state_ssm, state_ssm_conv, cache_swa_kv_w128, cache_swa_kv_w512,
           cache_swa_kv_w2048, cache_mem_kv, state_ffn_conv, norms, ssm_w_in, ssm_conv_w, ssm_conv_b,
           ssm_dt_bias, ssm_a_log, ssm_d, ssm_norm_w, ssm_w_out, att_w_qkv, att_w_o, mem_norm, xa_w_q,
           xa_w_kv, xa_w_o, ffn_w_gu, ffn_conv_w, ffn_conv_b, ffn_w_down):
    w = _prep_weights(norms, ssm_w_in, ssm_conv_b, ssm_dt_bias, ssm_a_log, ssm_d, ssm_norm_w, ssm_w_out,
                      att_w_qkv, att_w_o, mem_norm, xa_w_q, xa_w_kv, xa_w_o, ffn_w_gu, ffn_conv_b, ffn_w_down)
    caches = [cache_swa_kv_w128, cache_swa_kv_w512, cache_swa_kv_w2048]
    yp, p_ssm, p_conv, p_swa, p_mem, p_ffn = _prompt_trunk(x_prompt, mem_prompt, w, ssm_conv_w, ffn_conv_w)
    ys, s_ssm, s_conv, s_swa, s_ffn = _sample_trunk(x_sample, w, ssm_conv_w, ffn_conv_w, state_ssm,
                                                    state_ssm_conv, caches, cache_mem_kv, state_ffn_conv)
    return (yp, ys, p_ssm, p_conv, p_swa[0], p_swa[1], p_swa[2], p_mem, p_ffn,
            s_ssm, s_conv, s_swa[0], s_swa[1], s_swa[2], s_ffn)
```

```python
import functools

import numpy as np
import jax
import jax.numpy as jnp
from jax import lax
from jax.experimental import pallas as pl
from jax.experimental.pallas import tpu as pltpu

F32 = jnp.float32
BF16 = jnp.bfloat16

EPS = 1e-6
PAST_LEN = 8192
SSM_HEAD_DIM = 64
SSM_GROUPS = 4
D_STATE = 128
SSM_CONV = 4
SSM_CHUNK = 128
CONV_COLS = 1024
ATT_GROUPS = ((128, 1), (512, 4), (2048, 16))
ATT_HPG = 4
ATT_HEAD_DIM = 64
ATT_GW = ATT_HPG * ATT_HEAD_DIM
BAND_BLOCK = 1024
ROT_DIM = ATT_HEAD_DIM // 4
ROPE_THETA = 500000.0
MEM_HEADS = 4
FFN_CONV = 3

LANE = 128
SUBLANE = 8
VMEM_LIMIT = 56 * 1024 * 1024
PROMPT_ROWS = 512
PROMPT_ROWS_SSD = 256


def _cparams(sem):
    return pltpu.CompilerParams(dimension_semantics=sem, vmem_limit_bytes=VMEM_LIMIT)


def _rms(x, g):
    return x * lax.rsqrt(jnp.mean(x * x, axis=-1, keepdims=True) + EPS) * g


def _silu(x):
    return x / (1.0 + jnp.exp(-x))


def _softplus(x):
    return jnp.maximum(x, 0.0) + jnp.log(1.0 + jnp.exp(-jnp.abs(x)))


def _dot(a, b):
    return jnp.dot(a, b, preferred_element_type=F32)


def _dot_nt(a, b):
    return lax.dot_general(a, b, (((1,), (1,)), ((), ())), preferred_element_type=F32)


def _split3(v):
    hi = v.astype(BF16)
    r = v - hi.astype(F32)
    mid = r.astype(BF16)
    lo = (r - mid.astype(F32)).astype(BF16)
    return hi, mid, lo


def _hi_lo(v):
    hi = v.astype(BF16)
    return hi, (v - hi.astype(F32)).astype(BF16)


def _expand(v, e, n_heads):
    return _dot(_pack3(v, n_heads), e)


def _pack3(v, n_heads):
    hi = v.astype(BF16).astype(F32)
    r1 = v - hi
    mid = r1.astype(BF16).astype(F32)
    lo = r1 - mid
    lane = lax.broadcasted_iota(jnp.int32, v.shape, 1)
    packed = jnp.where(lane < n_heads, hi,
                       jnp.where(lane < 2 * n_heads, pltpu.roll(mid, n_heads, 1),
                                 jnp.where(lane < 3 * n_heads, pltpu.roll(lo, 2 * n_heads, 1), 0.0)))
    return packed.astype(BF16)


def _full(shape):
    return pl.BlockSpec(shape, lambda *_: (0,) * len(shape))


def _resident(shape):
    return pl.BlockSpec(shape, lambda *_: (0,) * len(shape), pipeline_mode=pl.Buffered(1))


def _norm_proj_body(x_ref, g_ref, w_ref, o_ref):
    u = _rms(x_ref[...], g_ref[...]).astype(BF16)
    o_ref[...] = _dot(u, w_ref[...])


def _norm_proj(x, g, w, tm):
    m, d = x.shape
    n = w.shape[1]
    return pl.pallas_call(
        _norm_proj_body,
        grid=(m // tm,),
        in_specs=[pl.BlockSpec((tm, d), lambda i: (i, 0)), _full((1, d)), _full((d, n))],
        out_specs=pl.BlockSpec((tm, n), lambda i: (i, 0)),
        out_shape=jax.ShapeDtypeStruct((m, n), F32),
        compiler_params=_cparams(("parallel",)),
        name="norm_proj",
    )(x, g, w)


def _norm_qkv_rope_body(x_ref, g_ref, w_ref, cos_ref, sa_ref, sb_ref, o_ref, *, n_rot):
    u = _rms(x_ref[...], g_ref[...]).astype(BF16)
    y = _dot(u, w_ref[...])
    cos, sa, sb = cos_ref[...], sa_ref[...], sb_ref[...]
    for c in range(n_rot // LANE):
        t = y[:, c * LANE:(c + 1) * LANE]
        o_ref[:, c * LANE:(c + 1) * LANE] = (t * cos + pltpu.roll(t, LANE - ROT_DIM // 2, 1) * sa
                                             + pltpu.roll(t, ROT_DIM // 2, 1) * sb)
    o_ref[:, n_rot:] = y[:, n_rot:]


def _norm_qkv_rope(x, g, w, tabs, tm, seq_blocks):
    m, d = x.shape
    n = w.shape[1]
    tab_spec = pl.BlockSpec((tm, LANE), lambda i: (i % seq_blocks, 0))
    return pl.pallas_call(
        functools.partial(_norm_qkv_rope_body, n_rot=2 * n // 3),
        grid=(m // tm,),
        in_specs=[pl.BlockSpec((tm, d), lambda i: (i, 0)), _full((1, d)), _full((d, n)),
                  tab_spec, tab_spec, tab_spec],
        out_specs=pl.BlockSpec((tm, n), lambda i: (i, 0)),
        out_shape=jax.ShapeDtypeStruct((m, n), F32),
        compiler_params=_cparams(("parallel",)),
        name="norm_qkv_rope",
    )(x, g, w, *tabs)


def _rope_chunk(t, cos, sa, sb):
    return t * cos + pltpu.roll(t, LANE - ROT_DIM // 2, 1) * sa + pltpu.roll(t, ROT_DIM // 2, 1) * sb


def _norm_qkv_prompt_body(*refs, n_rot, n_prev, nblks):
    x_ref, g_ref, w_ref, cos_ref, sa_ref, sb_ref = refs[:6]
    o_ref = refs[6 + n_prev]
    c_refs = refs[7 + n_prev:]
    ng = len(c_refs)
    t = pl.program_id(1)
    tb = pl.num_programs(1)
    tm = x_ref.shape[0]
    u = _rms(x_ref[...], g_ref[...]).astype(BF16)
    y = _dot(u, w_ref[...])
    cos, sa, sb = cos_ref[...], sa_ref[...], sb_ref[...]
    vals = []
    for c in range(y.shape[1] // LANE):
        v = y[:, c * LANE:(c + 1) * LANE]
        if c * LANE < n_rot:
            v = _rope_chunk(v, cos, sa, sb)
        o_ref[c] = v
        vals.append(v)
    per_sec = len(vals) // 3
    per_grp = ATT_GW // LANE
    for gi, c_ref in enumerate(c_refs):
        kw = c_ref.shape[2]
        nblk = nblks[gi]

        @pl.when(t >= tb - nblk)
        def _(gi=gi, c_ref=c_ref, kw=kw):
            for kv in range(2):
                for hp in range(per_grp):
                    v = vals[(1 + kv) * per_sec + gi * per_grp + hp]
                    c_ref[kv, hp * LANE:(hp + 1) * LANE, :] = v[tm - kw:, :].T


def _norm_qkv_prompt(x, g, w, tabs, prev, layer, n_layers, n_seq, tm):
    m, d = x.shape
    n = w.shape[1]
    t_len = m // n_seq
    tb = t_len // tm
    tab_spec = pl.BlockSpec((tm, LANE), lambda b, t: (t, 0))
    c_specs, c_shapes, nblks = [], [], []
    for gi, (win, _) in enumerate(ATT_GROUPS):
        keep = min(win, t_len)
        kw = min(keep, tm)
        nblk = keep // kw
        assert keep % kw == 0
        nblks.append(nblk)
        c_specs.append(pl.BlockSpec((None, 2, ATT_GW, kw),
                                    lambda b, t, nblk=nblk: (layer * n_seq + b, 0, 0, jnp.maximum(t - (tb - nblk), 0))))
        c_shapes.append(jax.ShapeDtypeStruct((n_layers * n_seq, 2, ATT_GW, keep), F32))
    res = pl.pallas_call(
        functools.partial(_norm_qkv_prompt_body, n_rot=2 * n // 3, n_prev=len(prev), nblks=tuple(nblks)),
        grid=(n_seq, tb),
        in_specs=[pl.BlockSpec((tm, d), lambda b, t: (b * tb + t, 0)), _full((1, d)), _resident((d, n)),
                  tab_spec, tab_spec, tab_spec] + [pl.BlockSpec(memory_space=pl.ANY)] * len(prev),
        out_specs=[pl.BlockSpec((n // LANE, tm, LANE), lambda b, t: (0, b * tb + t, 0))] + c_specs,
        out_shape=[jax.ShapeDtypeStruct((n // LANE, m, LANE), F32)] + c_shapes,
        input_output_aliases={6 + k: 1 + k for k in range(len(prev))},
        compiler_params=_cparams(("parallel", "arbitrary")),
        name="norm_qkv_prompt",
    )(x, g, w, *tabs, *prev)
    return res[0], list(res[1:])


def _rope_tables(pos):
    half = ROT_DIM // 2
    inv = ROPE_THETA ** (-jnp.arange(half, dtype=F32) / half)
    ang = pos.astype(F32)[:, None] * inv[None, :]
    cos, sin = jnp.cos(ang), jnp.sin(ang)
    p = pos.shape[0]
    rest = ATT_HEAD_DIM - ROT_DIM
    c = jnp.concatenate([cos, cos, jnp.ones((p, rest), F32)], axis=1)
    sa = jnp.concatenate([-sin, jnp.zeros((p, half + rest), F32)], axis=1)
    sb = jnp.concatenate([jnp.zeros((p, half), F32), sin, jnp.zeros((p, rest), F32)], axis=1)
    rep = LANE // ATT_HEAD_DIM
    return tuple(jnp.tile(t, (1, rep)) for t in (c, sa, sb))


def _proj_res_body(y_ref, w_ref, x_ref, g_ref, o_ref):
    f = _dot(y_ref[...].astype(BF16), w_ref[...])
    o_ref[...] = x_ref[...] + _rms(f, g_ref[...])


def _proj_res(y, w, x, g, tm):
    m, k = y.shape
    d = w.shape[1]
    return pl.pallas_call(
        _proj_res_body,
        grid=(m // tm,),
        in_specs=[pl.BlockSpec((tm, k), lambda i: (i, 0)), _full((k, d)),
                  pl.BlockSpec((tm, d), lambda i: (i, 0)), _full((1, d))],
        out_specs=pl.BlockSpec((tm, d), lambda i: (i, 0)),
        out_shape=jax.ShapeDtypeStruct((m, d), F32),
        compiler_params=_cparams(("parallel",)),
        name="proj_res",
    )(y, w, x, g)


def _attn_out_body(o0, o1, o2, l0, l1, l2, w_ref, x_ref, g_ref, out_ref):
    def load(ref):
        if len(ref.shape) == 2:
            return ref[...]
        return jnp.concatenate([ref[c] for c in range(ref.shape[0])], axis=1)

    ls = [load(l0), load(l1), load(l2)]
    mx = jnp.maximum(jnp.maximum(ls[0], ls[1]), ls[2])
    es = [jnp.exp(l - mx) for l in ls]
    den = es[0] + es[1] + es[2]
    og = jnp.concatenate([(es[gi] / den * load(o_ref)).astype(BF16) for gi, o_ref in enumerate((o0, o1, o2))], axis=1)
    out_ref[...] = x_ref[...] + _rms(_dot(og, w_ref[...]), g_ref[...])


def _attn_out(os_, ls_, w, x, g, tm):
    m, d = x.shape
    if os_[0].ndim == 2:
        blk = pl.BlockSpec((tm, ATT_GW), lambda i: (i, 0))
    else:
        blk = pl.BlockSpec((ATT_GW // LANE, tm, LANE), lambda i: (0, i, 0))
    return pl.pallas_call(
        _attn_out_body,
        grid=(m // tm,),
        in_specs=[blk] * 6 + [_full(w.shape), pl.BlockSpec((tm, d), lambda i: (i, 0)), _full((1, d))],
        out_specs=pl.BlockSpec((tm, d), lambda i: (i, 0)),
        out_shape=jax.ShapeDtypeStruct((m, d), F32),
        compiler_params=_cparams(("parallel",)),
        name="attn_out",
    )(*os_, *ls_, w, x, g)


def _mem_kv_body(x_ref, g_ref, w_ref, o_ref, t_ref):
    tm = x_ref.shape[0]
    u = _rms(x_ref[...], g_ref[...]).astype(BF16)
    y = _dot(u, w_ref[...])
    o_ref[...] = y
    hd = y.shape[1] // (2 * MEM_HEADS)
    nch = hd // LANE
    tok_rows = 2 * nch * MEM_HEADS
    for kv in range(2):
        for h in range(MEM_HEADS):
            for c in range(nch):
                col = (kv * MEM_HEADS + h) * hd + c * LANE
                t_ref[pl.ds((kv * nch + c) * MEM_HEADS + h, tm, stride=tok_rows), :] = y[:, col:col + LANE]


def _mem_kv(mem, g, w, tm):
    m, d = mem.shape
    depth, _, n = w.shape
    tok_rows = n // LANE
    return pl.pallas_call(
        _mem_kv_body,
        grid=(depth, m // tm),
        in_specs=[pl.BlockSpec((tm, d), lambda l, i: (i, 0)),
                  pl.BlockSpec((None, 1, d), lambda l, i: (l, 0, 0)),
                  pl.BlockSpec((None, d, n), lambda l, i: (l, 0, 0))],
        out_specs=[pl.BlockSpec((None, tm, n), lambda l, i: (l, i, 0)),
                   pl.BlockSpec((None, tm * tok_rows, LANE), lambda l, i: (l, i, 0))],
        out_shape=[jax.ShapeDtypeStruct((depth, m, n), F32),
                   jax.ShapeDtypeStruct((depth, m * tok_rows, LANE), F32)],
        compiler_params=_cparams(("parallel", "parallel")),
        name="mem_kv",
    )(mem, g, w)


def _xattn_prompt_body(x_ref, gpre_ref, gpost_ref, wq_ref, kv_ref, wo_ref, o_ref, obuf):
    x = x_ref[...]
    d = x.shape[1]
    hd = d // MEM_HEADS
    u = _rms(x, gpre_ref[...]).astype(BF16)
    q = _dot(u, wq_ref[...]).astype(BF16)
    scale = hd ** -0.5
    for h in range(MEM_HEADS):
        kh = kv_ref[:, h * hd:(h + 1) * hd].astype(BF16)
        vh = kv_ref[:, d + h * hd:d + (h + 1) * hd].astype(BF16)
        s = _dot_nt(q[:, h * hd:(h + 1) * hd], kh) * scale
        mx = jnp.max(s, axis=-1, keepdims=True)
        p = jnp.exp(s - mx)
        den = jnp.sum(p, axis=-1, keepdims=True)
        obuf[:, h * hd:(h + 1) * hd] = (_dot(p.astype(BF16), vh) / den).astype(BF16)
    f = _dot(obuf[...], wo_ref[...])
    o_ref[...] = x + _rms(f, gpost_ref[...])


def _xattn_prompt(x, gpre, gpost, wq, kv, layer, wo, n_seq, tm):
    m, d = x.shape
    tb = m // n_seq // tm
    n_mem = kv.shape[1] // n_seq
    return pl.pallas_call(
        _xattn_prompt_body,
        grid=(n_seq, tb),
        in_specs=[pl.BlockSpec((tm, d), lambda b, t: (b * tb + t, 0)), _full((1, d)), _full((1, d)),
                  _resident((d, d)), pl.BlockSpec((None, n_mem, 2 * d), lambda b, t: (layer, b, 0)),
                  _resident((d, d))],
        out_specs=pl.BlockSpec((tm, d), lambda b, t: (b * tb + t, 0)),
        out_shape=jax.ShapeDtypeStruct((m, d), F32),
        scratch_shapes=[pltpu.VMEM((tm, d), BF16)],
        compiler_params=_cparams(("parallel", "parallel")),
        name="xattn_prompt",
    )(x, gpre, gpost, wq, kv, wo)


def _xattn_sample_body(x_ref, gpre_ref, gpost_ref, wq_ref, kv_ref, wo_ref, o_ref, q_scr, o_scr):
    b = pl.program_id(0)
    d = x_ref.shape[1]
    hd = d // MEM_HEADS

    @pl.when(b == 0)
    def _():
        u = _rms(x_ref[...], gpre_ref[...]).astype(BF16)
        q_scr[...] = _dot(u, wq_ref[...]) * (hd ** -0.5)

    qb = q_scr[pl.ds(b, 1), :]
    nch = hd // LANE
    n_mem, _, rows, _ = kv_ref.shape
    row_id = lax.broadcasted_iota(jnp.int32, (rows, LANE), 0)
    q_tile = jnp.zeros((rows, LANE), F32)
    for c in range(nch):
        for h in range(MEM_HEADS):
            piece = qb[:, h * hd + c * LANE:h * hd + (c + 1) * LANE]
            q_tile = jnp.where(row_id == c * MEM_HEADS + h, piece, q_tile)
    prod = (kv_ref[:, 0] * q_tile).reshape(n_mem * rows, LANE)
    p_hi, p_lo = _hi_lo(prod)
    ones = jnp.ones((LANE, LANE), BF16)
    s = (_dot(p_hi, ones) + _dot(p_lo, ones)).reshape(n_mem, rows, LANE)
    part = s
    for c in range(1, nch):
        s = s + pltpu.roll(part, c * MEM_HEADS, 1)
    mx = jnp.max(s, axis=0, keepdims=True)
    p = jnp.exp(s - mx)
    den = jnp.sum(p, axis=0)
    o_tile = jnp.sum(p * kv_ref[:, 1], axis=0) / den
    o_scr[pl.ds(b, 1), :] = jnp.concatenate(
        [o_tile[c * MEM_HEADS + h:c * MEM_HEADS + h + 1, :] for h in range(MEM_HEADS) for c in range(nch)], axis=1)

    @pl.when(b == pl.num_programs(0) - 1)
    def _():
        f = _dot(o_scr[...].astype(BF16), wo_ref[...])
        o_ref[...] = x_ref[...] + _rms(f, gpost_ref[...])


def _xattn_sample(x, gpre, gpost, wq, kv_all, layer, wo):
    m, d = x.shape
    return pl.pallas_call(
        _xattn_sample_body,
        grid=(m,),
        in_specs=[_full((m, d)), _full((1, d)), _full((1, d)), _full((d, d)),
                  pl.BlockSpec((None,) + kv_all.shape[1:], lambda b: (layer * m + b, 0, 0, 0, 0)), _full((d, d))],
        out_specs=_full((m, d)),
        out_shape=jax.ShapeDtypeStruct((m, d), F32),
        scratch_shapes=[pltpu.VMEM((m, d), F32), pltpu.VMEM((m, d), F32)],
        compiler_params=_cparams(("arbitrary",)),
        name="xattn_sample",
    )(x, gpre, gpost, wq, kv_all, wo)


def _ffn_prompt_body(x_ref, gpre_ref, gpost_ref, wg_ref, wu_ref, cw_ref, cb_ref, wd_ref, hist_ref,
                     o_ref, nh_ref, gbuf):
    t = pl.program_id(1)
    tm = x_ref.shape[0]
    k = FFN_CONV - 1
    base = SUBLANE - k

    @pl.when(t == 0)
    def _():
        gbuf[...] = jnp.zeros(gbuf.shape, F32)
        gbuf[base:SUBLANE, :] = hist_ref[...]

    x = x_ref[...]
    u = _rms(x, gpre_ref[...]).astype(BF16)
    gate = _dot(u, wg_ref[...])
    up = _dot(u, wu_ref[...])
    full = jnp.concatenate([gbuf[...], gate], axis=0)
    gc = gate * cw_ref[k:k + 1, :] + cb_ref[...]
    for j in range(k):
        gc = gc + pltpu.roll(full, k - j, 0)[SUBLANE:, :] * cw_ref[j:j + 1, :]
    hmid = (_silu(gc) * up).astype(BF16)
    f = _dot(hmid, wd_ref[...])
    o_ref[...] = x + _rms(f, gpost_ref[...])
    gbuf[...] = gate[tm - SUBLANE:, :]
    nh_ref[...] = gbuf[base:SUBLANE, :]


def _ffn_prompt(x, gpre, gpost, w_gu, layer, cw, cb, w_down, hist, n_seq, tm):
    m, d = x.shape
    f = w_down.shape[1]
    tb = m // n_seq // tm
    k = FFN_CONV - 1
    one = pl.Buffered(1)
    return pl.pallas_call(
        _ffn_prompt_body,
        grid=(n_seq, tb),
        in_specs=[pl.BlockSpec((tm, d), lambda b, t: (b * tb + t, 0)), _full((1, d)), _full((1, d)),
                  pl.BlockSpec((None, d, f), lambda b, t: (layer, 0, 0), pipeline_mode=one),
                  pl.BlockSpec((None, d, f), lambda b, t: (layer, 0, 1), pipeline_mode=one),
                  _full((FFN_CONV, f)), _full((1, f)),
                  pl.BlockSpec((None, f, d), lambda b, t: (layer, 0, 0), pipeline_mode=one),
                  pl.BlockSpec((None, k, f), lambda b, t: (b, 0, 0))],
        out_specs=[pl.BlockSpec((tm, d), lambda b, t: (b * tb + t, 0)),
                   pl.BlockSpec((None, k, f), lambda b, t: (b, 0, 0))],
        out_shape=[jax.ShapeDtypeStruct((m, d), F32), jax.ShapeDtypeStruct((n_seq, k, f), F32)],
        scratch_shapes=[pltpu.VMEM((SUBLANE, f), F32)],
        compiler_params=_cparams(("parallel", "arbitrary")),
        name="ffn_prompt",
    )(x, gpre, gpost, w_gu, w_gu, cw, cb, w_down, hist)


def _ffn_sample_body(x_ref, gpre_ref, gpost_ref, wg_ref, wu_ref, cw_ref, cb_ref, wd_ref, hist_ref,
                     o_ref, nh_ref, u_scr, acc):
    j = pl.program_id(0)

    @pl.when(j == 0)
    def _():
        u_scr[...] = _rms(x_ref[...], gpre_ref[...]).astype(BF16)
        acc[...] = jnp.zeros_like(acc)

    u = u_scr[...]
    gate = _dot(u, wg_ref[...])
    up = _dot(u, wu_ref[...])
    gc = hist_ref[0] * cw_ref[0:1, :] + cb_ref[...]
    for k in range(1, FFN_CONV - 1):
        gc = gc + hist_ref[k] * cw_ref[k:k + 1, :]
        nh_ref[k - 1] = hist_ref[k]
    gc = gc + gate * cw_ref[FFN_CONV - 1:FFN_CONV, :]
    nh_ref[FFN_CONV - 2] = gate
    hmid = (_silu(gc) * up).astype(BF16)
    acc[...] += _dot(hmid, wd_ref[...])

    @pl.when(j == pl.num_programs(0) - 1)
    def _():
        o_ref[...] = x_ref[...] + _rms(acc[...], gpost_ref[...])


def _ffn_sample(x, gpre, gpost, w_gu, layer, cw, cb, w_down, hist_t, tn):
    m, d = x.shape
    f = w_down.shape[1]
    k = FFN_CONV - 1
    nb = f // tn
    return pl.pallas_call(
        _ffn_sample_body,
        grid=(nb,),
        in_specs=[_full((m, d)), _full((1, d)), _full((1, d)),
                  pl.BlockSpec((None, d, tn), lambda j: (layer, 0, j)),
                  pl.BlockSpec((None, d, tn), lambda j: (layer, 0, nb + j)),
                  pl.BlockSpec((FFN_CONV, tn), lambda j: (0, j)), pl.BlockSpec((1, tn), lambda j: (0, j)),
                  pl.BlockSpec((None, tn, d), lambda j: (layer, j, 0)),
                  pl.BlockSpec((k, m, tn), lambda j: (0, 0, j))],
        out_specs=[_full((m, d)), pl.BlockSpec((k, m, tn), lambda j: (0, 0, j))],
        out_shape=[jax.ShapeDtypeStruct((m, d), F32), jax.ShapeDtypeStruct((k, m, f), F32)],
        scratch_shapes=[pltpu.VMEM((m, d), BF16), pltpu.VMEM((m, d), F32)],
        compiler_params=_cparams(("arbitrary",)),
        name="ffn_sample",
    )(x, gpre, gpost, w_gu, w_gu, cw, cb, w_down, hist_t)


def _ssd_activations(u, w_ref, cw_ref, cb_ref, dtb_ref, act, cbuf, clast_ref, d_inner):
    tm = u.shape[0]
    k = SSM_CONV - 1
    base = SUBLANE - k
    conv_dim = cw_ref.shape[1]
    act[:, :d_inner] = _silu(_dot(u, w_ref[:, :d_inner]))
    act[:, d_inner + conv_dim:] = _softplus(_dot(u, w_ref[:, d_inner + conv_dim:]) + dtb_ref[...])
    for c0 in range(0, conv_dim, CONV_COLS):
        cs = slice(c0, c0 + CONV_COLS)
        cbuf[SUBLANE:SUBLANE + tm, cs] = _dot(u, w_ref[:, d_inner + c0:d_inner + c0 + CONV_COLS])
        full = cbuf[:, cs]
        xc = full[SUBLANE:, :] * cw_ref[k:k + 1, cs] + cb_ref[:, cs]
        for j in range(k):
            xc = xc + pltpu.roll(full, k - j, 0)[SUBLANE:, :] * cw_ref[j:j + 1, cs]
        act[:, d_inner + c0:d_inner + c0 + CONV_COLS] = _silu(xc)
    last = cbuf[tm + base:tm + SUBLANE, :]
    clast_ref[...] = last
    cbuf[base:SUBLANE, :] = last


def _ssd_chunk(act_ref, alog_ref, dx_ref, nw_ref, e_ref, tril_ref, state, y_ref, d_inner, n_heads):
    L = SSM_CHUNK
    gn = SSM_GROUPS * D_STATE
    conv_dim = d_inner + 2 * gn
    hpg = n_heads // SSM_GROUPS
    gw = d_inner // SSM_GROUPS
    xs = act_ref[:, d_inner:2 * d_inner]
    xs_b = xs.astype(BF16)
    e = e_ref[...]
    dt = act_ref[:, d_inner + conv_dim:]
    a = dt * (-jnp.exp(alog_ref[...]))
    tril = tril_ref[...]
    a_hi, a_mid, a_lo = _split3(a)
    acum = _dot(tril, a_hi) + _dot(tril, a_mid) + _dot(tril, a_lo)
    acum_t = acum.T
    dt_t = dt.T
    a_last = acum[L - 1:L, :]
    d_acc = _expand(jnp.exp(acum), e, n_heads)
    xde = (xs * _expand(dt * jnp.exp(a_last - acum), e, n_heads)).astype(BF16)
    row = lax.broadcasted_iota(jnp.int32, (L, L), 0)
    col = lax.broadcasted_iota(jnp.int32, (L, L), 1)
    causal = row >= col

    for g in range(SSM_GROUPS):
        b0 = 2 * d_inner + g * D_STATE
        gs = slice(g * gw, (g + 1) * gw)
        b_f = act_ref[:, b0:b0 + D_STATE]
        bg = b_f.astype(BF16)
        cg = act_ref[:, b0 + gn:b0 + gn + D_STATE].astype(BF16)
        cb = _dot_nt(cg, bg)
        y_inter = _dot(cg, state[:, gs].astype(BF16)) * d_acc[:, gs]
        tiles = []
        for t0 in range(g * gw, (g + 1) * gw, LANE):
            xt = xs_b[:, t0:t0 + LANE]
            lane = lax.broadcasted_iota(jnp.int32, xt.shape, 1)
            tile = None
            for k in range(LANE // SSM_HEAD_DIM):
                hd = t0 // SSM_HEAD_DIM + k
                seg = acum[:, hd:hd + 1] - acum_t[hd:hd + 1, :]
                w = cb * jnp.exp(jnp.where(causal, seg, -jnp.inf)) * dt_t[hd:hd + 1, :]
                mine = (lane >= k * SSM_HEAD_DIM) & (lane < (k + 1) * SSM_HEAD_DIM)
                part = _dot(w.astype(BF16), jnp.where(mine, xt, jnp.zeros_like(xt)))
                tile = part if tile is None else tile + part
            tiles.append(tile)
        y = jnp.concatenate(tiles, axis=1) + y_inter + xs[:, gs] * dx_ref[:, gs]
        y_ref[:, gs] = _rms(y * act_ref[:, gs], nw_ref[:, gs]).astype(BF16)
        state[:, gs] = state[:, gs] * d_acc[L - 1:L, gs] + _dot(b_f.T.astype(BF16), xde[:, gs])


N_SSD_IN = 15


def _ssd_layer_body(x_ref, gpre_ref, gpost_ref, w_ref, cw_ref, cb_ref, dtb_ref, alog_ref, dx_ref, nw_ref,
                    e_ref, tril_ref, wo_ref, hist_ref, h0_ref, *rest, d_inner, n_heads):
    o_ref, hlast_ref, clast_ref, act, cbuf, state, ybf = rest[-7:]
    t = pl.program_id(1)
    tm = x_ref.shape[0]
    L = SSM_CHUNK
    base = SUBLANE - (SSM_CONV - 1)

    @pl.when(t == 0)
    def _():
        cbuf[0:SUBLANE, :] = jnp.zeros((SUBLANE, cbuf.shape[1]), F32)
        cbuf[base:SUBLANE, :] = hist_ref[...]
        state[...] = h0_ref[...].T

    x = x_ref[...]
    u = _rms(x, gpre_ref[...]).astype(BF16)
    _ssd_activations(u, w_ref, cw_ref, cb_ref, dtb_ref, act, cbuf, clast_ref, d_inner)
    for ci in range(tm // L):
        _ssd_chunk(act.at[ci * L:(ci + 1) * L, :], alog_ref, dx_ref, nw_ref, e_ref, tril_ref, state,
                   ybf.at[ci * L:(ci + 1) * L, :], d_inner, n_heads)
    o_ref[...] = x + _rms(_dot(ybf[...], wo_ref[...]), gpost_ref[...])

    @pl.when(t == pl.num_programs(1) - 1)
    def _():
        hlast_ref[...] = state[...].T


def _ssd_layer(x, gpre, gpost, w, cw, cb, dtb, alog, dx, nw, e, tril, wo, hist, h0, prev, layer, n_layers,
               n_seq, d_inner, n_heads, tm):
    m, d = x.shape
    n = w.shape[1]
    tb = m // n_seq // tm
    conv_dim = cw.shape[1]
    k = SSM_CONV - 1
    rows = n_heads * SSM_HEAD_DIM
    assert conv_dim % CONV_COLS == 0 and tm % SSM_CHUNK == 0
    ins = (x, gpre, gpost, w, cw, cb, dtb, alog, dx, nw, e, tril, wo, hist, h0)
    assert len(ins) == N_SSD_IN
    x_spec = pl.BlockSpec((tm, d), lambda b, t: (b * tb + t, 0))
    return pl.pallas_call(
        functools.partial(_ssd_layer_body, d_inner=d_inner, n_heads=n_heads),
        grid=(n_seq, tb),
        in_specs=[x_spec, _full((1, d)), _full((1, d)), _resident(w.shape), _full(cw.shape), _full(cb.shape),
                  _full(dtb.shape), _full(alog.shape), _full(dx.shape), _full(nw.shape), _full(e.shape),
                  _full(tril.shape), _resident(wo.shape),
                  pl.BlockSpec((None, k, conv_dim), lambda b, t: (b, 0, 0)),
                  pl.BlockSpec((None, rows, D_STATE), lambda b, t: (b, 0, 0))]
                 + [pl.BlockSpec(memory_space=pl.ANY)] * len(prev),
        out_specs=[x_spec,
                   pl.BlockSpec((None, rows, D_STATE), lambda b, t: (layer * n_seq + b, 0, 0)),
                   pl.BlockSpec((None, k, conv_dim), lambda b, t: (b, 0, 0))],
        out_shape=[jax.ShapeDtypeStruct((m, d), F32),
                   jax.ShapeDtypeStruct((n_layers * n_seq, rows, D_STATE), F32),
                   jax.ShapeDtypeStruct((n_seq, k, conv_dim), F32)],
        input_output_aliases={N_SSD_IN: 1} if prev else {},
        scratch_shapes=[pltpu.VMEM((tm, n), F32), pltpu.VMEM((tm + SUBLANE, conv_dim), F32),
                        pltpu.VMEM((D_STATE, rows), F32),
                        pltpu.VMEM((tm, d_inner), BF16)],
        compiler_params=_cparams(("parallel", "arbitrary")),
        name="ssd_layer",
    )(*ins, *prev)


def _ssd_step_pre_body(proj_ref, hist_ref, cw_ref, cb_ref, dtb_ref, alog_ref, e_ref,
                       xs_ref, b_ref, c_ref, xdt_t_ref, dec_b_ref, clast_ref, *, d_inner):
    gn = SSM_GROUPS * D_STATE
    conv_dim = d_inner + 2 * gn
    n_heads = d_inner // SSM_HEAD_DIM
    m = proj_ref.shape[0]
    xbc = proj_ref[:, d_inner:d_inner + conv_dim]
    xc = hist_ref[0] * cw_ref[0:1, :] + cb_ref[...]
    for j in range(1, SSM_CONV - 1):
        xc = xc + hist_ref[j] * cw_ref[j:j + 1, :]
        clast_ref[j - 1] = hist_ref[j]
    xc = _silu(xc + xbc * cw_ref[SSM_CONV - 1:SSM_CONV, :])
    clast_ref[SSM_CONV - 2] = xbc
    xs = xc[:, :d_inner]
    xs_ref[...] = xs
    b_ref[...] = xc[:, d_inner:d_inner + gn]
    c_ref[...] = xc[:, d_inner + gn:]
    dt = _softplus(proj_ref[:, d_inner + conv_dim:] + dtb_ref[...])
    dec = jnp.exp(dt * (-jnp.exp(alog_ref[...])))
    pad = jnp.zeros((LANE - m, d_inner), F32)
    xdt_t_ref[...] = jnp.concatenate([xs * _expand(dt, e_ref[...], n_heads), pad], axis=0).T
    for j in range(n_heads):
        dec_b_ref[j] = jnp.broadcast_to(dec[:, j:j + 1], (m, LANE))


def _ssd_step_pre(proj, hist_t, cw, cb, dtb, alog, e, d_inner):
    m = proj.shape[0]
    gn = SSM_GROUPS * D_STATE
    conv_dim = cw.shape[1]
    k = SSM_CONV - 1
    n_heads = d_inner // SSM_HEAD_DIM
    args = (proj, hist_t, cw, cb, dtb, alog, e)
    return pl.pallas_call(
        functools.partial(_ssd_step_pre_body, d_inner=d_inner),
        grid=(1,),
        in_specs=[_full(a.shape) for a in args],
        out_specs=[_full((m, d_inner)), _full((m, gn)), _full((m, gn)), _full((d_inner, LANE)),
                   _full((n_heads, m, LANE)), _full((k, m, conv_dim))],
        out_shape=[jax.ShapeDtypeStruct((m, d_inner), F32), jax.ShapeDtypeStruct((m, gn), F32),
                   jax.ShapeDtypeStruct((m, gn), F32), jax.ShapeDtypeStruct((d_inner, LANE), F32),
                   jax.ShapeDtypeStruct((n_heads, m, LANE), F32), jax.ShapeDtypeStruct((k, m, conv_dim), F32)],
        compiler_params=_cparams(("arbitrary",)),
        name="ssd_step_pre",
    )(*args)


def _ssd_step_state_body(h0_ref, xdt_t_ref, dec_b_ref, b_ref, c_ref, *rest):
    hn_ref, y_t_ref = rest[-2:]
    m = h0_ref.shape[0]
    wide = jnp.concatenate([b_ref[...]] * m, axis=1)
    own = (lax.broadcasted_iota(jnp.int32, wide.shape, 1) // D_STATE
           == lax.broadcasted_iota(jnp.int32, wide.shape, 0))
    bbd = jnp.concatenate([jnp.where(own, wide, 0.0), jnp.zeros((LANE - m, m * D_STATE), F32)], axis=0)
    b_hi, b_lo = _hi_lo(bbd)
    x_hi, x_lo = _hi_lo(xdt_t_ref[...])
    upd = _dot(jnp.concatenate([x_hi, x_lo, x_hi], axis=1), jnp.concatenate([b_hi, b_hi, b_lo], axis=0))
    prods = []
    for b in range(m):
        h = h0_ref[b] * dec_b_ref[b:b + 1, :] + upd[:, b * D_STATE:(b + 1) * D_STATE]
        hn_ref[b] = h
        prods.append(h * c_ref[b:b + 1, :])
    p_hi, p_lo = _hi_lo(jnp.concatenate(prods, axis=0))
    ones = jnp.ones((D_STATE, LANE), BF16)
    sums = _dot(p_hi, ones) + _dot(p_lo, ones)
    lane = lax.broadcasted_iota(jnp.int32, (SSM_HEAD_DIM, LANE), 1)
    acc = jnp.zeros((SSM_HEAD_DIM, LANE), F32)
    for b in range(m):
        acc = jnp.where(lane == b, sums[b * SSM_HEAD_DIM:(b + 1) * SSM_HEAD_DIM, :], acc)
    y_t_ref[...] = acc


def _ssd_step_state(h_all, prev, layer, xdt_t, dec_t, bm, cm, n_heads):
    m = bm.shape[0]
    hpg = n_heads // SSM_GROUPS
    h_spec = pl.BlockSpec((m, None, SSM_HEAD_DIM, D_STATE), lambda j: (layer, j, 0, 0))
    return pl.pallas_call(
        _ssd_step_state_body,
        grid=(n_heads,),
        in_specs=[h_spec,
                  pl.BlockSpec((SSM_HEAD_DIM, LANE), lambda j: (j, 0)),
                  pl.BlockSpec((None, m, LANE), lambda j: (j, 0, 0)),
                  pl.BlockSpec((m, D_STATE), lambda j: (0, j // hpg)),
                  pl.BlockSpec((m, D_STATE), lambda j: (0, j // hpg))]
                 + [pl.BlockSpec(memory_space=pl.ANY)] * len(prev),
        out_specs=[h_spec, pl.BlockSpec((SSM_HEAD_DIM, LANE), lambda j: (j, 0))],
        out_shape=[jax.ShapeDtypeStruct(h_all.shape, F32),
                   jax.ShapeDtypeStruct((n_heads * SSM_HEAD_DIM, LANE), F32)],
        input_output_aliases={5: 0} if prev else {},
        compiler_params=_cparams(("parallel",)),
        name="ssd_step_state",
    )(h_all, xdt_t, dec_t, bm, cm, *prev)


def _ssd_step_post_body(y_t_ref, xs_ref, z_ref, dx_ref, nw_ref, o_ref):
    m, d_inner = xs_ref.shape
    gw = d_inner // SSM_GROUPS
    y = y_t_ref[...].T[:m, :] + xs_ref[...] * dx_ref[...]
    gated = y * _silu(z_ref[...])
    for g in range(SSM_GROUPS):
        seg = gated[:, g * gw:(g + 1) * gw]
        o_ref[:, g * gw:(g + 1) * gw] = _rms(seg, nw_ref[:, g * gw:(g + 1) * gw]).astype(BF16)


def _ssd_step_post(y_t, xs, proj, dx, nw):
    m, d_inner = xs.shape
    return pl.pallas_call(
        _ssd_step_post_body,
        grid=(1,),
        in_specs=[_full(y_t.shape), _full(xs.shape), pl.BlockSpec((m, d_inner), lambda i: (0, 0)),
                  _full(dx.shape), _full(nw.shape)],
        out_specs=_full((m, d_inner)),
        out_shape=jax.ShapeDtypeStruct((m, d_inner), BF16),
        compiler_params=_cparams(("arbitrary",)),
        name="ssd_step_post",
    )(y_t, xs, proj, dx, nw)


def _band_attn_body(q_ref, kc_ref, kp_ref, vc_ref, vp_ref, o_ref, l_ref, *, win, dil):
    blk = pl.program_id(1)
    n_hp, bt, _ = q_ref.shape
    span = win // dil
    heads = LANE // ATT_HEAD_DIM
    qi = lax.broadcasted_iota(jnp.int32, (span, 2 * span), 0)
    ki = lax.broadcasted_iota(jnp.int32, (span, 2 * span), 1)
    band = (ki >= qi) & (ki <= qi + span)
    band_first = band & ((blk > 0) | (ki >= span))
    lane_q = lax.broadcasted_iota(jnp.int32, (span, LANE), 1)
    lane_k = lax.broadcasted_iota(jnp.int32, (2 * span, LANE), 1)

    def rows(ref, hp, start, r):
        if dil == 1:
            return ref[hp, start:start + span, :]
        return ref[hp, pl.ds(start + r, span, stride=dil), :]

    for hp in range(n_hp):
        for wi in range(bt // win):
            for r in range(dil):
                q = (rows(q_ref, hp, wi * win, r) * (ATT_HEAD_DIM ** -0.5)).astype(BF16)
                if wi == 0:
                    k_prev = rows(kp_ref, hp, kp_ref.shape[1] - win, r)
                    v_prev = rows(vp_ref, hp, vp_ref.shape[1] - win, r)
                else:
                    k_prev = rows(kc_ref, hp, (wi - 1) * win, r)
                    v_prev = rows(vc_ref, hp, (wi - 1) * win, r)
                kk = jnp.concatenate([k_prev, rows(kc_ref, hp, wi * win, r)], axis=0).astype(BF16)
                vv = jnp.concatenate([v_prev, rows(vc_ref, hp, wi * win, r)], axis=0).astype(BF16)
                valid = band_first if wi == 0 else band
                o_acc, l_acc = None, None
                for h in range(heads):
                    in_q = (lane_q >= h * ATT_HEAD_DIM) & (lane_q < (h + 1) * ATT_HEAD_DIM)
                    in_k = (lane_k >= h * ATT_HEAD_DIM) & (lane_k < (h + 1) * ATT_HEAD_DIM)
                    s = jnp.where(valid, _dot_nt(jnp.where(in_q, q, jnp.zeros_like(q)), kk), -jnp.inf)
                    mx = jnp.max(s, axis=-1, keepdims=True)
                    p = jnp.exp(s - mx)
                    den = jnp.sum(p, axis=-1, keepdims=True)
                    o_h = _dot(p.astype(BF16), jnp.where(in_k, vv, jnp.zeros_like(vv))) / den
                    l_h = jnp.where(in_q, mx + jnp.log(den), 0.0)
                    o_acc = o_h if o_acc is None else o_acc + o_h
                    l_acc = l_h if l_acc is None else l_acc + l_h
                if dil == 1:
                    o_ref[hp, wi * win:wi * win + span, :] = o_acc
                    l_ref[hp, wi * win:wi * win + span, :] = l_acc
                else:
                    o_ref[hp, pl.ds(wi * win + r, span, stride=dil), :] = o_acc
                    l_ref[hp, pl.ds(wi * win + r, span, stride=dil), :] = l_acc


def _band_attn(qkv_c, n_seq, gi, win, dil):
    nch, m, _ = qkv_c.shape
    t_len = m // n_seq
    bt = max(win, min(BAND_BLOCK, t_len))
    assert bt % win == 0 and t_len % bt == 0
    nblk = t_len // bt
    wpb = bt // win
    per_grp = ATT_GW // LANE
    ng = len(ATT_GROUPS)

    def cur(sec):
        return pl.BlockSpec((per_grp, bt, LANE), lambda b, k: (sec * ng + gi, b * nblk + k, 0))

    def prev(sec):
        return pl.BlockSpec((per_grp, win, LANE),
                            lambda b, k: (sec * ng + gi, jnp.maximum((b * nblk + k) * wpb - 1, 0), 0))

    out_spec = pl.BlockSpec((per_grp, bt, LANE), lambda b, k: (0, b * nblk + k, 0))
    out_sd = jax.ShapeDtypeStruct((per_grp, m, LANE), F32)
    return pl.pallas_call(
        functools.partial(_band_attn_body, win=win, dil=dil),
        grid=(n_seq, nblk),
        in_specs=[cur(0), cur(1), prev(1), cur(2), prev(2)],
        out_specs=[out_spec, out_spec],
        out_shape=[out_sd, out_sd],
        compiler_params=_cparams(("parallel", "arbitrary")),
        name="band_attn_w%d" % win,
    )(qkv_c, qkv_c, qkv_c, qkv_c, qkv_c)


def _row_to_cols(v):
    return jnp.concatenate([jnp.broadcast_to(v[:, c * LANE:(c + 1) * LANE], (LANE, LANE)).T
                            for c in range(v.shape[1] // LANE)], axis=0)


def _col_to_row(v):
    return jnp.concatenate([jnp.broadcast_to(v[c * LANE:(c + 1) * LANE, :], (LANE, LANE)).T[0:1, :]
                            for c in range(v.shape[0] // LANE)], axis=1)


def _attn_sample_body(*refs, dils, n_prev):
    ng = len(dils)
    qkv_ref = refs[0]
    bufs = refs[1:1 + ng]
    outs = refs[1 + ng + n_prev:]
    o_refs, l_refs, c_refs = outs[0:ng], outs[ng:2 * ng], outs[2 * ng:3 * ng]
    scale = ATT_HEAD_DIM ** -0.5
    for gi in range(ng):
        buf, o_ref, l_ref, c_ref = bufs[gi], o_refs[gi], l_refs[gi], c_refs[gi]
        wb = buf.shape[2]
        nch = wb // LANE
        q_c = _row_to_cols(qkv_ref[:, gi * ATT_GW:(gi + 1) * ATT_GW] * scale)
        kn_c = _row_to_cols(qkv_ref[:, (ng + gi) * ATT_GW:(ng + gi + 1) * ATT_GW])
        vn_c = _row_to_cols(qkv_ref[:, (2 * ng + gi) * ATT_GW:(2 * ng + gi + 1) * ATT_GW])
        lane = lax.broadcasted_iota(jnp.int32, (1, wb), 1)
        valid = (lane & (dils[gi] - 1)) == 0
        o_cols = []
        for h in range(ATT_HPG):
            rs = slice(h * ATT_HEAD_DIM, (h + 1) * ATT_HEAD_DIM)
            qh = q_c[rs, :]
            s = jnp.concatenate([jnp.sum(buf[0, rs, c * LANE:(c + 1) * LANE] * qh, axis=0, keepdims=True)
                                 for c in range(nch)], axis=1)
            s = jnp.where(valid, s, -jnp.inf)
            sn = jnp.sum(qh[:, 0:1] * kn_c[rs, 0:1], axis=0, keepdims=True)
            mx = jnp.maximum(jnp.max(s, axis=1, keepdims=True), sn)
            p = jnp.exp(s - mx)
            p_new = jnp.exp(sn - mx)
            den = jnp.sum(p, axis=1, keepdims=True) + p_new
            acc = buf[1, rs, 0:LANE] * p[:, 0:LANE]
            for c in range(1, nch):
                acc = acc + buf[1, rs, c * LANE:(c + 1) * LANE] * p[:, c * LANE:(c + 1) * LANE]
            o_cols.append((jnp.sum(acc, axis=1, keepdims=True) + p_new * vn_c[rs, 0:1]) / den)
            l_ref[:, rs] = jnp.broadcast_to(mx + jnp.log(den), (1, ATT_HEAD_DIM))
        o_ref[...] = _col_to_row(jnp.concatenate(o_cols, axis=0))
        last = lax.broadcasted_iota(jnp.int32, (ATT_HEAD_DIM, LANE), 1) == LANE - 1
        for kv, new_c in ((0, kn_c), (1, vn_c)):
            for h in range(ATT_HPG):
                rs = slice(h * ATT_HEAD_DIM, (h + 1) * ATT_HEAD_DIM)
                rolled = pltpu.roll(buf[kv, rs, :], wb - 1, 1)
                if nch > 1:
                    c_ref[kv, rs, 0:wb - LANE] = rolled[:, 0:wb - LANE]
                c_ref[kv, rs, wb - LANE:wb] = jnp.where(last, new_c[rs, :], rolled[:, wb - LANE:wb])


def _attn_sample(qkv, cache_views, prev, layer):
    m, w3 = qkv.shape
    ng = len(ATT_GROUPS)
    dils = tuple(dil for _, dil in ATT_GROUPS)
    for v, (win, dil) in zip(cache_views, ATT_GROUPS):
        assert v.shape[3] == win and win % LANE == 0 and dil & (dil - 1) == 0
    c_specs = [pl.BlockSpec((None, 2, ATT_GW, v.shape[3]), lambda b: (layer * m + b, 0, 0, 0)) for v in cache_views]
    o_sd = jax.ShapeDtypeStruct((m, 1, ATT_GW), F32)
    o_spec = pl.BlockSpec((None, 1, ATT_GW), lambda b: (b, 0, 0))
    n_in = 1 + ng
    res = pl.pallas_call(
        functools.partial(_attn_sample_body, dils=dils, n_prev=len(prev)),
        grid=(m,),
        in_specs=[pl.BlockSpec((None, 1, w3), lambda b: (b, 0, 0))] + c_specs
                 + [pl.BlockSpec(memory_space=pl.ANY)] * len(prev),
        out_specs=[o_spec] * (2 * ng) + c_specs,
        out_shape=[o_sd] * (2 * ng) + [jax.ShapeDtypeStruct(v.shape, v.dtype) for v in cache_views],
        input_output_aliases={n_in + k: 2 * ng + k for k in range(len(prev))},
        compiler_params=_cparams(("parallel",)),
        name="attn_sample",
    )(qkv.reshape(m, 1, w3), *cache_views, *prev)
    os_ = [r.reshape(m, ATT_GW) for r in res[0:ng]]
    ls_ = [r.reshape(m, ATT_GW) for r in res[ng:2 * ng]]
    return os_, ls_, list(res[2 * ng:])


def _prep_weights(norms, ssm_w_in, ssm_conv_b, ssm_dt_bias, ssm_a_log, ssm_d, ssm_norm_w, ssm_w_out, att_w_qkv,
                  att_w_o, mem_norm, xa_w_q, xa_w_kv, xa_w_o, ffn_w_gu, ffn_conv_b, ffn_w_down):
    n_ssm = ssm_w_in.shape[0]
    n_heads = ssm_dt_bias.shape[1]
    d_inner = n_heads * SSM_HEAD_DIM
    pad_heads = LANE - n_heads
    w = {}
    w['n_heads'], w['d_inner'] = n_heads, d_inner
    w['norms'] = norms[:, :, None, :]
    def per_layer(a, cols=slice(None)):
        return [a[i, :, cols].astype(BF16) for i in range(a.shape[0])]

    w['ssm_w_in'] = [jnp.pad(ssm_w_in[i], ((0, 0), (0, pad_heads))).astype(BF16) for i in range(n_ssm)]
    w['ssm_conv_b'] = ssm_conv_b[:, None, :]
    w['ssm_dt_bias'] = jnp.pad(ssm_dt_bias, ((0, 0), (0, pad_heads)))[:, None, :]
    w['ssm_a_log'] = jnp.pad(ssm_a_log, ((0, 0), (0, pad_heads)))[:, None, :]
    w['ssm_dx'] = jnp.repeat(ssm_d, SSM_HEAD_DIM, axis=1)[:, None, :]
    w['ssm_norm_w'] = ssm_norm_w[:, None, :]
    w['ssm_w_out'] = per_layer(ssm_w_out)
    w['att_w_qkv'] = per_layer(att_w_qkv)
    w['att_w_o'] = per_layer(att_w_o)
    w['mem_norm'] = mem_norm[:, None, :]
    w['xa_w_q'] = per_layer(xa_w_q)
    w['xa_w_kv'] = xa_w_kv.astype(BF16)
    w['xa_w_o'] = per_layer(xa_w_o)
    w['ffn_w_gu'] = ffn_w_gu.astype(BF16)
    w['ffn_conv_b'] = ffn_conv_b[:, None, :]
    w['ffn_w_down'] = ffn_w_down.astype(BF16)
    e = np.zeros((LANE, d_inner), np.float32)
    assert 3 * n_heads <= LANE
    for h in range(n_heads):
        for part in range(3):
            e[part * n_heads + h, h * SSM_HEAD_DIM:(h + 1) * SSM_HEAD_DIM] = 1.0
    w['expand'] = jnp.asarray(e, BF16)
    w['tril'] = jnp.asarray(np.tril(np.ones((SSM_CHUNK, SSM_CHUNK), np.float32)), BF16)
    return w


def _prompt_trunk(x3, mem3, w, ssm_conv_w, ffn_conv_w):
    n, t_len, d = x3.shape
    assert t_len % SSM_CHUNK == 0 and all(t_len % win == 0 for win, _ in ATT_GROUPS)
    depth = w['norms'].shape[0]
    n_heads, d_inner = w['n_heads'], w['d_inner']
    conv_dim = ssm_conv_w.shape[2]
    d_ff = ffn_conv_w.shape[2]
    tm = PROMPT_ROWS_SSD
    tm_big = PROMPT_ROWS
    x = x3.reshape(n * t_len, d)
    mem = mem3.reshape(-1, d)
    kv_all, kv_rows = _mem_kv(mem, w['mem_norm'], w['xa_w_kv'], tm)
    tabs = _rope_tables(jnp.arange(t_len, dtype=jnp.int32))
    zero_hist = jnp.zeros((n, SSM_CONV - 1, conv_dim), F32)
    zero_h = jnp.zeros((n, n_heads * SSM_HEAD_DIM, D_STATE), F32)
    zero_fh = jnp.zeros((n, FFN_CONV - 1, d_ff), F32)
    n_ssm, n_att = (depth + 1) // 2, depth // 2
    new = {'ssm': [], 'ssm_conv': [], 'swa': [], 'ffn_conv': []}
    for i in range(depth):
        j = i // 2
        g = w['norms'][i]
        if i % 2 == 0:
            x, h_last, c_last = _ssd_layer(x, g[0], g[1], w['ssm_w_in'][j], ssm_conv_w[j], w['ssm_conv_b'][j],
                                           w['ssm_dt_bias'][j], w['ssm_a_log'][j], w['ssm_dx'][j],
                                           w['ssm_norm_w'][j], w['expand'], w['tril'], w['ssm_w_out'][j],
                                           zero_hist, zero_h, new['ssm'], j, n_ssm, n, d_inner, n_heads, tm)
            new['ssm'] = [h_last]
            new['ssm_conv'].append(c_last)
        else:
            qkv_c, new['swa'] = _norm_qkv_prompt(x, g[0], w['att_w_qkv'][j], tabs, new['swa'], j, n_att, n, tm_big)
            os_, ls_ = [], []
            for gi, (win, dil) in enumerate(ATT_GROUPS):
                o, l = _band_attn(qkv_c, n, gi, win, dil)
                os_.append(o)
                ls_.append(l)
            x = _attn_out(os_, ls_, w['att_w_o'][j], x, g[1], tm_big)
        x = _xattn_prompt(x, g[2], g[3], w['xa_w_q'][i], kv_all, i, w['xa_w_o'][i], n, tm_big)
        x, f_hist = _ffn_prompt(x, g[4], g[5], w['ffn_w_gu'], i, ffn_conv_w[i],
                                w['ffn_conv_b'][i], w['ffn_w_down'], zero_fh, n, tm_big)
        new['ffn_conv'].append(f_hist)
    n_mem = mem3.shape[1]
    hd = d // MEM_HEADS
    swa = [jnp.transpose(c.reshape(n_att, n, 2, ATT_HPG, ATT_HEAD_DIM, c.shape[3]), (0, 1, 5, 2, 3, 4))
           for c in new['swa']]
    p_mem = jnp.transpose(kv_rows.reshape(depth, n, n_mem, 2, hd // LANE, MEM_HEADS, LANE),
                          (0, 1, 2, 3, 5, 4, 6)).reshape(depth, n, n_mem, 2, MEM_HEADS, hd)
    return (x.reshape(n, t_len, d), new['ssm'][0].reshape(n_ssm, n, n_heads, SSM_HEAD_DIM, D_STATE),
            jnp.stack(new['ssm_conv']), swa, p_mem, jnp.stack(new['ffn_conv']))


def _sample_trunk(x3, w, ssm_conv_w, ffn_conv_w, state_ssm, state_ssm_conv, caches, cache_mem_kv, state_ffn_conv):
    m, t_len, d = x3.shape
    assert t_len == 1
    depth = w['norms'].shape[0]
    n_heads, d_inner = w['n_heads'], w['d_inner']
    n_ssm = state_ssm.shape[0]
    x = x3.reshape(m, d)
    tabs = _rope_tables(jnp.full((m,), PAST_LEN, jnp.int32))
    h_all = state_ssm.reshape(n_ssm * m, n_heads, SSM_HEAD_DIM, D_STATE)
    n_mem, hd = cache_mem_kv.shape[2], d // MEM_HEADS
    kv_all = jnp.transpose(cache_mem_kv.reshape(depth, m, n_mem, 2, MEM_HEADS, hd // LANE, LANE),
                           (0, 1, 2, 3, 5, 4, 6)).reshape(depth * m, n_mem, 2, MEM_HEADS * (hd // LANE), LANE)
    cache_views = [jnp.transpose(c, (0, 1, 3, 4, 5, 2)).reshape(c.shape[0] * m, 2, ATT_GW, c.shape[2])
                   for c in caches]
    new_caches = []
    new = {'ssm': [], 'ssm_conv': [], 'ffn_conv': []}
    for i in range(depth):
        j = i // 2
        g = w['norms'][i]
        if i % 2 == 0:
            proj = _norm_proj(x, g[0], w['ssm_w_in'][j], m)
            hist_t = jnp.swapaxes(state_ssm_conv[j], 0, 1)
            xs, bm, cm, xdt_t, dec_t, c_last_t = _ssd_step_pre(proj, hist_t, ssm_conv_w[j], w['ssm_conv_b'][j],
                                                                w['ssm_dt_bias'][j], w['ssm_a_log'][j],
                                                                w['expand'], d_inner)
            h_new, y_t = _ssd_step_state(h_all, new['ssm'], j, xdt_t, dec_t, bm, cm, n_heads)
            y = _ssd_step_post(y_t, xs, proj, w['ssm_dx'][j], w['ssm_norm_w'][j])
            x = _proj_res(y, w['ssm_w_out'][j], x, g[1], m)
            new['ssm'] = [h_new]
            new['ssm_conv'].append(jnp.swapaxes(c_last_t, 0, 1))
        else:
            qkv = _norm_qkv_rope(x, g[0], w['att_w_qkv'][j], tabs, m, 1)
            os_, ls_, new_caches = _attn_sample(qkv, cache_views, new_caches, j)
            x = _attn_out(os_, ls_, w['att_w_o'][j], x, g[1], m)
        x = _xattn_sample(x, g[2], g[3], w['xa_w_q'][i], kv_all, i, w['xa_w_o'][i])
        hist_t = jnp.swapaxes(state_ffn_conv[i], 0, 1)
        x, f_hist_t = _ffn_sample(x, g[4], g[5], w['ffn_w_gu'], i, ffn_conv_w[i],
                                  w['ffn_conv_b'][i], w['ffn_w_down'], hist_t, 256)
        new['ffn_conv'].append(jnp.swapaxes(f_hist_t, 0, 1))
    new_caches = [jnp.transpose(nc.reshape(c.shape[0], m, 2, ATT_HPG, ATT_HEAD_DIM, c.shape[2]), (0, 1, 5, 2, 3, 4))
                  for nc, c in zip(new_caches, caches)]
    return (x.reshape(m, 1, d), new['ssm'][0].reshape(state_ssm.shape), jnp.stack(new['ssm_conv']), new_caches,
            jnp.stack(new['ffn_conv']))


def kernel(x_prompt, x_sample, mem_prompt, state_ssm, state_ssm_conv, cache_swa_kv_w128, cache_swa_kv_w512,
           cache_swa_kv_w2048, cache_mem_kv, state_ffn_conv, norms, ssm_w_in, ssm_conv_w, ssm_conv_b,
           ssm_dt_bias, ssm_a_log, ssm_d, ssm_norm_w, ssm_w_out, att_w_qkv, att_w_o, mem_norm, xa_w_q,
           xa_w_kv, xa_w_o, ffn_w_gu, ffn_conv_w, ffn_conv_b, ffn_w_down):
    w = _prep_weights(norms, ssm_w_in, ssm_conv_b, ssm_dt_bias, ssm_a_log, ssm_d, ssm_norm_w, ssm_w_out,
                      att_w_qkv, att_w_o, mem_norm, xa_w_q, xa_w_kv, xa_w_o, ffn_w_gu, ffn_conv_b, ffn_w_down)
    caches = [cache_swa_kv_w128, cache_swa_kv_w512, cache_swa_kv_w2048]
    yp, p_ssm, p_conv, p_swa, p_mem, p_ffn = _prompt_trunk(x_prompt, mem_prompt, w, ssm_conv_w, ffn_conv_w)
    ys, s_ssm, s_conv, s_swa, s_ffn = _sample_trunk(x_sample, w, ssm_conv_w, ffn_conv_w, state_ssm,
                                                    state_ssm_conv, caches, cache_mem_kv, state_ffn_conv)
    return (yp, ys, p_ssm, p_conv, p_swa[0], p_swa[1], p_swa[2], p_mem, p_ffn,
            s_ssm, s_conv, s_swa[0], s_swa[1], s_swa[2], s_ffn)
```
